```python
import math
import jax, jax.numpy as jnp
from jax import lax
import numpy as np

D_MODEL = 1024
BATCH = 8
SEQ = 2048
DEPTH = 4

GRID_W = 64
CTX_LEN = 256
N_MIXERS = 2
N_MLSTM_LAYERS = (DEPTH + 1) // 2
N_RGLRU_LAYERS = DEPTH // 2
M_HEADS = 8
M_DQK = 64
M_DV = 128
M_CHUNK = 64
M_PROJ = 2 * M_HEADS * M_DQK + 2 * M_HEADS * M_DV + 4 * M_HEADS
R_WIDTH = D_MODEL
R_BLOCKS = 8
R_BW = R_WIDTH // R_BLOCKS
R_CONV = 4
R_C = 8.0
N_EXPERTS = 16
N_GROUPS = 4
EXPERTS_PER_GROUP = N_EXPERTS // N_GROUPS
TOP_K = 2
D_EXPERT = 512
EPS = 1e-6
POS_BASE = 10000.0

kernel_name = "hybrid_mlstm_rglru_moe_prefix_dit"

F32 = jnp.float32


def rmsnorm(x, g):
    x32 = x.astype(F32)
    y = x32 * lax.rsqrt(jnp.mean(x32 * x32, axis=-1, keepdims=True) + EPS) * g.astype(F32)
    return y.astype(x.dtype)


def modulate(x, g, shift, scale):
    return rmsnorm(x, g) * (1 + scale) + shift


def pos_embed_2d(n_tokens, dim, dtype):
    rows = n_tokens // GRID_W
    r, col = jnp.meshgrid(jnp.arange(rows, dtype=F32), jnp.arange(GRID_W, dtype=F32), indexing='ij')
    quarter = dim // 4
    freqs = jnp.exp(-math.log(POS_BASE) * jnp.arange(quarter, dtype=F32) / quarter)
    def enc(p):
        ang = p.reshape(-1, 1) * freqs
        return jnp.concatenate([jnp.sin(ang), jnp.cos(ang)], axis=-1)
    return jnp.concatenate([enc(r), enc(col)], axis=-1).astype(dtype)


def mlstm_chunkwise(q, k, v, ig, lf, state):
    bsz, nh, t, _ = q.shape
    nc = t // M_CHUNK
    def chunks(a):
        a = a.reshape(bsz, nh, nc, M_CHUNK, *a.shape[3:])
        return jnp.moveaxis(a, 2, 0)
    causal = jnp.tril(jnp.ones((M_CHUNK, M_CHUNK), dtype=bool))
    def step(carry, inp):
        C, n, m = carry
        qc, kc, vc, ic, fc = inp
        b = jnp.cumsum(fc, axis=-1)
        logd = jnp.where(causal, b[..., :, None] - b[..., None, :] + ic[..., None, :], -jnp.inf)
        log_inter = b + m[..., None]
        m_pos = jnp.maximum(log_inter, jnp.max(logd, axis=-1))
        w_intra = jnp.exp(logd - m_pos[..., None])
        w_inter = jnp.exp(log_inter - m_pos)
        s = jnp.einsum('bhjd,bhsd->bhjs', qc, kc) * w_intra
        num = jnp.einsum('bhjs,bhsv->bhjv', s, vc) + w_inter[..., None] * jnp.einsum('bhjd,bhdv->bhjv', qc, C)
        den = jnp.sum(s, axis=-1) + w_inter * jnp.einsum('bhjd,bhd->bhj', qc, n)
        h = num / jnp.maximum(jnp.abs(den), jnp.exp(-m_pos))[..., None]
        m_new = m_pos[..., -1]
        w_s = jnp.exp(b[..., -1:] - b + ic - m_new[..., None])
        decay = jnp.exp(b[..., -1] + m - m_new)
        C_new = decay[..., None, None] * C + jnp.einsum('bhs,bhsd,bhsv->bhdv', w_s, kc, vc)
        n_new = decay[..., None] * n + jnp.einsum('bhs,bhsd->bhd', w_s, kc)
        return (C_new, n_new, m_new), h
    state, h = lax.scan(step, state, (chunks(q), chunks(k), chunks(v), chunks(ig), chunks(lf)))
    h = jnp.moveaxis(h, 0, 2).reshape(bsz, nh, t, -1)
    return h, state


def mlstm_mixer(hl, hc, w_in, b_gate, g_head, w_out, ctx_out):
    split_at = [M_HEADS * M_DQK, 2 * M_HEADS * M_DQK, 2 * M_HEADS * M_DQK + M_HEADS * M_DV,
                2 * M_HEADS * M_DQK + 2 * M_HEADS * M_DV]
    def project(h):
        bsz, t, _ = h.shape
        p = jnp.einsum('btd,dp->btp', h, w_in).astype(F32)
        q, k, v, o, g = jnp.split(p, split_at, axis=-1)
        heads = lambda a: a.reshape(bsz, t, M_HEADS, -1).transpose(0, 2, 1, 3)
        g = g.reshape(bsz, t, 4, M_HEADS).transpose(2, 0, 3, 1) + b_gate.astype(F32)[:, None, :, None]
        return (heads(q) * (M_DQK ** -0.5), heads(k), heads(v), jax.nn.sigmoid(o),
                g[0], jax.nn.log_sigmoid(g[1]), g[2], jax.nn.log_sigmoid(g[3]))
    flip = lambda a: jnp.flip(a, axis=2)
    def bidir(q, k, v, i_f, lf_f, i_b, lf_b, st_f, st_b):
        h_f, st_f = mlstm_chunkwise(q, k, v, i_f, lf_f, st_f)
        h_b, st_b = mlstm_chunkwise(flip(q), flip(k), flip(v), flip(i_b), flip(lf_b), st_b)
        return h_f + flip(h_b), st_f, st_b
    def readout(h, o, dtype):
        bsz, _, t, _ = h.shape
        h = h * lax.rsqrt(jnp.mean(h * h, axis=-1, keepdims=True) + EPS)
        h = h.transpose(0, 2, 1, 3).reshape(bsz, t, M_HEADS * M_DV) * g_head.astype(F32) * o
        return jnp.einsum('bte,ed->btd', h.astype(dtype), w_out).astype(dtype)
    qc, kc, vc, oc, icf, lfcf, icb, lfcb = project(hc)
    bsz = hc.shape[0]
    zero = (jnp.zeros((bsz, M_HEADS, M_DQK, M_DV), F32), jnp.zeros((bsz, M_HEADS, M_DQK), F32),
            jnp.zeros((bsz, M_HEADS), F32))
    hc_sum, st_f, st_b = bidir(qc, kc, vc, icf, lfcf, icb, lfcb, zero, zero)
    ql, kl, vl, ol, ilf, lflf, ilb, lflb = project(hl)
    hl_sum, _, _ = bidir(ql, kl, vl, ilf, lflf, ilb, lflb, st_f, st_b)
    out_l = readout(hl_sum, ol, hl.dtype)
    out_c = readout(hc_sum, oc, hc.dtype) if ctx_out else None
    return out_l, out_c


def centred_dwconv(x, w, b):
    t = x.shape[1]
    left = R_CONV // 2
    xp = jnp.pad(x, ((0, 0), (left, R_CONV - 1 - left), (0, 0)))
    y = b + xp[:, 0:t] * w[0]
    for j in range(1, R_CONV):
        y = y + xp[:, j:j + t] * w[j]
    return y


def _lru_combine(left, right):
    a1, b1 = left
    a2, b2 = right
    return a1 * a2, a2 * b1 + b2


def lru_scan(a, bx, h0, reverse):
    edge = -1 if reverse else 0
    bx = bx.at[:, edge].add(a[:, edge] * h0)
    _, h = lax.associative_scan(_lru_combine, (a, bx), reverse=reverse, axis=1)
    return h


def rglru_mixer(hl, hc, w_in, conv_w, conv_b, w_gate, b_gate, lam, w_out, ctx_out):
    def branches(h):
        p = jnp.einsum('btd,de->bte', h, w_in)
        y, xr = jnp.split(p, 2, axis=-1)
        return y, centred_dwconv(xr, conv_w, conv_b).astype(F32)
    def gates(xr, d):
        bsz, t, _ = xr.shape
        xb = xr.reshape(bsz, t, R_BLOCKS, R_BW)
        g = jnp.einsum('btnd,nde->btne', xb, w_gate[d].astype(F32)) + b_gate[d].astype(F32)
        r = jax.nn.sigmoid(g[..., :R_BW]).reshape(bsz, t, R_WIDTH)
        i = jax.nn.sigmoid(g[..., R_BW:]).reshape(bsz, t, R_WIDTH)
        log_a = -R_C * r * jax.nn.softplus(-lam[d].astype(F32))
        a = jnp.exp(log_a)
        return a, jnp.sqrt(-jnp.expm1(2.0 * log_a)) * (i * xr)
    def readout(y, h, dtype):
        return jnp.einsum('bte,ed->btd', (jax.nn.gelu(y.astype(F32)) * h).astype(dtype), w_out).astype(dtype)
    yc, xcr = branches(hc)
    yl, xlr = branches(hl)
    h0 = jnp.zeros((hc.shape[0], R_WIDTH), F32)
    a, bx = gates(xcr, 0)
    hc_f = lru_scan(a, bx, h0, False)
    a, bx = gates(xcr, 1)
    hc_b = lru_scan(a, bx, h0, True)
    a, bx = gates(xlr, 0)
    hl_f = lru_scan(a, bx, hc_f[:, -1], False)
    a, bx = gates(xlr, 1)
    hl_b = lru_scan(a, bx, hc_b[:, 0], True)
    out_l = readout(yl, hl_f + hl_b, hl.dtype)
    out_c = readout(yc, hc_f + hc_b, hc.dtype) if ctx_out else None
    return out_l, out_c


def moe(h, w_router, b_router, w1, w3, w2):
    logits = jnp.einsum('bnd,de->bne', h.astype(F32), w_router.astype(F32))
    scores = jax.nn.softmax(logits, axis=-1)
    sel = scores + b_router.astype(F32)
    bsz, n, _ = sel.shape
    grp = sel.reshape(bsz, n, N_GROUPS, EXPERTS_PER_GROUP)
    grp_score = jnp.sum(lax.top_k(grp, TOP_K)[0], axis=-1)
    g_idx = jnp.argmax(grp_score, axis=-1)
    in_grp = (jnp.arange(N_EXPERTS) // EXPERTS_PER_GROUP) == g_idx[..., None]
    _, top_idx = lax.top_k(jnp.where(in_grp, sel, -jnp.inf), TOP_K)
    top_w = jnp.take_along_axis(scores, top_idx, axis=-1)
    top_w = top_w / jnp.sum(top_w, axis=-1, keepdims=True)
    gate = jnp.einsum('bnk,bnke->bne', top_w, jax.nn.one_hot(top_idx, N_EXPERTS, dtype=F32))
    y = jnp.zeros(h.shape, F32)
    for e in range(N_EXPERTS):
        u = jax.nn.silu(h @ w1[e]) * (h @ w3[e])
        y = y + gate[..., e:e + 1] * (u @ w2[e]).astype(F32)
    return y.astype(h.dtype)


def setup_inputs(seed: int = 0) -> dict:
    key = jax.random.key(seed)
    ks = jax.random.split(key, 25)
    nrm = lambda k, shape, scale: jax.random.normal(k, shape, F32) * scale
    D = D_MODEL
    f_base = 3.0 + 3.0 * jnp.linspace(0.0, 1.0, M_HEADS)
    zh = jnp.zeros((M_HEADS,), F32)
    gate_base = jnp.stack([zh, f_base, zh, f_base])
    u = jax.random.uniform(ks[18], (N_RGLRU_LAYERS, 2, R_WIDTH), F32, minval=0.9, maxval=0.999)
    return {
        "x": nrm(ks[0], (BATCH, SEQ, D), 1.0),
        "c": nrm(ks[1], (BATCH, D), 1.0),
        "ctx": nrm(ks[2], (BATCH, CTX_LEN, D), 1.0),
        "c_ctx": nrm(ks[3], (D,), 1.0),
        "w_ada": nrm(ks[4], (DEPTH, D, 6 * D), 0.5 * D ** -0.5),
        "b_ada": nrm(ks[5], (DEPTH, 6 * D), 0.02),
        "g_mix": 1.0 + nrm(ks[6], (DEPTH, D), 0.02),
        "g_ffn": 1.0 + nrm(ks[7], (DEPTH, D), 0.02),
        "g_final": 1.0 + nrm(ks[8], (D,), 0.02),
        "m_w_in": nrm(ks[9], (N_MLSTM_LAYERS, D, M_PROJ), D ** -0.5),
        "m_b_gate": gate_base[None] + nrm(ks[10], (N_MLSTM_LAYERS, 4, M_HEADS), 0.1),
        "m_g_head": 1.0 + nrm(ks[11], (N_MLSTM_LAYERS, M_HEADS * M_DV), 0.02),
        "m_w_out": nrm(ks[12], (N_MLSTM_LAYERS, M_HEADS * M_DV, D), (M_HEADS * M_DV) ** -0.5),
        "r_w_in": nrm(ks[13], (N_RGLRU_LAYERS, D, 2 * R_WIDTH), D ** -0.5),
        "r_conv_w": nrm(ks[14], (N_RGLRU_LAYERS, R_CONV, R_WIDTH), R_CONV ** -0.5),
        "r_conv_b": nrm(ks[15], (N_RGLRU_LAYERS, R_WIDTH), 0.02),
        "r_w_gate": nrm(ks[16], (N_RGLRU_LAYERS, 2, R_BLOCKS, R_BW, 2 * R_BW), R_BW ** -0.5),
        "r_b_gate": nrm(ks[17], (N_RGLRU_LAYERS, 2, R_BLOCKS, 2 * R_BW), 0.02),
        "r_lam": jnp.log(u) - jnp.log1p(-u),
        "r_w_out": nrm(ks[19], (N_RGLRU_LAYERS, R_WIDTH, D), R_WIDTH ** -0.5),
        "w_router": nrm(ks[20], (D, N_EXPERTS), D ** -0.5),
        "b_router": nrm(ks[21], (N_EXPERTS,), 0.01),
        "e_w1": nrm(ks[22], (DEPTH, N_EXPERTS, D, D_EXPERT), D ** -0.5),
        "e_w3": nrm(ks[23], (DEPTH, N_EXPERTS, D, D_EXPERT), D ** -0.5),
        "e_w2": nrm(ks[24], (DEPTH, N_EXPERTS, D_EXPERT, D), D_EXPERT ** -0.5),
    }


def reference(x, c, ctx, c_ctx, w_ada, b_ada, g_mix, g_ffn, g_final,
              m_w_in, m_b_gate, m_g_head, m_w_out,
              r_w_in, r_conv_w, r_conv_b, r_w_gate, r_b_gate, r_lam, r_w_out,
              w_router, b_router, e_w1, e_w3, e_w2):
    xl = x + pos_embed_2d(x.shape[1], x.shape[2], x.dtype)[None]
    xc = ctx
    silu_l = jax.nn.silu(c)[:, None, :]
    silu_c = jax.nn.silu(c_ctx)[None, None, :]
    for i in range(DEPTH):
        last = i == DEPTH - 1
        mod_l = jnp.split(silu_l @ w_ada[i] + b_ada[i], 6, axis=-1)
        mod_c = jnp.split(silu_c @ w_ada[i] + b_ada[i], 6, axis=-1)
        hl = modulate(xl, g_mix[i], mod_l[0], mod_l[1])
        hc = modulate(xc, g_mix[i], mod_c[0], mod_c[1])
        j = i // N_MIXERS
        if i % N_MIXERS == 0:
            ol, oc = mlstm_mixer(hl, hc, m_w_in[j], m_b_gate[j], m_g_head[j], m_w_out[j], not last)
        else:
            ol, oc = rglru_mixer(hl, hc, r_w_in[j], r_conv_w[j], r_conv_b[j], r_w_gate[j],
                                 r_b_gate[j], r_lam[j], r_w_out[j], not last)
        xl = xl + mod_l[2] * ol
        hl = modulate(xl, g_ffn[i], mod_l[3], mod_l[4])
        xl = xl + mod_l[5] * moe(hl, w_router, b_router, e_w1[i], e_w3[i], e_w2[i])
        if not last:
            xc = xc + mod_c[2] * oc
            hc = modulate(xc, g_ffn[i], mod_c[3], mod_c[4])
            xc = xc + mod_c[5] * moe(hc, w_router, b_router, e_w1[i], e_w3[i], e_w2[i])
    return rmsnorm(xl, g_final)
```

```python
import functools
import math

import jax
import jax.numpy as jnp
import numpy as np
from jax import lax
from jax.experimental import pallas as pl
from jax.experimental.pallas import tpu as pltpu

F32 = jnp.float32
BF16 = jnp.bfloat16
HIGHEST = lax.Precision.HIGHEST

TOKEN_TILE = 256
LANES = 128
SUBLANES = 8
VMEM_LIMIT_BYTES = 56 * 1024 * 1024
NORM_EPS = 1e-6
GRID_WIDTH = 64
POS_BASE = 10000.0
HEADS = 8
HEAD_QK = 64
HEAD_V = 128
LRU_BLOCKS = 8
LRU_C = 8.0
CONV_TAPS = 4
CONV_LEFT = CONV_TAPS // 2
N_EXPERTS = 16
N_GROUPS = 4
GROUP_SIZE = N_EXPERTS // N_GROUPS
PAIRS = [(a, b) for a in range(GROUP_SIZE) for b in range(a + 1, GROUP_SIZE)]
N_CLASSES = N_GROUPS * len(PAIRS)
CLASS_ROWS = 32
ROUTE_COLS = LANES
ADA_ROWS = 16


def _params(*sem):
    return pltpu.CompilerParams(dimension_semantics=sem, vmem_limit_bytes=VMEM_LIMIT_BYTES)


def _mod_row(i, tiles_per_batch, ctx_tiles, batch):
    return jnp.where(i % tiles_per_batch < ctx_tiles, batch, i // tiles_per_batch)


def _bwd_tile(i, tiles_per_batch, ctx_tiles):
    b = i // tiles_per_batch
    j = i % tiles_per_batch
    jb = jnp.where(j < ctx_tiles, ctx_tiles - 1 - j, tiles_per_batch - 1 - (j - ctx_tiles))
    return b * tiles_per_batch + jb


def _ada_kernel(c_ref, w_ref, b_ref, o_ref):
    c = c_ref[...]
    s = c * jax.nn.sigmoid(c)
    o_ref[...] = jnp.dot(s, w_ref[...], preferred_element_type=F32, precision=HIGHEST) + b_ref[...]


def _ada_call(cvec, w_ada, b_ada):
    depth, d, n = w_ada.shape
    tn = n // 4
    return pl.pallas_call(
        _ada_kernel,
        grid=(depth, n // tn),
        in_specs=[
            pl.BlockSpec((ADA_ROWS, d), lambda l, j: (0, 0)),
            pl.BlockSpec((None, d, tn), lambda l, j: (l, 0, j)),
            pl.BlockSpec((None, 1, tn), lambda l, j: (l, 0, j)),
        ],
        out_specs=pl.BlockSpec((None, ADA_ROWS, tn), lambda l, j: (l, 0, j)),
        out_shape=jax.ShapeDtypeStruct((depth, ADA_ROWS, n), F32),
        compiler_params=_params("arbitrary", "arbitrary"),
        name="ada",
    )(cvec, w_ada, b_ada.reshape(depth, 1, n))


def _stream_kernel(ctx_ref, x_ref, pos_ref, o_ref, *, tiles_per_batch, ctx_tiles):
    j = pl.program_id(0) % tiles_per_batch

    @pl.when(j < ctx_tiles)
    def _():
        o_ref[...] = ctx_ref[...]

    @pl.when(j >= ctx_tiles)
    def _():
        o_ref[...] = x_ref[...] + pos_ref[...]


def _stream_call(ctx2, x2, pos, dims):
    tm, d = TOKEN_TILE, dims["d"]
    tpb, ct = dims["tpb"], dims["ct"]
    lt = tpb - ct
    return pl.pallas_call(
        functools.partial(_stream_kernel, tiles_per_batch=tpb, ctx_tiles=ct),
        grid=(dims["tiles"],),
        in_specs=[
            pl.BlockSpec((tm, d), lambda i: ((i // tpb) * ct + jnp.minimum(i % tpb, ct - 1), 0)),
            pl.BlockSpec((tm, d), lambda i: ((i // tpb) * lt + jnp.maximum(i % tpb - ct, 0), 0)),
            pl.BlockSpec((tm, d), lambda i: (jnp.maximum(i % tpb - ct, 0), 0)),
        ],
        out_specs=pl.BlockSpec((tm, d), lambda i: (i, 0)),
        out_shape=jax.ShapeDtypeStruct((dims["n"], d), F32),
        compiler_params=_params("arbitrary"),
        name="stream",
    )(ctx2, x2, pos)


def _modulated(x, g, mod, shift_idx):
    d = x.shape[-1]
    ms = jnp.mean(x * x, axis=-1, keepdims=True)
    xn = x * lax.rsqrt(ms + NORM_EPS) * g
    shift = mod[:, shift_idx * d:(shift_idx + 1) * d]
    scale = mod[:, (shift_idx + 1) * d:(shift_idx + 2) * d]
    return xn * (1.0 + scale) + shift


def _mod_spec(layer, dims):
    tpb, ct, b = dims["tpb"], dims["ct"], dims["b"]
    return pl.BlockSpec((None, None, 1, 6 * dims["d"]),
                        lambda i, *_: (layer, _mod_row(i, tpb, ct, b), 0, 0))


def _gelu_tanh(y):
    return 0.5 * y * (1.0 + jnp.tanh(math.sqrt(2.0 / math.pi) * (y + 0.044715 * (y * y * y))))


def _mproj_kernel(x_ref, g_ref, mod_ref, w_ref, wgt_ref, qk_ref, v_ref, og_ref, gc_ref, gr_ref):
    d = x_ref.shape[-1]
    h = _modulated(x_ref[...], g_ref[...], mod_ref[...], 0).astype(BF16)
    qk_ref[...] = jnp.dot(h, w_ref[:, 0:d], preferred_element_type=F32).astype(BF16)
    v_ref[...] = jnp.dot(h, w_ref[:, d:2 * d], preferred_element_type=F32).astype(BF16)
    og_ref[...] = jax.nn.sigmoid(jnp.dot(h, w_ref[:, 2 * d:3 * d], preferred_element_type=F32))
    gc_ref[...] = jnp.dot(h, w_ref[:, 3 * d:3 * d + LANES], preferred_element_type=F32)
    gr_ref[...] = lax.dot_general(wgt_ref[...], h, (((1,), (1,)), ((), ())),
                                  preferred_element_type=F32)


def _mproj_call(xs, g, mods, layer, w_all, w_gate_t, dims):
    tm, d, n = TOKEN_TILE, dims["d"], dims["n"]
    row = lambda i: (i, 0)
    const = lambda i: (0, 0)
    return pl.pallas_call(
        _mproj_kernel,
        grid=(dims["tiles"],),
        in_specs=[
            pl.BlockSpec((tm, d), row),
            pl.BlockSpec((1, d), const),
            _mod_spec(layer, dims),
            pl.BlockSpec(w_all.shape, const),
            pl.BlockSpec(w_gate_t.shape, const),
        ],
        out_specs=[
            pl.BlockSpec((tm, d), row),
            pl.BlockSpec((tm, d), row),
            pl.BlockSpec((tm, d), row),
            pl.BlockSpec((tm, LANES), row),
            pl.BlockSpec((4 * HEADS, tm), lambda i: (0, i)),
        ],
        out_shape=[
            jax.ShapeDtypeStruct((n, d), BF16),
            jax.ShapeDtypeStruct((n, d), BF16),
            jax.ShapeDtypeStruct((n, d), F32),
            jax.ShapeDtypeStruct((n, LANES), F32),
            jax.ShapeDtypeStruct((4 * HEADS, n), F32),
        ],
        compiler_params=_params("arbitrary"),
        name="mlstm_proj",
    )(xs, g, mods, w_all, w_gate_t)


def _mlstm_direction(qk_ref, v_ref, gc_ref, gr_ref, bc_ref, br_ref, o_ref, c_ref, m_ref, *, backward):
    tm = qk_ref.shape[0]
    gate_i = 2 * HEADS if backward else 0
    gate_f = gate_i + HEADS
    rows = lax.broadcasted_iota(jnp.int32, (tm, tm), 0)
    cols = lax.broadcasted_iota(jnp.int32, (tm, tm), 1)
    lower = cols <= rows
    upper = cols >= rows
    mask = upper if backward else lower
    xc = gc_ref[...] + bc_ref[...]
    xr = gr_ref[...] + br_ref[...]
    lsc = jax.nn.log_sigmoid(xc)
    lsr = jax.nn.log_sigmoid(xr)
    cum_mat_c = (upper if backward else lower).astype(F32)
    cum_mat_r = (lower if backward else upper).astype(F32)
    cum_c = jnp.dot(cum_mat_c, lsc, preferred_element_type=F32, precision=HIGHEST)
    cum_r = jnp.dot(lsr, cum_mat_r, preferred_element_type=F32, precision=HIGHEST)
    last = 0 if backward else tm - 1
    ones = jnp.ones((tm, HEAD_V), BF16)
    for h in range(HEADS):
        q = qk_ref[:, h * LANES:h * LANES + HEAD_QK]
        k = qk_ref[:, h * LANES + HEAD_QK:(h + 1) * LANES]
        vaug = jnp.concatenate([v_ref[:, h * HEAD_V:(h + 1) * HEAD_V], ones], axis=1)
        bcol = cum_c[:, gate_f + h:gate_f + h + 1]
        brow = cum_r[gate_f + h:gate_f + h + 1, :]
        icol = xc[:, gate_i + h:gate_i + h + 1]
        irow = xr[gate_i + h:gate_i + h + 1, :]
        m = m_ref[h:h + 1, 0:1]
        logd = jnp.where(mask, bcol - brow + irow, -jnp.inf)
        log_inter = bcol + m
        m_pos = jnp.maximum(log_inter, jnp.max(logd, axis=-1, keepdims=True))
        w_intra = jnp.exp(logd - m_pos)
        w_inter = jnp.exp(log_inter - m_pos)
        s = lax.dot_general(q, k, (((1,), (1,)), ((), ())), preferred_element_type=F32)
        p = (s * w_intra).astype(BF16)
        c = c_ref[h]
        acc = jnp.dot(p, vaug, preferred_element_type=F32)
        acc = acc + w_inter * jnp.dot(q, c.astype(BF16), preferred_element_type=F32)
        num = acc[:, :HEAD_V]
        den = acc[:, HEAD_V:]
        o_ref[:, h * HEAD_V:(h + 1) * HEAD_V] = num / jnp.maximum(jnp.abs(den), jnp.exp(-m_pos))
        b_last = bcol[last:last + 1, :]
        m_new = m_pos[last:last + 1, :]
        w_s = jnp.exp(b_last - bcol + icol - m_new)
        decay = jnp.exp(b_last + m - m_new)
        kw = (k.astype(F32) * w_s).astype(BF16)
        c_ref[h] = decay * c + lax.dot_general(kw, vaug, (((0,), (0,)), ((), ())),
                                               preferred_element_type=F32)
        m_ref[h:h + 1, :] = jnp.broadcast_to(m_new, (1, LANES))


def _mlstm_kernel(qkf, vf, gcf, grf, qkb, vb, gcb, grb, bc_ref, br_ref, of_ref, ob_ref,
                  cf_ref, mf_ref, cb_ref, mb_ref, *, tiles_per_batch):
    @pl.when(pl.program_id(0) % tiles_per_batch == 0)
    def _():
        cf_ref[...] = jnp.zeros_like(cf_ref)
        cb_ref[...] = jnp.zeros_like(cb_ref)
        mf_ref[...] = jnp.zeros_like(mf_ref)
        mb_ref[...] = jnp.zeros_like(mb_ref)

    _mlstm_direction(qkf, vf, gcf, grf, bc_ref, br_ref, of_ref, cf_ref, mf_ref, backward=False)
    _mlstm_direction(qkb, vb, gcb, grb, bc_ref, br_ref, ob_ref, cb_ref, mb_ref, backward=True)


def _mlstm_call(qk, v, gc, gr, bias_c, bias_r, dims):
    tm, d, n = TOKEN_TILE, dims["d"], dims["n"]
    tpb, ct = dims["tpb"], dims["ct"]
    fwd = lambda i: (i, 0)
    bwd = lambda i: (_bwd_tile(i, tpb, ct), 0)
    fwd_t = lambda i: (0, i)
    bwd_t = lambda i: (0, _bwd_tile(i, tpb, ct))
    const = lambda i: (0, 0)
    state = [pltpu.VMEM((HEADS, HEAD_QK, 2 * HEAD_V), F32), pltpu.VMEM((HEADS, LANES), F32)]
    return pl.pallas_call(
        functools.partial(_mlstm_kernel, tiles_per_batch=tpb),
        grid=(dims["tiles"],),
        in_specs=[
            pl.BlockSpec((tm, d), fwd), pl.BlockSpec((tm, d), fwd),
            pl.BlockSpec((tm, LANES), fwd), pl.BlockSpec((4 * HEADS, tm), fwd_t),
            pl.BlockSpec((tm, d), bwd), pl.BlockSpec((tm, d), bwd),
            pl.BlockSpec((tm, LANES), bwd), pl.BlockSpec((4 * HEADS, tm), bwd_t),
            pl.BlockSpec((1, LANES), const), pl.BlockSpec((4 * HEADS, 1), const),
        ],
        out_specs=[pl.BlockSpec((tm, d), fwd), pl.BlockSpec((tm, d), bwd)],
        out_shape=[jax.ShapeDtypeStruct((n, d), F32), jax.ShapeDtypeStruct((n, d), F32)],
        scratch_shapes=state + state,
        compiler_params=_params("arbitrary"),
        name="mlstm_scan",
    )(qk, v, gc, gr, qk, v, gc, gr, bias_c, bias_r)


def _readout_kernel(hf_ref, hb_ref, aux_ref, gh_ref, w_ref, x_ref, mod_ref, o_ref, *, head_norm):
    d = x_ref.shape[-1]
    hs = hf_ref[...] + hb_ref[...]
    if head_norm:
        parts = []
        for h in range(HEADS):
            hh = hs[:, h * HEAD_V:(h + 1) * HEAD_V]
            parts.append(hh * lax.rsqrt(jnp.mean(hh * hh, axis=-1, keepdims=True) + NORM_EPS))
        t = jnp.concatenate(parts, axis=1) * gh_ref[...] * aux_ref[...]
    else:
        t = aux_ref[...] * hs
    out = jnp.dot(t.astype(BF16), w_ref[...], preferred_element_type=F32)
    o_ref[...] = x_ref[...] + mod_ref[:, 2 * d:3 * d] * out


def _readout_call(hf, hb, aux, g_head, w_out, xs, mods, layer, dims, head_norm):
    tm, d, n = TOKEN_TILE, dims["d"], dims["n"]
    row = lambda i: (i, 0)
    const = lambda i: (0, 0)
    return pl.pallas_call(
        functools.partial(_readout_kernel, head_norm=head_norm),
        grid=(dims["tiles"],),
        in_specs=[
            pl.BlockSpec((tm, d), row), pl.BlockSpec((tm, d), row), pl.BlockSpec((tm, d), row),
            pl.BlockSpec((1, d), const), pl.BlockSpec((d, d), const),
            pl.BlockSpec((tm, d), row), _mod_spec(layer, dims),
        ],
        out_specs=pl.BlockSpec((tm, d), row),
        out_shape=jax.ShapeDtypeStruct((n, d), F32),
        compiler_params=_params("arbitrary"),
        name="readout",
    )(hf, hb, aux, g_head, w_out, xs, mods)


def _rproj_kernel(x_ref, g_ref, mod_ref, w_ref, gy_ref, xr_ref):
    d = x_ref.shape[-1]
    h = _modulated(x_ref[...], g_ref[...], mod_ref[...], 0).astype(BF16)
    gy_ref[...] = _gelu_tanh(jnp.dot(h, w_ref[:, 0:d], preferred_element_type=F32))
    xr_ref[...] = jnp.dot(h, w_ref[:, d:2 * d], preferred_element_type=F32)


def _rproj_call(xs, g, mods, layer, w_in, dims):
    tm, d, n = TOKEN_TILE, dims["d"], dims["n"]
    row = lambda i: (i, 0)
    const = lambda i: (0, 0)
    return pl.pallas_call(
        _rproj_kernel,
        grid=(dims["tiles"],),
        in_specs=[pl.BlockSpec((tm, d), row), pl.BlockSpec((1, d), const), _mod_spec(layer, dims),
                  pl.BlockSpec(w_in.shape, const)],
        out_specs=[pl.BlockSpec((tm, d), row), pl.BlockSpec((tm, d), row)],
        out_shape=[jax.ShapeDtypeStruct((n, d), F32), jax.ShapeDtypeStruct((n, d), F32)],
        compiler_params=_params("arbitrary"),
        name="rglru_proj",
    )(xs, g, mods, w_in)


def _lru_direction(cur_ref, prev_ref, next_ref, prev_ok, next_ok, cw_ref, cb_ref, wg_ref, bg_ref,
                   lam_ref, ext_ref, a_ref, b_ref, *, direction):
    tm, d = cur_ref.shape
    bw = d // LRU_BLOCKS
    ext_ref[0:SUBLANES, :] = jnp.where(prev_ok, prev_ref[...], 0.0)
    ext_ref[SUBLANES:SUBLANES + tm, :] = cur_ref[...]
    ext_ref[SUBLANES + tm:2 * SUBLANES + tm, :] = jnp.where(next_ok, next_ref[...], 0.0)
    xc = cb_ref[...] + ext_ref[pl.ds(SUBLANES - CONV_LEFT, tm), :] * cw_ref[0:1, :]
    for j in range(1, CONV_TAPS):
        xc = xc + ext_ref[pl.ds(SUBLANES - CONV_LEFT + j, tm), :] * cw_ref[j:j + 1, :]
    z = -lam_ref[direction]
    softplus = jnp.maximum(z, 0.0) + jnp.log(1.0 + jnp.exp(-jnp.abs(z)))
    for nb in range(LRU_BLOCKS):
        sl = slice(nb * bw, (nb + 1) * bw)
        xb = xc[:, sl]
        gg = jnp.dot(xb.astype(BF16), wg_ref[direction, nb], preferred_element_type=F32)
        gg = gg + bg_ref[direction, :, nb * 2 * bw:(nb + 1) * 2 * bw]
        r = jax.nn.sigmoid(gg[:, :bw])
        gi = jax.nn.sigmoid(gg[:, bw:])
        log_a = -LRU_C * r * softplus[:, sl]
        a = jnp.exp(log_a)
        a_ref[:, sl] = a
        th = jnp.tanh(log_a)
        b_ref[:, sl] = jnp.sqrt(-2.0 * th / (1.0 - th)) * (gi * xb)


def _lru_kernel(cur_f, prev_f, next_f, cur_b, prev_b, next_b, cw_ref, cb_ref, wg_ref, bg_ref, lam_ref,
                of_ref, ob_ref, ext_ref, af_ref, bf_ref, ab_ref, bb_ref, h_ref,
                *, tiles_per_batch, ctx_tiles):
    tm = cur_f.shape[0]
    j = pl.program_id(0) % tiles_per_batch

    @pl.when(j == 0)
    def _():
        h_ref[...] = jnp.zeros_like(h_ref)

    def seg_edges(jj):
        first = jnp.logical_or(jj == 0, jj == ctx_tiles)
        last = jnp.logical_or(jj == ctx_tiles - 1, jj == tiles_per_batch - 1)
        return jnp.logical_not(first), jnp.logical_not(last)

    jb = _bwd_tile(pl.program_id(0), tiles_per_batch, ctx_tiles) % tiles_per_batch
    pf, nf = seg_edges(j)
    pb, nb = seg_edges(jb)
    common = (cw_ref, cb_ref, wg_ref, bg_ref, lam_ref, ext_ref)
    _lru_direction(cur_f, prev_f, next_f, pf, nf, *common, af_ref, bf_ref, direction=0)
    _lru_direction(cur_b, prev_b, next_b, pb, nb, *common, ab_ref, bb_ref, direction=1)

    def step(t, carry):
        hf, hb = carry
        tb = tm - 1 - t
        hf = af_ref[pl.ds(t, 1), :] * hf + bf_ref[pl.ds(t, 1), :]
        hb = ab_ref[pl.ds(tb, 1), :] * hb + bb_ref[pl.ds(tb, 1), :]
        of_ref[pl.ds(t, 1), :] = hf
        ob_ref[pl.ds(tb, 1), :] = hb
        return hf, hb

    hf, hb = lax.fori_loop(0, tm, step, (h_ref[0:1, :], h_ref[1:2, :]), unroll=8)
    h_ref[0:1, :] = hf
    h_ref[1:2, :] = hb


def _lru_call(xr, conv_w, conv_b, w_gate, b_gate, lam, dims):
    tm, d, n = TOKEN_TILE, dims["d"], dims["n"]
    tpb, ct = dims["tpb"], dims["ct"]
    per = tm // SUBLANES
    last8 = n // SUBLANES - 1
    fwd = lambda i: (i, 0)
    bwd = lambda i: (_bwd_tile(i, tpb, ct), 0)
    prev_of = lambda t: (jnp.maximum(t * per - 1, 0), 0)
    next_of = lambda t: (jnp.minimum((t + 1) * per, last8), 0)
    c2 = lambda i: (0, 0)
    c3 = lambda i: (0, 0, 0)
    c4 = lambda i: (0, 0, 0, 0)
    return pl.pallas_call(
        functools.partial(_lru_kernel, tiles_per_batch=tpb, ctx_tiles=ct),
        grid=(dims["tiles"],),
        in_specs=[
            pl.BlockSpec((tm, d), fwd),
            pl.BlockSpec((SUBLANES, d), lambda i: prev_of(i)),
            pl.BlockSpec((SUBLANES, d), lambda i: next_of(i)),
            pl.BlockSpec((tm, d), bwd),
            pl.BlockSpec((SUBLANES, d), lambda i: prev_of(_bwd_tile(i, tpb, ct))),
            pl.BlockSpec((SUBLANES, d), lambda i: next_of(_bwd_tile(i, tpb, ct))),
            pl.BlockSpec(conv_w.shape, c2), pl.BlockSpec(conv_b.shape, c2),
            pl.BlockSpec(w_gate.shape, c4), pl.BlockSpec(b_gate.shape, c3), pl.BlockSpec(lam.shape, c3),
        ],
        out_specs=[pl.BlockSpec((tm, d), fwd), pl.BlockSpec((tm, d), bwd)],
        out_shape=[jax.ShapeDtypeStruct((n, d), F32), jax.ShapeDtypeStruct((n, d), F32)],
        scratch_shapes=[pltpu.VMEM((tm + 2 * SUBLANES, d), F32)] + [pltpu.VMEM((tm, d), F32)] * 4
        + [pltpu.VMEM((SUBLANES, d), F32)],
        compiler_params=_params("arbitrary"),
        name="rglru_scan",
    )(xr, xr, xr, xr, xr, xr, conv_w, conv_b, w_gate, b_gate, lam)


def _first_argmax(vals):
    best = vals[0]
    idx = jnp.zeros_like(best)
    for k in range(1, len(vals)):
        better = vals[k] > best
        idx = jnp.where(better, float(k), idx)
        best = jnp.where(better, vals[k], best)
    return idx, best


def _pick(idx, vals):
    out = vals[0]
    for k in range(1, len(vals)):
        out = jnp.where(idx == float(k), vals[k], out)
    return out


def _router_kernel(x_ref, g_ref, mod_ref, wr_ref, br_ref, hx_ref, route_ref, cnt_ref, carry_ref):
    tm, d = x_ref.shape

    @pl.when(pl.program_id(0) == 0)
    def _():
        carry_ref[...] = jnp.zeros_like(carry_ref)

    h = _modulated(x_ref[...], g_ref[...], mod_ref[...], 3)
    hx_ref[:, 0:d] = h
    logits = lax.dot_general(wr_ref[...], h, (((1,), (1,)), ((), ())),
                             preferred_element_type=F32, precision=HIGHEST)
    e = jnp.exp(logits - jnp.max(logits, axis=0, keepdims=True))
    scores = e / jnp.sum(e, axis=0, keepdims=True)
    sel = scores + br_ref[...]
    sel_rows = [sel[k:k + 1, :] for k in range(N_EXPERTS)]
    score_rows = [scores[k:k + 1, :] for k in range(N_EXPERTS)]
    group_scores = []
    for gi in range(N_GROUPS):
        v = sel_rows[gi * GROUP_SIZE:(gi + 1) * GROUP_SIZE]
        best = v[0] + v[1]
        for a, b in PAIRS[1:]:
            best = jnp.maximum(best, v[a] + v[b])
        group_scores.append(best)
    grp, _ = _first_argmax(group_scores)
    in_sel = [_pick(grp, [sel_rows[gi * GROUP_SIZE + k] for gi in range(N_GROUPS)])
              for k in range(GROUP_SIZE)]
    in_score = [_pick(grp, [score_rows[gi * GROUP_SIZE + k] for gi in range(N_GROUPS)])
                for k in range(GROUP_SIZE)]
    i1, _ = _first_argmax(in_sel)
    rest = [jnp.where(i1 == float(k), -jnp.inf, in_sel[k]) for k in range(GROUP_SIZE)]
    i2, _ = _first_argmax(rest)
    lo = jnp.minimum(i1, i2)
    hi = jnp.maximum(i1, i2)
    s_lo = _pick(lo, in_score)
    s_hi = _pick(hi, in_score)
    tot = s_lo + s_hi
    w_lo = s_lo / tot
    w_hi = s_hi / tot
    pair = jnp.where(lo == 0.0, hi - 1.0, jnp.where(lo == 1.0, hi + 1.0, 5.0))
    cls = grp * float(len(PAIRS)) + pair
    crow = lax.broadcasted_iota(jnp.int32, (CLASS_ROWS, tm), 0).astype(F32)
    onehot = jnp.where(crow == cls, 1.0, 0.0)
    rows = lax.broadcasted_iota(jnp.int32, (tm, tm), 0)
    cols = lax.broadcasted_iota(jnp.int32, (tm, tm), 1)
    upper = jnp.where(rows <= cols, 1.0, 0.0).astype(BF16)
    cum = jnp.dot(onehot.astype(BF16), upper, preferred_element_type=F32)
    carry = carry_ref[:, 0:1]
    rank = jnp.sum(onehot * (cum - 1.0 + carry), axis=0, keepdims=True)
    new_carry = carry + jnp.sum(onehot, axis=1, keepdims=True)
    carry_ref[...] = jnp.broadcast_to(new_carry, carry_ref.shape)
    cnt_ref[...] = jnp.broadcast_to(new_carry, cnt_ref.shape)
    zero = jnp.zeros_like(cls)
    route_ref[...] = jnp.concatenate([cls, rank, w_lo, w_hi, zero, zero, zero, zero], axis=0)
    eye = rows == cols
    wl_col = jnp.sum(jnp.where(eye, w_lo, 0.0), axis=1, keepdims=True)
    wh_col = jnp.sum(jnp.where(eye, w_hi, 0.0), axis=1, keepdims=True)
    lane = lax.broadcasted_iota(jnp.int32, (tm, ROUTE_COLS), 1)
    hx_ref[:, d:d + ROUTE_COLS] = jnp.where(lane == 0, wl_col, jnp.where(lane == 1, wh_col, 0.0))


def _router_call(xs, g, mods, layer, w_router_t, b_router_col, dims):
    tm, d, n = TOKEN_TILE, dims["d"], dims["n"]
    row = lambda i: (i, 0)
    const = lambda i: (0, 0)
    return pl.pallas_call(
        _router_kernel,
        grid=(dims["tiles"],),
        in_specs=[pl.BlockSpec((tm, d), row), pl.BlockSpec((1, d), const), _mod_spec(layer, dims),
                  pl.BlockSpec(w_router_t.shape, const), pl.BlockSpec(b_router_col.shape, const)],
        out_specs=[pl.BlockSpec((tm, d + ROUTE_COLS), row),
                   pl.BlockSpec((SUBLANES, tm), lambda i: (0, i)),
                   pl.BlockSpec((CLASS_ROWS, LANES), const)],
        out_shape=[jax.ShapeDtypeStruct((n, d + ROUTE_COLS), F32),
                   jax.ShapeDtypeStruct((SUBLANES, n), F32),
                   jax.ShapeDtypeStruct((CLASS_ROWS, LANES), F32)],
        scratch_shapes=[pltpu.VMEM((CLASS_ROWS, LANES), F32)],
        compiler_params=_params("arbitrary"),
        name="router",
    )(xs, g, mods, w_router_t, b_router_col)


def _row_copy(src_ref, src_row, dst_ref, dst_row, sem):
    return pltpu.make_async_copy(src_ref.at[pl.ds(src_row, 1)], dst_ref.at[pl.ds(dst_row, 1)], sem)


def _dispatch_kernel(pos_ref, hx_ref, init_ref, o_ref, sem):
    del init_ref
    base = pl.program_id(0) * TOKEN_TILE

    def issue(r, carry):
        _row_copy(hx_ref, base + r, o_ref, pos_ref[base + r], sem).start()
        return carry

    def drain(r, carry):
        _row_copy(hx_ref, base + r, o_ref, pos_ref[base + r], sem).wait()
        return carry

    lax.fori_loop(0, TOKEN_TILE, issue, 0)
    lax.fori_loop(0, TOKEN_TILE, drain, 0)


def _dispatch_call(pos, hx, padded_rows, dims):
    width = hx.shape[1]
    init = jnp.zeros((padded_rows, width), F32)
    return pl.pallas_call(
        _dispatch_kernel,
        grid_spec=pltpu.PrefetchScalarGridSpec(
            num_scalar_prefetch=1,
            grid=(dims["tiles"],),
            in_specs=[pl.BlockSpec(memory_space=pl.ANY), pl.BlockSpec(memory_space=pl.ANY)],
            out_specs=pl.BlockSpec(memory_space=pl.ANY),
            scratch_shapes=[pltpu.SemaphoreType.DMA(())],
        ),
        out_shape=jax.ShapeDtypeStruct((padded_rows, width), F32),
        input_output_aliases={2: 0},
        compiler_params=_params("arbitrary"),
        name="dispatch",
    )(pos, hx, init)


def _expert_kernel(ea_ref, eb_ref, ok_ref, x_ref, w1a, w3a, w2a, w1b, w3b, w2b, o_ref):
    del ea_ref, eb_ref
    d = o_ref.shape[-1]
    t = pl.program_id(0)

    @pl.when(ok_ref[t] != 0)
    def _():
        x = x_ref[:, 0:d].astype(BF16)

        def expert(w1, w3, w2):
            a = jnp.dot(x, w1[...], preferred_element_type=F32)
            b = jnp.dot(x, w3[...], preferred_element_type=F32)
            u = (a * jax.nn.sigmoid(a)) * b
            return jnp.dot(u.astype(BF16), w2[...], preferred_element_type=F32)

        o_ref[...] = (x_ref[:, d:d + 1] * expert(w1a, w3a, w2a)
                      + x_ref[:, d + 1:d + 2] * expert(w1b, w3b, w2b))

    @pl.when(ok_ref[t] == 0)
    def _():
        o_ref[...] = jnp.zeros_like(o_ref)


def _expert_call(tile_a, tile_b, tile_ok, xsorted, w1, w3, w2, d):
    te = TOKEN_TILE
    rows, width = xsorted.shape
    de = w1.shape[-1]
    sel_a = lambda t, ea, eb, ok: (ea[t], 0, 0)
    sel_b = lambda t, ea, eb, ok: (eb[t], 0, 0)
    return pl.pallas_call(
        _expert_kernel,
        grid_spec=pltpu.PrefetchScalarGridSpec(
            num_scalar_prefetch=3,
            grid=(rows // te,),
            in_specs=[
                pl.BlockSpec((te, width), lambda t, ea, eb, ok: (t, 0)),
                pl.BlockSpec((None, d, de), sel_a), pl.BlockSpec((None, d, de), sel_a),
                pl.BlockSpec((None, de, d), sel_a),
                pl.BlockSpec((None, d, de), sel_b), pl.BlockSpec((None, d, de), sel_b),
                pl.BlockSpec((None, de, d), sel_b),
            ],
            out_specs=pl.BlockSpec((te, d), lambda t, ea, eb, ok: (t, 0)),
        ),
        out_shape=jax.ShapeDtypeStruct((rows, d), F32),
        compiler_params=_params("arbitrary"),
        name="experts",
    )(tile_a, tile_b, tile_ok, xsorted, w1, w3, w2, w1, w3, w2)


def _combine_kernel(pos_ref, y_ref, x_ref, mod_ref, gf_ref, o_ref, buf_ref, sem,
                    *, tile_of, final_norm):
    tm, d = x_ref.shape
    base = tile_of(pl.program_id(0)) * tm

    def issue(r, carry):
        _row_copy(y_ref, pos_ref[base + r], buf_ref, r, sem).start()
        return carry

    def drain(r, carry):
        _row_copy(y_ref, pos_ref[base + r], buf_ref, r, sem).wait()
        return carry

    lax.fori_loop(0, tm, issue, 0)
    lax.fori_loop(0, tm, drain, 0)
    out = x_ref[...] + mod_ref[:, 5 * d:6 * d] * buf_ref[...]
    if final_norm:
        ms = jnp.mean(out * out, axis=-1, keepdims=True)
        out = out * lax.rsqrt(ms + NORM_EPS) * gf_ref[...]
    o_ref[...] = out


def _combine_call(pos, ysorted, xs, mods, layer, g_final, dims, final_norm):
    tm, d = TOKEN_TILE, dims["d"]
    tpb, ct, b = dims["tpb"], dims["ct"], dims["b"]
    if final_norm:
        lt = tpb - ct
        tile_of = lambda i: (i // lt) * tpb + ct + i % lt
        n_tiles = b * lt
    else:
        tile_of = lambda i: i
        n_tiles = dims["tiles"]
    mod_spec = pl.BlockSpec((None, None, 1, 6 * d),
                            lambda i, p: (layer, _mod_row(tile_of(i), tpb, ct, b), 0, 0))
    return pl.pallas_call(
        functools.partial(_combine_kernel, tile_of=tile_of, final_norm=final_norm),
        grid_spec=pltpu.PrefetchScalarGridSpec(
            num_scalar_prefetch=1,
            grid=(n_tiles,),
            in_specs=[
                pl.BlockSpec(memory_space=pl.ANY),
                pl.BlockSpec((tm, d), lambda i, p: (tile_of(i), 0)),
                mod_spec,
                pl.BlockSpec((1, d), lambda i, p: (0, 0)),
            ],
            out_specs=pl.BlockSpec((tm, d), lambda i, p: (i, 0)),
            scratch_shapes=[pltpu.VMEM((tm, d), F32), pltpu.SemaphoreType.DMA(())],
        ),
        out_shape=jax.ShapeDtypeStruct((n_tiles * tm, d), F32),
        compiler_params=_params("arbitrary"),
        name="combine",
    )(pos, ysorted, xs, mods, g_final)


_CLASS_A = np.array([g * GROUP_SIZE + a for g in range(N_GROUPS) for a, _ in PAIRS], np.int32)
_CLASS_B = np.array([g * GROUP_SIZE + b for g in range(N_GROUPS) for _, b in PAIRS], np.int32)


def _routing_plan(route, counts, n_tiles_padded):
    te = TOKEN_TILE
    cls = route[0].astype(jnp.int32)
    rank = route[1].astype(jnp.int32)
    cnt = counts[:N_CLASSES, 0].astype(jnp.int32)
    tiles = (cnt + te - 1) // te
    tile_end = jnp.cumsum(tiles)
    tile_start = tile_end - tiles
    pos = jnp.take(tile_start * te, cls) + rank
    t = jnp.arange(n_tiles_padded, dtype=jnp.int32)
    ok = t < tile_end[-1]
    tcls = jnp.searchsorted(tile_end, jnp.minimum(t, tile_end[-1] - 1), side="right").astype(jnp.int32)
    tile_a = jnp.take(jnp.asarray(_CLASS_A), tcls)
    tile_b = jnp.take(jnp.asarray(_CLASS_B), tcls)
    return pos, tile_a, tile_b, ok.astype(jnp.int32)


def _moe(xs, g, mods, layer, w_router_t, b_router_col, w1, w3, w2, g_final, dims, final_norm):
    hx, route, counts = _router_call(xs, g, mods, layer, w_router_t, b_router_col, dims)
    n_tiles_padded = dims["tiles"] + N_CLASSES
    pos, tile_a, tile_b, tile_ok = _routing_plan(route, counts, n_tiles_padded)
    xsorted = _dispatch_call(pos, hx, n_tiles_padded * TOKEN_TILE, dims)
    ysorted = _expert_call(tile_a, tile_b, tile_ok, xsorted, w1, w3, w2, dims["d"])
    return _combine_call(pos, ysorted, xs, mods, layer, g_final, dims, final_norm)


def _pos_table(n_tokens, dim):
    rows = n_tokens // GRID_WIDTH
    r, col = jnp.meshgrid(jnp.arange(rows, dtype=F32), jnp.arange(GRID_WIDTH, dtype=F32), indexing="ij")
    quarter = dim // 4
    freqs = jnp.exp(-math.log(POS_BASE) * jnp.arange(quarter, dtype=F32) / quarter)

    def enc(p):
        ang = p.reshape(-1, 1) * freqs
        return jnp.concatenate([jnp.sin(ang), jnp.cos(ang)], axis=-1)

    return jnp.concatenate([enc(r), enc(col)], axis=-1)


def _mlstm_weights(w_in, b_gate):
    d = w_in.shape[0]
    nq = HEADS * HEAD_QK
    nv = HEADS * HEAD_V
    wq = w_in[:, 0:nq].reshape(d, HEADS, HEAD_QK) * (HEAD_QK ** -0.5)
    wk = w_in[:, nq:2 * nq].reshape(d, HEADS, HEAD_QK)
    wqk = jnp.concatenate([wq, wk], axis=-1).reshape(d, HEADS * 2 * HEAD_QK)
    wv = w_in[:, 2 * nq:2 * nq + nv]
    wo = w_in[:, 2 * nq + nv:2 * nq + 2 * nv]
    wg = w_in[:, 2 * nq + 2 * nv:]
    wg_pad = jnp.pad(wg, ((0, 0), (0, LANES - wg.shape[1])))
    w_all = jnp.concatenate([wqk, wv, wo, wg_pad], axis=1).astype(BF16)
    bias = b_gate.reshape(-1).astype(F32)
    bias_c = jnp.pad(bias, (0, LANES - bias.shape[0])).reshape(1, LANES)
    bias_r = bias.reshape(-1, 1)
    return w_all, wg.T.astype(BF16), bias_c, bias_r


def kernel(x, c, ctx, c_ctx, w_ada, b_ada, g_mix, g_ffn, g_final, m_w_in, m_b_gate, m_g_head, m_w_out, r_w_in, r_conv_w, r_conv_b, r_w_gate, r_b_gate, r_lam, r_w_out, w_router, b_router, e_w1, e_w3, e_w2):
    batch, t_len, d = x.shape
    ctx_len = ctx.shape[1]
    depth = w_ada.shape[0]
    tm = TOKEN_TILE
    assert t_len % tm == 0 and ctx_len % tm == 0 and batch + 1 <= ADA_ROWS
    assert d == HEADS * HEAD_V and t_len % GRID_WIDTH == 0
    s_len = ctx_len + t_len
    dims = dict(b=batch, d=d, n=batch * s_len, tpb=s_len // tm, ct=ctx_len // tm,
                tiles=batch * s_len // tm)

    cvec = jnp.concatenate([c, c_ctx[None, :], jnp.zeros((ADA_ROWS - batch - 1, d), F32)], axis=0)
    mods = _ada_call(cvec, w_ada, b_ada).reshape(depth, ADA_ROWS, 1, 6 * d)
    xs = _stream_call(ctx.reshape(batch * ctx_len, d), x.reshape(batch * t_len, d),
                      _pos_table(t_len, d), dims)

    w_router_t = w_router.T.astype(F32)
    b_router_col = b_router.reshape(-1, 1).astype(F32)
    g_final2 = g_final.reshape(1, d)
    out = None
    for i in range(depth):
        j = i // 2
        g_mix_i = g_mix[i].reshape(1, d)
        if i % 2 == 0:
            w_all, w_gate_t, bias_c, bias_r = _mlstm_weights(m_w_in[j], m_b_gate[j])
            qk, v, og, gc, gr = _mproj_call(xs, g_mix_i, mods, i, w_all, w_gate_t, dims)
            hf, hb = _mlstm_call(qk, v, gc, gr, bias_c, bias_r, dims)
            xs = _readout_call(hf, hb, og, m_g_head[j].reshape(1, d), m_w_out[j].astype(BF16),
                               xs, mods, i, dims, True)
        else:
            gy, xr = _rproj_call(xs, g_mix_i, mods, i, r_w_in[j].astype(BF16), dims)
            hf, hb = _lru_call(xr, r_conv_w[j], r_conv_b[j].reshape(1, d), r_w_gate[j].astype(BF16),
                               r_b_gate[j].reshape(2, 1, -1), r_lam[j].reshape(2, 1, d), dims)
            xs = _readout_call(hf, hb, gy, jnp.ones((1, d), F32), r_w_out[j].astype(BF16),
                               xs, mods, i, dims, False)
        last = i == depth - 1
        res = _moe(xs, g_ffn[i].reshape(1, d), mods, i, w_router_t, b_router_col,
                   e_w1[i].astype(BF16), e_w3[i].astype(BF16), e_w2[i].astype(BF16),
                   g_final2, dims, last)
        if last:
            out = res
        else:
            xs = res
    return out.reshape(batch, t_len, d)
```

```python
import functools
import math

import jax
import jax.numpy as jnp
import numpy as np
from jax import lax
from jax.experimental import pallas as pl
from jax.experimental.pallas import tpu as pltpu

F32 = jnp.float32
BF16 = jnp.bfloat16
HIGHEST = lax.Precision.HIGHEST

TOKEN_TILE = 256
LANES = 128
SUBLANES = 8
VMEM_LIMIT_BYTES = 56 * 1024 * 1024
NORM_EPS = 1e-6
GRID_WIDTH = 64
POS_BASE = 10000.0
HEADS = 8
HEAD_QK = 64
HEAD_V = 128
LRU_BLOCKS = 8
LRU_C = 8.0
CONV_TAPS = 4
CONV_LEFT = CONV_TAPS // 2
N_EXPERTS = 16
N_GROUPS = 4
GROUP_SIZE = N_EXPERTS // N_GROUPS
PAIRS = [(a, b) for a in range(GROUP_SIZE) for b in range(a + 1, GROUP_SIZE)]
N_CLASSES = N_GROUPS * len(PAIRS)
CLASS_ROWS = 32
ROUTE_COLS = LANES
ADA_ROWS = 16


def _params(*sem):
    return pltpu.CompilerParams(dimension_semantics=sem, vmem_limit_bytes=VMEM_LIMIT_BYTES)


def _mod_row(i, tiles_per_batch, ctx_tiles, batch):
    return jnp.where(i % tiles_per_batch < ctx_tiles, batch, i // tiles_per_batch)


def _bwd_tile(i, tiles_per_batch, ctx_tiles):
    b = i // tiles_per_batch
    j = i % tiles_per_batch
    jb = jnp.where(j < ctx_tiles, ctx_tiles - 1 - j, tiles_per_batch - 1 - (j - ctx_tiles))
    return b * tiles_per_batch + jb


def _ada_kernel(c_ref, w_ref, b_ref, o_ref):
    c = c_ref[...]
    s = c * jax.nn.sigmoid(c)
    o_ref[...] = jnp.dot(s, w_ref[...], preferred_element_type=F32, precision=HIGHEST) + b_ref[...]


def _ada_call(cvec, w_ada, b_ada):
    depth, d, n = w_ada.shape
    tn = n // 4
    return pl.pallas_call(
        _ada_kernel,
        grid=(depth, n // tn),
        in_specs=[
            pl.BlockSpec((ADA_ROWS, d), lambda l, j: (0, 0)),
            pl.BlockSpec((None, d, tn), lambda l, j: (l, 0, j)),
            pl.BlockSpec((None, 1, tn), lambda l, j: (l, 0, j)),
        ],
        out_specs=pl.BlockSpec((None, ADA_ROWS, tn), lambda l, j: (l, 0, j)),
        out_shape=jax.ShapeDtypeStruct((depth, ADA_ROWS, n), F32),
        compiler_params=_params("arbitrary", "arbitrary"),
        name="ada",
    )(cvec, w_ada, b_ada.reshape(depth, 1, n))


def _stream_kernel(ctx_ref, x_ref, pos_ref, o_ref, *, tiles_per_batch, ctx_tiles):
    j = pl.program_id(0) % tiles_per_batch

    @pl.when(j < ctx_tiles)
    def _():
        o_ref[...] = ctx_ref[...]

    @pl.when(j >= ctx_tiles)
    def _():
        o_ref[...] = x_ref[...] + pos_ref[...]


def _stream_call(ctx2, x2, pos, dims):
    tm, d = TOKEN_TILE, dims["d"]
    tpb, ct = dims["tpb"], dims["ct"]
    lt = tpb - ct
    return pl.pallas_call(
        functools.partial(_stream_kernel, tiles_per_batch=tpb, ctx_tiles=ct),
        grid=(dims["tiles"],),
        in_specs=[
            pl.BlockSpec((tm, d), lambda i: ((i // tpb) * ct + jnp.minimum(i % tpb, ct - 1), 0)),
            pl.BlockSpec((tm, d), lambda i: ((i // tpb) * lt + jnp.maximum(i % tpb - ct, 0), 0)),
            pl.BlockSpec((tm, d), lambda i: (jnp.maximum(i % tpb - ct, 0), 0)),
        ],
        out_specs=pl.BlockSpec((tm, d), lambda i: (i, 0)),
        out_shape=jax.ShapeDtypeStruct((dims["n"], d), F32),
        compiler_params=_params("arbitrary"),
        name="stream",
    )(ctx2, x2, pos)


def _modulated(x, g, mod, shift_idx):
    d = x.shape[-1]
    ms = jnp.mean(x * x, axis=-1, keepdims=True)
    xn = x * lax.rsqrt(ms + NORM_EPS) * g
    shift = mod[:, shift_idx * d:(shift_idx + 1) * d]
    scale = mod[:, (shift_idx + 1) * d:(shift_idx + 2) * d]
    return xn * (1.0 + scale) + shift


def _mod_spec(layer, dims):
    tpb, ct, b = dims["tpb"], dims["ct"], dims["b"]
    return pl.BlockSpec((None, None, 1, 6 * dims["d"]),
                        lambda i, *_: (layer, _mod_row(i, tpb, ct, b), 0, 0))


def _gelu_tanh(y):
    return 0.5 * y * (1.0 + jnp.tanh(math.sqrt(2.0 / math.pi) * (y + 0.044715 * (y * y * y))))


def _mproj_kernel(x_ref, g_ref, mod_ref, w_ref, wgt_ref, qk_ref, v_ref, og_ref, gc_ref, gr_ref):
    d = x_ref.shape[-1]
    h = _modulated(x_ref[...], g_ref[...], mod_ref[...], 0).astype(BF16)
    qk_ref[...] = jnp.dot(h, w_ref[:, 0:d], preferred_element_type=F32).astype(BF16)
    v_ref[...] = jnp.dot(h, w_ref[:, d:2 * d], preferred_element_type=F32).astype(BF16)
    og_ref[...] = jax.nn.sigmoid(jnp.dot(h, w_ref[:, 2 * d:3 * d], preferred_element_type=F32))
    gc_ref[...] = jnp.dot(h, w_ref[:, 3 * d:3 * d + LANES], preferred_element_type=F32)
    gr_ref[...] = lax.dot_general(wgt_ref[...], h, (((1,), (1,)), ((), ())),
                                  preferred_element_type=F32)


def _mproj_call(xs, g, mods, layer, w_all, w_gate_t, dims):
    tm, d, n = TOKEN_TILE, dims["d"], dims["n"]
    row = lambda i: (i, 0)
    const = lambda i: (0, 0)
    return pl.pallas_call(
        _mproj_kernel,
        grid=(dims["tiles"],),
        in_specs=[
            pl.BlockSpec((tm, d), row),
            pl.BlockSpec((1, d), const),
            _mod_spec(layer, dims),
            pl.BlockSpec(w_all.shape, const),
            pl.BlockSpec(w_gate_t.shape, const),
        ],
        out_specs=[
            pl.BlockSpec((tm, d), row),
            pl.BlockSpec((tm, d), row),
            pl.BlockSpec((tm, d), row),
            pl.BlockSpec((tm, LANES), row),
            pl.BlockSpec((4 * HEADS, tm), lambda i: (0, i)),
        ],
        out_shape=[
            jax.ShapeDtypeStruct((n, d), BF16),
            jax.ShapeDtypeStruct((n, d), BF16),
            jax.ShapeDtypeStruct((n, d), F32),
            jax.ShapeDtypeStruct((n, LANES), F32),
            jax.ShapeDtypeStruct((4 * HEADS, n), F32),
        ],
        compiler_params=_params("arbitrary"),
        name="mlstm_proj",
    )(xs, g, mods, w_all, w_gate_t)


def _mlstm_direction(qk_ref, v_ref, gc_ref, gr_ref, bc_ref, br_ref, o_ref, c_ref, m_ref, *, backward):
    tm = qk_ref.shape[0]
    gate_i = 2 * HEADS if backward else 0
    gate_f = gate_i + HEADS
    rows = lax.broadcasted_iota(jnp.int32, (tm, tm), 0)
    cols = lax.broadcasted_iota(jnp.int32, (tm, tm), 1)
    lower = cols <= rows
    upper = cols >= rows
    mask = upper if backward else lower
    xc = gc_ref[...] + bc_ref[...]
    xr = gr_ref[...] + br_ref[...]
    lsc = jax.nn.log_sigmoid(xc)
    lsr = jax.nn.log_sigmoid(xr)
    cum_mat_c = (upper if backward else lower).astype(F32)
    cum_mat_r = (lower if backward else upper).astype(F32)
    cum_c = jnp.dot(cum_mat_c, lsc, preferred_element_type=F32, precision=HIGHEST)
    cum_r = jnp.dot(lsr, cum_mat_r, preferred_element_type=F32, precision=HIGHEST)
    last = 0 if backward else tm - 1
    ones = jnp.ones((tm, HEAD_V), BF16)
    for h in range(HEADS):
        q = qk_ref[:, h * LANES:h * LANES + HEAD_QK]
        k = qk_ref[:, h * LANES + HEAD_QK:(h + 1) * LANES]
        vaug = jnp.concatenate([v_ref[:, h * HEAD_V:(h + 1) * HEAD_V], ones], axis=1)
        bcol = cum_c[:, gate_f + h:gate_f + h + 1]
        brow = cum_r[gate_f + h:gate_f + h + 1, :]
        icol = xc[:, gate_i + h:gate_i + h + 1]
        irow = xr[gate_i + h:gate_i + h + 1, :]
        m = m_ref[h:h + 1, 0:1]
        logd = jnp.where(mask, bcol - brow + irow, -jnp.inf)
        log_inter = bcol + m
        m_pos = jnp.maximum(log_inter, jnp.max(logd, axis=-1, keepdims=True))
        w_intra = jnp.exp(logd - m_pos)
        w_inter = jnp.exp(log_inter - m_pos)
        s = lax.dot_general(q, k, (((1,), (1,)), ((), ())), preferred_element_type=F32)
        p = (s * w_intra).astype(BF16)
        c = c_ref[h]
        acc = jnp.dot(p, vaug, preferred_element_type=F32)
        acc = acc + w_inter * jnp.dot(q, c.astype(BF16), preferred_element_type=F32)
        num = acc[:, :HEAD_V]
        den = acc[:, HEAD_V:]
        o_ref[:, h * HEAD_V:(h + 1) * HEAD_V] = num / jnp.maximum(jnp.abs(den), jnp.exp(-m_pos))
        b_last = bcol[last:last + 1, :]
        m_new = m_pos[last:last + 1, :]
        w_s = jnp.exp(b_last - bcol + icol - m_new)
        decay = jnp.exp(b_last + m - m_new)
        kw = (k.astype(F32) * w_s).astype(BF16)
        c_ref[h] = decay * c + lax.dot_general(kw, vaug, (((0,), (0,)), ((), ())),
                                               preferred_element_type=F32)
        m_ref[h:h + 1, :] = jnp.broadcast_to(m_new, (1, LANES))


def _mlstm_kernel(qkf, vf, gcf, grf, qkb, vb, gcb, grb, bc_ref, br_ref, of_ref, ob_ref,
                  cf_ref, mf_ref, cb_ref, mb_ref, *, tiles_per_batch):
    @pl.when(pl.program_id(0) % tiles_per_batch == 0)
    def _():
        cf_ref[...] = jnp.zeros_like(cf_ref)
        cb_ref[...] = jnp.zeros_like(cb_ref)
        mf_ref[...] = jnp.zeros_like(mf_ref)
        mb_ref[...] = jnp.zeros_like(mb_ref)

    _mlstm_direction(qkf, vf, gcf, grf, bc_ref, br_ref, of_ref, cf_ref, mf_ref, backward=False)
    _mlstm_direction(qkb, vb, gcb, grb, bc_ref, br_ref, ob_ref, cb_ref, mb_ref, backward=True)


def _mlstm_call(qk, v, gc, gr, bias_c, bias_r, dims):
    tm, d, n = TOKEN_TILE, dims["d"], dims["n"]
    tpb, ct = dims["tpb"], dims["ct"]
    fwd = lambda i: (i, 0)
    bwd = lambda i: (_bwd_tile(i, tpb, ct), 0)
    fwd_t = lambda i: (0, i)
    bwd_t = lambda i: (0, _bwd_tile(i, tpb, ct))
    const = lambda i: (0, 0)
    state = [pltpu.VMEM((HEADS, HEAD_QK, 2 * HEAD_V), F32), pltpu.VMEM((HEADS, LANES), F32)]
    return pl.pallas_call(
        functools.partial(_mlstm_kernel, tiles_per_batch=tpb),
        grid=(dims["tiles"],),
        in_specs=[
            pl.BlockSpec((tm, d), fwd), pl.BlockSpec((tm, d), fwd),
            pl.BlockSpec((tm, LANES), fwd), pl.BlockSpec((4 * HEADS, tm), fwd_t),
            pl.BlockSpec((tm, d), bwd), pl.BlockSpec((tm, d), bwd),
            pl.BlockSpec((tm, LANES), bwd), pl.BlockSpec((4 * HEADS, tm), bwd_t),
            pl.BlockSpec((1, LANES), const), pl.BlockSpec((4 * HEADS, 1), const),
        ],
        out_specs=[pl.BlockSpec((tm, d), fwd), pl.BlockSpec((tm, d), bwd)],
        out_shape=[jax.ShapeDtypeStruct((n, d), F32), jax.ShapeDtypeStruct((n, d), F32)],
        scratch_shapes=state + state,
        compiler_params=_params("arbitrary"),
        name="mlstm_scan",
    )(qk, v, gc, gr, qk, v, gc, gr, bias_c, bias_r)


def _readout_kernel(hf_ref, hb_ref, aux_ref, gh_ref, w_ref, x_ref, mod_ref, o_ref, *, head_norm):
    d = x_ref.shape[-1]
    hs = hf_ref[...] + hb_ref[...]
    if head_norm:
        parts = []
        for h in range(HEADS):
            hh = hs[:, h * HEAD_V:(h + 1) * HEAD_V]
            parts.append(hh * lax.rsqrt(jnp.mean(hh * hh, axis=-1, keepdims=True) + NORM_EPS))
        t = jnp.concatenate(parts, axis=1) * gh_ref[...] * aux_ref[...]
    else:
        t = aux_ref[...] * hs
    out = jnp.dot(t.astype(BF16), w_ref[...], preferred_element_type=F32)
    o_ref[...] = x_ref[...] + mod_ref[:, 2 * d:3 * d] * out


def _readout_call(hf, hb, aux, g_head, w_out, xs, mods, layer, dims, head_norm):
    tm, d, n = TOKEN_TILE, dims["d"], dims["n"]
    row = lambda i: (i, 0)
    const = lambda i: (0, 0)
    return pl.pallas_call(
        functools.partial(_readout_kernel, head_norm=head_norm),
        grid=(dims["tiles"],),
        in_specs=[
            pl.BlockSpec((tm, d), row), pl.BlockSpec((tm, d), row), pl.BlockSpec((tm, d), row),
            pl.BlockSpec((1, d), const), pl.BlockSpec((d, d), const),
            pl.BlockSpec((tm, d), row), _mod_spec(layer, dims),
        ],
        out_specs=pl.BlockSpec((tm, d), row),
        out_shape=jax.ShapeDtypeStruct((n, d), F32),
        compiler_params=_params("arbitrary"),
        name="readout",
    )(hf, hb, aux, g_head, w_out, xs, mods)


def _rproj_kernel(x_ref, g_ref, mod_ref, w_ref, gy_ref, xr_ref):
    d = x_ref.shape[-1]
    h = _modulated(x_ref[...], g_ref[...], mod_ref[...], 0).astype(BF16)
    gy_ref[...] = _gelu_tanh(jnp.dot(h, w_ref[:, 0:d], preferred_element_type=F32))
    xr_ref[...] = jnp.dot(h, w_ref[:, d:2 * d], preferred_element_type=F32)


def _rproj_call(xs, g, mods, layer, w_in, dims):
    tm, d, n = TOKEN_TILE, dims["d"], dims["n"]
    row = lambda i: (i, 0)
    const = lambda i: (0, 0)
    return pl.pallas_call(
        _rproj_kernel,
        grid=(dims["tiles"],),
        in_specs=[pl.BlockSpec((tm, d), row), pl.BlockSpec((1, d), const), _mod_spec(layer, dims),
                  pl.BlockSpec(w_in.shape, const)],
        out_specs=[pl.BlockSpec((tm, d), row), pl.BlockSpec((tm, d), row)],
        out_shape=[jax.ShapeDtypeStruct((n, d), F32), jax.ShapeDtypeStruct((n, d), F32)],
        compiler_params=_params("arbitrary"),
        name="rglru_proj",
    )(xs, g, mods, w_in)


def _lru_direction(cur_ref, prev_ref, next_ref, prev_ok, next_ok, cw_ref, cb_ref, wg_ref, bg_ref,
                   lam_ref, ext_ref, a_ref, b_ref, *, direction):
    tm, d = cur_ref.shape
    bw = d // LRU_BLOCKS
    ext_ref[0:SUBLANES, :] = jnp.where(prev_ok, prev_ref[...], 0.0)
    ext_ref[SUBLANES:SUBLANES + tm, :] = cur_ref[...]
    ext_ref[SUBLANES + tm:2 * SUBLANES + tm, :] = jnp.where(next_ok, next_ref[...], 0.0)
    xc = cb_ref[...] + ext_ref[pl.ds(SUBLANES - CONV_LEFT, tm), :] * cw_ref[0:1, :]
    for j in range(1, CONV_TAPS):
        xc = xc + ext_ref[pl.ds(SUBLANES - CONV_LEFT + j, tm), :] * cw_ref[j:j + 1, :]
    z = -lam_ref[direction]
    softplus = jnp.maximum(z, 0.0) + jnp.log(1.0 + jnp.exp(-jnp.abs(z)))
    for nb in range(LRU_BLOCKS):
        sl = slice(nb * bw, (nb + 1) * bw)
        xb = xc[:, sl]
        gg = jnp.dot(xb.astype(BF16), wg_ref[direction, nb], preferred_element_type=F32)
        gg = gg + bg_ref[direction, :, nb * 2 * bw:(nb + 1) * 2 * bw]
        r = jax.nn.sigmoid(gg[:, :bw])
        gi = jax.nn.sigmoid(gg[:, bw:])
        log_a = -LRU_C * r * softplus[:, sl]
        a = jnp.exp(log_a)
        a_ref[:, sl] = a
        th = jnp.tanh(log_a)
        b_ref[:, sl] = jnp.sqrt(-2.0 * th / (1.0 - th)) * (gi * xb)


def _lru_kernel(cur_f, prev_f, next_f, cur_b, prev_b, next_b, cw_ref, cb_ref, wg_ref, bg_ref, lam_ref,
                of_ref, ob_ref, ext_ref, af_ref, bf_ref, ab_ref, bb_ref, h_ref,
                *, tiles_per_batch, ctx_tiles):
    tm = cur_f.shape[0]
    j = pl.program_id(0) % tiles_per_batch

    @pl.when(j == 0)
    def _():
        h_ref[...] = jnp.zeros_like(h_ref)

    def seg_edges(jj):
        first = jnp.logical_or(jj == 0, jj == ctx_tiles)
        last = jnp.logical_or(jj == ctx_tiles - 1, jj == tiles_per_batch - 1)
        return jnp.logical_not(first), jnp.logical_not(last)

    jb = _bwd_tile(pl.program_id(0), tiles_per_batch, ctx_tiles) % tiles_per_batch
    pf, nf = seg_edges(j)
    pb, nb = seg_edges(jb)
    common = (cw_ref, cb_ref, wg_ref, bg_ref, lam_ref, ext_ref)
    _lru_direction(cur_f, prev_f, next_f, pf, nf, *common, af_ref, bf_ref, direction=0)
    _lru_direction(cur_b, prev_b, next_b, pb, nb, *common, ab_ref, bb_ref, direction=1)

    def step(t, carry):
        hf, hb = carry
        tb = tm - 1 - t
        hf = af_ref[pl.ds(t, 1), :] * hf + bf_ref[pl.ds(t, 1), :]
        hb = ab_ref[pl.ds(tb, 1), :] * hb + bb_ref[pl.ds(tb, 1), :]
        of_ref[pl.ds(t, 1), :] = hf
        ob_ref[pl.ds(tb, 1), :] = hb
        return hf, hb

    hf, hb = lax.fori_loop(0, tm, step, (h_ref[0:1, :], h_ref[1:2, :]), unroll=8)
    h_ref[0:1, :] = hf
    h_ref[1:2, :] = hb


def _lru_call(xr, conv_w, conv_b, w_gate, b_gate, lam, dims):
    tm, d, n = TOKEN_TILE, dims["d"], dims["n"]
    tpb, ct = dims["tpb"], dims["ct"]
    per = tm // SUBLANES
    last8 = n // SUBLANES - 1
    fwd = lambda i: (i, 0)
    bwd = lambda i: (_bwd_tile(i, tpb, ct), 0)
    prev_of = lambda t: (jnp.maximum(t * per - 1, 0), 0)
    next_of = lambda t: (jnp.minimum((t + 1) * per, last8), 0)
    c2 = lambda i: (0, 0)
    c3 = lambda i: (0, 0, 0)
    c4 = lambda i: (0, 0, 0, 0)
    return pl.pallas_call(
        functools.partial(_lru_kernel, tiles_per_batch=tpb, ctx_tiles=ct),
        grid=(dims["tiles"],),
        in_specs=[
            pl.BlockSpec((tm, d), fwd),
            pl.BlockSpec((SUBLANES, d), lambda i: prev_of(i)),
            pl.BlockSpec((SUBLANES, d), lambda i: next_of(i)),
            pl.BlockSpec((tm, d), bwd),
            pl.BlockSpec((SUBLANES, d), lambda i: prev_of(_bwd_tile(i, tpb, ct))),
            pl.BlockSpec((SUBLANES, d), lambda i: next_of(_bwd_tile(i, tpb, ct))),
            pl.BlockSpec(conv_w.shape, c2), pl.BlockSpec(conv_b.shape, c2),
            pl.BlockSpec(w_gate.shape, c4), pl.BlockSpec(b_gate.shape, c3), pl.BlockSpec(lam.shape, c3),
        ],
        out_specs=[pl.BlockSpec((tm, d), fwd), pl.BlockSpec((tm, d), bwd)],
        out_shape=[jax.ShapeDtypeStruct((n, d), F32), jax.ShapeDtypeStruct((n, d), F32)],
        scratch_shapes=[pltpu.VMEM((tm + 2 * SUBLANES, d), F32)] + [pltpu.VMEM((tm, d), F32)] * 4
        + [pltpu.VMEM((SUBLANES, d), F32)],
        compiler_params=_params("arbitrary"),
        name="rglru_scan",
    )(xr, xr, xr, xr, xr, xr, conv_w, conv_b, w_gate, b_gate, lam)


def _first_argmax(vals):
    best = vals[0]
    idx = jnp.zeros_like(best)
    for k in range(1, len(vals)):
        better = vals[k] > best
        idx = jnp.where(better, float(k), idx)
        best = jnp.where(better, vals[k], best)
    return idx, best


def _pick(idx, vals):
    out = vals[0]
    for k in range(1, len(vals)):
        out = jnp.where(idx == float(k), vals[k], out)
    return out


def _router_kernel(x_ref, g_ref, mod_ref, wr_ref, br_ref, hx_ref, route_ref, cnt_ref, carry_ref):
    tm, d = x_ref.shape

    @pl.when(pl.program_id(0) == 0)
    def _():
        carry_ref[...] = jnp.zeros_like(carry_ref)

    h = _modulated(x_ref[...], g_ref[...], mod_ref[...], 3)
    hx_ref[:, 0:d] = h
    logits = lax.dot_general(wr_ref[...], h, (((1,), (1,)), ((), ())),
                             preferred_element_type=F32, precision=HIGHEST)
    e = jnp.exp(logits - jnp.max(logits, axis=0, keepdims=True))
    scores = e / jnp.sum(e, axis=0, keepdims=True)
    sel = scores + br_ref[...]
    sel_rows = [sel[k:k + 1, :] for k in range(N_EXPERTS)]
    score_rows = [scores[k:k + 1, :] for k in range(N_EXPERTS)]
    group_scores = []
    for gi in range(N_GROUPS):
        v = sel_rows[gi * GROUP_SIZE:(gi + 1) * GROUP_SIZE]
        best = v[0] + v[1]
        for a, b in PAIRS[1:]:
            best = jnp.maximum(best, v[a] + v[b])
        group_scores.append(best)
    grp, _ = _first_argmax(group_scores)
    in_sel = [_pick(grp, [sel_rows[gi * GROUP_SIZE + k] for gi in range(N_GROUPS)])
              for k in range(GROUP_SIZE)]
    in_score = [_pick(grp, [score_rows[gi * GROUP_SIZE + k] for gi in range(N_GROUPS)])
                for k in range(GROUP_SIZE)]
    i1, _ = _first_argmax(in_sel)
    rest = [jnp.where(i1 == float(k), -jnp.inf, in_sel[k]) for k in range(GROUP_SIZE)]
    i2, _ = _first_argmax(rest)
    lo = jnp.minimum(i1, i2)
    hi = jnp.maximum(i1, i2)
    s_lo = _pick(lo, in_score)
    s_hi = _pick(hi, in_score)
    tot = s_lo + s_hi
    w_lo = s_lo / tot
    w_hi = s_hi / tot
    pair = jnp.where(lo == 0.0, hi - 1.0, jnp.where(lo == 1.0, hi + 1.0, 5.0))
    cls = grp * float(len(PAIRS)) + pair
    crow = lax.broadcasted_iota(jnp.int32, (CLASS_ROWS, tm), 0).astype(F32)
    onehot = jnp.where(crow == cls, 1.0, 0.0)
    rows = lax.broadcasted_iota(jnp.int32, (tm, tm), 0)
    cols = lax.broadcasted_iota(jnp.int32, (tm, tm), 1)
    upper = jnp.where(rows <= cols, 1.0, 0.0).astype(BF16)
    cum = jnp.dot(onehot.astype(BF16), upper, preferred_element_type=F32)
    carry = carry_ref[:, 0:1]
    rank = jnp.sum(onehot * (cum - 1.0 + carry), axis=0, keepdims=True)
    new_carry = carry + jnp.sum(onehot, axis=1, keepdims=True)
    carry_ref[...] = jnp.broadcast_to(new_carry, carry_ref.shape)
    cnt_ref[...] = jnp.broadcast_to(new_carry, cnt_ref.shape)
    zero = jnp.zeros_like(cls)
    route_ref[...] = jnp.concatenate([cls, rank, w_lo, w_hi, zero, zero, zero, zero], axis=0)
    eye = rows == cols
    wl_col = jnp.sum(jnp.where(eye, w_lo, 0.0), axis=1, keepdims=True)
    wh_col = jnp.sum(jnp.where(eye, w_hi, 0.0), axis=1, keepdims=True)
    lane = lax.broadcasted_iota(jnp.int32, (tm, ROUTE_COLS), 1)
    hx_ref[:, d:d + ROUTE_COLS] = jnp.where(lane == 0, wl_col, jnp.where(lane == 1, wh_col, 0.0))


def _router_call(xs, g, mods, layer, w_router_t, b_router_col, dims):
    tm, d, n = TOKEN_TILE, dims["d"], dims["n"]
    row = lambda i: (i, 0)
    const = lambda i: (0, 0)
    return pl.pallas_call(
        _router_kernel,
        grid=(dims["tiles"],),
        in_specs=[pl.BlockSpec((tm, d), row), pl.BlockSpec((1, d), const), _mod_spec(layer, dims),
                  pl.BlockSpec(w_router_t.shape, const), pl.BlockSpec(b_router_col.shape, const)],
        out_specs=[pl.BlockSpec((tm, d + ROUTE_COLS), row),
                   pl.BlockSpec((SUBLANES, tm), lambda i: (0, i)),
                   pl.BlockSpec((CLASS_ROWS, LANES), const)],
        out_shape=[jax.ShapeDtypeStruct((n, d + ROUTE_COLS), F32),
                   jax.ShapeDtypeStruct((SUBLANES, n), F32),
                   jax.ShapeDtypeStruct((CLASS_ROWS, LANES), F32)],
        scratch_shapes=[pltpu.VMEM((CLASS_ROWS, LANES), F32)],
        compiler_params=_params("arbitrary"),
        name="router",
    )(xs, g, mods, w_router_t, b_router_col)


ROW_DMA_UNROLL = 8


def _dispatch_kernel(pos_ref, hx_ref, init_ref, o_ref, stage_ref, sems):
    del init_ref
    tm = hx_ref.shape[0]
    i = pl.program_id(0)
    slot = i % 2
    base = i * tm
    stage_ref[slot] = hx_ref[...]

    def issue(r, carry):
        pltpu.make_async_copy(stage_ref.at[slot, pl.ds(r, 1)], o_ref.at[pl.ds(pos_ref[base + r], 1)],
                              sems.at[slot]).start()
        return carry

    lax.fori_loop(0, tm, issue, 0, unroll=ROW_DMA_UNROLL)

    def wait_tile(s):
        pltpu.make_async_copy(stage_ref.at[s], o_ref.at[pl.ds(0, tm)], sems.at[s]).wait()

    @pl.when(i > 0)
    def _():
        wait_tile(1 - slot)

    @pl.when(i == pl.num_programs(0) - 1)
    def _():
        wait_tile(slot)


def _dispatch_call(pos, hx, padded_rows, dims):
    tm = TOKEN_TILE
    width = hx.shape[1]
    init = jnp.zeros((padded_rows, width), F32)
    return pl.pallas_call(
        _dispatch_kernel,
        grid_spec=pltpu.PrefetchScalarGridSpec(
            num_scalar_prefetch=1,
            grid=(dims["tiles"],),
            in_specs=[pl.BlockSpec((tm, width), lambda i, p: (i, 0)), pl.BlockSpec(memory_space=pl.ANY)],
            out_specs=pl.BlockSpec(memory_space=pl.ANY),
            scratch_shapes=[pltpu.VMEM((2, tm, width), F32), pltpu.SemaphoreType.DMA((2,))],
        ),
        out_shape=jax.ShapeDtypeStruct((padded_rows, width), F32),
        input_output_aliases={2: 0},
        compiler_params=_params("arbitrary"),
        name="dispatch",
    )(pos, hx, init)


def _expert_kernel(ea_ref, eb_ref, ok_ref, x_ref, w1a, w3a, w2a, w1b, w3b, w2b, o_ref):
    del ea_ref, eb_ref
    d = o_ref.shape[-1]
    t = pl.program_id(0)

    @pl.when(ok_ref[t] != 0)
    def _():
        x = x_ref[:, 0:d].astype(BF16)

        def expert(w1, w3, w2):
            a = jnp.dot(x, w1[...], preferred_element_type=F32)
            b = jnp.dot(x, w3[...], preferred_element_type=F32)
            u = (a * jax.nn.sigmoid(a)) * b
            return jnp.dot(u.astype(BF16), w2[...], preferred_element_type=F32)

        o_ref[...] = (x_ref[:, d:d + 1] * expert(w1a, w3a, w2a)
                      + x_ref[:, d + 1:d + 2] * expert(w1b, w3b, w2b))

    @pl.when(ok_ref[t] == 0)
    def _():
        o_ref[...] = jnp.zeros_like(o_ref)


def _expert_call(tile_a, tile_b, tile_ok, xsorted, w1, w3, w2, d):
    te = TOKEN_TILE
    rows, width = xsorted.shape
    de = w1.shape[-1]
    sel_a = lambda t, ea, eb, ok: (ea[t], 0, 0)
    sel_b = lambda t, ea, eb, ok: (eb[t], 0, 0)
    return pl.pallas_call(
        _expert_kernel,
        grid_spec=pltpu.PrefetchScalarGridSpec(
            num_scalar_prefetch=3,
            grid=(rows // te,),
            in_specs=[
                pl.BlockSpec((te, width), lambda t, ea, eb, ok: (t, 0)),
                pl.BlockSpec((None, d, de), sel_a), pl.BlockSpec((None, d, de), sel_a),
                pl.BlockSpec((None, de, d), sel_a),
                pl.BlockSpec((None, d, de), sel_b), pl.BlockSpec((None, d, de), sel_b),
                pl.BlockSpec((None, de, d), sel_b),
            ],
            out_specs=pl.BlockSpec((te, d), lambda t, ea, eb, ok: (t, 0)),
        ),
        out_shape=jax.ShapeDtypeStruct((rows, d), F32),
        compiler_params=_params("arbitrary"),
        name="experts",
    )(tile_a, tile_b, tile_ok, xsorted, w1, w3, w2, w1, w3, w2)


def _combine_kernel(pos_ref, y_ref, x_ref, mod_ref, gf_ref, o_ref, buf_ref, sems,
                    *, tile_of, final_norm):
    tm, d = x_ref.shape
    i = pl.program_id(0)
    slot = i % 2

    def gather(step, s):
        base = tile_of(step) * tm

        def issue(r, carry):
            pltpu.make_async_copy(y_ref.at[pl.ds(pos_ref[base + r], 1)], buf_ref.at[s, pl.ds(r, 1)],
                                  sems.at[s]).start()
            return carry

        lax.fori_loop(0, tm, issue, 0, unroll=ROW_DMA_UNROLL)

    @pl.when(i == 0)
    def _():
        gather(0, 0)

    @pl.when(i + 1 < pl.num_programs(0))
    def _():
        gather(i + 1, 1 - slot)

    pltpu.make_async_copy(y_ref.at[pl.ds(0, tm)], buf_ref.at[slot], sems.at[slot]).wait()
    out = x_ref[...] + mod_ref[:, 5 * d:6 * d] * buf_ref[slot]
    if final_norm:
        ms = jnp.mean(out * out, axis=-1, keepdims=True)
        out = out * lax.rsqrt(ms + NORM_EPS) * gf_ref[...]
    o_ref[...] = out


def _combine_call(pos, ysorted, xs, mods, layer, g_final, dims, final_norm):
    tm, d = TOKEN_TILE, dims["d"]
    tpb, ct, b = dims["tpb"], dims["ct"], dims["b"]
    if final_norm:
        lt = tpb - ct
        tile_of = lambda i: (i // lt) * tpb + ct + i % lt
        n_tiles = b * lt
    else:
        tile_of = lambda i: i
        n_tiles = dims["tiles"]
    mod_spec = pl.BlockSpec((None, None, 1, 6 * d),
                            lambda i, p: (layer, _mod_row(tile_of(i), tpb, ct, b), 0, 0))
    return pl.pallas_call(
        functools.partial(_combine_kernel, tile_of=tile_of, final_norm=final_norm),
        grid_spec=pltpu.PrefetchScalarGridSpec(
            num_scalar_prefetch=1,
            grid=(n_tiles,),
            in_specs=[
                pl.BlockSpec(memory_space=pl.ANY),
                pl.BlockSpec((tm, d), lambda i, p: (tile_of(i), 0)),
                mod_spec,
                pl.BlockSpec((1, d), lambda i, p: (0, 0)),
            ],
            out_specs=pl.BlockSpec((tm, d), lambda i, p: (i, 0)),
            scratch_shapes=[pltpu.VMEM((2, tm, d), F32), pltpu.SemaphoreType.DMA((2,))],
        ),
        out_shape=jax.ShapeDtypeStruct((n_tiles * tm, d), F32),
        compiler_params=_params("arbitrary"),
        name="combine",
    )(pos, ysorted, xs, mods, g_final)


_CLASS_A = np.array([g * GROUP_SIZE + a for g in range(N_GROUPS) for a, _ in PAIRS], np.int32)
_CLASS_B = np.array([g * GROUP_SIZE + b for g in range(N_GROUPS) for _, b in PAIRS], np.int32)


def _routing_plan(route, counts, n_tiles_padded):
    te = TOKEN_TILE
    cls = route[0].astype(jnp.int32)
    rank = route[1].astype(jnp.int32)
    cnt = counts[:N_CLASSES, 0].astype(jnp.int32)
    tiles = (cnt + te - 1) // te
    tile_end = jnp.cumsum(tiles)
    tile_start = tile_end - tiles
    pos = jnp.take(tile_start * te, cls) + rank
    t = jnp.arange(n_tiles_padded, dtype=jnp.int32)
    ok = t < tile_end[-1]
    tcls = jnp.sum((tile_end[None, :] <= jnp.minimum(t, tile_end[-1] - 1)[:, None]).astype(jnp.int32), axis=1)
    tile_a = jnp.take(jnp.asarray(_CLASS_A), tcls)
    tile_b = jnp.take(jnp.asarray(_CLASS_B), tcls)
    return pos, tile_a, tile_b, ok.astype(jnp.int32)


def _moe(xs, g, mods, layer, w_router_t, b_router_col, w1, w3, w2, g_final, dims, final_norm):
    hx, route, counts = _router_call(xs, g, mods, layer, w_router_t, b_router_col, dims)
    n_tiles_padded = dims["tiles"] + N_CLASSES
    pos, tile_a, tile_b, tile_ok = _routing_plan(route, counts, n_tiles_padded)
    xsorted = _dispatch_call(pos, hx, n_tiles_padded * TOKEN_TILE, dims)
    ysorted = _expert_call(tile_a, tile_b, tile_ok, xsorted, w1, w3, w2, dims["d"])
    return _combine_call(pos, ysorted, xs, mods, layer, g_final, dims, final_norm)


def _pos_table(n_tokens, dim):
    rows = n_tokens // GRID_WIDTH
    r, col = jnp.meshgrid(jnp.arange(rows, dtype=F32), jnp.arange(GRID_WIDTH, dtype=F32), indexing="ij")
    quarter = dim // 4
    freqs = jnp.exp(-math.log(POS_BASE) * jnp.arange(quarter, dtype=F32) / quarter)

    def enc(p):
        ang = p.reshape(-1, 1) * freqs
        return jnp.concatenate([jnp.sin(ang), jnp.cos(ang)], axis=-1)

    return jnp.concatenate([enc(r), enc(col)], axis=-1)


def _mlstm_weights(w_in, b_gate):
    d = w_in.shape[0]
    nq = HEADS * HEAD_QK
    nv = HEADS * HEAD_V
    wq = w_in[:, 0:nq].reshape(d, HEADS, HEAD_QK) * (HEAD_QK ** -0.5)
    wk = w_in[:, nq:2 * nq].reshape(d, HEADS, HEAD_QK)
    wqk = jnp.concatenate([wq, wk], axis=-1).reshape(d, HEADS * 2 * HEAD_QK)
    wv = w_in[:, 2 * nq:2 * nq + nv]
    wo = w_in[:, 2 * nq + nv:2 * nq + 2 * nv]
    wg = w_in[:, 2 * nq + 2 * nv:]
    wg_pad = jnp.pad(wg, ((0, 0), (0, LANES - wg.shape[1])))
    w_all = jnp.concatenate([wqk, wv, wo, wg_pad], axis=1).astype(BF16)
    bias = b_gate.reshape(-1).astype(F32)
    bias_c = jnp.pad(bias, (0, LANES - bias.shape[0])).reshape(1, LANES)
    bias_r = bias.reshape(-1, 1)
    return w_all, wg.T.astype(BF16), bias_c, bias_r


def kernel(x, c, ctx, c_ctx, w_ada, b_ada, g_mix, g_ffn, g_final, m_w_in, m_b_gate, m_g_head, m_w_out, r_w_in, r_conv_w, r_conv_b, r_w_gate, r_b_gate, r_lam, r_w_out, w_router, b_router, e_w1, e_w3, e_w2):
    batch, t_len, d = x.shape
    ctx_len = ctx.shape[1]
    depth = w_ada.shape[0]
    tm = TOKEN_TILE
    assert t_len % tm == 0 and ctx_len % tm == 0 and batch + 1 <= ADA_ROWS
    assert d == HEADS * HEAD_V and t_len % GRID_WIDTH == 0
    s_len = ctx_len + t_len
    dims = dict(b=batch, d=d, n=batch * s_len, tpb=s_len // tm, ct=ctx_len // tm,
                tiles=batch * s_len // tm)

    cvec = jnp.concatenate([c, c_ctx[None, :], jnp.zeros((ADA_ROWS - batch - 1, d), F32)], axis=0)
    mods = _ada_call(cvec, w_ada, b_ada).reshape(depth, ADA_ROWS, 1, 6 * d)
    xs = _stream_call(ctx.reshape(batch * ctx_len, d), x.reshape(batch * t_len, d),
                      _pos_table(t_len, d), dims)

    w_router_t = w_router.T.astype(F32)
    b_router_col = b_router.reshape(-1, 1).astype(F32)
    g_final2 = g_final.reshape(1, d)
    out = None
    for i in range(depth):
        j = i // 2
        g_mix_i = g_mix[i].reshape(1, d)
        if i % 2 == 0:
            w_all, w_gate_t, bias_c, bias_r = _mlstm_weights(m_w_in[j], m_b_gate[j])
            qk, v, og, gc, gr = _mproj_call(xs, g_mix_i, mods, i, w_all, w_gate_t, dims)
            hf, hb = _mlstm_call(qk, v, gc, gr, bias_c, bias_r, dims)
            xs = _readout_call(hf, hb, og, m_g_head[j].reshape(1, d), m_w_out[j].astype(BF16),
                               xs, mods, i, dims, True)
        else:
            gy, xr = _rproj_call(xs, g_mix_i, mods, i, r_w_in[j].astype(BF16), dims)
            hf, hb = _lru_call(xr, r_conv_w[j], r_conv_b[j].reshape(1, d), r_w_gate[j].astype(BF16),
                               r_b_gate[j].reshape(2, 1, -1), r_lam[j].reshape(2, 1, d), dims)
            xs = _readout_call(hf, hb, gy, jnp.ones((1, d), F32), r_w_out[j].astype(BF16),
                               xs, mods, i, dims, False)
        last = i == depth - 1
        res = _moe(xs, g_ffn[i].reshape(1, d), mods, i, w_router_t, b_router_col,
                   e_w1[i].astype(BF16), e_w3[i].astype(BF16), e_w2[i].astype(BF16),
                   g_final2, dims, last)
        if last:
            out = res
        else:
            xs = res
    return out.reshape(batch, t_len, d)
```

```python
import functools
import math

import jax
import jax.numpy as jnp
import numpy as np
from jax import lax
from jax.experimental import pallas as pl
from jax.experimental.pallas import tpu as pltpu

F32 = jnp.float32
BF16 = jnp.bfloat16
HIGHEST = lax.Precision.HIGHEST

TOKEN_TILE = 256
LANES = 128
SUBLANES = 8
VMEM_LIMIT_BYTES = 56 * 1024 * 1024
NORM_EPS = 1e-6
GRID_WIDTH = 64
POS_BASE = 10000.0
HEADS = 8
HEAD_QK = 64
HEAD_V = 128
LRU_BLOCKS = 8
LRU_C = 8.0
CONV_TAPS = 4
CONV_LEFT = CONV_TAPS // 2
N_EXPERTS = 16
N_GROUPS = 4
GROUP_SIZE = N_EXPERTS // N_GROUPS
PAIRS = [(a, b) for a in range(GROUP_SIZE) for b in range(a + 1, GROUP_SIZE)]
N_CLASSES = N_GROUPS * len(PAIRS)
CLASS_ROWS = 32
ROUTE_COLS = LANES
ADA_ROWS = 16


def _params(*sem):
    return pltpu.CompilerParams(dimension_semantics=sem, vmem_limit_bytes=VMEM_LIMIT_BYTES)


def _mod_row(i, tiles_per_batch, ctx_tiles, batch):
    return jnp.where(i % tiles_per_batch < ctx_tiles, batch, i // tiles_per_batch)


def _bwd_tile(i, tiles_per_batch, ctx_tiles):
    b = i // tiles_per_batch
    j = i % tiles_per_batch
    jb = jnp.where(j < ctx_tiles, ctx_tiles - 1 - j, tiles_per_batch - 1 - (j - ctx_tiles))
    return b * tiles_per_batch + jb


def _ada_kernel(c_ref, w_ref, b_ref, o_ref):
    c = c_ref[...]
    s = c * jax.nn.sigmoid(c)
    o_ref[...] = jnp.dot(s, w_ref[...], preferred_element_type=F32, precision=HIGHEST) + b_ref[...]


def _ada_call(cvec, w_ada, b_ada):
    depth, d, n = w_ada.shape
    tn = n // 4
    return pl.pallas_call(
        _ada_kernel,
        grid=(depth, n // tn),
        in_specs=[
            pl.BlockSpec((ADA_ROWS, d), lambda l, j: (0, 0)),
            pl.BlockSpec((None, d, tn), lambda l, j: (l, 0, j)),
            pl.BlockSpec((None, 1, tn), lambda l, j: (l, 0, j)),
        ],
        out_specs=pl.BlockSpec((None, ADA_ROWS, tn), lambda l, j: (l, 0, j)),
        out_shape=jax.ShapeDtypeStruct((depth, ADA_ROWS, n), F32),
        compiler_params=_params("arbitrary", "arbitrary"),
        name="ada",
    )(cvec, w_ada, b_ada.reshape(depth, 1, n))


def _stream_kernel(ctx_ref, x_ref, pos_ref, o_ref, *, tiles_per_batch, ctx_tiles):
    j = pl.program_id(0) % tiles_per_batch

    @pl.when(j < ctx_tiles)
    def _():
        o_ref[...] = ctx_ref[...]

    @pl.when(j >= ctx_tiles)
    def _():
        o_ref[...] = x_ref[...] + pos_ref[...]


def _stream_call(ctx2, x2, pos, dims):
    tm, d = TOKEN_TILE, dims["d"]
    tpb, ct = dims["tpb"], dims["ct"]
    lt = tpb - ct
    return pl.pallas_call(
        functools.partial(_stream_kernel, tiles_per_batch=tpb, ctx_tiles=ct),
        grid=(dims["tiles"],),
        in_specs=[
            pl.BlockSpec((tm, d), lambda i: ((i // tpb) * ct + jnp.minimum(i % tpb, ct - 1), 0)),
            pl.BlockSpec((tm, d), lambda i: ((i // tpb) * lt + jnp.maximum(i % tpb - ct, 0), 0)),
            pl.BlockSpec((tm, d), lambda i: (jnp.maximum(i % tpb - ct, 0), 0)),
        ],
        out_specs=pl.BlockSpec((tm, d), lambda i: (i, 0)),
        out_shape=jax.ShapeDtypeStruct((dims["n"], d), F32),
        compiler_params=_params("arbitrary"),
        name="stream",
    )(ctx2, x2, pos)


def _modulated(x, g, mod, shift_idx):
    d = x.shape[-1]
    ms = jnp.mean(x * x, axis=-1, keepdims=True)
    xn = x * lax.rsqrt(ms + NORM_EPS) * g
    shift = mod[:, shift_idx * d:(shift_idx + 1) * d]
    scale = mod[:, (shift_idx + 1) * d:(shift_idx + 2) * d]
    return xn * (1.0 + scale) + shift


def _mod_spec(layer, dims):
    tpb, ct, b = dims["tpb"], dims["ct"], dims["b"]
    return pl.BlockSpec((None, None, 1, 6 * dims["d"]),
                        lambda i, *_: (layer, _mod_row(i, tpb, ct, b), 0, 0))


def _gelu_tanh(y):
    return 0.5 * y * (1.0 + jnp.tanh(math.sqrt(2.0 / math.pi) * (y + 0.044715 * (y * y * y))))


_NT = (((1,), (1,)), ((), ()))
_TN = (((0,), (0,)), ((), ()))


def _mproj_kernel(x_ref, g_ref, mod_ref, w_ref, wt_ref, q_ref, k_ref, vt_ref, ot_ref, gc_ref, gr_ref):
    d = x_ref.shape[-1]
    nq = q_ref.shape[-1]
    h = _modulated(x_ref[...], g_ref[...], mod_ref[...], 0).astype(BF16)
    q_ref[...] = jnp.dot(h, w_ref[:, 0:nq], preferred_element_type=F32).astype(BF16)
    k_ref[...] = jnp.dot(h, w_ref[:, nq:2 * nq], preferred_element_type=F32).astype(BF16)
    gc_ref[...] = jnp.dot(h, w_ref[:, 2 * nq:2 * nq + LANES], preferred_element_type=F32)
    vt_ref[...] = lax.dot_general(wt_ref[0:d, :], h, _NT, preferred_element_type=F32).astype(BF16)
    ot = lax.dot_general(wt_ref[d:2 * d, :], h, _NT, preferred_element_type=F32)
    ot_ref[...] = jax.nn.sigmoid(ot).astype(BF16)
    gr_ref[...] = lax.dot_general(wt_ref[2 * d:2 * d + 4 * HEADS, :], h, _NT, preferred_element_type=F32)


def _mproj_call(xs, g, mods, layer, w_tok, w_feat, dims):
    tm, d, n = TOKEN_TILE, dims["d"], dims["n"]
    nq = HEADS * HEAD_QK
    row = lambda i: (i, 0)
    col = lambda i: (0, i)
    const = lambda i: (0, 0)
    return pl.pallas_call(
        _mproj_kernel,
        grid=(dims["tiles"],),
        in_specs=[
            pl.BlockSpec((tm, d), row),
            pl.BlockSpec((1, d), const),
            _mod_spec(layer, dims),
            pl.BlockSpec(w_tok.shape, const),
            pl.BlockSpec(w_feat.shape, const),
        ],
        out_specs=[
            pl.BlockSpec((tm, nq), row),
            pl.BlockSpec((tm, nq), row),
            pl.BlockSpec((d, tm), col),
            pl.BlockSpec((d, tm), col),
            pl.BlockSpec((tm, LANES), row),
            pl.BlockSpec((4 * HEADS, tm), col),
        ],
        out_shape=[
            jax.ShapeDtypeStruct((n, nq), BF16),
            jax.ShapeDtypeStruct((n, nq), BF16),
            jax.ShapeDtypeStruct((d, n), BF16),
            jax.ShapeDtypeStruct((d, n), BF16),
            jax.ShapeDtypeStruct((n, LANES), F32),
            jax.ShapeDtypeStruct((4 * HEADS, n), F32),
        ],
        compiler_params=_params("arbitrary"),
        name="mlstm_proj",
    )(xs, g, mods, w_tok, w_feat)


def _mlstm_direction(q_ref, k_ref, vt_ref, gc_ref, gr_ref, bc_ref, br_ref, o_ref, ct_ref, m_ref,
                     *, backward):
    tm = q_ref.shape[0]
    gate_i = 2 * HEADS if backward else 0
    gate_f = gate_i + HEADS
    last = 0 if backward else tm - 1
    src = lax.broadcasted_iota(jnp.int32, (tm, tm), 0)
    tgt = lax.broadcasted_iota(jnp.int32, (tm, tm), 1)
    visible = (src >= tgt) if backward else (src <= tgt)
    neg_mask = jnp.where(visible, 0.0, -jnp.inf)
    xc = gc_ref[...] + bc_ref[...]
    xr = gr_ref[...] + br_ref[...]
    before = (src <= tgt) if backward else (src >= tgt)
    cum_c = _split_dot(jnp.where(before, 1.0, 0.0).astype(BF16), jax.nn.log_sigmoid(xc), left=True)
    cum_r = _split_dot(jnp.where(visible, 1.0, 0.0).astype(BF16), jax.nn.log_sigmoid(xr), left=False)
    b_row = cum_r[gate_f:gate_f + HEADS, :]
    c_row = xr[gate_i:gate_i + HEADS, :] - b_row
    c_col = xc - pltpu.roll(cum_c, LANES - HEADS, axis=1)
    lane = lax.broadcasted_iota(jnp.int32, (HEADS, tm), 1)
    run = c_row
    shift = 1
    while shift < tm:
        if backward:
            moved = jnp.where(lane + shift < tm, pltpu.roll(run, tm - shift, axis=1), -jnp.inf)
        else:
            moved = jnp.where(lane >= shift, pltpu.roll(run, shift, axis=1), -jnp.inf)
        run = jnp.maximum(run, moved)
        shift *= 2
    m = m_ref[:, 0:1]
    log_inter = b_row + m
    m_pos = jnp.maximum(log_inter, b_row + run)
    shift_row = b_row - m_pos
    w_inter = jnp.exp(log_inter - m_pos)
    floor = jnp.exp(-m_pos)
    b_last = b_row[:, last:last + 1]
    m_new = m_pos[:, last:last + 1]
    w_src = jnp.exp(b_last + c_row - m_new)
    decay = jnp.exp(b_last + m - m_new)
    m_ref[...] = jnp.broadcast_to(m_new, m_ref.shape)
    half = lax.broadcasted_iota(jnp.int32, (tm, LANES), 1) < HEAD_QK
    ones = jnp.ones((2 * SUBLANES, tm), BF16)
    for h in range(HEADS):
        pair = slice((h // 2) * LANES, (h // 2 + 1) * LANES)
        qp = q_ref[:, pair]
        km = jnp.where(half if h % 2 == 0 else jnp.logical_not(half), k_ref[:, pair], 0.0).astype(BF16)
        st = lax.dot_general(km, qp, _NT, preferred_element_type=F32)
        e = (neg_mask + shift_row[h:h + 1, :]) + c_col[:, gate_i + h:gate_i + h + 1]
        p = (st * jnp.exp(e)).astype(BF16)
        vaug = jnp.concatenate([vt_ref[h * HEAD_V:(h + 1) * HEAD_V, :], ones], axis=0)
        ct = ct_ref[h]
        acc = jnp.dot(vaug, p, preferred_element_type=F32)
        acc = acc + w_inter[h:h + 1, :] * lax.dot_general(ct.astype(BF16), qp, _NT,
                                                          preferred_element_type=F32)
        den = acc[HEAD_V:HEAD_V + 1, :]
        scale = 1.0 / jnp.maximum(jnp.abs(den), floor[h:h + 1, :])
        o_ref[h * HEAD_V:(h + 1) * HEAD_V, :] = acc[0:HEAD_V, :] * scale
        vs = (vaug.astype(F32) * w_src[h:h + 1, :]).astype(BF16)
        ct_ref[h] = decay[h:h + 1, :] * ct + jnp.dot(vs, km, preferred_element_type=F32)


def _split_dot(ones_mat, x, left):
    hi = x.astype(BF16)
    r1 = x - hi.astype(F32)
    mid = r1.astype(BF16)
    lo = (r1 - mid.astype(F32)).astype(BF16)
    out = None
    for piece in (lo, mid, hi):
        t = (jnp.dot(ones_mat, piece, preferred_element_type=F32) if left
             else jnp.dot(piece, ones_mat, preferred_element_type=F32))
        out = t if out is None else out + t
    return out


def _mlstm_kernel(qf, kf, vtf, gcf, grf, qb, kb, vtb, gcb, grb, bc_ref, br_ref, of_ref, ob_ref,
                  cf_ref, mf_ref, cb_ref, mb_ref, *, tiles_per_batch):
    @pl.when(pl.program_id(0) % tiles_per_batch == 0)
    def _():
        cf_ref[...] = jnp.zeros_like(cf_ref)
        cb_ref[...] = jnp.zeros_like(cb_ref)
        mf_ref[...] = jnp.zeros_like(mf_ref)
        mb_ref[...] = jnp.zeros_like(mb_ref)

    _mlstm_direction(qf, kf, vtf, gcf, grf, bc_ref, br_ref, of_ref, cf_ref, mf_ref, backward=False)
    _mlstm_direction(qb, kb, vtb, gcb, grb, bc_ref, br_ref, ob_ref, cb_ref, mb_ref, backward=True)


def _mlstm_call(q, k, vt, gc, gr, bias_c, bias_r, dims):
    tm, d, n = TOKEN_TILE, dims["d"], dims["n"]
    tpb, ct = dims["tpb"], dims["ct"]
    nq = q.shape[1]
    fwd = lambda i: (i, 0)
    bwd = lambda i: (_bwd_tile(i, tpb, ct), 0)
    fwd_t = lambda i: (0, i)
    bwd_t = lambda i: (0, _bwd_tile(i, tpb, ct))
    const = lambda i: (0, 0)
    state = [pltpu.VMEM((HEADS, HEAD_V + 2 * SUBLANES, LANES), F32), pltpu.VMEM((HEADS, LANES), F32)]
    return pl.pallas_call(
        functools.partial(_mlstm_kernel, tiles_per_batch=tpb),
        grid=(dims["tiles"],),
        in_specs=[
            pl.BlockSpec((tm, nq), fwd), pl.BlockSpec((tm, nq), fwd), pl.BlockSpec((d, tm), fwd_t),
            pl.BlockSpec((tm, LANES), fwd), pl.BlockSpec((4 * HEADS, tm), fwd_t),
            pl.BlockSpec((tm, nq), bwd), pl.BlockSpec((tm, nq), bwd), pl.BlockSpec((d, tm), bwd_t),
            pl.BlockSpec((tm, LANES), bwd), pl.BlockSpec((4 * HEADS, tm), bwd_t),
            pl.BlockSpec((1, LANES), const), pl.BlockSpec((4 * HEADS, 1), const),
        ],
        out_specs=[pl.BlockSpec((d, tm), fwd_t), pl.BlockSpec((d, tm), bwd_t)],
        out_shape=[jax.ShapeDtypeStruct((d, n), F32), jax.ShapeDtypeStruct((d, n), F32)],
        scratch_shapes=state + state,
        compiler_params=_params("arbitrary"),
        name="mlstm_scan",
    )(q, k, vt, gc, gr, q, k, vt, gc, gr, bias_c, bias_r)


def _mreadout_kernel(hf_ref, hb_ref, ot_ref, w_ref, x_ref, mod_ref, o_ref):
    d = x_ref.shape[-1]
    parts = []
    for h in range(HEADS):
        rows = slice(h * HEAD_V, (h + 1) * HEAD_V)
        hs = hf_ref[rows, :] + hb_ref[rows, :]
        r = lax.rsqrt(jnp.mean(hs * hs, axis=0, keepdims=True) + NORM_EPS)
        parts.append((hs * r * ot_ref[rows, :]).astype(BF16))
    t = jnp.concatenate(parts, axis=0)
    out = lax.dot_general(t, w_ref[...], _TN, preferred_element_type=F32)
    o_ref[...] = x_ref[...] + mod_ref[:, 2 * d:3 * d] * out


def _mreadout_call(hf, hb, ot, w_out, xs, mods, layer, dims):
    tm, d, n = TOKEN_TILE, dims["d"], dims["n"]
    row = lambda i: (i, 0)
    col = lambda i: (0, i)
    const = lambda i: (0, 0)
    return pl.pallas_call(
        _mreadout_kernel,
        grid=(dims["tiles"],),
        in_specs=[
            pl.BlockSpec((d, tm), col), pl.BlockSpec((d, tm), col), pl.BlockSpec((d, tm), col),
            pl.BlockSpec((d, d), const), pl.BlockSpec((tm, d), row), _mod_spec(layer, dims),
        ],
        out_specs=pl.BlockSpec((tm, d), row),
        out_shape=jax.ShapeDtypeStruct((n, d), F32),
        compiler_params=_params("arbitrary"),
        name="mlstm_readout",
    )(hf, hb, ot, w_out, xs, mods)


def _readout_kernel(hf_ref, hb_ref, gy_ref, w_ref, x_ref, mod_ref, o_ref):
    d = x_ref.shape[-1]
    t = gy_ref[...] * (hf_ref[...] + hb_ref[...])
    out = jnp.dot(t.astype(BF16), w_ref[...], preferred_element_type=F32)
    o_ref[...] = x_ref[...] + mod_ref[:, 2 * d:3 * d] * out


def _readout_call(hf, hb, gy, w_out, xs, mods, layer, dims):
    tm, d, n = TOKEN_TILE, dims["d"], dims["n"]
    row = lambda i: (i, 0)
    const = lambda i: (0, 0)
    return pl.pallas_call(
        _readout_kernel,
        grid=(dims["tiles"],),
        in_specs=[
            pl.BlockSpec((tm, d), row), pl.BlockSpec((tm, d), row), pl.BlockSpec((tm, d), row),
            pl.BlockSpec((d, d), const), pl.BlockSpec((tm, d), row), _mod_spec(layer, dims),
        ],
        out_specs=pl.BlockSpec((tm, d), row),
        out_shape=jax.ShapeDtypeStruct((n, d), F32),
        compiler_params=_params("arbitrary"),
        name="rglru_readout",
    )(hf, hb, gy, w_out, xs, mods)


def _rproj_kernel(x_ref, g_ref, mod_ref, w_ref, gy_ref, xr_ref):
    d = x_ref.shape[-1]
    h = _modulated(x_ref[...], g_ref[...], mod_ref[...], 0).astype(BF16)
    gy_ref[...] = _gelu_tanh(jnp.dot(h, w_ref[:, 0:d], preferred_element_type=F32))
    xr_ref[...] = jnp.dot(h, w_ref[:, d:2 * d], preferred_element_type=F32)


def _rproj_call(xs, g, mods, layer, w_in, dims):
    tm, d, n = TOKEN_TILE, dims["d"], dims["n"]
    row = lambda i: (i, 0)
    const = lambda i: (0, 0)
    return pl.pallas_call(
        _rproj_kernel,
        grid=(dims["tiles"],),
        in_specs=[pl.BlockSpec((tm, d), row), pl.BlockSpec((1, d), const), _mod_spec(layer, dims),
                  pl.BlockSpec(w_in.shape, const)],
        out_specs=[pl.BlockSpec((tm, d), row), pl.BlockSpec((tm, d), row)],
        out_shape=[jax.ShapeDtypeStruct((n, d), F32), jax.ShapeDtypeStruct((n, d), F32)],
        compiler_params=_params("arbitrary"),
        name="rglru_proj",
    )(xs, g, mods, w_in)


def _lru_direction(cur_ref, prev_ref, next_ref, prev_ok, next_ok, cw_ref, cb_ref, wg_ref, bg_ref,
                   lam_ref, ext_ref, a_ref, b_ref, *, direction):
    tm, d = cur_ref.shape
    bw = d // LRU_BLOCKS
    ext_ref[0:SUBLANES, :] = jnp.where(prev_ok, prev_ref[...], 0.0)
    ext_ref[SUBLANES:SUBLANES + tm, :] = cur_ref[...]
    ext_ref[SUBLANES + tm:2 * SUBLANES + tm, :] = jnp.where(next_ok, next_ref[...], 0.0)
    xc = cb_ref[...] + ext_ref[pl.ds(SUBLANES - CONV_LEFT, tm), :] * cw_ref[0:1, :]
    for j in range(1, CONV_TAPS):
        xc = xc + ext_ref[pl.ds(SUBLANES - CONV_LEFT + j, tm), :] * cw_ref[j:j + 1, :]
    z = -lam_ref[direction]
    softplus = jnp.maximum(z, 0.0) + jnp.log(1.0 + jnp.exp(-jnp.abs(z)))
    for nb in range(LRU_BLOCKS):
        sl = slice(nb * bw, (nb + 1) * bw)
        xb = xc[:, sl]
        gg = jnp.dot(xb.astype(BF16), wg_ref[direction, nb], preferred_element_type=F32)
        gg = gg + bg_ref[direction, :, nb * 2 * bw:(nb + 1) * 2 * bw]
        r = jax.nn.sigmoid(gg[:, :bw])
        gi = jax.nn.sigmoid(gg[:, bw:])
        log_a = -LRU_C * r * softplus[:, sl]
        a = jnp.exp(log_a)
        a_ref[:, sl] = a
        th = jnp.tanh(log_a)
        b_ref[:, sl] = jnp.sqrt(-2.0 * th / (1.0 - th)) * (gi * xb)


def _lru_kernel(cur_f, prev_f, next_f, cur_b, prev_b, next_b, cw_ref, cb_ref, wg_ref, bg_ref, lam_ref,
                of_ref, ob_ref, ext_ref, af_ref, bf_ref, ab_ref, bb_ref, h_ref,
                *, tiles_per_batch, ctx_tiles):
    tm = cur_f.shape[0]
    j = pl.program_id(0) % tiles_per_batch

    @pl.when(j == 0)
    def _():
        h_ref[...] = jnp.zeros_like(h_ref)

    def seg_edges(jj):
        first = jnp.logical_or(jj == 0, jj == ctx_tiles)
        last = jnp.logical_or(jj == ctx_tiles - 1, jj == tiles_per_batch - 1)
        return jnp.logical_not(first), jnp.logical_not(last)

    jb = _bwd_tile(pl.program_id(0), tiles_per_batch, ctx_tiles) % tiles_per_batch
    pf, nf = seg_edges(j)
    pb, nb = seg_edges(jb)
    common = (cw_ref, cb_ref, wg_ref, bg_ref, lam_ref, ext_ref)
    _lru_direction(cur_f, prev_f, next_f, pf, nf, *common, af_ref, bf_ref, direction=0)
    _lru_direction(cur_b, prev_b, next_b, pb, nb, *common, ab_ref, bb_ref, direction=1)

    def step(t, carry):
        hf, hb = carry
        tb = tm - 1 - t
        hf = af_ref[pl.ds(t, 1), :] * hf + bf_ref[pl.ds(t, 1), :]
        hb = ab_ref[pl.ds(tb, 1), :] * hb + bb_ref[pl.ds(tb, 1), :]
        of_ref[pl.ds(t, 1), :] = hf
        ob_ref[pl.ds(tb, 1), :] = hb
        return hf, hb

    hf, hb = lax.fori_loop(0, tm, step, (h_ref[0:1, :], h_ref[1:2, :]), unroll=8)
    h_ref[0:1, :] = hf
    h_ref[1:2, :] = hb


def _lru_call(xr, conv_w, conv_b, w_gate, b_gate, lam, dims):
    tm, d, n = TOKEN_TILE, dims["d"], dims["n"]
    tpb, ct = dims["tpb"], dims["ct"]
    per = tm // SUBLANES
    last8 = n // SUBLANES - 1
    fwd = lambda i: (i, 0)
    bwd = lambda i: (_bwd_tile(i, tpb, ct), 0)
    prev_of = lambda t: (jnp.maximum(t * per - 1, 0), 0)
    next_of = lambda t: (jnp.minimum((t + 1) * per, last8), 0)
    c2 = lambda i: (0, 0)
    c3 = lambda i: (0, 0, 0)
    c4 = lambda i: (0, 0, 0, 0)
    return pl.pallas_call(
        functools.partial(_lru_kernel, tiles_per_batch=tpb, ctx_tiles=ct),
        grid=(dims["tiles"],),
        in_specs=[
            pl.BlockSpec((tm, d), fwd),
            pl.BlockSpec((SUBLANES, d), lambda i: prev_of(i)),
            pl.BlockSpec((SUBLANES, d), lambda i: next_of(i)),
            pl.BlockSpec((tm, d), bwd),
            pl.BlockSpec((SUBLANES, d), lambda i: prev_of(_bwd_tile(i, tpb, ct))),
            pl.BlockSpec((SUBLANES, d), lambda i: next_of(_bwd_tile(i, tpb, ct))),
            pl.BlockSpec(conv_w.shape, c2), pl.BlockSpec(conv_b.shape, c2),
            pl.BlockSpec(w_gate.shape, c4), pl.BlockSpec(b_gate.shape, c3), pl.BlockSpec(lam.shape, c3),
        ],
        out_specs=[pl.BlockSpec((tm, d), fwd), pl.BlockSpec((tm, d), bwd)],
        out_shape=[jax.ShapeDtypeStruct((n, d), F32), jax.ShapeDtypeStruct((n, d), F32)],
        scratch_shapes=[pltpu.VMEM((tm + 2 * SUBLANES, d), F32)] + [pltpu.VMEM((tm, d), F32)] * 4
        + [pltpu.VMEM((SUBLANES, d), F32)],
        compiler_params=_params("arbitrary"),
        name="rglru_scan",
    )(xr, xr, xr, xr, xr, xr, conv_w, conv_b, w_gate, b_gate, lam)


def _first_argmax(vals):
    best = vals[0]
    idx = jnp.zeros_like(best)
    for k in range(1, len(vals)):
        better = vals[k] > best
        idx = jnp.where(better, float(k), idx)
        best = jnp.where(better, vals[k], best)
    return idx, best


def _pick(idx, vals):
    out = vals[0]
    for k in range(1, len(vals)):
        out = jnp.where(idx == float(k), vals[k], out)
    return out


def _router_kernel(x_ref, g_ref, mod_ref, wr_ref, br_ref, hx_ref, route_ref, cnt_ref, carry_ref):
    tm, d = x_ref.shape

    @pl.when(pl.program_id(0) == 0)
    def _():
        carry_ref[...] = jnp.zeros_like(carry_ref)

    h = _modulated(x_ref[...], g_ref[...], mod_ref[...], 3)
    hx_ref[:, 0:d] = h
    logits = lax.dot_general(wr_ref[...], h, (((1,), (1,)), ((), ())),
                             preferred_element_type=F32, precision=HIGHEST)
    e = jnp.exp(logits - jnp.max(logits, axis=0, keepdims=True))
    scores = e / jnp.sum(e, axis=0, keepdims=True)
    sel = scores + br_ref[...]
    sel_rows = [sel[k:k + 1, :] for k in range(N_EXPERTS)]
    score_rows = [scores[k:k + 1, :] for k in range(N_EXPERTS)]
    group_scores = []
    for gi in range(N_GROUPS):
        v = sel_rows[gi * GROUP_SIZE:(gi + 1) * GROUP_SIZE]
        best = v[0] + v[1]
        for a, b in PAIRS[1:]:
            best = jnp.maximum(best, v[a] + v[b])
        group_scores.append(best)
    grp, _ = _first_argmax(group_scores)
    in_sel = [_pick(grp, [sel_rows[gi * GROUP_SIZE + k] for gi in range(N_GROUPS)])
              for k in range(GROUP_SIZE)]
    in_score = [_pick(grp, [score_rows[gi * GROUP_SIZE + k] for gi in range(N_GROUPS)])
                for k in range(GROUP_SIZE)]
    i1, _ = _first_argmax(in_sel)
    rest = [jnp.where(i1 == float(k), -jnp.inf, in_sel[k]) for k in range(GROUP_SIZE)]
    i2, _ = _first_argmax(rest)
    lo = jnp.minimum(i1, i2)
    hi = jnp.maximum(i1, i2)
    s_lo = _pick(lo, in_score)
    s_hi = _pick(hi, in_score)
    tot = s_lo + s_hi
    w_lo = s_lo / tot
    w_hi = s_hi / tot
    pair = jnp.where(lo == 0.0, hi - 1.0, jnp.where(lo == 1.0, hi + 1.0, 5.0))
    cls = grp * float(len(PAIRS)) + pair
    crow = lax.broadcasted_iota(jnp.int32, (CLASS_ROWS, tm), 0).astype(F32)
    onehot = jnp.where(crow == cls, 1.0, 0.0)
    rows = lax.broadcasted_iota(jnp.int32, (tm, tm), 0)
    cols = lax.broadcasted_iota(jnp.int32, (tm, tm), 1)
    upper = jnp.where(rows <= cols, 1.0, 0.0).astype(BF16)
    cum = jnp.dot(onehot.astype(BF16), upper, preferred_element_type=F32)
    carry = carry_ref[:, 0:1]
    rank = jnp.sum(onehot * (cum - 1.0 + carry), axis=0, keepdims=True)
    new_carry = carry + jnp.sum(onehot, axis=1, keepdims=True)
    carry_ref[...] = jnp.broadcast_to(new_carry, carry_ref.shape)
    cnt_ref[...] = jnp.broadcast_to(new_carry, cnt_ref.shape)
    zero = jnp.zeros_like(cls)
    route_ref[...] = jnp.concatenate([cls, rank, w_lo, w_hi, zero, zero, zero, zero], axis=0)
    eye = rows == cols
    wl_col = jnp.sum(jnp.where(eye, w_lo, 0.0), axis=1, keepdims=True)
    wh_col = jnp.sum(jnp.where(eye, w_hi, 0.0), axis=1, keepdims=True)
    lane = lax.broadcasted_iota(jnp.int32, (tm, ROUTE_COLS), 1)
    hx_ref[:, d:d + ROUTE_COLS] = jnp.where(lane == 0, wl_col, jnp.where(lane == 1, wh_col, 0.0))


def _router_call(xs, g, mods, layer, w_router_t, b_router_col, dims):
    tm, d, n = TOKEN_TILE, dims["d"], dims["n"]
    row = lambda i: (i, 0)
    const = lambda i: (0, 0)
    return pl.pallas_call(
        _router_kernel,
        grid=(dims["tiles"],),
        in_specs=[pl.BlockSpec((tm, d), row), pl.BlockSpec((1, d), const), _mod_spec(layer, dims),
                  pl.BlockSpec(w_router_t.shape, const), pl.BlockSpec(b_router_col.shape, const)],
        out_specs=[pl.BlockSpec((tm, d + ROUTE_COLS), row),
                   pl.BlockSpec((SUBLANES, tm), lambda i: (0, i)),
                   pl.BlockSpec((CLASS_ROWS, LANES), const)],
        out_shape=[jax.ShapeDtypeStruct((n, d + ROUTE_COLS), F32),
                   jax.ShapeDtypeStruct((SUBLANES, n), F32),
                   jax.ShapeDtypeStruct((CLASS_ROWS, LANES), F32)],
        scratch_shapes=[pltpu.VMEM((CLASS_ROWS, LANES), F32)],
        compiler_params=_params("arbitrary"),
        name="router",
    )(xs, g, mods, w_router_t, b_router_col)


ROW_DMA_UNROLL = 8


def _dispatch_kernel(pos_ref, hx_ref, init_ref, o_ref, stage_ref, sems):
    del init_ref
    tm = hx_ref.shape[0]
    i = pl.program_id(0)
    slot = i % 2
    base = i * tm
    stage_ref[slot] = hx_ref[...]

    def issue(r, carry):
        pltpu.make_async_copy(stage_ref.at[slot, pl.ds(r, 1)], o_ref.at[pl.ds(pos_ref[base + r], 1)],
                              sems.at[slot]).start()
        return carry

    lax.fori_loop(0, tm, issue, 0, unroll=ROW_DMA_UNROLL)

    def wait_tile(s):
        pltpu.make_async_copy(stage_ref.at[s], o_ref.at[pl.ds(0, tm)], sems.at[s]).wait()

    @pl.when(i > 0)
    def _():
        wait_tile(1 - slot)

    @pl.when(i == pl.num_programs(0) - 1)
    def _():
        wait_tile(slot)


def _dispatch_call(pos, hx, padded_rows, dims):
    tm = TOKEN_TILE
    width = hx.shape[1]
    init = jnp.zeros((padded_rows, width), F32)
    return pl.pallas_call(
        _dispatch_kernel,
        grid_spec=pltpu.PrefetchScalarGridSpec(
            num_scalar_prefetch=1,
            grid=(dims["tiles"],),
            in_specs=[pl.BlockSpec((tm, width), lambda i, p: (i, 0)), pl.BlockSpec(memory_space=pl.ANY)],
            out_specs=pl.BlockSpec(memory_space=pl.ANY),
            scratch_shapes=[pltpu.VMEM((2, tm, width), F32), pltpu.SemaphoreType.DMA((2,))],
        ),
        out_shape=jax.ShapeDtypeStruct((padded_rows, width), F32),
        input_output_aliases={2: 0},
        compiler_params=_params("arbitrary"),
        name="dispatch",
    )(pos, hx, init)


def _expert_kernel(ea_ref, eb_ref, ok_ref, x_ref, w1a, w3a, w2a, w1b, w3b, w2b, o_ref):
    del ea_ref, eb_ref
    d = o_ref.shape[-1]
    t = pl.program_id(0)

    @pl.when(ok_ref[t] != 0)
    def _():
        x = x_ref[:, 0:d].astype(BF16)

        def expert(w1, w3, w2):
            a = jnp.dot(x, w1[...], preferred_element_type=F32)
            b = jnp.dot(x, w3[...], preferred_element_type=F32)
            u = (a * jax.nn.sigmoid(a)) * b
            return jnp.dot(u.astype(BF16), w2[...], preferred_element_type=F32)

        o_ref[...] = (x_ref[:, d:d + 1] * expert(w1a, w3a, w2a)
                      + x_ref[:, d + 1:d + 2] * expert(w1b, w3b, w2b))

    @pl.when(ok_ref[t] == 0)
    def _():
        o_ref[...] = jnp.zeros_like(o_ref)


def _expert_call(tile_a, tile_b, tile_ok, xsorted, w1, w3, w2, d):
    te = TOKEN_TILE
    rows, width = xsorted.shape
    de = w1.shape[-1]
    sel_a = lambda t, ea, eb, ok: (ea[t], 0, 0)
    sel_b = lambda t, ea, eb, ok: (eb[t], 0, 0)
    return pl.pallas_call(
        _expert_kernel,
        grid_spec=pltpu.PrefetchScalarGridSpec(
            num_scalar_prefetch=3,
            grid=(rows // te,),
            in_specs=[
                pl.BlockSpec((te, width), lambda t, ea, eb, ok: (t, 0)),
                pl.BlockSpec((None, d, de), sel_a), pl.BlockSpec((None, d, de), sel_a),
                pl.BlockSpec((None, de, d), sel_a),
                pl.BlockSpec((None, d, de), sel_b), pl.BlockSpec((None, d, de), sel_b),
                pl.BlockSpec((None, de, d), sel_b),
            ],
            out_specs=pl.BlockSpec((te, d), lambda t, ea, eb, ok: (t, 0)),
        ),
        out_shape=jax.ShapeDtypeStruct((rows, d), F32),
        compiler_params=_params("arbitrary"),
        name="experts",
    )(tile_a, tile_b, tile_ok, xsorted, w1, w3, w2, w1, w3, w2)


def _combine_kernel(pos_ref, y_ref, x_ref, mod_ref, gf_ref, o_ref, buf_ref, sems,
                    *, tile_of, final_norm):
    tm, d = x_ref.shape
    i = pl.program_id(0)
    slot = i % 2

    def gather(step, s):
        base = tile_of(step) * tm

        def issue(r, carry):
            pltpu.make_async_copy(y_ref.at[pl.ds(pos_ref[base + r], 1)], buf_ref.at[s, pl.ds(r, 1)],
                                  sems.at[s]).start()
            return carry

        lax.fori_loop(0, tm, issue, 0, unroll=ROW_DMA_UNROLL)

    @pl.when(i == 0)
    def _():
        gather(0, 0)

    @pl.when(i + 1 < pl.num_programs(0))
    def _():
        gather(i + 1, 1 - slot)

    pltpu.make_async_copy(y_ref.at[pl.ds(0, tm)], buf_ref.at[slot], sems.at[slot]).wait()
    out = x_ref[...] + mod_ref[:, 5 * d:6 * d] * buf_ref[slot]
    if final_norm:
        ms = jnp.mean(out * out, axis=-1, keepdims=True)
        out = out * lax.rsqrt(ms + NORM_EPS) * gf_ref[...]
    o_ref[...] = out


def _combine_call(pos, ysorted, xs, mods, layer, g_final, dims, final_norm):
    tm, d = TOKEN_TILE, dims["d"]
    tpb, ct, b = dims["tpb"], dims["ct"], dims["b"]
    if final_norm:
        lt = tpb - ct
        tile_of = lambda i: (i // lt) * tpb + ct + i % lt
        n_tiles = b * lt
    else:
        tile_of = lambda i: i
        n_tiles = dims["tiles"]
    mod_spec = pl.BlockSpec((None, None, 1, 6 * d),
                            lambda i, p: (layer, _mod_row(tile_of(i), tpb, ct, b), 0, 0))
    return pl.pallas_call(
        functools.partial(_combine_kernel, tile_of=tile_of, final_norm=final_norm),
        grid_spec=pltpu.PrefetchScalarGridSpec(
            num_scalar_prefetch=1,
            grid=(n_tiles,),
            in_specs=[
                pl.BlockSpec(memory_space=pl.ANY),
                pl.BlockSpec((tm, d), lambda i, p: (tile_of(i), 0)),
                mod_spec,
                pl.BlockSpec((1, d), lambda i, p: (0, 0)),
            ],
            out_specs=pl.BlockSpec((tm, d), lambda i, p: (i, 0)),
            scratch_shapes=[pltpu.VMEM((2, tm, d), F32), pltpu.SemaphoreType.DMA((2,))],
        ),
        out_shape=jax.ShapeDtypeStruct((n_tiles * tm, d), F32),
        compiler_params=_params("arbitrary"),
        name="combine",
    )(pos, ysorted, xs, mods, g_final)


_CLASS_A = np.array([g * GROUP_SIZE + a for g in range(N_GROUPS) for a, _ in PAIRS], np.int32)
_CLASS_B = np.array([g * GROUP_SIZE + b for g in range(N_GROUPS) for _, b in PAIRS], np.int32)


def _routing_plan(route, counts, n_tiles_padded):
    te = TOKEN_TILE
    cls = route[0].astype(jnp.int32)
    rank = route[1].astype(jnp.int32)
    cnt = counts[:N_CLASSES, 0].astype(jnp.int32)
    tiles = (cnt + te - 1) // te
    tile_end = jnp.cumsum(tiles)
    tile_start = tile_end - tiles
    pos = jnp.take(tile_start * te, cls) + rank
    t = jnp.arange(n_tiles_padded, dtype=jnp.int32)
    ok = t < tile_end[-1]
    tcls = jnp.sum((tile_end[None, :] <= jnp.minimum(t, tile_end[-1] - 1)[:, None]).astype(jnp.int32), axis=1)
    tile_a = jnp.take(jnp.asarray(_CLASS_A), tcls)
    tile_b = jnp.take(jnp.asarray(_CLASS_B), tcls)
    return pos, tile_a, tile_b, ok.astype(jnp.int32)


def _moe(xs, g, mods, layer, w_router_t, b_router_col, w1, w3, w2, g_final, dims, final_norm):
    hx, route, counts = _router_call(xs, g, mods, layer, w_router_t, b_router_col, dims)
    n_tiles_padded = dims["tiles"] + N_CLASSES
    pos, tile_a, tile_b, tile_ok = _routing_plan(route, counts, n_tiles_padded)
    xsorted = _dispatch_call(pos, hx, n_tiles_padded * TOKEN_TILE, dims)
    ysorted = _expert_call(tile_a, tile_b, tile_ok, xsorted, w1, w3, w2, dims["d"])
    return _combine_call(pos, ysorted, xs, mods, layer, g_final, dims, final_norm)


def _pos_table(n_tokens, dim):
    rows = n_tokens // GRID_WIDTH
    r, col = jnp.meshgrid(jnp.arange(rows, dtype=F32), jnp.arange(GRID_WIDTH, dtype=F32), indexing="ij")
    quarter = dim // 4
    freqs = jnp.exp(-math.log(POS_BASE) * jnp.arange(quarter, dtype=F32) / quarter)

    def enc(p):
        ang = p.reshape(-1, 1) * freqs
        return jnp.concatenate([jnp.sin(ang), jnp.cos(ang)], axis=-1)

    return jnp.concatenate([enc(r), enc(col)], axis=-1)


def _mlstm_weights(w_in, b_gate):
    nq = HEADS * HEAD_QK
    nv = HEADS * HEAD_V
    wq = w_in[:, 0:nq] * (HEAD_QK ** -0.5)
    wk = w_in[:, nq:2 * nq]
    wv = w_in[:, 2 * nq:2 * nq + nv]
    wo = w_in[:, 2 * nq + nv:2 * nq + 2 * nv]
    wg = w_in[:, 2 * nq + 2 * nv:]
    wg_pad = jnp.pad(wg, ((0, 0), (0, LANES - wg.shape[1])))
    w_tok = jnp.concatenate([wq, wk, wg_pad], axis=1).astype(BF16)
    w_feat = jnp.concatenate([wv, wo, wg], axis=1).T.astype(BF16)
    bias = b_gate.reshape(-1).astype(F32)
    bias_c = jnp.pad(bias, (0, LANES - bias.shape[0])).reshape(1, LANES)
    bias_r = bias.reshape(-1, 1)
    return w_tok, w_feat, bias_c, bias_r


def kernel(x, c, ctx, c_ctx, w_ada, b_ada, g_mix, g_ffn, g_final, m_w_in, m_b_gate, m_g_head, m_w_out, r_w_in, r_conv_w, r_conv_b, r_w_gate, r_b_gate, r_lam, r_w_out, w_router, b_router, e_w1, e_w3, e_w2):
    batch, t_len, d = x.shape
    ctx_len = ctx.shape[1]
    depth = w_ada.shape[0]
    tm = TOKEN_TILE
    assert t_len % tm == 0 and ctx_len % tm == 0 and batch + 1 <= ADA_ROWS
    assert d == HEADS * HEAD_V and t_len % GRID_WIDTH == 0
    s_len = ctx_len + t_len
    dims = dict(b=batch, d=d, n=batch * s_len, tpb=s_len // tm, ct=ctx_len // tm,
                tiles=batch * s_len // tm)

    cvec = jnp.concatenate([c, c_ctx[None, :], jnp.zeros((ADA_ROWS - batch - 1, d), F32)], axis=0)
    mods = _ada_call(cvec, w_ada, b_ada).reshape(depth, ADA_ROWS, 1, 6 * d)
    xs = _stream_call(ctx.reshape(batch * ctx_len, d), x.reshape(batch * t_len, d),
                      _pos_table(t_len, d), dims)

    w_router_t = w_router.T.astype(F32)
    b_router_col = b_router.reshape(-1, 1).astype(F32)
    g_final2 = g_final.reshape(1, d)
    out = None
    for i in range(depth):
        j = i // 2
        g_mix_i = g_mix[i].reshape(1, d)
        if i % 2 == 0:
            w_tok, w_feat, bias_c, bias_r = _mlstm_weights(m_w_in[j], m_b_gate[j])
            q, k, vt, ot, gc, gr = _mproj_call(xs, g_mix_i, mods, i, w_tok, w_feat, dims)
            hf, hb = _mlstm_call(q, k, vt, gc, gr, bias_c, bias_r, dims)
            w_out = (m_g_head[j][:, None] * m_w_out[j]).astype(BF16)
            xs = _mreadout_call(hf, hb, ot, w_out, xs, mods, i, dims)
        else:
            gy, xr = _rproj_call(xs, g_mix_i, mods, i, r_w_in[j].astype(BF16), dims)
            hf, hb = _lru_call(xr, r_conv_w[j], r_conv_b[j].reshape(1, d), r_w_gate[j].astype(BF16),
                               r_b_gate[j].reshape(2, 1, -1), r_lam[j].reshape(2, 1, d), dims)
            xs = _readout_call(hf, hb, gy, r_w_out[j].astype(BF16), xs, mods, i, dims)
        last = i == depth - 1
        res = _moe(xs, g_ffn[i].reshape(1, d), mods, i, w_router_t, b_router_col,
                   e_w1[i].astype(BF16), e_w3[i].astype(BF16), e_w2[i].astype(BF16),
                   g_final2, dims, last)
        if last:
            out = res
        else:
            xs = res
    return out.reshape(batch, t_len, d)
```

```python
import functools
import math

import jax
import jax.numpy as jnp
import numpy as np
from jax import lax
from jax.experimental import pallas as pl
from jax.experimental.pallas import tpu as pltpu

F32 = jnp.float32
BF16 = jnp.bfloat16
HIGHEST = lax.Precision.HIGHEST

TOKEN_TILE = 256
LANES = 128
SUBLANES = 8
VMEM_LIMIT_BYTES = 56 * 1024 * 1024
NORM_EPS = 1e-6
GRID_WIDTH = 64
POS_BASE = 10000.0
HEADS = 8
HEAD_QK = 64
HEAD_V = 128
LRU_BLOCKS = 8
LRU_C = 8.0
CONV_TAPS = 4
CONV_LEFT = CONV_TAPS // 2
N_EXPERTS = 16
N_GROUPS = 4
GROUP_SIZE = N_EXPERTS // N_GROUPS
PAIRS = [(a, b) for a in range(GROUP_SIZE) for b in range(a + 1, GROUP_SIZE)]
N_CLASSES = N_GROUPS * len(PAIRS)
CLASS_ROWS = 32
ROUTE_COLS = LANES
ADA_ROWS = 16


def _params(*sem):
    return pltpu.CompilerParams(dimension_semantics=sem, vmem_limit_bytes=VMEM_LIMIT_BYTES)


def _mod_row(i, tiles_per_batch, ctx_tiles, batch):
    return jnp.where(i % tiles_per_batch < ctx_tiles, batch, i // tiles_per_batch)


def _bwd_tile(i, tiles_per_batch, ctx_tiles):
    b = i // tiles_per_batch
    j = i % tiles_per_batch
    jb = jnp.where(j < ctx_tiles, ctx_tiles - 1 - j, tiles_per_batch - 1 - (j - ctx_tiles))
    return b * tiles_per_batch + jb


def _ada_kernel(c_ref, w_ref, b_ref, o_ref):
    c = c_ref[...]
    s = c * jax.nn.sigmoid(c)
    o_ref[...] = jnp.dot(s, w_ref[...], preferred_element_type=F32, precision=HIGHEST) + b_ref[...]


def _ada_call(cvec, w_ada, b_ada):
    depth, d, n = w_ada.shape
    tn = n // 4
    return pl.pallas_call(
        _ada_kernel,
        grid=(depth, n // tn),
        in_specs=[
            pl.BlockSpec((ADA_ROWS, d), lambda l, j: (0, 0)),
            pl.BlockSpec((None, d, tn), lambda l, j: (l, 0, j)),
            pl.BlockSpec((None, 1, tn), lambda l, j: (l, 0, j)),
        ],
        out_specs=pl.BlockSpec((None, ADA_ROWS, tn), lambda l, j: (l, 0, j)),
        out_shape=jax.ShapeDtypeStruct((depth, ADA_ROWS, n), F32),
        compiler_params=_params("arbitrary", "arbitrary"),
        name="ada",
    )(cvec, w_ada, b_ada.reshape(depth, 1, n))


def _stream_kernel(ctx_ref, x_ref, pos_ref, o_ref, *, tiles_per_batch, ctx_tiles):
    j = pl.program_id(0) % tiles_per_batch

    @pl.when(j < ctx_tiles)
    def _():
        o_ref[...] = ctx_ref[...]

    @pl.when(j >= ctx_tiles)
    def _():
        o_ref[...] = x_ref[...] + pos_ref[...]


def _stream_call(ctx2, x2, pos, dims):
    tm, d = TOKEN_TILE, dims["d"]
    tpb, ct = dims["tpb"], dims["ct"]
    lt = tpb - ct
    return pl.pallas_call(
        functools.partial(_stream_kernel, tiles_per_batch=tpb, ctx_tiles=ct),
        grid=(dims["tiles"],),
        in_specs=[
            pl.BlockSpec((tm, d), lambda i: ((i // tpb) * ct + jnp.minimum(i % tpb, ct - 1), 0)),
            pl.BlockSpec((tm, d), lambda i: ((i // tpb) * lt + jnp.maximum(i % tpb - ct, 0), 0)),
            pl.BlockSpec((tm, d), lambda i: (jnp.maximum(i % tpb - ct, 0), 0)),
        ],
        out_specs=pl.BlockSpec((tm, d), lambda i: (i, 0)),
        out_shape=jax.ShapeDtypeStruct((dims["n"], d), F32),
        compiler_params=_params("arbitrary"),
        name="stream",
    )(ctx2, x2, pos)


def _modulated(x, g, mod, shift_idx):
    d = x.shape[-1]
    ms = jnp.mean(x * x, axis=-1, keepdims=True)
    xn = x * lax.rsqrt(ms + NORM_EPS) * g
    shift = mod[:, shift_idx * d:(shift_idx + 1) * d]
    scale = mod[:, (shift_idx + 1) * d:(shift_idx + 2) * d]
    return xn * (1.0 + scale) + shift


def _mod_spec(layer, dims):
    tpb, ct, b = dims["tpb"], dims["ct"], dims["b"]
    return pl.BlockSpec((None, None, 1, 6 * dims["d"]),
                        lambda i, *_: (layer, _mod_row(i, tpb, ct, b), 0, 0))


def _gelu_tanh(y):
    return 0.5 * y * (1.0 + jnp.tanh(math.sqrt(2.0 / math.pi) * (y + 0.044715 * (y * y * y))))


_NT = (((1,), (1,)), ((), ()))
_TN = (((0,), (0,)), ((), ()))


def _mproj_kernel(x_ref, g_ref, mod_ref, w_ref, wt_ref, q_ref, k_ref, vt_ref, ot_ref, gc_ref, gr_ref):
    d = x_ref.shape[-1]
    nq = q_ref.shape[-1]
    h = _modulated(x_ref[...], g_ref[...], mod_ref[...], 0).astype(BF16)
    q_ref[...] = jnp.dot(h, w_ref[:, 0:nq], preferred_element_type=F32).astype(BF16)
    k_ref[...] = jnp.dot(h, w_ref[:, nq:2 * nq], preferred_element_type=F32).astype(BF16)
    gc_ref[...] = jnp.dot(h, w_ref[:, 2 * nq:2 * nq + LANES], preferred_element_type=F32)
    vt_ref[...] = lax.dot_general(wt_ref[0:d, :], h, _NT, preferred_element_type=F32).astype(BF16)
    ot = lax.dot_general(wt_ref[d:2 * d, :], h, _NT, preferred_element_type=F32)
    ot_ref[...] = jax.nn.sigmoid(ot).astype(BF16)
    gr_ref[...] = lax.dot_general(wt_ref[2 * d:2 * d + 4 * HEADS, :], h, _NT, preferred_element_type=F32)


def _mproj_call(xs, g, mods, layer, w_tok, w_feat, dims):
    tm, d, n = TOKEN_TILE, dims["d"], dims["n"]
    nq = HEADS * HEAD_QK
    row = lambda i: (i, 0)
    col = lambda i: (0, i)
    const = lambda i: (0, 0)
    return pl.pallas_call(
        _mproj_kernel,
        grid=(dims["tiles"],),
        in_specs=[
            pl.BlockSpec((tm, d), row),
            pl.BlockSpec((1, d), const),
            _mod_spec(layer, dims),
            pl.BlockSpec(w_tok.shape, const),
            pl.BlockSpec(w_feat.shape, const),
        ],
        out_specs=[
            pl.BlockSpec((tm, nq), row),
            pl.BlockSpec((tm, nq), row),
            pl.BlockSpec((d, tm), col),
            pl.BlockSpec((d, tm), col),
            pl.BlockSpec((tm, LANES), row),
            pl.BlockSpec((4 * HEADS, tm), col),
        ],
        out_shape=[
            jax.ShapeDtypeStruct((n, nq), BF16),
            jax.ShapeDtypeStruct((n, nq), BF16),
            jax.ShapeDtypeStruct((d, n), BF16),
            jax.ShapeDtypeStruct((d, n), BF16),
            jax.ShapeDtypeStruct((n, LANES), F32),
            jax.ShapeDtypeStruct((4 * HEADS, n), F32),
        ],
        compiler_params=_params("arbitrary"),
        name="mlstm_proj",
    )(xs, g, mods, w_tok, w_feat)


def _mlstm_direction(q_ref, k_ref, vt_ref, gc_ref, gr_ref, bc_ref, br_ref, o_ref, ct_ref, m_ref,
                     *, backward):
    tm = q_ref.shape[0]
    gate_i = 2 * HEADS if backward else 0
    gate_f = gate_i + HEADS
    last = 0 if backward else tm - 1
    src = lax.broadcasted_iota(jnp.int32, (tm, tm), 0)
    tgt = lax.broadcasted_iota(jnp.int32, (tm, tm), 1)
    visible = (src >= tgt) if backward else (src <= tgt)
    neg_mask = jnp.where(visible, 0.0, -jnp.inf)
    xc = gc_ref[...] + bc_ref[...]
    xr = gr_ref[...] + br_ref[...]
    before = (src <= tgt) if backward else (src >= tgt)
    cum_c = _split_dot(jnp.where(before, 1.0, 0.0).astype(BF16), jax.nn.log_sigmoid(xc), left=True)
    cum_r = _split_dot(jnp.where(visible, 1.0, 0.0).astype(BF16), jax.nn.log_sigmoid(xr), left=False)
    b_row = cum_r[gate_f:gate_f + HEADS, :]
    c_row = xr[gate_i:gate_i + HEADS, :] - b_row
    c_col = xc - pltpu.roll(cum_c, LANES - HEADS, axis=1)
    lane = lax.broadcasted_iota(jnp.int32, (HEADS, tm), 1)
    run = c_row
    shift = 1
    while shift < tm:
        if backward:
            moved = jnp.where(lane + shift < tm, pltpu.roll(run, tm - shift, axis=1), -jnp.inf)
        else:
            moved = jnp.where(lane >= shift, pltpu.roll(run, shift, axis=1), -jnp.inf)
        run = jnp.maximum(run, moved)
        shift *= 2
    m = m_ref[:, 0:1]
    log_inter = b_row + m
    m_pos = jnp.maximum(log_inter, b_row + run)
    shift_row = b_row - m_pos
    w_inter = jnp.exp(log_inter - m_pos)
    floor = jnp.exp(-m_pos)
    b_last = b_row[:, last:last + 1]
    m_new = m_pos[:, last:last + 1]
    w_src = jnp.exp(b_last + c_row - m_new)
    decay = jnp.exp(b_last + m - m_new)
    m_ref[...] = jnp.broadcast_to(m_new, m_ref.shape)
    half = lax.broadcasted_iota(jnp.int32, (tm, LANES), 1) < HEAD_QK
    ones = jnp.ones((2 * SUBLANES, tm), BF16)
    for h in range(HEADS):
        pair = slice((h // 2) * LANES, (h // 2 + 1) * LANES)
        qp = q_ref[:, pair]
        km = jnp.where(half if h % 2 == 0 else jnp.logical_not(half), k_ref[:, pair], 0.0).astype(BF16)
        st = lax.dot_general(km, qp, _NT, preferred_element_type=F32)
        e = (neg_mask + shift_row[h:h + 1, :]) + c_col[:, gate_i + h:gate_i + h + 1]
        p = (st * jnp.exp(e)).astype(BF16)
        vaug = jnp.concatenate([vt_ref[h * HEAD_V:(h + 1) * HEAD_V, :], ones], axis=0)
        ct = ct_ref[h]
        acc = jnp.dot(vaug, p, preferred_element_type=F32)
        acc = acc + w_inter[h:h + 1, :] * lax.dot_general(ct.astype(BF16), qp, _NT,
                                                          preferred_element_type=F32)
        den = acc[HEAD_V:HEAD_V + 1, :]
        scale = 1.0 / jnp.maximum(jnp.abs(den), floor[h:h + 1, :])
        o_ref[h * HEAD_V:(h + 1) * HEAD_V, :] = (acc[0:HEAD_V, :] * scale).astype(o_ref.dtype)
        vs = (vaug.astype(F32) * w_src[h:h + 1, :]).astype(BF16)
        ct_ref[h] = decay[h:h + 1, :] * ct + jnp.dot(vs, km, preferred_element_type=F32)


def _split_dot(ones_mat, x, left):
    hi = x.astype(BF16)
    r1 = x - hi.astype(F32)
    mid = r1.astype(BF16)
    lo = (r1 - mid.astype(F32)).astype(BF16)
    out = None
    for piece in (lo, mid, hi):
        t = (jnp.dot(ones_mat, piece, preferred_element_type=F32) if left
             else jnp.dot(piece, ones_mat, preferred_element_type=F32))
        out = t if out is None else out + t
    return out


def _mlstm_kernel(qf, kf, vtf, gcf, grf, qb, kb, vtb, gcb, grb, bc_ref, br_ref, of_ref, ob_ref,
                  cf_ref, mf_ref, cb_ref, mb_ref, *, tiles_per_batch):
    @pl.when(pl.program_id(0) % tiles_per_batch == 0)
    def _():
        cf_ref[...] = jnp.zeros_like(cf_ref)
        cb_ref[...] = jnp.zeros_like(cb_ref)
        mf_ref[...] = jnp.zeros_like(mf_ref)
        mb_ref[...] = jnp.zeros_like(mb_ref)

    _mlstm_direction(qf, kf, vtf, gcf, grf, bc_ref, br_ref, of_ref, cf_ref, mf_ref, backward=False)
    _mlstm_direction(qb, kb, vtb, gcb, grb, bc_ref, br_ref, ob_ref, cb_ref, mb_ref, backward=True)


def _mlstm_call(q, k, vt, gc, gr, bias_c, bias_r, dims):
    tm, d, n = TOKEN_TILE, dims["d"], dims["n"]
    tpb, ct = dims["tpb"], dims["ct"]
    nq = q.shape[1]
    fwd = lambda i: (i, 0)
    bwd = lambda i: (_bwd_tile(i, tpb, ct), 0)
    fwd_t = lambda i: (0, i)
    bwd_t = lambda i: (0, _bwd_tile(i, tpb, ct))
    const = lambda i: (0, 0)
    state = [pltpu.VMEM((HEADS, HEAD_V + 2 * SUBLANES, LANES), F32), pltpu.VMEM((HEADS, LANES), F32)]
    return pl.pallas_call(
        functools.partial(_mlstm_kernel, tiles_per_batch=tpb),
        grid=(dims["tiles"],),
        in_specs=[
            pl.BlockSpec((tm, nq), fwd), pl.BlockSpec((tm, nq), fwd), pl.BlockSpec((d, tm), fwd_t),
            pl.BlockSpec((tm, LANES), fwd), pl.BlockSpec((4 * HEADS, tm), fwd_t),
            pl.BlockSpec((tm, nq), bwd), pl.BlockSpec((tm, nq), bwd), pl.BlockSpec((d, tm), bwd_t),
            pl.BlockSpec((tm, LANES), bwd), pl.BlockSpec((4 * HEADS, tm), bwd_t),
            pl.BlockSpec((1, LANES), const), pl.BlockSpec((4 * HEADS, 1), const),
        ],
        out_specs=[pl.BlockSpec((d, tm), fwd_t), pl.BlockSpec((d, tm), bwd_t)],
        out_shape=[jax.ShapeDtypeStruct((d, n), BF16), jax.ShapeDtypeStruct((d, n), BF16)],
        scratch_shapes=state + state,
        compiler_params=_params("arbitrary"),
        name="mlstm_scan",
    )(q, k, vt, gc, gr, q, k, vt, gc, gr, bias_c, bias_r)


def _mreadout_kernel(hf_ref, hb_ref, ot_ref, w_ref, x_ref, mod_ref, o_ref):
    d = x_ref.shape[-1]
    parts = []
    for h in range(HEADS):
        rows = slice(h * HEAD_V, (h + 1) * HEAD_V)
        hs = hf_ref[rows, :].astype(F32) + hb_ref[rows, :].astype(F32)
        r = lax.rsqrt(jnp.mean(hs * hs, axis=0, keepdims=True) + NORM_EPS)
        parts.append((hs * r * ot_ref[rows, :]).astype(BF16))
    t = jnp.concatenate(parts, axis=0)
    out = lax.dot_general(t, w_ref[...], _TN, preferred_element_type=F32)
    o_ref[...] = x_ref[...] + mod_ref[:, 2 * d:3 * d] * out


def _mreadout_call(hf, hb, ot, w_out, xs, mods, layer, dims):
    tm, d, n = TOKEN_TILE, dims["d"], dims["n"]
    row = lambda i: (i, 0)
    col = lambda i: (0, i)
    const = lambda i: (0, 0)
    return pl.pallas_call(
        _mreadout_kernel,
        grid=(dims["tiles"],),
        in_specs=[
            pl.BlockSpec((d, tm), col), pl.BlockSpec((d, tm), col), pl.BlockSpec((d, tm), col),
            pl.BlockSpec((d, d), const), pl.BlockSpec((tm, d), row), _mod_spec(layer, dims),
        ],
        out_specs=pl.BlockSpec((tm, d), row),
        out_shape=jax.ShapeDtypeStruct((n, d), F32),
        compiler_params=_params("arbitrary"),
        name="mlstm_readout",
    )(hf, hb, ot, w_out, xs, mods)


def _readout_kernel(hf_ref, hb_ref, gy_ref, w_ref, x_ref, mod_ref, o_ref):
    d = x_ref.shape[-1]
    t = _gelu_tanh(gy_ref[...].astype(F32)) * (hf_ref[...].astype(F32) + hb_ref[...].astype(F32))
    out = jnp.dot(t.astype(BF16), w_ref[...], preferred_element_type=F32)
    o_ref[...] = x_ref[...] + mod_ref[:, 2 * d:3 * d] * out


def _readout_call(hf, hb, gy, w_out, xs, mods, layer, dims):
    tm, d, n = TOKEN_TILE, dims["d"], dims["n"]
    row = lambda i: (i, 0)
    const = lambda i: (0, 0)
    return pl.pallas_call(
        _readout_kernel,
        grid=(dims["tiles"],),
        in_specs=[
            pl.BlockSpec((tm, d), row), pl.BlockSpec((tm, d), row), pl.BlockSpec((tm, d), row),
            pl.BlockSpec((d, d), const), pl.BlockSpec((tm, d), row), _mod_spec(layer, dims),
        ],
        out_specs=pl.BlockSpec((tm, d), row),
        out_shape=jax.ShapeDtypeStruct((n, d), F32),
        compiler_params=_params("arbitrary"),
        name="rglru_readout",
    )(hf, hb, gy, w_out, xs, mods)


def _rproj_kernel(x_ref, g_ref, mod_ref, w_ref, cw_ref, cb_ref, gy_ref, xc_ref, ext_ref, cur_ref,
                  *, tiles, tiles_per_batch, ctx_tiles):
    tm, d = x_ref.shape
    i = pl.program_id(0)

    def in_segment_neighbours(t):
        j = t % tiles_per_batch
        first = jnp.logical_or(j == 0, j == ctx_tiles)
        last = jnp.logical_or(j == ctx_tiles - 1, j == tiles_per_batch - 1)
        return jnp.logical_not(first), jnp.logical_not(last)

    @pl.when(i == 0)
    def _():
        ext_ref[...] = jnp.zeros_like(ext_ref)
        cur_ref[...] = jnp.zeros_like(cur_ref)

    @pl.when(i < tiles)
    def _():
        h = _modulated(x_ref[...], g_ref[...], mod_ref[...], 0).astype(BF16)
        gy_ref[...] = jnp.dot(h, w_ref[:, 0:d], preferred_element_type=F32).astype(BF16)
        cur_ref[...] = jnp.dot(h, w_ref[:, d:2 * d], preferred_element_type=F32)

    @pl.when(i > 0)
    def _():
        _, next_ok = in_segment_neighbours(i - 1)
        head_ok = jnp.logical_and(next_ok, i < tiles)
        ext_ref[SUBLANES + tm:2 * SUBLANES + tm, :] = jnp.where(head_ok, cur_ref[0:SUBLANES, :], 0.0)
        xc = cb_ref[...] + ext_ref[pl.ds(SUBLANES - CONV_LEFT, tm), :] * cw_ref[0:1, :]
        for k in range(1, CONV_TAPS):
            xc = xc + ext_ref[pl.ds(SUBLANES - CONV_LEFT + k, tm), :] * cw_ref[k:k + 1, :]
        xc_ref[...] = xc

    prev_ok, _ = in_segment_neighbours(i)
    ext_ref[0:SUBLANES, :] = jnp.where(prev_ok, ext_ref[tm:tm + SUBLANES, :], 0.0)
    ext_ref[SUBLANES:SUBLANES + tm, :] = cur_ref[...]


def _rproj_call(xs, g, mods, layer, w_in, conv_w, conv_b, dims):
    tm, d, n, tiles = TOKEN_TILE, dims["d"], dims["n"], dims["tiles"]
    tpb, ct, b = dims["tpb"], dims["ct"], dims["b"]
    cur = lambda i: (jnp.minimum(i, tiles - 1), 0)
    lag = lambda i: (jnp.maximum(i - 1, 0), 0)
    const = lambda i: (0, 0)
    mod_spec = pl.BlockSpec((None, None, 1, 6 * d),
                            lambda i: (layer, _mod_row(jnp.minimum(i, tiles - 1), tpb, ct, b), 0, 0))
    return pl.pallas_call(
        functools.partial(_rproj_kernel, tiles=tiles, tiles_per_batch=tpb, ctx_tiles=ct),
        grid=(tiles + 1,),
        in_specs=[pl.BlockSpec((tm, d), cur), pl.BlockSpec((1, d), const), mod_spec,
                  pl.BlockSpec(w_in.shape, const), pl.BlockSpec(conv_w.shape, const),
                  pl.BlockSpec(conv_b.shape, const)],
        out_specs=[pl.BlockSpec((tm, d), cur), pl.BlockSpec((tm, d), lag)],
        out_shape=[jax.ShapeDtypeStruct((n, d), BF16), jax.ShapeDtypeStruct((n, d), F32)],
        scratch_shapes=[pltpu.VMEM((tm + 2 * SUBLANES, d), F32), pltpu.VMEM((tm, d), F32)],
        compiler_params=_params("arbitrary"),
        name="rglru_proj",
    )(xs, g, mods, w_in, conv_w, conv_b)


def _sigmoid_tanh(x):
    return 0.5 * jnp.tanh(0.5 * x) + 0.5


def _lru_direction(xc_ref, wg_ref, bg_ref, lam_ref, a_ref, b_ref, *, direction):
    tm, d = xc_ref.shape
    bw = d // LRU_BLOCKS
    z = -lam_ref[direction]
    softplus = jnp.maximum(z, 0.0) + jnp.log(1.0 + jnp.exp(-jnp.abs(z)))
    for nb in range(LRU_BLOCKS):
        sl = slice(nb * bw, (nb + 1) * bw)
        xb = xc_ref[:, sl]
        gg = jnp.dot(xb.astype(BF16), wg_ref[direction, nb], preferred_element_type=F32)
        gg = gg + bg_ref[direction, :, nb * 2 * bw:(nb + 1) * 2 * bw]
        r = _sigmoid_tanh(gg[:, :bw])
        gi = _sigmoid_tanh(gg[:, bw:])
        log_a = -LRU_C * r * softplus[:, sl]
        th = jnp.tanh(log_a)
        a_ref[pl.ds(nb, tm, stride=LRU_BLOCKS), :] = jnp.exp(log_a)
        b_ref[pl.ds(nb, tm, stride=LRU_BLOCKS), :] = jnp.sqrt(-2.0 * th / (1.0 - th)) * (gi * xb)


def _lru_kernel(xc_f, xc_b, wg_ref, bg_ref, lam_ref, of_ref, ob_ref, af_ref, bf_ref, ab_ref, bb_ref,
                hf_ref, hb_ref, h_ref, *, tiles_per_batch):
    tm, d = xc_f.shape
    bw = d // LRU_BLOCKS

    @pl.when(pl.program_id(0) % tiles_per_batch == 0)
    def _():
        h_ref[...] = jnp.zeros_like(h_ref)

    _lru_direction(xc_f, wg_ref, bg_ref, lam_ref, af_ref, bf_ref, direction=0)
    _lru_direction(xc_b, wg_ref, bg_ref, lam_ref, ab_ref, bb_ref, direction=1)

    def step(t, carry):
        hf, hb = carry
        rf = pl.multiple_of(t * LRU_BLOCKS, LRU_BLOCKS)
        rb = pl.multiple_of((tm - 1 - t) * LRU_BLOCKS, LRU_BLOCKS)
        hf = af_ref[pl.ds(rf, LRU_BLOCKS), :] * hf + bf_ref[pl.ds(rf, LRU_BLOCKS), :]
        hb = ab_ref[pl.ds(rb, LRU_BLOCKS), :] * hb + bb_ref[pl.ds(rb, LRU_BLOCKS), :]
        hf_ref[pl.ds(rf, LRU_BLOCKS), :] = hf
        hb_ref[pl.ds(rb, LRU_BLOCKS), :] = hb
        return hf, hb

    hf, hb = lax.fori_loop(0, tm, step, (h_ref[0], h_ref[1]), unroll=8)
    h_ref[0] = hf
    h_ref[1] = hb
    for nb in range(LRU_BLOCKS):
        sl = slice(nb * bw, (nb + 1) * bw)
        of_ref[:, sl] = hf_ref[pl.ds(nb, tm, stride=LRU_BLOCKS), :].astype(BF16)
        ob_ref[:, sl] = hb_ref[pl.ds(nb, tm, stride=LRU_BLOCKS), :].astype(BF16)


def _lru_call(xc, w_gate, b_gate, lam, dims):
    tm, d, n = TOKEN_TILE, dims["d"], dims["n"]
    tpb, ct = dims["tpb"], dims["ct"]
    fwd = lambda i: (i, 0)
    bwd = lambda i: (_bwd_tile(i, tpb, ct), 0)
    c3 = lambda i: (0, 0, 0)
    c4 = lambda i: (0, 0, 0, 0)
    time_major = pltpu.VMEM((tm * LRU_BLOCKS, d // LRU_BLOCKS), F32)
    return pl.pallas_call(
        functools.partial(_lru_kernel, tiles_per_batch=tpb),
        grid=(dims["tiles"],),
        in_specs=[
            pl.BlockSpec((tm, d), fwd), pl.BlockSpec((tm, d), bwd),
            pl.BlockSpec(w_gate.shape, c4), pl.BlockSpec(b_gate.shape, c3), pl.BlockSpec(lam.shape, c3),
        ],
        out_specs=[pl.BlockSpec((tm, d), fwd), pl.BlockSpec((tm, d), bwd)],
        out_shape=[jax.ShapeDtypeStruct((n, d), BF16), jax.ShapeDtypeStruct((n, d), BF16)],
        scratch_shapes=[time_major] * 6 + [pltpu.VMEM((2, LRU_BLOCKS, d // LRU_BLOCKS), F32)],
        compiler_params=_params("arbitrary"),
        name="rglru_scan",
    )(xc, xc, w_gate, b_gate, lam)


def _first_argmax(vals):
    best = vals[0]
    idx = jnp.zeros_like(best)
    for k in range(1, len(vals)):
        better = vals[k] > best
        idx = jnp.where(better, float(k), idx)
        best = jnp.where(better, vals[k], best)
    return idx, best


def _pick(idx, vals):
    out = vals[0]
    for k in range(1, len(vals)):
        out = jnp.where(idx == float(k), vals[k], out)
    return out


def _router_kernel(x_ref, g_ref, mod_ref, wr_ref, br_ref, hx_ref, route_ref, cnt_ref, carry_ref):
    tm, d = x_ref.shape

    @pl.when(pl.program_id(0) == 0)
    def _():
        carry_ref[...] = jnp.zeros_like(carry_ref)

    h = _modulated(x_ref[...], g_ref[...], mod_ref[...], 3)
    hx_ref[:, 0:d] = h
    logits = lax.dot_general(wr_ref[...], h, (((1,), (1,)), ((), ())),
                             preferred_element_type=F32, precision=HIGHEST)
    e = jnp.exp(logits - jnp.max(logits, axis=0, keepdims=True))
    scores = e / jnp.sum(e, axis=0, keepdims=True)
    sel = scores + br_ref[...]
    sel_rows = [sel[k:k + 1, :] for k in range(N_EXPERTS)]
    score_rows = [scores[k:k + 1, :] for k in range(N_EXPERTS)]
    group_scores = []
    for gi in range(N_GROUPS):
        v = sel_rows[gi * GROUP_SIZE:(gi + 1) * GROUP_SIZE]
        best = v[0] + v[1]
        for a, b in PAIRS[1:]:
            best = jnp.maximum(best, v[a] + v[b])
        group_scores.append(best)
    grp, _ = _first_argmax(group_scores)
    in_sel = [_pick(grp, [sel_rows[gi * GROUP_SIZE + k] for gi in range(N_GROUPS)])
              for k in range(GROUP_SIZE)]
    in_score = [_pick(grp, [score_rows[gi * GROUP_SIZE + k] for gi in range(N_GROUPS)])
                for k in range(GROUP_SIZE)]
    i1, _ = _first_argmax(in_sel)
    rest = [jnp.where(i1 == float(k), -jnp.inf, in_sel[k]) for k in range(GROUP_SIZE)]
    i2, _ = _first_argmax(rest)
    lo = jnp.minimum(i1, i2)
    hi = jnp.maximum(i1, i2)
    s_lo = _pick(lo, in_score)
    s_hi = _pick(hi, in_score)
    tot = s_lo + s_hi
    w_lo = s_lo / tot
    w_hi = s_hi / tot
    pair = jnp.where(lo == 0.0, hi - 1.0, jnp.where(lo == 1.0, hi + 1.0, 5.0))
    cls = grp * float(len(PAIRS)) + pair
    crow = lax.broadcasted_iota(jnp.int32, (CLASS_ROWS, tm), 0).astype(F32)
    onehot = jnp.where(crow == cls, 1.0, 0.0)
    rows = lax.broadcasted_iota(jnp.int32, (tm, tm), 0)
    cols = lax.broadcasted_iota(jnp.int32, (tm, tm), 1)
    upper = jnp.where(rows <= cols, 1.0, 0.0).astype(BF16)
    cum = jnp.dot(onehot.astype(BF16), upper, preferred_element_type=F32)
    carry = carry_ref[:, 0:1]
    rank = jnp.sum(onehot * (cum - 1.0 + carry), axis=0, keepdims=True)
    new_carry = carry + jnp.sum(onehot, axis=1, keepdims=True)
    carry_ref[...] = jnp.broadcast_to(new_carry, carry_ref.shape)
    cnt_ref[...] = jnp.broadcast_to(new_carry, cnt_ref.shape)
    zero = jnp.zeros_like(cls)
    route_ref[...] = jnp.concatenate([cls, rank, w_lo, w_hi, zero, zero, zero, zero], axis=0)
    eye = rows == cols
    wl_col = jnp.sum(jnp.where(eye, w_lo, 0.0), axis=1, keepdims=True)
    wh_col = jnp.sum(jnp.where(eye, w_hi, 0.0), axis=1, keepdims=True)
    lane = lax.broadcasted_iota(jnp.int32, (tm, ROUTE_COLS), 1)
    hx_ref[:, d:d + ROUTE_COLS] = jnp.where(lane == 0, wl_col, jnp.where(lane == 1, wh_col, 0.0))


def _router_call(xs, g, mods, layer, w_router_t, b_router_col, dims):
    tm, d, n = TOKEN_TILE, dims["d"], dims["n"]
    row = lambda i: (i, 0)
    const = lambda i: (0, 0)
    return pl.pallas_call(
        _router_kernel,
        grid=(dims["tiles"],),
        in_specs=[pl.BlockSpec((tm, d), row), pl.BlockSpec((1, d), const), _mod_spec(layer, dims),
                  pl.BlockSpec(w_router_t.shape, const), pl.BlockSpec(b_router_col.shape, const)],
        out_specs=[pl.BlockSpec((tm, d + ROUTE_COLS), row),
                   pl.BlockSpec((SUBLANES, tm), lambda i: (0, i)),
                   pl.BlockSpec((CLASS_ROWS, LANES), const)],
        out_shape=[jax.ShapeDtypeStruct((n, d + ROUTE_COLS), F32),
                   jax.ShapeDtypeStruct((SUBLANES, n), F32),
                   jax.ShapeDtypeStruct((CLASS_ROWS, LANES), F32)],
        scratch_shapes=[pltpu.VMEM((CLASS_ROWS, LANES), F32)],
        compiler_params=_params("arbitrary"),
        name="router",
    )(xs, g, mods, w_router_t, b_router_col)


ROW_DMA_UNROLL = 8


def _dispatch_kernel(pos_ref, hx_ref, init_ref, o_ref, stage_ref, sems):
    del init_ref
    tm = hx_ref.shape[0]
    i = pl.program_id(0)
    slot = i % 2
    base = i * tm
    stage_ref[slot] = hx_ref[...]

    def issue(r, carry):
        pltpu.make_async_copy(stage_ref.at[slot, pl.ds(r, 1)], o_ref.at[pl.ds(pos_ref[base + r], 1)],
                              sems.at[slot]).start()
        return carry

    lax.fori_loop(0, tm, issue, 0, unroll=ROW_DMA_UNROLL)

    def wait_tile(s):
        pltpu.make_async_copy(stage_ref.at[s], o_ref.at[pl.ds(0, tm)], sems.at[s]).wait()

    @pl.when(i > 0)
    def _():
        wait_tile(1 - slot)

    @pl.when(i == pl.num_programs(0) - 1)
    def _():
        wait_tile(slot)


def _dispatch_call(pos, hx, padded_rows, dims):
    tm = TOKEN_TILE
    width = hx.shape[1]
    init = jnp.zeros((padded_rows, width), F32)
    return pl.pallas_call(
        _dispatch_kernel,
        grid_spec=pltpu.PrefetchScalarGridSpec(
            num_scalar_prefetch=1,
            grid=(dims["tiles"],),
            in_specs=[pl.BlockSpec((tm, width), lambda i, p: (i, 0)), pl.BlockSpec(memory_space=pl.ANY)],
            out_specs=pl.BlockSpec(memory_space=pl.ANY),
            scratch_shapes=[pltpu.VMEM((2, tm, width), F32), pltpu.SemaphoreType.DMA((2,))],
        ),
        out_shape=jax.ShapeDtypeStruct((padded_rows, width), F32),
        input_output_aliases={2: 0},
        compiler_params=_params("arbitrary"),
        name="dispatch",
    )(pos, hx, init)


def _expert_kernel(ea_ref, eb_ref, ok_ref, x_ref, w1a, w3a, w2a, w1b, w3b, w2b, o_ref):
    del ea_ref, eb_ref
    d = o_ref.shape[-1]
    t = pl.program_id(0)

    @pl.when(ok_ref[t] != 0)
    def _():
        x = x_ref[:, 0:d].astype(BF16)

        def expert(w1, w3, w2):
            a = jnp.dot(x, w1[...], preferred_element_type=F32)
            b = jnp.dot(x, w3[...], preferred_element_type=F32)
            u = (a * jax.nn.sigmoid(a)) * b
            return jnp.dot(u.astype(BF16), w2[...], preferred_element_type=F32)

        o_ref[...] = (x_ref[:, d:d + 1] * expert(w1a, w3a, w2a)
                      + x_ref[:, d + 1:d + 2] * expert(w1b, w3b, w2b))

    @pl.when(ok_ref[t] == 0)
    def _():
        o_ref[...] = jnp.zeros_like(o_ref)


def _expert_call(tile_a, tile_b, tile_ok, xsorted, w1, w3, w2, d):
    te = TOKEN_TILE
    rows, width = xsorted.shape
    de = w1.shape[-1]
    sel_a = lambda t, ea, eb, ok: (ea[t], 0, 0)
    sel_b = lambda t, ea, eb, ok: (eb[t], 0, 0)
    return pl.pallas_call(
        _expert_kernel,
        grid_spec=pltpu.PrefetchScalarGridSpec(
            num_scalar_prefetch=3,
            grid=(rows // te,),
            in_specs=[
                pl.BlockSpec((te, width), lambda t, ea, eb, ok: (t, 0)),
                pl.BlockSpec((None, d, de), sel_a), pl.BlockSpec((None, d, de), sel_a),
                pl.BlockSpec((None, de, d), sel_a),
                pl.BlockSpec((None, d, de), sel_b), pl.BlockSpec((None, d, de), sel_b),
                pl.BlockSpec((None, de, d), sel_b),
            ],
            out_specs=pl.BlockSpec((te, d), lambda t, ea, eb, ok: (t, 0)),
        ),
        out_shape=jax.ShapeDtypeStruct((rows, d), F32),
        compiler_params=_params("arbitrary"),
        name="experts",
    )(tile_a, tile_b, tile_ok, xsorted, w1, w3, w2, w1, w3, w2)


def _combine_kernel(pos_ref, y_ref, x_ref, mod_ref, gf_ref, o_ref, buf_ref, sems,
                    *, tile_of, final_norm):
    tm, d = x_ref.shape
    i = pl.program_id(0)
    slot = i % 2

    def gather(step, s):
        base = tile_of(step) * tm

        def issue(r, carry):
            pltpu.make_async_copy(y_ref.at[pl.ds(pos_ref[base + r], 1)], buf_ref.at[s, pl.ds(r, 1)],
                                  sems.at[s]).start()
            return carry

        lax.fori_loop(0, tm, issue, 0, unroll=ROW_DMA_UNROLL)

    @pl.when(i == 0)
    def _():
        gather(0, 0)

    @pl.when(i + 1 < pl.num_programs(0))
    def _():
        gather(i + 1, 1 - slot)

    pltpu.make_async_copy(y_ref.at[pl.ds(0, tm)], buf_ref.at[slot], sems.at[slot]).wait()
    out = x_ref[...] + mod_ref[:, 5 * d:6 * d] * buf_ref[slot]
    if final_norm:
        ms = jnp.mean(out * out, axis=-1, keepdims=True)
        out = out * lax.rsqrt(ms + NORM_EPS) * gf_ref[...]
    o_ref[...] = out


def _combine_call(pos, ysorted, xs, mods, layer, g_final, dims, final_norm):
    tm, d = TOKEN_TILE, dims["d"]
    tpb, ct, b = dims["tpb"], dims["ct"], dims["b"]
    if final_norm:
        lt = tpb - ct
        tile_of = lambda i: (i // lt) * tpb + ct + i % lt
        n_tiles = b * lt
    else:
        tile_of = lambda i: i
        n_tiles = dims["tiles"]
    mod_spec = pl.BlockSpec((None, None, 1, 6 * d),
                            lambda i, p: (layer, _mod_row(tile_of(i), tpb, ct, b), 0, 0))
    return pl.pallas_call(
        functools.partial(_combine_kernel, tile_of=tile_of, final_norm=final_norm),
        grid_spec=pltpu.PrefetchScalarGridSpec(
            num_scalar_prefetch=1,
            grid=(n_tiles,),
            in_specs=[
                pl.BlockSpec(memory_space=pl.ANY),
                pl.BlockSpec((tm, d), lambda i, p: (tile_of(i), 0)),
                mod_spec,
                pl.BlockSpec((1, d), lambda i, p: (0, 0)),
            ],
            out_specs=pl.BlockSpec((tm, d), lambda i, p: (i, 0)),
            scratch_shapes=[pltpu.VMEM((2, tm, d), F32), pltpu.SemaphoreType.DMA((2,))],
        ),
        out_shape=jax.ShapeDtypeStruct((n_tiles * tm, d), F32),
        compiler_params=_params("arbitrary"),
        name="combine",
    )(pos, ysorted, xs, mods, g_final)


_CLASS_A = np.array([g * GROUP_SIZE + a for g in range(N_GROUPS) for a, _ in PAIRS], np.int32)
_CLASS_B = np.array([g * GROUP_SIZE + b for g in range(N_GROUPS) for _, b in PAIRS], np.int32)


def _routing_plan(route, counts, n_tiles_padded):
    te = TOKEN_TILE
    cls = route[0].astype(jnp.int32)
    rank = route[1].astype(jnp.int32)
    cnt = counts[:N_CLASSES, 0].astype(jnp.int32)
    tiles = (cnt + te - 1) // te
    tile_end = jnp.cumsum(tiles)
    tile_start = tile_end - tiles
    pos = jnp.take(tile_start * te, cls) + rank
    t = jnp.arange(n_tiles_padded, dtype=jnp.int32)
    ok = t < tile_end[-1]
    tcls = jnp.sum((tile_end[None, :] <= jnp.minimum(t, tile_end[-1] - 1)[:, None]).astype(jnp.int32), axis=1)
    tile_a = jnp.take(jnp.asarray(_CLASS_A), tcls)
    tile_b = jnp.take(jnp.asarray(_CLASS_B), tcls)
    return pos, tile_a, tile_b, ok.astype(jnp.int32)


def _moe(xs, g, mods, layer, w_router_t, b_router_col, w1, w3, w2, g_final, dims, final_norm):
    hx, route, counts = _router_call(xs, g, mods, layer, w_router_t, b_router_col, dims)
    n_tiles_padded = dims["tiles"] + N_CLASSES
    pos, tile_a, tile_b, tile_ok = _routing_plan(route, counts, n_tiles_padded)
    xsorted = _dispatch_call(pos, hx, n_tiles_padded * TOKEN_TILE, dims)
    ysorted = _expert_call(tile_a, tile_b, tile_ok, xsorted, w1, w3, w2, dims["d"])
    return _combine_call(pos, ysorted, xs, mods, layer, g_final, dims, final_norm)


def _pos_table(n_tokens, dim):
    rows = n_tokens // GRID_WIDTH
    r, col = jnp.meshgrid(jnp.arange(rows, dtype=F32), jnp.arange(GRID_WIDTH, dtype=F32), indexing="ij")
    quarter = dim // 4
    freqs = jnp.exp(-math.log(POS_BASE) * jnp.arange(quarter, dtype=F32) / quarter)

    def enc(p):
        ang = p.reshape(-1, 1) * freqs
        return jnp.concatenate([jnp.sin(ang), jnp.cos(ang)], axis=-1)

    return jnp.concatenate([enc(r), enc(col)], axis=-1)


def _mlstm_weights(w_in, b_gate):
    nq = HEADS * HEAD_QK
    nv = HEADS * HEAD_V
    wq = w_in[:, 0:nq] * (HEAD_QK ** -0.5)
    wk = w_in[:, nq:2 * nq]
    wv = w_in[:, 2 * nq:2 * nq + nv]
    wo = w_in[:, 2 * nq + nv:2 * nq + 2 * nv]
    wg = w_in[:, 2 * nq + 2 * nv:]
    wg_pad = jnp.pad(wg, ((0, 0), (0, LANES - wg.shape[1])))
    w_tok = jnp.concatenate([wq, wk, wg_pad], axis=1).astype(BF16)
    w_feat = jnp.concatenate([wv, wo, wg], axis=1).T.astype(BF16)
    bias = b_gate.reshape(-1).astype(F32)
    bias_c = jnp.pad(bias, (0, LANES - bias.shape[0])).reshape(1, LANES)
    bias_r = bias.reshape(-1, 1)
    return w_tok, w_feat, bias_c, bias_r


def kernel(x, c, ctx, c_ctx, w_ada, b_ada, g_mix, g_ffn, g_final, m_w_in, m_b_gate, m_g_head, m_w_out, r_w_in, r_conv_w, r_conv_b, r_w_gate, r_b_gate, r_lam, r_w_out, w_router, b_router, e_w1, e_w3, e_w2):
    batch, t_len, d = x.shape
    ctx_len = ctx.shape[1]
    depth = w_ada.shape[0]
    tm = TOKEN_TILE
    assert t_len % tm == 0 and ctx_len % tm == 0 and batch + 1 <= ADA_ROWS
    assert d == HEADS * HEAD_V and t_len % GRID_WIDTH == 0
    s_len = ctx_len + t_len
    dims = dict(b=batch, d=d, n=batch * s_len, tpb=s_len // tm, ct=ctx_len // tm,
                tiles=batch * s_len // tm)

    cvec = jnp.concatenate([c, c_ctx[None, :], jnp.zeros((ADA_ROWS - batch - 1, d), F32)], axis=0)
    mods = _ada_call(cvec, w_ada, b_ada).reshape(depth, ADA_ROWS, 1, 6 * d)
    xs = _stream_call(ctx.reshape(batch * ctx_len, d), x.reshape(batch * t_len, d),
                      _pos_table(t_len, d), dims)

    w_router_t = w_router.T.astype(F32)
    b_router_col = b_router.reshape(-1, 1).astype(F32)
    g_final2 = g_final.reshape(1, d)
    out = None
    for i in range(depth):
        j = i // 2
        g_mix_i = g_mix[i].reshape(1, d)
        if i % 2 == 0:
            w_tok, w_feat, bias_c, bias_r = _mlstm_weights(m_w_in[j], m_b_gate[j])
            q, k, vt, ot, gc, gr = _mproj_call(xs, g_mix_i, mods, i, w_tok, w_feat, dims)
            hf, hb = _mlstm_call(q, k, vt, gc, gr, bias_c, bias_r, dims)
            w_out = (m_g_head[j][:, None] * m_w_out[j]).astype(BF16)
            xs = _mreadout_call(hf, hb, ot, w_out, xs, mods, i, dims)
        else:
            gy, xc = _rproj_call(xs, g_mix_i, mods, i, r_w_in[j].astype(BF16), r_conv_w[j],
                                 r_conv_b[j].reshape(1, d), dims)
            hf, hb = _lru_call(xc, r_w_gate[j].astype(BF16), r_b_gate[j].reshape(2, 1, -1),
                               r_lam[j].reshape(2, 1, d), dims)
            xs = _readout_call(hf, hb, gy, r_w_out[j].astype(BF16), xs, mods, i, dims)
        last = i == depth - 1
        res = _moe(xs, g_ffn[i].reshape(1, d), mods, i, w_router_t, b_router_col,
                   e_w1[i].astype(BF16), e_w3[i].astype(BF16), e_w2[i].astype(BF16),
                   g_final2, dims, last)
        if last:
            out = res
        else:
            xs = res
    return out.reshape(batch, t_len, d)
```

```python
import functools
import math

import jax
import jax.numpy as jnp
import numpy as np
from jax import lax
from jax.experimental import pallas as pl
from jax.experimental.pallas import tpu as pltpu

F32 = jnp.float32
BF16 = jnp.bfloat16
HIGHEST = lax.Precision.HIGHEST

TOKEN_TILE = 256
LANES = 128
SUBLANES = 8
VMEM_LIMIT_BYTES = 56 * 1024 * 1024
NORM_EPS = 1e-6
GRID_WIDTH = 64
POS_BASE = 10000.0
HEADS = 8
HEAD_QK = 64
HEAD_V = 128
LRU_BLOCKS = 8
LRU_C = 8.0
CONV_TAPS = 4
CONV_LEFT = CONV_TAPS // 2
N_EXPERTS = 16
N_GROUPS = 4
GROUP_SIZE = N_EXPERTS // N_GROUPS
PAIRS = [(a, b) for a in range(GROUP_SIZE) for b in range(a + 1, GROUP_SIZE)]
PAIR_SLOTS = [(0, 1), (0, 2), (0, 3), (1, 3), (1, 2), (3, 2)]
N_CLASSES = N_GROUPS * len(PAIRS)
CLASS_ROWS = 32
ROUTE_COLS = LANES
ADA_ROWS = 16


def _params(*sem):
    return pltpu.CompilerParams(dimension_semantics=sem, vmem_limit_bytes=VMEM_LIMIT_BYTES)


def _mod_row(i, tiles_per_batch, ctx_tiles, batch):
    return jnp.where(i % tiles_per_batch < ctx_tiles, batch, i // tiles_per_batch)


def _bwd_tile(i, tiles_per_batch, ctx_tiles):
    b = i // tiles_per_batch
    j = i % tiles_per_batch
    jb = jnp.where(j < ctx_tiles, ctx_tiles - 1 - j, tiles_per_batch - 1 - (j - ctx_tiles))
    return b * tiles_per_batch + jb


def _ada_kernel(c_ref, w_ref, b_ref, o_ref):
    c = c_ref[...]
    s = c * jax.nn.sigmoid(c)
    o_ref[...] = jnp.dot(s, w_ref[...], preferred_element_type=F32, precision=HIGHEST) + b_ref[...]


def _ada_call(cvec, w_ada, b_ada):
    depth, d, n = w_ada.shape
    tn = n // 4
    return pl.pallas_call(
        _ada_kernel,
        grid=(depth, n // tn),
        in_specs=[
            pl.BlockSpec((ADA_ROWS, d), lambda l, j: (0, 0)),
            pl.BlockSpec((None, d, tn), lambda l, j: (l, 0, j)),
            pl.BlockSpec((None, 1, tn), lambda l, j: (l, 0, j)),
        ],
        out_specs=pl.BlockSpec((None, ADA_ROWS, tn), lambda l, j: (l, 0, j)),
        out_shape=jax.ShapeDtypeStruct((depth, ADA_ROWS, n), F32),
        compiler_params=_params("arbitrary", "arbitrary"),
        name="ada",
    )(cvec, w_ada, b_ada.reshape(depth, 1, n))


def _stream_kernel(ctx_ref, x_ref, pos_ref, o_ref, *, tiles_per_batch, ctx_tiles):
    j = pl.program_id(0) % tiles_per_batch

    @pl.when(j < ctx_tiles)
    def _():
        o_ref[...] = ctx_ref[...]

    @pl.when(j >= ctx_tiles)
    def _():
        o_ref[...] = x_ref[...] + pos_ref[...]


def _stream_call(ctx2, x2, pos, dims):
    tm, d = TOKEN_TILE, dims["d"]
    tpb, ct = dims["tpb"], dims["ct"]
    lt = tpb - ct
    return pl.pallas_call(
        functools.partial(_stream_kernel, tiles_per_batch=tpb, ctx_tiles=ct),
        grid=(dims["tiles"],),
        in_specs=[
            pl.BlockSpec((tm, d), lambda i: ((i // tpb) * ct + jnp.minimum(i % tpb, ct - 1), 0)),
            pl.BlockSpec((tm, d), lambda i: ((i // tpb) * lt + jnp.maximum(i % tpb - ct, 0), 0)),
            pl.BlockSpec((tm, d), lambda i: (jnp.maximum(i % tpb - ct, 0), 0)),
        ],
        out_specs=pl.BlockSpec((tm, d), lambda i: (i, 0)),
        out_shape=jax.ShapeDtypeStruct((dims["n"], d), F32),
        compiler_params=_params("arbitrary"),
        name="stream",
    )(ctx2, x2, pos)


def _modulated(x, g, mod, shift_idx):
    d = x.shape[-1]
    ms = jnp.mean(x * x, axis=-1, keepdims=True)
    xn = x * lax.rsqrt(ms + NORM_EPS) * g
    shift = mod[:, shift_idx * d:(shift_idx + 1) * d]
    scale = mod[:, (shift_idx + 1) * d:(shift_idx + 2) * d]
    return xn * (1.0 + scale) + shift


def _mod_spec(layer, dims):
    tpb, ct, b = dims["tpb"], dims["ct"], dims["b"]
    return pl.BlockSpec((None, None, 1, 6 * dims["d"]),
                        lambda i, *_: (layer, _mod_row(i, tpb, ct, b), 0, 0))


def _pack_bf16_pairs(x):
    k = x.shape[1] // 2
    rounded = x.astype(BF16).astype(F32)
    hi = pltpu.bitcast(rounded[:, :k], jnp.uint32)
    lo = pltpu.bitcast(rounded[:, k:], jnp.uint32)
    return hi | (lo >> 16)


def _unpack_bf16_pairs(u):
    hi = pltpu.bitcast(u & jnp.uint32(0xFFFF0000), F32)
    lo = pltpu.bitcast(u << 16, F32)
    return jnp.concatenate([hi, lo], axis=1)


def _gelu_tanh(y):
    return 0.5 * y * (1.0 + jnp.tanh(math.sqrt(2.0 / math.pi) * (y + 0.044715 * (y * y * y))))


_NT = (((1,), (1,)), ((), ()))
_TN = (((0,), (0,)), ((), ()))


def _mproj_kernel(x_ref, g_ref, mod_ref, w_ref, wt_ref, q_ref, k_ref, vt_ref, ot_ref, gc_ref, gr_ref):
    d = x_ref.shape[-1]
    nq = q_ref.shape[-1]
    h = _modulated(x_ref[...], g_ref[...], mod_ref[...], 0).astype(BF16)
    q_ref[...] = jnp.dot(h, w_ref[:, 0:nq], preferred_element_type=F32).astype(BF16)
    k_ref[...] = jnp.dot(h, w_ref[:, nq:2 * nq], preferred_element_type=F32).astype(BF16)
    gc_ref[...] = jnp.dot(h, w_ref[:, 2 * nq:2 * nq + LANES], preferred_element_type=F32)
    vt_ref[...] = lax.dot_general(wt_ref[0:d, :], h, _NT, preferred_element_type=F32).astype(BF16)
    ot = lax.dot_general(wt_ref[d:2 * d, :], h, _NT, preferred_element_type=F32)
    ot_ref[...] = jax.nn.sigmoid(ot).astype(BF16)
    gr_ref[...] = lax.dot_general(wt_ref[2 * d:2 * d + 4 * HEADS, :], h, _NT, preferred_element_type=F32)


def _mproj_call(xs, g, mods, layer, w_tok, w_feat, dims):
    tm, d, n = TOKEN_TILE, dims["d"], dims["n"]
    nq = HEADS * HEAD_QK
    row = lambda i: (i, 0)
    col = lambda i: (0, i)
    const = lambda i: (0, 0)
    return pl.pallas_call(
        _mproj_kernel,
        grid=(dims["tiles"],),
        in_specs=[
            pl.BlockSpec((tm, d), row),
            pl.BlockSpec((1, d), const),
            _mod_spec(layer, dims),
            pl.BlockSpec(w_tok.shape, const),
            pl.BlockSpec(w_feat.shape, const),
        ],
        out_specs=[
            pl.BlockSpec((tm, nq), row),
            pl.BlockSpec((tm, nq), row),
            pl.BlockSpec((d, tm), col),
            pl.BlockSpec((d, tm), col),
            pl.BlockSpec((tm, LANES), row),
            pl.BlockSpec((4 * HEADS, tm), col),
        ],
        out_shape=[
            jax.ShapeDtypeStruct((n, nq), BF16),
            jax.ShapeDtypeStruct((n, nq), BF16),
            jax.ShapeDtypeStruct((d, n), BF16),
            jax.ShapeDtypeStruct((d, n), BF16),
            jax.ShapeDtypeStruct((n, LANES), F32),
            jax.ShapeDtypeStruct((4 * HEADS, n), F32),
        ],
        compiler_params=_params("arbitrary"),
        name="mlstm_proj",
    )(xs, g, mods, w_tok, w_feat)


def _mlstm_direction(q_ref, k_ref, vt_ref, gc_ref, gr_ref, bc_ref, br_ref, o_ref, ct_ref, m_ref,
                     *, backward):
    tm = q_ref.shape[0]
    gate_i = 2 * HEADS if backward else 0
    gate_f = gate_i + HEADS
    last = 0 if backward else tm - 1
    src = lax.broadcasted_iota(jnp.int32, (tm, tm), 0)
    tgt = lax.broadcasted_iota(jnp.int32, (tm, tm), 1)
    visible = (src >= tgt) if backward else (src <= tgt)
    neg_mask = jnp.where(visible, 0.0, -jnp.inf)
    xc = gc_ref[...] + bc_ref[...]
    xr = gr_ref[...] + br_ref[...]
    before = (src <= tgt) if backward else (src >= tgt)
    cum_c = _split_dot(jnp.where(before, 1.0, 0.0).astype(BF16), jax.nn.log_sigmoid(xc), left=True)
    cum_r = _split_dot(jnp.where(visible, 1.0, 0.0).astype(BF16), jax.nn.log_sigmoid(xr), left=False)
    b_row = cum_r[gate_f:gate_f + HEADS, :]
    c_row = xr[gate_i:gate_i + HEADS, :] - b_row
    c_col = xc - pltpu.roll(cum_c, LANES - HEADS, axis=1)
    lane = lax.broadcasted_iota(jnp.int32, (HEADS, tm), 1)
    run = c_row
    shift = 1
    while shift < tm:
        if backward:
            moved = jnp.where(lane + shift < tm, pltpu.roll(run, tm - shift, axis=1), -jnp.inf)
        else:
            moved = jnp.where(lane >= shift, pltpu.roll(run, shift, axis=1), -jnp.inf)
        run = jnp.maximum(run, moved)
        shift *= 2
    m = m_ref[:, 0:1]
    log_inter = b_row + m
    m_pos = jnp.maximum(log_inter, b_row + run)
    shift_row = b_row - m_pos
    w_inter = jnp.exp(log_inter - m_pos)
    floor = jnp.exp(-m_pos)
    b_last = b_row[:, last:last + 1]
    m_new = m_pos[:, last:last + 1]
    w_src = jnp.exp(b_last + c_row - m_new)
    decay = jnp.exp(b_last + m - m_new)
    m_ref[...] = jnp.broadcast_to(m_new, m_ref.shape)
    half = lax.broadcasted_iota(jnp.int32, (tm, LANES), 1) < HEAD_QK
    ones = jnp.ones((2 * SUBLANES, tm), BF16)
    for h in range(HEADS):
        pair = slice((h // 2) * LANES, (h // 2 + 1) * LANES)
        qp = q_ref[:, pair]
        km = jnp.where(half if h % 2 == 0 else jnp.logical_not(half), k_ref[:, pair], 0.0).astype(BF16)
        st = lax.dot_general(km, qp, _NT, preferred_element_type=F32)
        e = (neg_mask + shift_row[h:h + 1, :]) + c_col[:, gate_i + h:gate_i + h + 1]
        p = (st * jnp.exp(e)).astype(BF16)
        vaug = jnp.concatenate([vt_ref[h * HEAD_V:(h + 1) * HEAD_V, :], ones], axis=0)
        ct = ct_ref[h]
        acc = jnp.dot(vaug, p, preferred_element_type=F32)
        acc = acc + w_inter[h:h + 1, :] * lax.dot_general(ct.astype(BF16), qp, _NT,
                                                          preferred_element_type=F32)
        den = acc[HEAD_V:HEAD_V + 1, :]
        scale = 1.0 / jnp.maximum(jnp.abs(den), floor[h:h + 1, :])
        o_ref[h * HEAD_V:(h + 1) * HEAD_V, :] = (acc[0:HEAD_V, :] * scale).astype(o_ref.dtype)
        vs = (vaug.astype(F32) * w_src[h:h + 1, :]).astype(BF16)
        ct_ref[h] = decay[h:h + 1, :] * ct + jnp.dot(vs, km, preferred_element_type=F32)


def _split_dot(ones_mat, x, left):
    hi = x.astype(BF16)
    r1 = x - hi.astype(F32)
    mid = r1.astype(BF16)
    lo = (r1 - mid.astype(F32)).astype(BF16)
    out = None
    for piece in (lo, mid, hi):
        t = (jnp.dot(ones_mat, piece, preferred_element_type=F32) if left
             else jnp.dot(piece, ones_mat, preferred_element_type=F32))
        out = t if out is None else out + t
    return out


def _mlstm_kernel(qf, kf, vtf, gcf, grf, qb, kb, vtb, gcb, grb, bc_ref, br_ref, of_ref, ob_ref,
                  cf_ref, mf_ref, cb_ref, mb_ref, *, tiles_per_batch):
    @pl.when(pl.program_id(0) % tiles_per_batch == 0)
    def _():
        cf_ref[...] = jnp.zeros_like(cf_ref)
        cb_ref[...] = jnp.zeros_like(cb_ref)
        mf_ref[...] = jnp.zeros_like(mf_ref)
        mb_ref[...] = jnp.zeros_like(mb_ref)

    _mlstm_direction(qf, kf, vtf, gcf, grf, bc_ref, br_ref, of_ref, cf_ref, mf_ref, backward=False)
    _mlstm_direction(qb, kb, vtb, gcb, grb, bc_ref, br_ref, ob_ref, cb_ref, mb_ref, backward=True)


def _mlstm_call(q, k, vt, gc, gr, bias_c, bias_r, dims):
    tm, d, n = TOKEN_TILE, dims["d"], dims["n"]
    tpb, ct = dims["tpb"], dims["ct"]
    nq = q.shape[1]
    fwd = lambda i: (i, 0)
    bwd = lambda i: (_bwd_tile(i, tpb, ct), 0)
    fwd_t = lambda i: (0, i)
    bwd_t = lambda i: (0, _bwd_tile(i, tpb, ct))
    const = lambda i: (0, 0)
    state = [pltpu.VMEM((HEADS, HEAD_V + 2 * SUBLANES, LANES), F32), pltpu.VMEM((HEADS, LANES), F32)]
    return pl.pallas_call(
        functools.partial(_mlstm_kernel, tiles_per_batch=tpb),
        grid=(dims["tiles"],),
        in_specs=[
            pl.BlockSpec((tm, nq), fwd), pl.BlockSpec((tm, nq), fwd), pl.BlockSpec((d, tm), fwd_t),
            pl.BlockSpec((tm, LANES), fwd), pl.BlockSpec((4 * HEADS, tm), fwd_t),
            pl.BlockSpec((tm, nq), bwd), pl.BlockSpec((tm, nq), bwd), pl.BlockSpec((d, tm), bwd_t),
            pl.BlockSpec((tm, LANES), bwd), pl.BlockSpec((4 * HEADS, tm), bwd_t),
            pl.BlockSpec((1, LANES), const), pl.BlockSpec((4 * HEADS, 1), const),
        ],
        out_specs=[pl.BlockSpec((d, tm), fwd_t), pl.BlockSpec((d, tm), bwd_t)],
        out_shape=[jax.ShapeDtypeStruct((d, n), BF16), jax.ShapeDtypeStruct((d, n), BF16)],
        scratch_shapes=state + state,
        compiler_params=_params("arbitrary"),
        name="mlstm_scan",
    )(q, k, vt, gc, gr, q, k, vt, gc, gr, bias_c, bias_r)


def _mreadout_kernel(hf_ref, hb_ref, ot_ref, w_ref, x_ref, mod_ref, o_ref):
    d = x_ref.shape[-1]
    parts = []
    for h in range(HEADS):
        rows = slice(h * HEAD_V, (h + 1) * HEAD_V)
        hs = hf_ref[rows, :].astype(F32) + hb_ref[rows, :].astype(F32)
        r = lax.rsqrt(jnp.mean(hs * hs, axis=0, keepdims=True) + NORM_EPS)
        parts.append((hs * r * ot_ref[rows, :]).astype(BF16))
    t = jnp.concatenate(parts, axis=0)
    out = lax.dot_general(t, w_ref[...], _TN, preferred_element_type=F32)
    o_ref[...] = x_ref[...] + mod_ref[:, 2 * d:3 * d] * out


def _mreadout_call(hf, hb, ot, w_out, xs, mods, layer, dims):
    tm, d, n = TOKEN_TILE, dims["d"], dims["n"]
    row = lambda i: (i, 0)
    col = lambda i: (0, i)
    const = lambda i: (0, 0)
    return pl.pallas_call(
        _mreadout_kernel,
        grid=(dims["tiles"],),
        in_specs=[
            pl.BlockSpec((d, tm), col), pl.BlockSpec((d, tm), col), pl.BlockSpec((d, tm), col),
            pl.BlockSpec((d, d), const), pl.BlockSpec((tm, d), row), _mod_spec(layer, dims),
        ],
        out_specs=pl.BlockSpec((tm, d), row),
        out_shape=jax.ShapeDtypeStruct((n, d), F32),
        compiler_params=_params("arbitrary"),
        name="mlstm_readout",
    )(hf, hb, ot, w_out, xs, mods)


def _readout_kernel(hf_ref, hb_ref, gy_ref, w_ref, x_ref, mod_ref, o_ref):
    d = x_ref.shape[-1]
    t = _gelu_tanh(gy_ref[...].astype(F32)) * (hf_ref[...].astype(F32) + hb_ref[...].astype(F32))
    out = jnp.dot(t.astype(BF16), w_ref[...], preferred_element_type=F32)
    o_ref[...] = x_ref[...] + mod_ref[:, 2 * d:3 * d] * out


def _readout_call(hf, hb, gy, w_out, xs, mods, layer, dims):
    tm, d, n = TOKEN_TILE, dims["d"], dims["n"]
    row = lambda i: (i, 0)
    const = lambda i: (0, 0)
    return pl.pallas_call(
        _readout_kernel,
        grid=(dims["tiles"],),
        in_specs=[
            pl.BlockSpec((tm, d), row), pl.BlockSpec((tm, d), row), pl.BlockSpec((tm, d), row),
            pl.BlockSpec((d, d), const), pl.BlockSpec((tm, d), row), _mod_spec(layer, dims),
        ],
        out_specs=pl.BlockSpec((tm, d), row),
        out_shape=jax.ShapeDtypeStruct((n, d), F32),
        compiler_params=_params("arbitrary"),
        name="rglru_readout",
    )(hf, hb, gy, w_out, xs, mods)


def _rproj_kernel(x_ref, g_ref, mod_ref, w_ref, cw_ref, cb_ref, gy_ref, xc_ref, ext_ref, cur_ref,
                  *, tiles, tiles_per_batch, ctx_tiles):
    tm, d = x_ref.shape
    i = pl.program_id(0)

    def in_segment_neighbours(t):
        j = t % tiles_per_batch
        first = jnp.logical_or(j == 0, j == ctx_tiles)
        last = jnp.logical_or(j == ctx_tiles - 1, j == tiles_per_batch - 1)
        return jnp.logical_not(first), jnp.logical_not(last)

    @pl.when(i == 0)
    def _():
        ext_ref[...] = jnp.zeros_like(ext_ref)
        cur_ref[...] = jnp.zeros_like(cur_ref)

    @pl.when(i < tiles)
    def _():
        h = _modulated(x_ref[...], g_ref[...], mod_ref[...], 0).astype(BF16)
        gy_ref[...] = jnp.dot(h, w_ref[:, 0:d], preferred_element_type=F32).astype(BF16)
        cur_ref[...] = jnp.dot(h, w_ref[:, d:2 * d], preferred_element_type=F32)

    @pl.when(i > 0)
    def _():
        _, next_ok = in_segment_neighbours(i - 1)
        head_ok = jnp.logical_and(next_ok, i < tiles)
        ext_ref[SUBLANES + tm:2 * SUBLANES + tm, :] = jnp.where(head_ok, cur_ref[0:SUBLANES, :], 0.0)
        xc = cb_ref[...] + ext_ref[pl.ds(SUBLANES - CONV_LEFT, tm), :] * cw_ref[0:1, :]
        for k in range(1, CONV_TAPS):
            xc = xc + ext_ref[pl.ds(SUBLANES - CONV_LEFT + k, tm), :] * cw_ref[k:k + 1, :]
        xc_ref[...] = xc

    prev_ok, _ = in_segment_neighbours(i)
    ext_ref[0:SUBLANES, :] = jnp.where(prev_ok, ext_ref[tm:tm + SUBLANES, :], 0.0)
    ext_ref[SUBLANES:SUBLANES + tm, :] = cur_ref[...]


def _rproj_call(xs, g, mods, layer, w_in, conv_w, conv_b, dims):
    tm, d, n, tiles = TOKEN_TILE, dims["d"], dims["n"], dims["tiles"]
    tpb, ct, b = dims["tpb"], dims["ct"], dims["b"]
    cur = lambda i: (jnp.minimum(i, tiles - 1), 0)
    lag = lambda i: (jnp.maximum(i - 1, 0), 0)
    const = lambda i: (0, 0)
    mod_spec = pl.BlockSpec((None, None, 1, 6 * d),
                            lambda i: (layer, _mod_row(jnp.minimum(i, tiles - 1), tpb, ct, b), 0, 0))
    return pl.pallas_call(
        functools.partial(_rproj_kernel, tiles=tiles, tiles_per_batch=tpb, ctx_tiles=ct),
        grid=(tiles + 1,),
        in_specs=[pl.BlockSpec((tm, d), cur), pl.BlockSpec((1, d), const), mod_spec,
                  pl.BlockSpec(w_in.shape, const), pl.BlockSpec(conv_w.shape, const),
                  pl.BlockSpec(conv_b.shape, const)],
        out_specs=[pl.BlockSpec((tm, d), cur), pl.BlockSpec((tm, d), lag)],
        out_shape=[jax.ShapeDtypeStruct((n, d), BF16), jax.ShapeDtypeStruct((n, d), F32)],
        scratch_shapes=[pltpu.VMEM((tm + 2 * SUBLANES, d), F32), pltpu.VMEM((tm, d), F32)],
        compiler_params=_params("arbitrary"),
        name="rglru_proj",
    )(xs, g, mods, w_in, conv_w, conv_b)


def _sigmoid_tanh(x):
    return 0.5 * jnp.tanh(0.5 * x) + 0.5


def _lru_direction(xc_ref, wg_ref, bg_ref, lam_ref, a_ref, b_ref, *, direction):
    tm, d = xc_ref.shape
    bw = d // LRU_BLOCKS
    z = -lam_ref[direction]
    softplus = jnp.maximum(z, 0.0) + jnp.log(1.0 + jnp.exp(-jnp.abs(z)))
    for nb in range(LRU_BLOCKS):
        sl = slice(nb * bw, (nb + 1) * bw)
        xb = xc_ref[:, sl]
        gg = jnp.dot(xb.astype(BF16), wg_ref[direction, nb], preferred_element_type=F32)
        gg = gg + bg_ref[direction, :, nb * 2 * bw:(nb + 1) * 2 * bw]
        r = _sigmoid_tanh(gg[:, :bw])
        gi = _sigmoid_tanh(gg[:, bw:])
        log_a = -LRU_C * r * softplus[:, sl]
        th = jnp.tanh(log_a)
        a_ref[pl.ds(nb, tm, stride=LRU_BLOCKS), :] = jnp.exp(log_a)
        b_ref[pl.ds(nb, tm, stride=LRU_BLOCKS), :] = jnp.sqrt(-2.0 * th / (1.0 - th)) * (gi * xb)


def _lru_kernel(xc_f, xc_b, wg_ref, bg_ref, lam_ref, of_ref, ob_ref, af_ref, bf_ref, ab_ref, bb_ref,
                hf_ref, hb_ref, h_ref, *, tiles_per_batch):
    tm, d = xc_f.shape
    bw = d // LRU_BLOCKS

    @pl.when(pl.program_id(0) % tiles_per_batch == 0)
    def _():
        h_ref[...] = jnp.zeros_like(h_ref)

    _lru_direction(xc_f, wg_ref, bg_ref, lam_ref, af_ref, bf_ref, direction=0)
    _lru_direction(xc_b, wg_ref, bg_ref, lam_ref, ab_ref, bb_ref, direction=1)

    def step(t, carry):
        hf, hb = carry
        rf = pl.multiple_of(t * LRU_BLOCKS, LRU_BLOCKS)
        rb = pl.multiple_of((tm - 1 - t) * LRU_BLOCKS, LRU_BLOCKS)
        hf = af_ref[pl.ds(rf, LRU_BLOCKS), :] * hf + bf_ref[pl.ds(rf, LRU_BLOCKS), :]
        hb = ab_ref[pl.ds(rb, LRU_BLOCKS), :] * hb + bb_ref[pl.ds(rb, LRU_BLOCKS), :]
        hf_ref[pl.ds(rf, LRU_BLOCKS), :] = hf
        hb_ref[pl.ds(rb, LRU_BLOCKS), :] = hb
        return hf, hb

    hf, hb = lax.fori_loop(0, tm, step, (h_ref[0], h_ref[1]), unroll=8)
    h_ref[0] = hf
    h_ref[1] = hb
    for nb in range(LRU_BLOCKS):
        sl = slice(nb * bw, (nb + 1) * bw)
        of_ref[:, sl] = hf_ref[pl.ds(nb, tm, stride=LRU_BLOCKS), :].astype(BF16)
        ob_ref[:, sl] = hb_ref[pl.ds(nb, tm, stride=LRU_BLOCKS), :].astype(BF16)


def _lru_call(xc, w_gate, b_gate, lam, dims):
    tm, d, n = TOKEN_TILE, dims["d"], dims["n"]
    tpb, ct = dims["tpb"], dims["ct"]
    fwd = lambda i: (i, 0)
    bwd = lambda i: (_bwd_tile(i, tpb, ct), 0)
    c3 = lambda i: (0, 0, 0)
    c4 = lambda i: (0, 0, 0, 0)
    time_major = pltpu.VMEM((tm * LRU_BLOCKS, d // LRU_BLOCKS), F32)
    return pl.pallas_call(
        functools.partial(_lru_kernel, tiles_per_batch=tpb),
        grid=(dims["tiles"],),
        in_specs=[
            pl.BlockSpec((tm, d), fwd), pl.BlockSpec((tm, d), bwd),
            pl.BlockSpec(w_gate.shape, c4), pl.BlockSpec(b_gate.shape, c3), pl.BlockSpec(lam.shape, c3),
        ],
        out_specs=[pl.BlockSpec((tm, d), fwd), pl.BlockSpec((tm, d), bwd)],
        out_shape=[jax.ShapeDtypeStruct((n, d), BF16), jax.ShapeDtypeStruct((n, d), BF16)],
        scratch_shapes=[time_major] * 6 + [pltpu.VMEM((2, LRU_BLOCKS, d // LRU_BLOCKS), F32)],
        compiler_params=_params("arbitrary"),
        name="rglru_scan",
    )(xc, xc, w_gate, b_gate, lam)


def _first_argmax(vals):
    best = vals[0]
    idx = jnp.zeros_like(best)
    for k in range(1, len(vals)):
        better = vals[k] > best
        idx = jnp.where(better, float(k), idx)
        best = jnp.where(better, vals[k], best)
    return idx, best


def _pick(idx, vals):
    out = vals[0]
    for k in range(1, len(vals)):
        out = jnp.where(idx == float(k), vals[k], out)
    return out


def _router_kernel(x_ref, g_ref, mod_ref, wr_ref, br_ref, hx_ref, route_ref, cnt_ref, carry_ref):
    tm, d = x_ref.shape

    @pl.when(pl.program_id(0) == 0)
    def _():
        carry_ref[...] = jnp.zeros_like(carry_ref)

    h = _modulated(x_ref[...], g_ref[...], mod_ref[...], 3)
    hx_ref[:, 0:d // 2] = _pack_bf16_pairs(h)
    logits = lax.dot_general(wr_ref[...], h, (((1,), (1,)), ((), ())),
                             preferred_element_type=F32, precision=HIGHEST)
    e = jnp.exp(logits - jnp.max(logits, axis=0, keepdims=True))
    scores = e / jnp.sum(e, axis=0, keepdims=True)
    sel = scores + br_ref[...]
    sel_rows = [sel[k:k + 1, :] for k in range(N_EXPERTS)]
    score_rows = [scores[k:k + 1, :] for k in range(N_EXPERTS)]
    group_scores = []
    for gi in range(N_GROUPS):
        v = sel_rows[gi * GROUP_SIZE:(gi + 1) * GROUP_SIZE]
        best = v[0] + v[1]
        for a, b in PAIRS[1:]:
            best = jnp.maximum(best, v[a] + v[b])
        group_scores.append(best)
    grp, _ = _first_argmax(group_scores)
    in_sel = [_pick(grp, [sel_rows[gi * GROUP_SIZE + k] for gi in range(N_GROUPS)])
              for k in range(GROUP_SIZE)]
    in_score = [_pick(grp, [score_rows[gi * GROUP_SIZE + k] for gi in range(N_GROUPS)])
                for k in range(GROUP_SIZE)]
    i1, _ = _first_argmax(in_sel)
    rest = [jnp.where(i1 == float(k), -jnp.inf, in_sel[k]) for k in range(GROUP_SIZE)]
    i2, _ = _first_argmax(rest)
    lo = jnp.minimum(i1, i2)
    hi = jnp.maximum(i1, i2)
    s_lo = _pick(lo, in_score)
    s_hi = _pick(hi, in_score)
    tot = s_lo + s_hi
    w_lo = s_lo / tot
    w_hi = s_hi / tot
    pair = jnp.zeros_like(lo)
    w_a, w_b = w_lo, w_hi
    for k, (slot_a, slot_b) in enumerate(PAIR_SLOTS):
        here = jnp.logical_and(lo == float(min(slot_a, slot_b)), hi == float(max(slot_a, slot_b)))
        pair = jnp.where(here, float(k), pair)
        if slot_a > slot_b:
            w_a = jnp.where(here, w_hi, w_a)
            w_b = jnp.where(here, w_lo, w_b)
    cls = grp * float(len(PAIR_SLOTS)) + pair
    crow = lax.broadcasted_iota(jnp.int32, (CLASS_ROWS, tm), 0).astype(F32)
    onehot = jnp.where(crow == cls, 1.0, 0.0)
    rows = lax.broadcasted_iota(jnp.int32, (tm, tm), 0)
    cols = lax.broadcasted_iota(jnp.int32, (tm, tm), 1)
    upper = jnp.where(rows <= cols, 1.0, 0.0).astype(BF16)
    cum = jnp.dot(onehot.astype(BF16), upper, preferred_element_type=F32)
    carry = carry_ref[:, 0:1]
    rank = jnp.sum(onehot * (cum - 1.0 + carry), axis=0, keepdims=True)
    new_carry = carry + jnp.sum(onehot, axis=1, keepdims=True)
    carry_ref[...] = jnp.broadcast_to(new_carry, carry_ref.shape)
    cnt_ref[...] = jnp.broadcast_to(new_carry, cnt_ref.shape)
    zero = jnp.zeros_like(cls)
    route_ref[...] = jnp.concatenate([cls, rank, w_a, w_b, zero, zero, zero, zero], axis=0)
    eye = rows == cols
    wa_col = jnp.sum(jnp.where(eye, w_a, 0.0), axis=1, keepdims=True)
    wb_col = jnp.sum(jnp.where(eye, w_b, 0.0), axis=1, keepdims=True)
    lane = lax.broadcasted_iota(jnp.int32, (tm, ROUTE_COLS), 1)
    extra = jnp.where(lane == 0, wa_col, jnp.where(lane == 1, wb_col, 0.0))
    hx_ref[:, d // 2:d // 2 + ROUTE_COLS] = pltpu.bitcast(extra, jnp.uint32)


def _router_call(xs, g, mods, layer, w_router_t, b_router_col, dims):
    tm, d, n = TOKEN_TILE, dims["d"], dims["n"]
    row = lambda i: (i, 0)
    const = lambda i: (0, 0)
    return pl.pallas_call(
        _router_kernel,
        grid=(dims["tiles"],),
        in_specs=[pl.BlockSpec((tm, d), row), pl.BlockSpec((1, d), const), _mod_spec(layer, dims),
                  pl.BlockSpec(w_router_t.shape, const), pl.BlockSpec(b_router_col.shape, const)],
        out_specs=[pl.BlockSpec((tm, d // 2 + ROUTE_COLS), row),
                   pl.BlockSpec((SUBLANES, tm), lambda i: (0, i)),
                   pl.BlockSpec((CLASS_ROWS, LANES), const)],
        out_shape=[jax.ShapeDtypeStruct((n, d // 2 + ROUTE_COLS), jnp.uint32),
                   jax.ShapeDtypeStruct((SUBLANES, n), F32),
                   jax.ShapeDtypeStruct((CLASS_ROWS, LANES), F32)],
        scratch_shapes=[pltpu.VMEM((CLASS_ROWS, LANES), F32)],
        compiler_params=_params("arbitrary"),
        name="router",
    )(xs, g, mods, w_router_t, b_router_col)


ROW_DMA_UNROLL = 8


def _dispatch_kernel(pos_ref, hx_ref, init_ref, o_ref, stage_ref, sems):
    del init_ref
    tm = hx_ref.shape[0]
    i = pl.program_id(0)
    slot = i % 2
    base = i * tm
    stage_ref[slot] = hx_ref[...]

    def issue(r, carry):
        pltpu.make_async_copy(stage_ref.at[slot, pl.ds(r, 1)], o_ref.at[pl.ds(pos_ref[base + r], 1)],
                              sems.at[slot]).start()
        return carry

    lax.fori_loop(0, tm, issue, 0, unroll=ROW_DMA_UNROLL)

    def wait_tile(s):
        pltpu.make_async_copy(stage_ref.at[s], o_ref.at[pl.ds(0, tm)], sems.at[s]).wait()

    @pl.when(i > 0)
    def _():
        wait_tile(1 - slot)

    @pl.when(i == pl.num_programs(0) - 1)
    def _():
        wait_tile(slot)


def _dispatch_call(pos, hx, padded_rows, dims):
    tm = TOKEN_TILE
    width = hx.shape[1]
    init = jnp.zeros((padded_rows, width), hx.dtype)
    return pl.pallas_call(
        _dispatch_kernel,
        grid_spec=pltpu.PrefetchScalarGridSpec(
            num_scalar_prefetch=1,
            grid=(dims["tiles"],),
            in_specs=[pl.BlockSpec((tm, width), lambda i, p: (i, 0)), pl.BlockSpec(memory_space=pl.ANY)],
            out_specs=pl.BlockSpec(memory_space=pl.ANY),
            scratch_shapes=[pltpu.VMEM((2, tm, width), hx.dtype), pltpu.SemaphoreType.DMA((2,))],
        ),
        out_shape=jax.ShapeDtypeStruct((padded_rows, width), hx.dtype),
        input_output_aliases={2: 0},
        compiler_params=_params("arbitrary"),
        name="dispatch",
    )(pos, hx, init)


def _expert_kernel(ea_ref, eb_ref, ok_ref, x_ref, w1a, w3a, w2a, w1b, w3b, w2b, o_ref):
    del ea_ref, eb_ref
    half = o_ref.shape[-1]
    t = pl.program_id(0)

    @pl.when(ok_ref[t] != 0)
    def _():
        x = _unpack_bf16_pairs(x_ref[:, 0:half]).astype(BF16)
        gates = pltpu.bitcast(x_ref[:, half:half + ROUTE_COLS], F32)

        def expert(w1, w3, w2):
            a = jnp.dot(x, w1[...], preferred_element_type=F32)
            b = jnp.dot(x, w3[...], preferred_element_type=F32)
            u = (a * jax.nn.sigmoid(a)) * b
            return jnp.dot(u.astype(BF16), w2[...], preferred_element_type=F32)

        o_ref[...] = _pack_bf16_pairs(gates[:, 0:1] * expert(w1a, w3a, w2a)
                                      + gates[:, 1:2] * expert(w1b, w3b, w2b))

    @pl.when(ok_ref[t] == 0)
    def _():
        o_ref[...] = jnp.zeros_like(o_ref)


def _expert_call(tile_a, tile_b, tile_ok, xsorted, w1, w3, w2, layer, d):
    te = TOKEN_TILE
    rows, width = xsorted.shape
    de = w1.shape[-1]
    sel_a = lambda t, ea, eb, ok: (layer, ea[t], 0, 0)
    sel_b = lambda t, ea, eb, ok: (layer, eb[t], 0, 0)
    return pl.pallas_call(
        _expert_kernel,
        grid_spec=pltpu.PrefetchScalarGridSpec(
            num_scalar_prefetch=3,
            grid=(rows // te,),
            in_specs=[
                pl.BlockSpec((te, width), lambda t, ea, eb, ok: (t, 0)),
                pl.BlockSpec((None, None, d, de), sel_a), pl.BlockSpec((None, None, d, de), sel_a),
                pl.BlockSpec((None, None, de, d), sel_a),
                pl.BlockSpec((None, None, d, de), sel_b), pl.BlockSpec((None, None, d, de), sel_b),
                pl.BlockSpec((None, None, de, d), sel_b),
            ],
            out_specs=pl.BlockSpec((te, d // 2), lambda t, ea, eb, ok: (t, 0)),
        ),
        out_shape=jax.ShapeDtypeStruct((rows, d // 2), jnp.uint32),
        compiler_params=_params("arbitrary"),
        name="experts",
    )(tile_a, tile_b, tile_ok, xsorted, w1, w3, w2, w1, w3, w2)


def _combine_kernel(pos_ref, y_ref, x_ref, mod_ref, gf_ref, o_ref, buf_ref, sems,
                    *, tile_of, final_norm):
    tm, d = x_ref.shape
    i = pl.program_id(0)
    slot = i % 2

    def gather(step, s):
        base = tile_of(step) * tm

        def issue(r, carry):
            pltpu.make_async_copy(y_ref.at[pl.ds(pos_ref[base + r], 1)], buf_ref.at[s, pl.ds(r, 1)],
                                  sems.at[s]).start()
            return carry

        lax.fori_loop(0, tm, issue, 0, unroll=ROW_DMA_UNROLL)

    @pl.when(i == 0)
    def _():
        gather(0, 0)

    @pl.when(i + 1 < pl.num_programs(0))
    def _():
        gather(i + 1, 1 - slot)

    pltpu.make_async_copy(y_ref.at[pl.ds(0, tm)], buf_ref.at[slot], sems.at[slot]).wait()
    out = x_ref[...] + mod_ref[:, 5 * d:6 * d] * _unpack_bf16_pairs(buf_ref[slot])
    if final_norm:
        ms = jnp.mean(out * out, axis=-1, keepdims=True)
        out = out * lax.rsqrt(ms + NORM_EPS) * gf_ref[...]
    o_ref[...] = out


def _combine_call(pos, ysorted, xs, mods, layer, g_final, dims, final_norm):
    tm, d = TOKEN_TILE, dims["d"]
    tpb, ct, b = dims["tpb"], dims["ct"], dims["b"]
    if final_norm:
        lt = tpb - ct
        tile_of = lambda i: (i // lt) * tpb + ct + i % lt
        n_tiles = b * lt
    else:
        tile_of = lambda i: i
        n_tiles = dims["tiles"]
    mod_spec = pl.BlockSpec((None, None, 1, 6 * d),
                            lambda i, p: (layer, _mod_row(tile_of(i), tpb, ct, b), 0, 0))
    return pl.pallas_call(
        functools.partial(_combine_kernel, tile_of=tile_of, final_norm=final_norm),
        grid_spec=pltpu.PrefetchScalarGridSpec(
            num_scalar_prefetch=1,
            grid=(n_tiles,),
            in_specs=[
                pl.BlockSpec(memory_space=pl.ANY),
                pl.BlockSpec((tm, d), lambda i, p: (tile_of(i), 0)),
                mod_spec,
                pl.BlockSpec((1, d), lambda i, p: (0, 0)),
            ],
            out_specs=pl.BlockSpec((tm, d), lambda i, p: (i, 0)),
            scratch_shapes=[pltpu.VMEM((2, tm, d // 2), jnp.uint32), pltpu.SemaphoreType.DMA((2,))],
        ),
        out_shape=jax.ShapeDtypeStruct((n_tiles * tm, d), F32),
        compiler_params=_params("arbitrary"),
        name="combine",
    )(pos, ysorted, xs, mods, g_final)


_CLASS_A = np.array([g * GROUP_SIZE + a for g in range(N_GROUPS) for a, _ in PAIR_SLOTS], np.int32)
_CLASS_B = np.array([g * GROUP_SIZE + b for g in range(N_GROUPS) for _, b in PAIR_SLOTS], np.int32)


def _routing_plan(route, counts, n_tiles_padded):
    te = TOKEN_TILE
    cls = route[0].astype(jnp.int32)
    rank = route[1].astype(jnp.int32)
    cnt = counts[:N_CLASSES, 0].astype(jnp.int32)
    tiles = (cnt + te - 1) // te
    tile_end = jnp.cumsum(tiles)
    tile_start = tile_end - tiles
    onehot = (cls[:, None] == jnp.arange(N_CLASSES, dtype=jnp.int32)[None, :]).astype(jnp.int32)
    pos = jnp.sum(onehot * (tile_start * te)[None, :], axis=1) + rank
    t = jnp.arange(n_tiles_padded, dtype=jnp.int32)
    ok = t < tile_end[-1]
    tcls = jnp.sum((tile_end[None, :] <= jnp.minimum(t, tile_end[-1] - 1)[:, None]).astype(jnp.int32), axis=1)
    tile_a = jnp.take(jnp.asarray(_CLASS_A), tcls)
    tile_b = jnp.take(jnp.asarray(_CLASS_B), tcls)
    return pos, tile_a, tile_b, ok.astype(jnp.int32)


def _moe(xs, g, mods, layer, w_router_t, b_router_col, w1, w3, w2, g_final, dims, final_norm):
    hx, route, counts = _router_call(xs, g, mods, layer, w_router_t, b_router_col, dims)
    n_tiles_padded = dims["tiles"] + N_CLASSES
    pos, tile_a, tile_b, tile_ok = _routing_plan(route, counts, n_tiles_padded)
    xsorted = _dispatch_call(pos, hx, n_tiles_padded * TOKEN_TILE, dims)
    ysorted = _expert_call(tile_a, tile_b, tile_ok, xsorted, w1, w3, w2, layer, dims["d"])
    return _combine_call(pos, ysorted, xs, mods, layer, g_final, dims, final_norm)


def _pos_table(n_tokens, dim):
    rows = n_tokens // GRID_WIDTH
    r, col = jnp.meshgrid(jnp.arange(rows, dtype=F32), jnp.arange(GRID_WIDTH, dtype=F32), indexing="ij")
    quarter = dim // 4
    freqs = jnp.exp(-math.log(POS_BASE) * jnp.arange(quarter, dtype=F32) / quarter)

    def enc(p):
        ang = p.reshape(-1, 1) * freqs
        return jnp.concatenate([jnp.sin(ang), jnp.cos(ang)], axis=-1)

    return jnp.concatenate([enc(r), enc(col)], axis=-1)


def _mlstm_weights(w_in, b_gate):
    nq = HEADS * HEAD_QK
    nv = HEADS * HEAD_V
    wq = w_in[:, 0:nq] * (HEAD_QK ** -0.5)
    wk = w_in[:, nq:2 * nq]
    wv = w_in[:, 2 * nq:2 * nq + nv]
    wo = w_in[:, 2 * nq + nv:2 * nq + 2 * nv]
    wg = w_in[:, 2 * nq + 2 * nv:]
    wg_pad = jnp.pad(wg, ((0, 0), (0, LANES - wg.shape[1])))
    w_tok = jnp.concatenate([wq, wk, wg_pad], axis=1).astype(BF16)
    w_feat = jnp.concatenate([wv, wo, wg], axis=1).T.astype(BF16)
    bias = b_gate.reshape(-1).astype(F32)
    bias_c = jnp.pad(bias, (0, LANES - bias.shape[0])).reshape(1, LANES)
    bias_r = bias.reshape(-1, 1)
    return w_tok, w_feat, bias_c, bias_r


def kernel(x, c, ctx, c_ctx, w_ada, b_ada, g_mix, g_ffn, g_final, m_w_in, m_b_gate, m_g_head, m_w_out, r_w_in, r_conv_w, r_conv_b, r_w_gate, r_b_gate, r_lam, r_w_out, w_router, b_router, e_w1, e_w3, e_w2):
    batch, t_len, d = x.shape
    ctx_len = ctx.shape[1]
    depth = w_ada.shape[0]
    tm = TOKEN_TILE
    assert t_len % tm == 0 and ctx_len % tm == 0 and batch + 1 <= ADA_ROWS
    assert d == HEADS * HEAD_V and t_len % GRID_WIDTH == 0
    s_len = ctx_len + t_len
    dims = dict(b=batch, d=d, n=batch * s_len, tpb=s_len // tm, ct=ctx_len // tm,
                tiles=batch * s_len // tm)

    cvec = jnp.concatenate([c, c_ctx[None, :], jnp.zeros((ADA_ROWS - batch - 1, d), F32)], axis=0)
    mods = _ada_call(cvec, w_ada, b_ada).reshape(depth, ADA_ROWS, 1, 6 * d)
    xs = _stream_call(ctx.reshape(batch * ctx_len, d), x.reshape(batch * t_len, d),
                      _pos_table(t_len, d), dims)

    w_router_t = w_router.T.astype(F32)
    b_router_col = b_router.reshape(-1, 1).astype(F32)
    g_final2 = g_final.reshape(1, d)
    e_w1b, e_w3b, e_w2b = e_w1.astype(BF16), e_w3.astype(BF16), e_w2.astype(BF16)
    out = None
    for i in range(depth):
        j = i // 2
        g_mix_i = g_mix[i].reshape(1, d)
        if i % 2 == 0:
            w_tok, w_feat, bias_c, bias_r = _mlstm_weights(m_w_in[j], m_b_gate[j])
            q, k, vt, ot, gc, gr = _mproj_call(xs, g_mix_i, mods, i, w_tok, w_feat, dims)
            hf, hb = _mlstm_call(q, k, vt, gc, gr, bias_c, bias_r, dims)
            w_out = (m_g_head[j][:, None] * m_w_out[j]).astype(BF16)
            xs = _mreadout_call(hf, hb, ot, w_out, xs, mods, i, dims)
        else:
            gy, xc = _rproj_call(xs, g_mix_i, mods, i, r_w_in[j].astype(BF16), r_conv_w[j],
                                 r_conv_b[j].reshape(1, d), dims)
            hf, hb = _lru_call(xc, r_w_gate[j].astype(BF16), r_b_gate[j].reshape(2, 1, -1),
                               r_lam[j].reshape(2, 1, d), dims)
            xs = _readout_call(hf, hb, gy, r_w_out[j].astype(BF16), xs, mods, i, dims)
        last = i == depth - 1
        res = _moe(xs, g_ffn[i].reshape(1, d), mods, i, w_router_t, b_router_col,
                   e_w1b, e_w3b, e_w2b, g_final2, dims, last)
        if last:
            out = res
        else:
            xs = res
    return out.reshape(batch, t_len, d)
```

```python
import functools
import math

import jax
import jax.numpy as jnp
import numpy as np
from jax import lax
from jax.experimental import pallas as pl
from jax.experimental.pallas import tpu as pltpu

F32 = jnp.float32
BF16 = jnp.bfloat16
HIGHEST = lax.Precision.HIGHEST

TOKEN_TILE = 256
LANES = 128
SUBLANES = 8
VMEM_LIMIT_BYTES = 56 * 1024 * 1024
NORM_EPS = 1e-6
GRID_WIDTH = 64
POS_BASE = 10000.0
HEADS = 8
HEAD_QK = 64
HEAD_V = 128
LRU_BLOCKS = 8
LRU_C = 8.0
CONV_TAPS = 4
CONV_LEFT = CONV_TAPS // 2
N_EXPERTS = 16
N_GROUPS = 4
GROUP_SIZE = N_EXPERTS // N_GROUPS
PAIRS = [(a, b) for a in range(GROUP_SIZE) for b in range(a + 1, GROUP_SIZE)]
PAIR_SLOTS = [(0, 1), (0, 2), (0, 3), (1, 3), (1, 2), (3, 2)]
N_CLASSES = N_GROUPS * len(PAIRS)
CLASS_ROWS = 32
ROUTE_COLS = LANES
ADA_ROWS = 16


def _params(*sem):
    return pltpu.CompilerParams(dimension_semantics=sem, vmem_limit_bytes=VMEM_LIMIT_BYTES)


def _mod_row(i, tiles_per_batch, ctx_tiles, batch):
    return jnp.where(i % tiles_per_batch < ctx_tiles, batch, i // tiles_per_batch)


def _bwd_tile(i, tiles_per_batch, ctx_tiles):
    b = i // tiles_per_batch
    j = i % tiles_per_batch
    jb = jnp.where(j < ctx_tiles, ctx_tiles - 1 - j, tiles_per_batch - 1 - (j - ctx_tiles))
    return b * tiles_per_batch + jb


def _ada_kernel(c_ref, w_ref, b_ref, o_ref):
    c = c_ref[...]
    s = c * jax.nn.sigmoid(c)
    o_ref[...] = jnp.dot(s, w_ref[...], preferred_element_type=F32, precision=HIGHEST) + b_ref[...]


def _ada_call(cvec, w_ada, b_ada):
    depth, d, n = w_ada.shape
    tn = n // 4
    return pl.pallas_call(
        _ada_kernel,
        grid=(depth, n // tn),
        in_specs=[
            pl.BlockSpec((ADA_ROWS, d), lambda l, j: (0, 0)),
            pl.BlockSpec((None, d, tn), lambda l, j: (l, 0, j)),
            pl.BlockSpec((None, 1, tn), lambda l, j: (l, 0, j)),
        ],
        out_specs=pl.BlockSpec((None, ADA_ROWS, tn), lambda l, j: (l, 0, j)),
        out_shape=jax.ShapeDtypeStruct((depth, ADA_ROWS, n), F32),
        compiler_params=_params("arbitrary", "arbitrary"),
        name="ada",
    )(cvec, w_ada, b_ada.reshape(depth, 1, n))


def _stream_kernel(ctx_ref, x_ref, pos_ref, o_ref, *, tiles_per_batch, ctx_tiles):
    j = pl.program_id(0) % tiles_per_batch

    @pl.when(j < ctx_tiles)
    def _():
        o_ref[...] = ctx_ref[...]

    @pl.when(j >= ctx_tiles)
    def _():
        o_ref[...] = x_ref[...] + pos_ref[...]


def _stream_call(ctx2, x2, pos, dims):
    tm, d = TOKEN_TILE, dims["d"]
    tpb, ct = dims["tpb"], dims["ct"]
    lt = tpb - ct
    return pl.pallas_call(
        functools.partial(_stream_kernel, tiles_per_batch=tpb, ctx_tiles=ct),
        grid=(dims["tiles"],),
        in_specs=[
            pl.BlockSpec((tm, d), lambda i: ((i // tpb) * ct + jnp.minimum(i % tpb, ct - 1), 0)),
            pl.BlockSpec((tm, d), lambda i: ((i // tpb) * lt + jnp.maximum(i % tpb - ct, 0), 0)),
            pl.BlockSpec((tm, d), lambda i: (jnp.maximum(i % tpb - ct, 0), 0)),
        ],
        out_specs=pl.BlockSpec((tm, d), lambda i: (i, 0)),
        out_shape=jax.ShapeDtypeStruct((dims["n"], d), F32),
        compiler_params=_params("arbitrary"),
        name="stream",
    )(ctx2, x2, pos)


def _modulated(x, g, mod, shift_idx):
    d = x.shape[-1]
    ms = jnp.mean(x * x, axis=-1, keepdims=True)
    xn = x * lax.rsqrt(ms + NORM_EPS) * g
    shift = mod[:, shift_idx * d:(shift_idx + 1) * d]
    scale = mod[:, (shift_idx + 1) * d:(shift_idx + 2) * d]
    return xn * (1.0 + scale) + shift


def _mod_spec(layer, dims):
    tpb, ct, b = dims["tpb"], dims["ct"], dims["b"]
    return pl.BlockSpec((None, None, 1, 6 * dims["d"]),
                        lambda i, *_: (layer, _mod_row(i, tpb, ct, b), 0, 0))


def _store_token_tiles(ref, x):
    tokens, d = x.shape
    per = d // LANES
    for s in range(per):
        ref[pl.ds(s, tokens, stride=per), :] = x[:, s * LANES:(s + 1) * LANES]


def _load_token_tiles(ref, tokens, d):
    per = d // LANES
    return jnp.concatenate([ref[pl.ds(s, tokens, stride=per), :] for s in range(per)], axis=1)


def _gelu_tanh(y):
    return 0.5 * y * (1.0 + jnp.tanh(math.sqrt(2.0 / math.pi) * (y + 0.044715 * (y * y * y))))


_NT = (((1,), (1,)), ((), ()))
_TN = (((0,), (0,)), ((), ()))


def _mproj_kernel(x_ref, g_ref, mod_ref, w_ref, wt_ref, q_ref, k_ref, vt_ref, ot_ref, gc_ref, gr_ref):
    d = x_ref.shape[-1]
    nq = q_ref.shape[-1]
    h = _modulated(x_ref[...], g_ref[...], mod_ref[...], 0).astype(BF16)
    q_ref[...] = jnp.dot(h, w_ref[:, 0:nq], preferred_element_type=F32).astype(BF16)
    k_ref[...] = jnp.dot(h, w_ref[:, nq:2 * nq], preferred_element_type=F32).astype(BF16)
    gc_ref[...] = jnp.dot(h, w_ref[:, 2 * nq:2 * nq + LANES], preferred_element_type=F32)
    vt_ref[...] = lax.dot_general(wt_ref[0:d, :], h, _NT, preferred_element_type=F32).astype(BF16)
    ot = lax.dot_general(wt_ref[d:2 * d, :], h, _NT, preferred_element_type=F32)
    ot_ref[...] = jax.nn.sigmoid(ot).astype(BF16)
    gr_ref[...] = lax.dot_general(wt_ref[2 * d:2 * d + 4 * HEADS, :], h, _NT, preferred_element_type=F32)


def _mproj_call(xs, g, mods, layer, w_tok, w_feat, dims):
    tm, d, n = TOKEN_TILE, dims["d"], dims["n"]
    nq = HEADS * HEAD_QK
    row = lambda i: (i, 0)
    col = lambda i: (0, i)
    const = lambda i: (0, 0)
    return pl.pallas_call(
        _mproj_kernel,
        grid=(dims["tiles"],),
        in_specs=[
            pl.BlockSpec((tm, d), row),
            pl.BlockSpec((1, d), const),
            _mod_spec(layer, dims),
            pl.BlockSpec(w_tok.shape, const),
            pl.BlockSpec(w_feat.shape, const),
        ],
        out_specs=[
            pl.BlockSpec((tm, nq), row),
            pl.BlockSpec((tm, nq), row),
            pl.BlockSpec((d, tm), col),
            pl.BlockSpec((d, tm), col),
            pl.BlockSpec((tm, LANES), row),
            pl.BlockSpec((4 * HEADS, tm), col),
        ],
        out_shape=[
            jax.ShapeDtypeStruct((n, nq), BF16),
            jax.ShapeDtypeStruct((n, nq), BF16),
            jax.ShapeDtypeStruct((d, n), BF16),
            jax.ShapeDtypeStruct((d, n), BF16),
            jax.ShapeDtypeStruct((n, LANES), F32),
            jax.ShapeDtypeStruct((4 * HEADS, n), F32),
        ],
        compiler_params=_params("arbitrary"),
        name="mlstm_proj",
    )(xs, g, mods, w_tok, w_feat)


def _mlstm_direction(q_ref, k_ref, vt_ref, gc_ref, gr_ref, bc_ref, br_ref, o_ref, ct_ref, m_ref,
                     *, backward):
    tm = q_ref.shape[0]
    gate_i = 2 * HEADS if backward else 0
    gate_f = gate_i + HEADS
    last = 0 if backward else tm - 1
    src = lax.broadcasted_iota(jnp.int32, (tm, tm), 0)
    tgt = lax.broadcasted_iota(jnp.int32, (tm, tm), 1)
    visible = (src >= tgt) if backward else (src <= tgt)
    neg_mask = jnp.where(visible, 0.0, -jnp.inf)
    xc = gc_ref[...] + bc_ref[...]
    xr = gr_ref[...] + br_ref[...]
    before = (src <= tgt) if backward else (src >= tgt)
    cum_c = _split_dot(jnp.where(before, 1.0, 0.0).astype(BF16), jax.nn.log_sigmoid(xc), left=True)
    cum_r = _split_dot(jnp.where(visible, 1.0, 0.0).astype(BF16), jax.nn.log_sigmoid(xr), left=False)
    b_row = cum_r[gate_f:gate_f + HEADS, :]
    c_row = xr[gate_i:gate_i + HEADS, :] - b_row
    c_col = xc - pltpu.roll(cum_c, LANES - HEADS, axis=1)
    lane = lax.broadcasted_iota(jnp.int32, (HEADS, tm), 1)
    run = c_row
    shift = 1
    while shift < tm:
        if backward:
            moved = jnp.where(lane + shift < tm, pltpu.roll(run, tm - shift, axis=1), -jnp.inf)
        else:
            moved = jnp.where(lane >= shift, pltpu.roll(run, shift, axis=1), -jnp.inf)
        run = jnp.maximum(run, moved)
        shift *= 2
    m = m_ref[:, 0:1]
    log_inter = b_row + m
    m_pos = jnp.maximum(log_inter, b_row + run)
    shift_row = b_row - m_pos
    w_inter = jnp.exp(log_inter - m_pos)
    floor = jnp.exp(-m_pos)
    b_last = b_row[:, last:last + 1]
    m_new = m_pos[:, last:last + 1]
    w_src = jnp.exp(b_last + c_row - m_new)
    decay = jnp.exp(b_last + m - m_new)
    m_ref[...] = jnp.broadcast_to(m_new, m_ref.shape)
    half = lax.broadcasted_iota(jnp.int32, (tm, LANES), 1) < HEAD_QK
    ones = jnp.ones((2 * SUBLANES, tm), BF16)
    for h in range(HEADS):
        pair = slice((h // 2) * LANES, (h // 2 + 1) * LANES)
        qp = q_ref[:, pair]
        km = jnp.where(half if h % 2 == 0 else jnp.logical_not(half), k_ref[:, pair], 0.0).astype(BF16)
        st = lax.dot_general(km, qp, _NT, preferred_element_type=F32)
        e = (neg_mask + shift_row[h:h + 1, :]) + c_col[:, gate_i + h:gate_i + h + 1]
        p = (st * jnp.exp(e)).astype(BF16)
        vaug = jnp.concatenate([vt_ref[h * HEAD_V:(h + 1) * HEAD_V, :], ones], axis=0)
        ct = ct_ref[h]
        acc = jnp.dot(vaug, p, preferred_element_type=F32)
        acc = acc + w_inter[h:h + 1, :] * lax.dot_general(ct.astype(BF16), qp, _NT,
                                                          preferred_element_type=F32)
        den = acc[HEAD_V:HEAD_V + 1, :]
        scale = 1.0 / jnp.maximum(jnp.abs(den), floor[h:h + 1, :])
        o_ref[h * HEAD_V:(h + 1) * HEAD_V, :] = (acc[0:HEAD_V, :] * scale).astype(o_ref.dtype)
        vs = (vaug.astype(F32) * w_src[h:h + 1, :]).astype(BF16)
        ct_ref[h] = decay[h:h + 1, :] * ct + jnp.dot(vs, km, preferred_element_type=F32)


def _split_dot(ones_mat, x, left):
    hi = x.astype(BF16)
    r1 = x - hi.astype(F32)
    mid = r1.astype(BF16)
    lo = (r1 - mid.astype(F32)).astype(BF16)
    out = None
    for piece in (lo, mid, hi):
        t = (jnp.dot(ones_mat, piece, preferred_element_type=F32) if left
             else jnp.dot(piece, ones_mat, preferred_element_type=F32))
        out = t if out is None else out + t
    return out


def _mlstm_kernel(qf, kf, vtf, gcf, grf, qb, kb, vtb, gcb, grb, bc_ref, br_ref, of_ref, ob_ref,
                  cf_ref, mf_ref, cb_ref, mb_ref, *, tiles_per_batch):
    @pl.when(pl.program_id(0) % tiles_per_batch == 0)
    def _():
        cf_ref[...] = jnp.zeros_like(cf_ref)
        cb_ref[...] = jnp.zeros_like(cb_ref)
        mf_ref[...] = jnp.zeros_like(mf_ref)
        mb_ref[...] = jnp.zeros_like(mb_ref)

    _mlstm_direction(qf, kf, vtf, gcf, grf, bc_ref, br_ref, of_ref, cf_ref, mf_ref, backward=False)
    _mlstm_direction(qb, kb, vtb, gcb, grb, bc_ref, br_ref, ob_ref, cb_ref, mb_ref, backward=True)


def _mlstm_call(q, k, vt, gc, gr, bias_c, bias_r, dims):
    tm, d, n = TOKEN_TILE, dims["d"], dims["n"]
    tpb, ct = dims["tpb"], dims["ct"]
    nq = q.shape[1]
    fwd = lambda i: (i, 0)
    bwd = lambda i: (_bwd_tile(i, tpb, ct), 0)
    fwd_t = lambda i: (0, i)
    bwd_t = lambda i: (0, _bwd_tile(i, tpb, ct))
    const = lambda i: (0, 0)
    state = [pltpu.VMEM((HEADS, HEAD_V + 2 * SUBLANES, LANES), F32), pltpu.VMEM((HEADS, LANES), F32)]
    return pl.pallas_call(
        functools.partial(_mlstm_kernel, tiles_per_batch=tpb),
        grid=(dims["tiles"],),
        in_specs=[
            pl.BlockSpec((tm, nq), fwd), pl.BlockSpec((tm, nq), fwd), pl.BlockSpec((d, tm), fwd_t),
            pl.BlockSpec((tm, LANES), fwd), pl.BlockSpec((4 * HEADS, tm), fwd_t),
            pl.BlockSpec((tm, nq), bwd), pl.BlockSpec((tm, nq), bwd), pl.BlockSpec((d, tm), bwd_t),
            pl.BlockSpec((tm, LANES), bwd), pl.BlockSpec((4 * HEADS, tm), bwd_t),
            pl.BlockSpec((1, LANES), const), pl.BlockSpec((4 * HEADS, 1), const),
        ],
        out_specs=[pl.BlockSpec((d, tm), fwd_t), pl.BlockSpec((d, tm), bwd_t)],
        out_shape=[jax.ShapeDtypeStruct((d, n), BF16), jax.ShapeDtypeStruct((d, n), BF16)],
        scratch_shapes=state + state,
        compiler_params=_params("arbitrary"),
        name="mlstm_scan",
    )(q, k, vt, gc, gr, q, k, vt, gc, gr, bias_c, bias_r)


def _mreadout_kernel(hf_ref, hb_ref, ot_ref, w_ref, x_ref, mod_ref, o_ref):
    d = x_ref.shape[-1]
    parts = []
    for h in range(HEADS):
        rows = slice(h * HEAD_V, (h + 1) * HEAD_V)
        hs = hf_ref[rows, :].astype(F32) + hb_ref[rows, :].astype(F32)
        r = lax.rsqrt(jnp.mean(hs * hs, axis=0, keepdims=True) + NORM_EPS)
        parts.append((hs * r * ot_ref[rows, :]).astype(BF16))
    t = jnp.concatenate(parts, axis=0)
    out = lax.dot_general(t, w_ref[...], _TN, preferred_element_type=F32)
    o_ref[...] = x_ref[...] + mod_ref[:, 2 * d:3 * d] * out


def _mreadout_call(hf, hb, ot, w_out, xs, mods, layer, dims):
    tm, d, n = TOKEN_TILE, dims["d"], dims["n"]
    row = lambda i: (i, 0)
    col = lambda i: (0, i)
    const = lambda i: (0, 0)
    return pl.pallas_call(
        _mreadout_kernel,
        grid=(dims["tiles"],),
        in_specs=[
            pl.BlockSpec((d, tm), col), pl.BlockSpec((d, tm), col), pl.BlockSpec((d, tm), col),
            pl.BlockSpec((d, d), const), pl.BlockSpec((tm, d), row), _mod_spec(layer, dims),
        ],
        out_specs=pl.BlockSpec((tm, d), row),
        out_shape=jax.ShapeDtypeStruct((n, d), F32),
        compiler_params=_params("arbitrary"),
        name="mlstm_readout",
    )(hf, hb, ot, w_out, xs, mods)


def _readout_kernel(hf_ref, hb_ref, gy_ref, w_ref, x_ref, mod_ref, o_ref):
    d = x_ref.shape[-1]
    t = _gelu_tanh(gy_ref[...].astype(F32)) * (hf_ref[...].astype(F32) + hb_ref[...].astype(F32))
    out = jnp.dot(t.astype(BF16), w_ref[...], preferred_element_type=F32)
    o_ref[...] = x_ref[...] + mod_ref[:, 2 * d:3 * d] * out


def _readout_call(hf, hb, gy, w_out, xs, mods, layer, dims):
    tm, d, n = TOKEN_TILE, dims["d"], dims["n"]
    row = lambda i: (i, 0)
    const = lambda i: (0, 0)
    return pl.pallas_call(
        _readout_kernel,
        grid=(dims["tiles"],),
        in_specs=[
            pl.BlockSpec((tm, d), row), pl.BlockSpec((tm, d), row), pl.BlockSpec((tm, d), row),
            pl.BlockSpec((d, d), const), pl.BlockSpec((tm, d), row), _mod_spec(layer, dims),
        ],
        out_specs=pl.BlockSpec((tm, d), row),
        out_shape=jax.ShapeDtypeStruct((n, d), F32),
        compiler_params=_params("arbitrary"),
        name="rglru_readout",
    )(hf, hb, gy, w_out, xs, mods)


def _rproj_kernel(x_ref, g_ref, mod_ref, w_ref, cw_ref, cb_ref, gy_ref, xc_ref, ext_ref, cur_ref,
                  *, tiles, tiles_per_batch, ctx_tiles):
    tm, d = x_ref.shape
    i = pl.program_id(0)

    def in_segment_neighbours(t):
        j = t % tiles_per_batch
        first = jnp.logical_or(j == 0, j == ctx_tiles)
        last = jnp.logical_or(j == ctx_tiles - 1, j == tiles_per_batch - 1)
        return jnp.logical_not(first), jnp.logical_not(last)

    @pl.when(i == 0)
    def _():
        ext_ref[...] = jnp.zeros_like(ext_ref)
        cur_ref[...] = jnp.zeros_like(cur_ref)

    @pl.when(i < tiles)
    def _():
        h = _modulated(x_ref[...], g_ref[...], mod_ref[...], 0).astype(BF16)
        gy_ref[...] = jnp.dot(h, w_ref[:, 0:d], preferred_element_type=F32).astype(BF16)
        cur_ref[...] = jnp.dot(h, w_ref[:, d:2 * d], preferred_element_type=F32)

    @pl.when(i > 0)
    def _():
        _, next_ok = in_segment_neighbours(i - 1)
        head_ok = jnp.logical_and(next_ok, i < tiles)
        ext_ref[SUBLANES + tm:2 * SUBLANES + tm, :] = jnp.where(head_ok, cur_ref[0:SUBLANES, :], 0.0)
        xc = cb_ref[...] + ext_ref[pl.ds(SUBLANES - CONV_LEFT, tm), :] * cw_ref[0:1, :]
        for k in range(1, CONV_TAPS):
            xc = xc + ext_ref[pl.ds(SUBLANES - CONV_LEFT + k, tm), :] * cw_ref[k:k + 1, :]
        xc_ref[...] = xc

    prev_ok, _ = in_segment_neighbours(i)
    ext_ref[0:SUBLANES, :] = jnp.where(prev_ok, ext_ref[tm:tm + SUBLANES, :], 0.0)
    ext_ref[SUBLANES:SUBLANES + tm, :] = cur_ref[...]


def _rproj_call(xs, g, mods, layer, w_in, conv_w, conv_b, dims):
    tm, d, n, tiles = TOKEN_TILE, dims["d"], dims["n"], dims["tiles"]
    tpb, ct, b = dims["tpb"], dims["ct"], dims["b"]
    cur = lambda i: (jnp.minimum(i, tiles - 1), 0)
    lag = lambda i: (jnp.maximum(i - 1, 0), 0)
    const = lambda i: (0, 0)
    mod_spec = pl.BlockSpec((None, None, 1, 6 * d),
                            lambda i: (layer, _mod_row(jnp.minimum(i, tiles - 1), tpb, ct, b), 0, 0))
    return pl.pallas_call(
        functools.partial(_rproj_kernel, tiles=tiles, tiles_per_batch=tpb, ctx_tiles=ct),
        grid=(tiles + 1,),
        in_specs=[pl.BlockSpec((tm, d), cur), pl.BlockSpec((1, d), const), mod_spec,
                  pl.BlockSpec(w_in.shape, const), pl.BlockSpec(conv_w.shape, const),
                  pl.BlockSpec(conv_b.shape, const)],
        out_specs=[pl.BlockSpec((tm, d), cur), pl.BlockSpec((tm, d), lag)],
        out_shape=[jax.ShapeDtypeStruct((n, d), BF16), jax.ShapeDtypeStruct((n, d), F32)],
        scratch_shapes=[pltpu.VMEM((tm + 2 * SUBLANES, d), F32), pltpu.VMEM((tm, d), F32)],
        compiler_params=_params("arbitrary"),
        name="rglru_proj",
    )(xs, g, mods, w_in, conv_w, conv_b)


def _sigmoid_tanh(x):
    return 0.5 * jnp.tanh(0.5 * x) + 0.5


def _lru_direction(xc_ref, wg_ref, bg_ref, lam_ref, a_ref, b_ref, *, direction):
    tm, d = xc_ref.shape
    bw = d // LRU_BLOCKS
    z = -lam_ref[direction]
    softplus = jnp.maximum(z, 0.0) + jnp.log(1.0 + jnp.exp(-jnp.abs(z)))
    for nb in range(LRU_BLOCKS):
        sl = slice(nb * bw, (nb + 1) * bw)
        xb = xc_ref[:, sl]
        gg = jnp.dot(xb.astype(BF16), wg_ref[direction, nb], preferred_element_type=F32)
        gg = gg + bg_ref[direction, :, nb * 2 * bw:(nb + 1) * 2 * bw]
        r = _sigmoid_tanh(gg[:, :bw])
        gi = _sigmoid_tanh(gg[:, bw:])
        log_a = -LRU_C * r * softplus[:, sl]
        th = jnp.tanh(log_a)
        a_ref[pl.ds(nb, tm, stride=LRU_BLOCKS), :] = jnp.exp(log_a)
        b_ref[pl.ds(nb, tm, stride=LRU_BLOCKS), :] = jnp.sqrt(-2.0 * th / (1.0 - th)) * (gi * xb)


def _lru_kernel(xc_f, xc_b, wg_ref, bg_ref, lam_ref, of_ref, ob_ref, af_ref, bf_ref, ab_ref, bb_ref,
                hf_ref, hb_ref, h_ref, *, tiles_per_batch):
    tm, d = xc_f.shape
    bw = d // LRU_BLOCKS

    @pl.when(pl.program_id(0) % tiles_per_batch == 0)
    def _():
        h_ref[...] = jnp.zeros_like(h_ref)

    _lru_direction(xc_f, wg_ref, bg_ref, lam_ref, af_ref, bf_ref, direction=0)
    _lru_direction(xc_b, wg_ref, bg_ref, lam_ref, ab_ref, bb_ref, direction=1)

    def step(t, carry):
        hf, hb = carry
        rf = pl.multiple_of(t * LRU_BLOCKS, LRU_BLOCKS)
        rb = pl.multiple_of((tm - 1 - t) * LRU_BLOCKS, LRU_BLOCKS)
        hf = af_ref[pl.ds(rf, LRU_BLOCKS), :] * hf + bf_ref[pl.ds(rf, LRU_BLOCKS), :]
        hb = ab_ref[pl.ds(rb, LRU_BLOCKS), :] * hb + bb_ref[pl.ds(rb, LRU_BLOCKS), :]
        hf_ref[pl.ds(rf, LRU_BLOCKS), :] = hf
        hb_ref[pl.ds(rb, LRU_BLOCKS), :] = hb
        return hf, hb

    hf, hb = lax.fori_loop(0, tm, step, (h_ref[0], h_ref[1]), unroll=8)
    h_ref[0] = hf
    h_ref[1] = hb
    for nb in range(LRU_BLOCKS):
        sl = slice(nb * bw, (nb + 1) * bw)
        of_ref[:, sl] = hf_ref[pl.ds(nb, tm, stride=LRU_BLOCKS), :].astype(BF16)
        ob_ref[:, sl] = hb_ref[pl.ds(nb, tm, stride=LRU_BLOCKS), :].astype(BF16)


def _lru_call(xc, w_gate, b_gate, lam, dims):
    tm, d, n = TOKEN_TILE, dims["d"], dims["n"]
    tpb, ct = dims["tpb"], dims["ct"]
    fwd = lambda i: (i, 0)
    bwd = lambda i: (_bwd_tile(i, tpb, ct), 0)
    c3 = lambda i: (0, 0, 0)
    c4 = lambda i: (0, 0, 0, 0)
    time_major = pltpu.VMEM((tm * LRU_BLOCKS, d // LRU_BLOCKS), F32)
    return pl.pallas_call(
        functools.partial(_lru_kernel, tiles_per_batch=tpb),
        grid=(dims["tiles"],),
        in_specs=[
            pl.BlockSpec((tm, d), fwd), pl.BlockSpec((tm, d), bwd),
            pl.BlockSpec(w_gate.shape, c4), pl.BlockSpec(b_gate.shape, c3), pl.BlockSpec(lam.shape, c3),
        ],
        out_specs=[pl.BlockSpec((tm, d), fwd), pl.BlockSpec((tm, d), bwd)],
        out_shape=[jax.ShapeDtypeStruct((n, d), BF16), jax.ShapeDtypeStruct((n, d), BF16)],
        scratch_shapes=[time_major] * 6 + [pltpu.VMEM((2, LRU_BLOCKS, d // LRU_BLOCKS), F32)],
        compiler_params=_params("arbitrary"),
        name="rglru_scan",
    )(xc, xc, w_gate, b_gate, lam)


def _first_argmax(vals):
    best = vals[0]
    idx = jnp.zeros_like(best)
    for k in range(1, len(vals)):
        better = vals[k] > best
        idx = jnp.where(better, float(k), idx)
        best = jnp.where(better, vals[k], best)
    return idx, best


def _pick(idx, vals):
    out = vals[0]
    for k in range(1, len(vals)):
        out = jnp.where(idx == float(k), vals[k], out)
    return out


def _router_kernel(x_ref, g_ref, mod_ref, wr_ref, br_ref, hx_ref, route_ref, cnt_ref, carry_ref):
    tm, d = x_ref.shape

    @pl.when(pl.program_id(0) == 0)
    def _():
        carry_ref[...] = jnp.zeros_like(carry_ref)

    h = _modulated(x_ref[...], g_ref[...], mod_ref[...], 3)
    _store_token_tiles(hx_ref, h)
    logits = lax.dot_general(wr_ref[...], h, (((1,), (1,)), ((), ())),
                             preferred_element_type=F32, precision=HIGHEST)
    e = jnp.exp(logits - jnp.max(logits, axis=0, keepdims=True))
    scores = e / jnp.sum(e, axis=0, keepdims=True)
    sel = scores + br_ref[...]
    sel_rows = [sel[k:k + 1, :] for k in range(N_EXPERTS)]
    group_scores = []
    for gi in range(N_GROUPS):
        v = sel_rows[gi * GROUP_SIZE:(gi + 1) * GROUP_SIZE]
        best = v[0] + v[1]
        for a, b in PAIRS[1:]:
            best = jnp.maximum(best, v[a] + v[b])
        group_scores.append(best)
    grp, _ = _first_argmax(group_scores)
    in_sel = [_pick(grp, [sel_rows[gi * GROUP_SIZE + k] for gi in range(N_GROUPS)])
              for k in range(GROUP_SIZE)]
    i1, _ = _first_argmax(in_sel)
    rest = [jnp.where(i1 == float(k), -jnp.inf, in_sel[k]) for k in range(GROUP_SIZE)]
    i2, _ = _first_argmax(rest)
    lo = jnp.minimum(i1, i2)
    hi = jnp.maximum(i1, i2)
    pair = jnp.zeros_like(lo)
    for k, (slot_a, slot_b) in enumerate(PAIR_SLOTS):
        here = jnp.logical_and(lo == float(min(slot_a, slot_b)), hi == float(max(slot_a, slot_b)))
        pair = jnp.where(here, float(k), pair)
    cls = grp * float(len(PAIR_SLOTS)) + pair
    crow = lax.broadcasted_iota(jnp.int32, (CLASS_ROWS, tm), 0).astype(F32)
    onehot = jnp.where(crow == cls, 1.0, 0.0)
    rows = lax.broadcasted_iota(jnp.int32, (tm, tm), 0)
    cols = lax.broadcasted_iota(jnp.int32, (tm, tm), 1)
    upper = jnp.where(rows <= cols, 1.0, 0.0).astype(BF16)
    cum = jnp.dot(onehot.astype(BF16), upper, preferred_element_type=F32)
    carry = carry_ref[:, 0:1]
    rank = jnp.sum(onehot * (cum - 1.0 + carry), axis=0, keepdims=True)
    new_carry = carry + jnp.sum(onehot, axis=1, keepdims=True)
    carry_ref[...] = jnp.broadcast_to(new_carry, carry_ref.shape)
    cnt_ref[...] = jnp.broadcast_to(new_carry, cnt_ref.shape)
    zero = jnp.zeros_like(cls)
    route_ref[...] = jnp.concatenate([cls, rank, zero, zero, zero, zero, zero, zero], axis=0)


def _router_call(xs, g, mods, layer, w_router_t, b_router_col, dims):
    tm, d, n = TOKEN_TILE, dims["d"], dims["n"]
    row = lambda i: (i, 0)
    const = lambda i: (0, 0)
    tiles_per_token = d // LANES
    return pl.pallas_call(
        _router_kernel,
        grid=(dims["tiles"],),
        in_specs=[pl.BlockSpec((tm, d), row), pl.BlockSpec((1, d), const), _mod_spec(layer, dims),
                  pl.BlockSpec(w_router_t.shape, const), pl.BlockSpec(b_router_col.shape, const)],
        out_specs=[pl.BlockSpec((tm * tiles_per_token, LANES), row),
                   pl.BlockSpec((SUBLANES, tm), lambda i: (0, i)),
                   pl.BlockSpec((CLASS_ROWS, LANES), const)],
        out_shape=[jax.ShapeDtypeStruct((n * tiles_per_token, LANES), F32),
                   jax.ShapeDtypeStruct((SUBLANES, n), F32),
                   jax.ShapeDtypeStruct((CLASS_ROWS, LANES), F32)],
        scratch_shapes=[pltpu.VMEM((CLASS_ROWS, LANES), F32)],
        compiler_params=_params("arbitrary"),
        name="router",
    )(xs, g, mods, w_router_t, b_router_col)


ROW_DMA_UNROLL = 8


def _dispatch_kernel(pos_ref, hx_ref, init_ref, o_ref, stage_ref, sems, *, tokens):
    del init_ref
    per = hx_ref.shape[0] // tokens
    i = pl.program_id(0)
    slot = i % 2
    base = i * tokens
    stage_ref[slot] = hx_ref[...]

    def issue(r, carry):
        src = pl.multiple_of(r * per, per)
        dst = pl.multiple_of(pos_ref[base + r], per)
        pltpu.make_async_copy(stage_ref.at[slot, pl.ds(src, per)], o_ref.at[pl.ds(dst, per)],
                              sems.at[slot]).start()
        return carry

    lax.fori_loop(0, tokens, issue, 0, unroll=ROW_DMA_UNROLL)

    def wait_tile(s):
        pltpu.make_async_copy(stage_ref.at[s], o_ref.at[pl.ds(0, tokens * per)], sems.at[s]).wait()

    @pl.when(i > 0)
    def _():
        wait_tile(1 - slot)

    @pl.when(i == pl.num_programs(0) - 1)
    def _():
        wait_tile(slot)


def _dispatch_call(pos_rows, hx, padded_tokens, dims):
    tm = TOKEN_TILE
    per = dims["d"] // LANES
    init = jnp.zeros((padded_tokens * per, LANES), hx.dtype)
    return pl.pallas_call(
        functools.partial(_dispatch_kernel, tokens=tm),
        grid_spec=pltpu.PrefetchScalarGridSpec(
            num_scalar_prefetch=1,
            grid=(dims["tiles"],),
            in_specs=[pl.BlockSpec((tm * per, LANES), lambda i, p: (i, 0)),
                      pl.BlockSpec(memory_space=pl.ANY)],
            out_specs=pl.BlockSpec(memory_space=pl.ANY),
            scratch_shapes=[pltpu.VMEM((2, tm * per, LANES), hx.dtype), pltpu.SemaphoreType.DMA((2,))],
        ),
        out_shape=jax.ShapeDtypeStruct((padded_tokens * per, LANES), hx.dtype),
        input_output_aliases={2: 0},
        compiler_params=_params("arbitrary"),
        name="dispatch",
    )(pos_rows, hx, init)


def _expert_kernel(ea_ref, eb_ref, ok_ref, x_ref, wr_ref, w1a, w3a, w2a, w1b, w3b, w2b, o_ref):
    d = w1a.shape[0]
    te = x_ref.shape[0] // (d // LANES)
    t = pl.program_id(0)

    @pl.when(ok_ref[t] != 0)
    def _():
        h = _load_token_tiles(x_ref, te, d)
        x = h.astype(BF16)
        dw = wr_ref[pl.ds(ea_ref[t], 1), :] - wr_ref[pl.ds(eb_ref[t], 1), :]
        gate_a = jax.nn.sigmoid(jnp.sum(h * dw, axis=1, keepdims=True))
        gate_b = 1.0 - gate_a

        def expert(w1, w3, w2):
            a = jnp.dot(x, w1[...], preferred_element_type=F32)
            b = jnp.dot(x, w3[...], preferred_element_type=F32)
            u = (a * jax.nn.sigmoid(a)) * b
            return jnp.dot(u.astype(BF16), w2[...], preferred_element_type=F32)

        _store_token_tiles(o_ref, gate_a * expert(w1a, w3a, w2a) + gate_b * expert(w1b, w3b, w2b))

    @pl.when(ok_ref[t] == 0)
    def _():
        o_ref[...] = jnp.zeros_like(o_ref)


def _expert_call(tile_a, tile_b, tile_ok, xsorted, w_router_t, w1, w3, w2, layer, d):
    te = TOKEN_TILE
    per = d // LANES
    tokens = xsorted.shape[0] // per
    de = w1.shape[-1]
    sel_a = lambda t, ea, eb, ok: (layer, ea[t], 0, 0)
    sel_b = lambda t, ea, eb, ok: (layer, eb[t], 0, 0)
    return pl.pallas_call(
        _expert_kernel,
        grid_spec=pltpu.PrefetchScalarGridSpec(
            num_scalar_prefetch=3,
            grid=(tokens // te,),
            in_specs=[
                pl.BlockSpec((te * per, LANES), lambda t, ea, eb, ok: (t, 0)),
                pl.BlockSpec(w_router_t.shape, lambda t, ea, eb, ok: (0, 0)),
                pl.BlockSpec((None, None, d, de), sel_a), pl.BlockSpec((None, None, d, de), sel_a),
                pl.BlockSpec((None, None, de, d), sel_a),
                pl.BlockSpec((None, None, d, de), sel_b), pl.BlockSpec((None, None, d, de), sel_b),
                pl.BlockSpec((None, None, de, d), sel_b),
            ],
            out_specs=pl.BlockSpec((te * per, LANES), lambda t, ea, eb, ok: (t, 0)),
        ),
        out_shape=jax.ShapeDtypeStruct((tokens * per, LANES), F32),
        compiler_params=_params("arbitrary"),
        name="experts",
    )(tile_a, tile_b, tile_ok, xsorted, w_router_t, w1, w3, w2, w1, w3, w2)


def _combine_kernel(pos_ref, y_ref, x_ref, mod_ref, gf_ref, o_ref, buf_ref, sems,
                    *, tile_of, final_norm):
    tm, d = x_ref.shape
    per = d // LANES
    i = pl.program_id(0)
    slot = i % 2

    def gather(step, s):
        base = tile_of(step) * tm

        def issue(r, carry):
            src = pl.multiple_of(pos_ref[base + r], per)
            dst = pl.multiple_of(r * per, per)
            pltpu.make_async_copy(y_ref.at[pl.ds(src, per)], buf_ref.at[s, pl.ds(dst, per)],
                                  sems.at[s]).start()
            return carry

        lax.fori_loop(0, tm, issue, 0, unroll=ROW_DMA_UNROLL)

    @pl.when(i == 0)
    def _():
        gather(0, 0)

    @pl.when(i + 1 < pl.num_programs(0))
    def _():
        gather(i + 1, 1 - slot)

    pltpu.make_async_copy(y_ref.at[pl.ds(0, tm * per)], buf_ref.at[slot], sems.at[slot]).wait()
    out = x_ref[...] + mod_ref[:, 5 * d:6 * d] * _load_token_tiles(buf_ref.at[slot], tm, d)
    if final_norm:
        ms = jnp.mean(out * out, axis=-1, keepdims=True)
        out = out * lax.rsqrt(ms + NORM_EPS) * gf_ref[...]
    o_ref[...] = out


def _combine_call(pos, ysorted, xs, mods, layer, g_final, dims, final_norm):
    tm, d = TOKEN_TILE, dims["d"]
    tpb, ct, b = dims["tpb"], dims["ct"], dims["b"]
    if final_norm:
        lt = tpb - ct
        tile_of = lambda i: (i // lt) * tpb + ct + i % lt
        n_tiles = b * lt
    else:
        tile_of = lambda i: i
        n_tiles = dims["tiles"]
    mod_spec = pl.BlockSpec((None, None, 1, 6 * d),
                            lambda i, p: (layer, _mod_row(tile_of(i), tpb, ct, b), 0, 0))
    return pl.pallas_call(
        functools.partial(_combine_kernel, tile_of=tile_of, final_norm=final_norm),
        grid_spec=pltpu.PrefetchScalarGridSpec(
            num_scalar_prefetch=1,
            grid=(n_tiles,),
            in_specs=[
                pl.BlockSpec(memory_space=pl.ANY),
                pl.BlockSpec((tm, d), lambda i, p: (tile_of(i), 0)),
                mod_spec,
                pl.BlockSpec((1, d), lambda i, p: (0, 0)),
            ],
            out_specs=pl.BlockSpec((tm, d), lambda i, p: (i, 0)),
            scratch_shapes=[pltpu.VMEM((2, tm * (d // LANES), LANES), F32), pltpu.SemaphoreType.DMA((2,))],
        ),
        out_shape=jax.ShapeDtypeStruct((n_tiles * tm, d), F32),
        compiler_params=_params("arbitrary"),
        name="combine",
    )(pos, ysorted, xs, mods, g_final)


_CLASS_A = np.array([g * GROUP_SIZE + a for g in range(N_GROUPS) for a, _ in PAIR_SLOTS], np.int32)
_CLASS_B = np.array([g * GROUP_SIZE + b for g in range(N_GROUPS) for _, b in PAIR_SLOTS], np.int32)


def _routing_plan(route, counts, n_tiles_padded):
    te = TOKEN_TILE
    cls = route[0].astype(jnp.int32)
    rank = route[1].astype(jnp.int32)
    cnt = counts[:N_CLASSES, 0].astype(jnp.int32)
    tiles = (cnt + te - 1) // te
    tile_end = jnp.cumsum(tiles)
    tile_start = tile_end - tiles
    onehot = (cls[:, None] == jnp.arange(N_CLASSES, dtype=jnp.int32)[None, :]).astype(jnp.int32)
    pos = jnp.sum(onehot * (tile_start * te)[None, :], axis=1) + rank
    t = jnp.arange(n_tiles_padded, dtype=jnp.int32)
    ok = t < tile_end[-1]
    tcls = jnp.sum((tile_end[None, :] <= jnp.minimum(t, tile_end[-1] - 1)[:, None]).astype(jnp.int32), axis=1)
    tile_a = jnp.take(jnp.asarray(_CLASS_A), tcls)
    tile_b = jnp.take(jnp.asarray(_CLASS_B), tcls)
    return pos, tile_a, tile_b, ok.astype(jnp.int32)


def _moe(xs, g, mods, layer, w_router_t, b_router_col, w1, w3, w2, g_final, dims, final_norm):
    hx, route, counts = _router_call(xs, g, mods, layer, w_router_t, b_router_col, dims)
    n_tiles_padded = dims["tiles"] + N_CLASSES
    pos, tile_a, tile_b, tile_ok = _routing_plan(route, counts, n_tiles_padded)
    pos_rows = pos * (dims["d"] // LANES)
    xsorted = _dispatch_call(pos_rows, hx, n_tiles_padded * TOKEN_TILE, dims)
    ysorted = _expert_call(tile_a, tile_b, tile_ok, xsorted, w_router_t, w1, w3, w2, layer, dims["d"])
    return _combine_call(pos_rows, ysorted, xs, mods, layer, g_final, dims, final_norm)


def _pos_table(n_tokens, dim):
    rows = n_tokens // GRID_WIDTH
    r, col = jnp.meshgrid(jnp.arange(rows, dtype=F32), jnp.arange(GRID_WIDTH, dtype=F32), indexing="ij")
    quarter = dim // 4
    freqs = jnp.exp(-math.log(POS_BASE) * jnp.arange(quarter, dtype=F32) / quarter)

    def enc(p):
        ang = p.reshape(-1, 1) * freqs
        return jnp.concatenate([jnp.sin(ang), jnp.cos(ang)], axis=-1)

    return jnp.concatenate([enc(r), enc(col)], axis=-1)


def _mlstm_weights(w_in, b_gate):
    nq = HEADS * HEAD_QK
    nv = HEADS * HEAD_V
    wq = w_in[:, 0:nq] * (HEAD_QK ** -0.5)
    wk = w_in[:, nq:2 * nq]
    wv = w_in[:, 2 * nq:2 * nq + nv]
    wo = w_in[:, 2 * nq + nv:2 * nq + 2 * nv]
    wg = w_in[:, 2 * nq + 2 * nv:]
    wg_pad = jnp.pad(wg, ((0, 0), (0, LANES - wg.shape[1])))
    w_tok = jnp.concatenate([wq, wk, wg_pad], axis=1).astype(BF16)
    w_feat = jnp.concatenate([wv, wo, wg], axis=1).T.astype(BF16)
    bias = b_gate.reshape(-1).astype(F32)
    bias_c = jnp.pad(bias, (0, LANES - bias.shape[0])).reshape(1, LANES)
    bias_r = bias.reshape(-1, 1)
    return w_tok, w_feat, bias_c, bias_r


def kernel(x, c, ctx, c_ctx, w_ada, b_ada, g_mix, g_ffn, g_final, m_w_in, m_b_gate, m_g_head, m_w_out, r_w_in, r_conv_w, r_conv_b, r_w_gate, r_b_gate, r_lam, r_w_out, w_router, b_router, e_w1, e_w3, e_w2):
    batch, t_len, d = x.shape
    ctx_len = ctx.shape[1]
    depth = w_ada.shape[0]
    tm = TOKEN_TILE
    assert t_len % tm == 0 and ctx_len % tm == 0 and batch + 1 <= ADA_ROWS
    assert d == HEADS * HEAD_V and t_len % GRID_WIDTH == 0
    s_len = ctx_len + t_len
    dims = dict(b=batch, d=d, n=batch * s_len, tpb=s_len // tm, ct=ctx_len // tm,
                tiles=batch * s_len // tm)

    cvec = jnp.concatenate([c, c_ctx[None, :], jnp.zeros((ADA_ROWS - batch - 1, d), F32)], axis=0)
    mods = _ada_call(cvec, w_ada, b_ada).reshape(depth, ADA_ROWS, 1, 6 * d)
    xs = _stream_call(ctx.reshape(batch * ctx_len, d), x.reshape(batch * t_len, d),
                      _pos_table(t_len, d), dims)

    w_router_t = w_router.T.astype(F32)
    b_router_col = b_router.reshape(-1, 1).astype(F32)
    g_final2 = g_final.reshape(1, d)
    e_w1b, e_w3b, e_w2b = e_w1.astype(BF16), e_w3.astype(BF16), e_w2.astype(BF16)
    out = None
    for i in range(depth):
        j = i // 2
        g_mix_i = g_mix[i].reshape(1, d)
        if i % 2 == 0:
            w_tok, w_feat, bias_c, bias_r = _mlstm_weights(m_w_in[j], m_b_gate[j])
            q, k, vt, ot, gc, gr = _mproj_call(xs, g_mix_i, mods, i, w_tok, w_feat, dims)
            hf, hb = _mlstm_call(q, k, vt, gc, gr, bias_c, bias_r, dims)
            w_out = (m_g_head[j][:, None] * m_w_out[j]).astype(BF16)
            xs = _mreadout_call(hf, hb, ot, w_out, xs, mods, i, dims)
        else:
            gy, xc = _rproj_call(xs, g_mix_i, mods, i, r_w_in[j].astype(BF16), r_conv_w[j],
                                 r_conv_b[j].reshape(1, d), dims)
            hf, hb = _lru_call(xc, r_w_gate[j].astype(BF16), r_b_gate[j].reshape(2, 1, -1),
                               r_lam[j].reshape(2, 1, d), dims)
            xs = _readout_call(hf, hb, gy, r_w_out[j].astype(BF16), xs, mods, i, dims)
        last = i == depth - 1
        res = _moe(xs, g_ffn[i].reshape(1, d), mods, i, w_router_t, b_router_col,
                   e_w1b, e_w3b, e_w2b, g_final2, dims, last)
        if last:
            out = res
        else:
            xs = res
    return out.reshape(batch, t_len, d)
```

```python
import functools
import math

import jax
import jax.numpy as jnp
import numpy as np
from jax import lax
from jax.experimental import pallas as pl
from jax.experimental.pallas import tpu as pltpu

F32 = jnp.float32
BF16 = jnp.bfloat16
HIGHEST = lax.Precision.HIGHEST

TOKEN_TILE = 256
LANES = 128
SUBLANES = 8
VMEM_LIMIT_BYTES = 56 * 1024 * 1024
NORM_EPS = 1e-6
GRID_WIDTH = 64
POS_BASE = 10000.0
HEADS = 8
HEAD_QK = 64
HEAD_V = 128
LRU_BLOCKS = 8
LRU_C = 8.0
CONV_TAPS = 4
CONV_LEFT = CONV_TAPS // 2
N_EXPERTS = 16
N_GROUPS = 4
GROUP_SIZE = N_EXPERTS // N_GROUPS
PAIRS = [(a, b) for a in range(GROUP_SIZE) for b in range(a + 1, GROUP_SIZE)]
PAIR_SLOTS = [(0, 1), (0, 2), (0, 3), (1, 3), (1, 2), (3, 2)]
N_CLASSES = N_GROUPS * len(PAIRS)
CLASS_ROWS = 32
ROUTE_COLS = LANES
ADA_ROWS = 16


def _params(*sem):
    return pltpu.CompilerParams(dimension_semantics=sem, vmem_limit_bytes=VMEM_LIMIT_BYTES)


def _mod_row(i, tiles_per_batch, ctx_tiles, batch):
    return jnp.where(i % tiles_per_batch < ctx_tiles, batch, i // tiles_per_batch)


def _bwd_tile(i, tiles_per_batch, ctx_tiles):
    b = i // tiles_per_batch
    j = i % tiles_per_batch
    jb = jnp.where(j < ctx_tiles, ctx_tiles - 1 - j, tiles_per_batch - 1 - (j - ctx_tiles))
    return b * tiles_per_batch + jb


def _ada_kernel(c_ref, w_ref, b_ref, o_ref):
    c = c_ref[...]
    s = c * jax.nn.sigmoid(c)
    o_ref[...] = jnp.dot(s, w_ref[...], preferred_element_type=F32, precision=HIGHEST) + b_ref[...]


def _ada_call(cvec, w_ada, b_ada):
    depth, d, n = w_ada.shape
    tn = n // 4
    return pl.pallas_call(
        _ada_kernel,
        grid=(depth, n // tn),
        in_specs=[
            pl.BlockSpec((ADA_ROWS, d), lambda l, j: (0, 0)),
            pl.BlockSpec((None, d, tn), lambda l, j: (l, 0, j)),
            pl.BlockSpec((None, 1, tn), lambda l, j: (l, 0, j)),
        ],
        out_specs=pl.BlockSpec((None, ADA_ROWS, tn), lambda l, j: (l, 0, j)),
        out_shape=jax.ShapeDtypeStruct((depth, ADA_ROWS, n), F32),
        compiler_params=_params("arbitrary", "arbitrary"),
        name="ada",
    )(cvec, w_ada, b_ada.reshape(depth, 1, n))


def _stream_kernel(ctx_ref, x_ref, pos_ref, o_ref, *, tiles_per_batch, ctx_tiles):
    j = pl.program_id(0) % tiles_per_batch

    @pl.when(j < ctx_tiles)
    def _():
        o_ref[...] = ctx_ref[...]

    @pl.when(j >= ctx_tiles)
    def _():
        o_ref[...] = x_ref[...] + pos_ref[...]


def _stream_call(ctx2, x2, pos, dims):
    tm, d = TOKEN_TILE, dims["d"]
    tpb, ct = dims["tpb"], dims["ct"]
    lt = tpb - ct
    return pl.pallas_call(
        functools.partial(_stream_kernel, tiles_per_batch=tpb, ctx_tiles=ct),
        grid=(dims["tiles"],),
        in_specs=[
            pl.BlockSpec((tm, d), lambda i: ((i // tpb) * ct + jnp.minimum(i % tpb, ct - 1), 0)),
            pl.BlockSpec((tm, d), lambda i: ((i // tpb) * lt + jnp.maximum(i % tpb - ct, 0), 0)),
            pl.BlockSpec((tm, d), lambda i: (jnp.maximum(i % tpb - ct, 0), 0)),
        ],
        out_specs=pl.BlockSpec((tm, d), lambda i: (i, 0)),
        out_shape=jax.ShapeDtypeStruct((dims["n"], d), F32),
        compiler_params=_params("arbitrary"),
        name="stream",
    )(ctx2, x2, pos)


def _modulated(x, g, mod, shift_idx):
    d = x.shape[-1]
    ms = jnp.mean(x * x, axis=-1, keepdims=True)
    xn = x * lax.rsqrt(ms + NORM_EPS) * g
    shift = mod[:, shift_idx * d:(shift_idx + 1) * d]
    scale = mod[:, (shift_idx + 1) * d:(shift_idx + 2) * d]
    return xn * (1.0 + scale) + shift


def _mod_spec(layer, dims):
    tpb, ct, b = dims["tpb"], dims["ct"], dims["b"]
    return pl.BlockSpec((None, None, 1, 6 * dims["d"]),
                        lambda i, *_: (layer, _mod_row(i, tpb, ct, b), 0, 0))


def _store_token_tiles(ref, x):
    tokens, d = x.shape
    per = d // LANES
    for s in range(per):
        ref[pl.ds(s, tokens, stride=per), :] = x[:, s * LANES:(s + 1) * LANES]


def _load_token_tiles(ref, tokens, d):
    per = d // LANES
    return jnp.concatenate([ref[pl.ds(s, tokens, stride=per), :] for s in range(per)], axis=1)


def _gelu_tanh(y):
    return 0.5 * y * (1.0 + jnp.tanh(math.sqrt(2.0 / math.pi) * (y + 0.044715 * (y * y * y))))


SRC_BLOCK = 128
_NT = (((1,), (1,)), ((), ()))
_TN = (((0,), (0,)), ((), ()))


def _mproj_kernel(x_ref, g_ref, mod_ref, w_ref, wt_ref, q_ref, k_ref, vt_ref, ot_ref, gc_ref, gr_ref):
    d = x_ref.shape[-1]
    nq = q_ref.shape[-1]
    h = _modulated(x_ref[...], g_ref[...], mod_ref[...], 0).astype(BF16)
    q_ref[...] = jnp.dot(h, w_ref[:, 0:nq], preferred_element_type=F32).astype(BF16)
    k_ref[...] = jnp.dot(h, w_ref[:, nq:2 * nq], preferred_element_type=F32).astype(BF16)
    gc_ref[...] = jnp.dot(h, w_ref[:, 2 * nq:2 * nq + LANES], preferred_element_type=F32)
    vt_ref[...] = lax.dot_general(wt_ref[0:d, :], h, _NT, preferred_element_type=F32).astype(BF16)
    ot = lax.dot_general(wt_ref[d:2 * d, :], h, _NT, preferred_element_type=F32)
    ot_ref[...] = jax.nn.sigmoid(ot).astype(BF16)
    gr_ref[...] = lax.dot_general(wt_ref[2 * d:2 * d + 4 * HEADS, :], h, _NT, preferred_element_type=F32)


def _mproj_call(xs, g, mods, layer, w_tok, w_feat, dims):
    tm, d, n = TOKEN_TILE, dims["d"], dims["n"]
    nq = HEADS * HEAD_QK
    row = lambda i: (i, 0)
    col = lambda i: (0, i)
    const = lambda i: (0, 0)
    return pl.pallas_call(
        _mproj_kernel,
        grid=(dims["tiles"],),
        in_specs=[
            pl.BlockSpec((tm, d), row),
            pl.BlockSpec((1, d), const),
            _mod_spec(layer, dims),
            pl.BlockSpec(w_tok.shape, const),
            pl.BlockSpec(w_feat.shape, const),
        ],
        out_specs=[
            pl.BlockSpec((tm, nq), row),
            pl.BlockSpec((tm, nq), row),
            pl.BlockSpec((d, tm), col),
            pl.BlockSpec((d, tm), col),
            pl.BlockSpec((tm, LANES), row),
            pl.BlockSpec((4 * HEADS, tm), col),
        ],
        out_shape=[
            jax.ShapeDtypeStruct((n, nq), BF16),
            jax.ShapeDtypeStruct((n, nq), BF16),
            jax.ShapeDtypeStruct((d, n), BF16),
            jax.ShapeDtypeStruct((d, n), BF16),
            jax.ShapeDtypeStruct((n, LANES), F32),
            jax.ShapeDtypeStruct((4 * HEADS, n), F32),
        ],
        compiler_params=_params("arbitrary"),
        name="mlstm_proj",
    )(xs, g, mods, w_tok, w_feat)


def _mlstm_gate_stats(gc_ref, gr_ref, bc_ref, br_ref, m_ref, *, backward):
    tm = gc_ref.shape[0]
    gate_i = 2 * HEADS if backward else 0
    gate_f = gate_i + HEADS
    last = 0 if backward else tm - 1
    src = lax.broadcasted_iota(jnp.int32, (tm, tm), 0)
    tgt = lax.broadcasted_iota(jnp.int32, (tm, tm), 1)
    visible = (src >= tgt) if backward else (src <= tgt)
    neg_mask = jnp.where(visible, 0.0, -jnp.inf)
    xc = gc_ref[...] + bc_ref[...]
    xr = gr_ref[...] + br_ref[...]
    before = (src <= tgt) if backward else (src >= tgt)
    cum_c = _split_dot(jnp.where(before, 1.0, 0.0).astype(BF16), jax.nn.log_sigmoid(xc), left=True)
    cum_r = _split_dot(jnp.where(visible, 1.0, 0.0).astype(BF16), jax.nn.log_sigmoid(xr), left=False)
    b_row = cum_r[gate_f:gate_f + HEADS, :]
    c_row = xr[gate_i:gate_i + HEADS, :] - b_row
    c_col = xc - pltpu.roll(cum_c, LANES - HEADS, axis=1)
    run_col = c_col
    shift = 1
    while shift < tm:
        fill = jnp.full((shift, LANES), -jnp.inf, F32)
        if backward:
            moved = jnp.concatenate([run_col[shift:, :], fill], axis=0)
        else:
            moved = jnp.concatenate([fill, run_col[:tm - shift, :]], axis=0)
        run_col = jnp.maximum(run_col, moved)
        shift *= 2
    run = run_col.T[gate_i:gate_i + HEADS, :]
    m = m_ref[:, 0:1]
    log_inter = b_row + m
    m_pos = jnp.maximum(log_inter, b_row + run)
    shift_row = b_row - m_pos
    w_inter = jnp.exp(log_inter - m_pos)
    floor = jnp.exp(-m_pos)
    b_last = b_row[:, last:last + 1]
    m_new = m_pos[:, last:last + 1]
    w_src = jnp.exp(b_last + c_row - m_new)
    decay = jnp.exp(b_last + m - m_new)
    m_ref[...] = jnp.broadcast_to(m_new, m_ref.shape)
    return neg_mask, c_col, shift_row, w_inter, floor, w_src, decay


def _mlstm_head(h, q_ref, k_ref, vt_ref, o_ref, ct_ref, stats, *, backward):
    neg_mask, c_col, shift_row, w_inter, floor, w_src, decay = stats
    tm = q_ref.shape[0]
    gate_i = 2 * HEADS if backward else 0
    half = lax.broadcasted_iota(jnp.int32, (tm, LANES), 1) < HEAD_QK
    ones = jnp.ones((2 * SUBLANES, tm), BF16)
    if True:
        pair = slice((h // 2) * LANES, (h // 2 + 1) * LANES)
        qp = q_ref[:, pair]
        km = jnp.where(half if h % 2 == 0 else jnp.logical_not(half), k_ref[:, pair], 0.0).astype(BF16)
        vaug = jnp.concatenate([vt_ref[h * HEAD_V:(h + 1) * HEAD_V, :], ones], axis=0)
        ct = ct_ref[h]
        acc = w_inter[h:h + 1, :] * lax.dot_general(ct.astype(BF16), qp, _NT,
                                                    preferred_element_type=F32)
        for blk in range(tm // SRC_BLOCK):
            rows = slice(blk * SRC_BLOCK, (blk + 1) * SRC_BLOCK)
            st = lax.dot_general(km[rows, :], qp, _NT, preferred_element_type=F32)
            e = (neg_mask[rows, :] + shift_row[h:h + 1, :]) + c_col[rows, gate_i + h:gate_i + h + 1]
            p = (st * jnp.exp(e)).astype(BF16)
            acc = acc + jnp.dot(vaug[:, rows], p, preferred_element_type=F32)
        den = acc[HEAD_V:HEAD_V + 1, :]
        scale = 1.0 / jnp.maximum(jnp.abs(den), floor[h:h + 1, :])
        o_ref[h * HEAD_V:(h + 1) * HEAD_V, :] = (acc[0:HEAD_V, :] * scale).astype(o_ref.dtype)
        vs = (vaug.astype(F32) * w_src[h:h + 1, :]).astype(BF16)
        ct_ref[h] = decay[h:h + 1, :] * ct + jnp.dot(vs, km, preferred_element_type=F32)


def _split_dot(ones_mat, x, left):
    hi = x.astype(BF16)
    r1 = x - hi.astype(F32)
    mid = r1.astype(BF16)
    lo = (r1 - mid.astype(F32)).astype(BF16)
    out = None
    for piece in (lo, mid, hi):
        t = (jnp.dot(ones_mat, piece, preferred_element_type=F32) if left
             else jnp.dot(piece, ones_mat, preferred_element_type=F32))
        out = t if out is None else out + t
    return out


def _mlstm_kernel(qf, kf, vtf, gcf, grf, qb, kb, vtb, gcb, grb, bc_ref, br_ref, of_ref, ob_ref,
                  cf_ref, mf_ref, cb_ref, mb_ref, *, tiles_per_batch):
    @pl.when(pl.program_id(0) % tiles_per_batch == 0)
    def _():
        cf_ref[...] = jnp.zeros_like(cf_ref)
        cb_ref[...] = jnp.zeros_like(cb_ref)
        mf_ref[...] = jnp.zeros_like(mf_ref)
        mb_ref[...] = jnp.zeros_like(mb_ref)

    stats_f = _mlstm_gate_stats(gcf, grf, bc_ref, br_ref, mf_ref, backward=False)
    stats_b = _mlstm_gate_stats(gcb, grb, bc_ref, br_ref, mb_ref, backward=True)
    for h in range(HEADS):
        _mlstm_head(h, qf, kf, vtf, of_ref, cf_ref, stats_f, backward=False)
    for h in range(HEADS):
        _mlstm_head(h, qb, kb, vtb, ob_ref, cb_ref, stats_b, backward=True)


def _mlstm_call(q, k, vt, gc, gr, bias_c, bias_r, dims):
    tm, d, n = TOKEN_TILE, dims["d"], dims["n"]
    tpb, ct = dims["tpb"], dims["ct"]
    nq = q.shape[1]
    fwd = lambda i: (i, 0)
    bwd = lambda i: (_bwd_tile(i, tpb, ct), 0)
    fwd_t = lambda i: (0, i)
    bwd_t = lambda i: (0, _bwd_tile(i, tpb, ct))
    const = lambda i: (0, 0)
    state = [pltpu.VMEM((HEADS, HEAD_V + 2 * SUBLANES, LANES), F32), pltpu.VMEM((HEADS, LANES), F32)]
    return pl.pallas_call(
        functools.partial(_mlstm_kernel, tiles_per_batch=tpb),
        grid=(dims["tiles"],),
        in_specs=[
            pl.BlockSpec((tm, nq), fwd), pl.BlockSpec((tm, nq), fwd), pl.BlockSpec((d, tm), fwd_t),
            pl.BlockSpec((tm, LANES), fwd), pl.BlockSpec((4 * HEADS, tm), fwd_t),
            pl.BlockSpec((tm, nq), bwd), pl.BlockSpec((tm, nq), bwd), pl.BlockSpec((d, tm), bwd_t),
            pl.BlockSpec((tm, LANES), bwd), pl.BlockSpec((4 * HEADS, tm), bwd_t),
            pl.BlockSpec((1, LANES), const), pl.BlockSpec((4 * HEADS, 1), const),
        ],
        out_specs=[pl.BlockSpec((d, tm), fwd_t), pl.BlockSpec((d, tm), bwd_t)],
        out_shape=[jax.ShapeDtypeStruct((d, n), BF16), jax.ShapeDtypeStruct((d, n), BF16)],
        scratch_shapes=state + state,
        compiler_params=_params("arbitrary"),
        name="mlstm_scan",
    )(q, k, vt, gc, gr, q, k, vt, gc, gr, bias_c, bias_r)


def _mreadout_kernel(hf_ref, hb_ref, ot_ref, w_ref, x_ref, mod_ref, o_ref):
    d = x_ref.shape[-1]
    parts = []
    for h in range(HEADS):
        rows = slice(h * HEAD_V, (h + 1) * HEAD_V)
        hs = hf_ref[rows, :].astype(F32) + hb_ref[rows, :].astype(F32)
        r = lax.rsqrt(jnp.mean(hs * hs, axis=0, keepdims=True) + NORM_EPS)
        parts.append((hs * r * ot_ref[rows, :]).astype(BF16))
    t = jnp.concatenate(parts, axis=0)
    out = lax.dot_general(t, w_ref[...], _TN, preferred_element_type=F32)
    o_ref[...] = x_ref[...] + mod_ref[:, 2 * d:3 * d] * out


def _readout_kernel(hf_ref, hb_ref, gy_ref, w_ref, x_ref, mod_ref, o_ref):
    d = x_ref.shape[-1]
    t = _gelu_tanh(gy_ref[...].astype(F32)) * (hf_ref[...].astype(F32) + hb_ref[...].astype(F32))
    out = jnp.dot(t.astype(BF16), w_ref[...], preferred_element_type=F32)
    o_ref[...] = x_ref[...] + mod_ref[:, 2 * d:3 * d] * out


def _readout_call(kernel_fn, name, feature_major, hf, hb, aux, w_out, xs, mods, layer, dims):
    tm, d, n = TOKEN_TILE, dims["d"], dims["n"]
    row = lambda i: (i, 0)
    col = lambda i: (0, i)
    const = lambda i: (0, 0)
    mixer = pl.BlockSpec((d, tm), col) if feature_major else pl.BlockSpec((tm, d), row)
    return pl.pallas_call(
        kernel_fn,
        grid=(dims["tiles"],),
        in_specs=[mixer, mixer, mixer,
                  pl.BlockSpec((d, d), const), pl.BlockSpec((tm, d), row), _mod_spec(layer, dims)],
        out_specs=pl.BlockSpec((tm, d), row),
        out_shape=jax.ShapeDtypeStruct((n, d), F32),
        compiler_params=_params("arbitrary"),
        name=name,
    )(hf, hb, aux, w_out, xs, mods)


def _rproj_kernel(x_ref, g_ref, mod_ref, w_ref, cw_ref, cb_ref, gy_ref, xc_ref, ext_ref, cur_ref,
                  *, tiles, tiles_per_batch, ctx_tiles):
    tm, d = x_ref.shape
    i = pl.program_id(0)

    def in_segment_neighbours(t):
        j = t % tiles_per_batch
        first = jnp.logical_or(j == 0, j == ctx_tiles)
        last = jnp.logical_or(j == ctx_tiles - 1, j == tiles_per_batch - 1)
        return jnp.logical_not(first), jnp.logical_not(last)

    @pl.when(i == 0)
    def _():
        ext_ref[...] = jnp.zeros_like(ext_ref)
        cur_ref[...] = jnp.zeros_like(cur_ref)

    @pl.when(i < tiles)
    def _():
        h = _modulated(x_ref[...], g_ref[...], mod_ref[...], 0).astype(BF16)
        gy_ref[...] = jnp.dot(h, w_ref[:, 0:d], preferred_element_type=F32).astype(BF16)
        cur_ref[...] = jnp.dot(h, w_ref[:, d:2 * d], preferred_element_type=F32)

    @pl.when(i > 0)
    def _():
        _, next_ok = in_segment_neighbours(i - 1)
        head_ok = jnp.logical_and(next_ok, i < tiles)
        ext_ref[SUBLANES + tm:2 * SUBLANES + tm, :] = jnp.where(head_ok, cur_ref[0:SUBLANES, :], 0.0)
        xc = cb_ref[...] + ext_ref[pl.ds(SUBLANES - CONV_LEFT, tm), :] * cw_ref[0:1, :]
        for k in range(1, CONV_TAPS):
            xc = xc + ext_ref[pl.ds(SUBLANES - CONV_LEFT + k, tm), :] * cw_ref[k:k + 1, :]
        xc_ref[...] = xc

    prev_ok, _ = in_segment_neighbours(i)
    ext_ref[0:SUBLANES, :] = jnp.where(prev_ok, ext_ref[tm:tm + SUBLANES, :], 0.0)
    ext_ref[SUBLANES:SUBLANES + tm, :] = cur_ref[...]


def _rproj_call(xs, g, mods, layer, w_in, conv_w, conv_b, dims):
    tm, d, n, tiles = TOKEN_TILE, dims["d"], dims["n"], dims["tiles"]
    tpb, ct, b = dims["tpb"], dims["ct"], dims["b"]
    cur = lambda i: (jnp.minimum(i, tiles - 1), 0)
    lag = lambda i: (jnp.maximum(i - 1, 0), 0)
    const = lambda i: (0, 0)
    mod_spec = pl.BlockSpec((None, None, 1, 6 * d),
                            lambda i: (layer, _mod_row(jnp.minimum(i, tiles - 1), tpb, ct, b), 0, 0))
    return pl.pallas_call(
        functools.partial(_rproj_kernel, tiles=tiles, tiles_per_batch=tpb, ctx_tiles=ct),
        grid=(tiles + 1,),
        in_specs=[pl.BlockSpec((tm, d), cur), pl.BlockSpec((1, d), const), mod_spec,
                  pl.BlockSpec(w_in.shape, const), pl.BlockSpec(conv_w.shape, const),
                  pl.BlockSpec(conv_b.shape, const)],
        out_specs=[pl.BlockSpec((tm, d), cur), pl.BlockSpec((tm, d), lag)],
        out_shape=[jax.ShapeDtypeStruct((n, d), BF16), jax.ShapeDtypeStruct((n, d), F32)],
        scratch_shapes=[pltpu.VMEM((tm + 2 * SUBLANES, d), F32), pltpu.VMEM((tm, d), F32)],
        compiler_params=_params("arbitrary"),
        name="rglru_proj",
    )(xs, g, mods, w_in, conv_w, conv_b)


def _sigmoid_tanh(x):
    return 0.5 * jnp.tanh(0.5 * x) + 0.5


def _lru_direction(xc_ref, wg_ref, bg_ref, lam_ref, a_ref, b_ref, *, direction):
    tm, d = xc_ref.shape
    bw = d // LRU_BLOCKS
    z = -lam_ref[direction]
    softplus = jnp.maximum(z, 0.0) + jnp.log(1.0 + jnp.exp(-jnp.abs(z)))
    for nb in range(LRU_BLOCKS):
        sl = slice(nb * bw, (nb + 1) * bw)
        xb = xc_ref[:, sl]
        gg = jnp.dot(xb.astype(BF16), wg_ref[direction, nb], preferred_element_type=F32)
        gg = gg + bg_ref[direction, :, nb * 2 * bw:(nb + 1) * 2 * bw]
        r = _sigmoid_tanh(gg[:, :bw])
        gi = _sigmoid_tanh(gg[:, bw:])
        log_a = -LRU_C * r * softplus[:, sl]
        th = jnp.tanh(log_a)
        a_ref[pl.ds(nb, tm, stride=LRU_BLOCKS), :] = jnp.exp(log_a)
        b_ref[pl.ds(nb, tm, stride=LRU_BLOCKS), :] = jnp.sqrt(-2.0 * th / (1.0 - th)) * (gi * xb)


def _lru_kernel(xc_f, xc_b, wg_ref, bg_ref, lam_ref, of_ref, ob_ref, af_ref, bf_ref, ab_ref, bb_ref,
                hf_ref, hb_ref, h_ref, *, tiles_per_batch):
    tm, d = xc_f.shape
    bw = d // LRU_BLOCKS

    @pl.when(pl.program_id(0) % tiles_per_batch == 0)
    def _():
        h_ref[...] = jnp.zeros_like(h_ref)

    _lru_direction(xc_f, wg_ref, bg_ref, lam_ref, af_ref, bf_ref, direction=0)
    _lru_direction(xc_b, wg_ref, bg_ref, lam_ref, ab_ref, bb_ref, direction=1)

    def step(t, carry):
        hf, hb = carry
        rf = pl.multiple_of(t * LRU_BLOCKS, LRU_BLOCKS)
        rb = pl.multiple_of((tm - 1 - t) * LRU_BLOCKS, LRU_BLOCKS)
        hf = af_ref[pl.ds(rf, LRU_BLOCKS), :] * hf + bf_ref[pl.ds(rf, LRU_BLOCKS), :]
        hb = ab_ref[pl.ds(rb, LRU_BLOCKS), :] * hb + bb_ref[pl.ds(rb, LRU_BLOCKS), :]
        hf_ref[pl.ds(rf, LRU_BLOCKS), :] = hf
        hb_ref[pl.ds(rb, LRU_BLOCKS), :] = hb
        return hf, hb

    hf, hb = lax.fori_loop(0, tm, step, (h_ref[0], h_ref[1]), unroll=8)
    h_ref[0] = hf
    h_ref[1] = hb
    for nb in range(LRU_BLOCKS):
        sl = slice(nb * bw, (nb + 1) * bw)
        of_ref[:, sl] = hf_ref[pl.ds(nb, tm, stride=LRU_BLOCKS), :].astype(BF16)
        ob_ref[:, sl] = hb_ref[pl.ds(nb, tm, stride=LRU_BLOCKS), :].astype(BF16)


def _lru_call(xc, w_gate, b_gate, lam, dims):
    tm, d, n = TOKEN_TILE, dims["d"], dims["n"]
    tpb, ct = dims["tpb"], dims["ct"]
    fwd = lambda i: (i, 0)
    bwd = lambda i: (_bwd_tile(i, tpb, ct), 0)
    c3 = lambda i: (0, 0, 0)
    c4 = lambda i: (0, 0, 0, 0)
    time_major = pltpu.VMEM((tm * LRU_BLOCKS, d // LRU_BLOCKS), F32)
    return pl.pallas_call(
        functools.partial(_lru_kernel, tiles_per_batch=tpb),
        grid=(dims["tiles"],),
        in_specs=[
            pl.BlockSpec((tm, d), fwd), pl.BlockSpec((tm, d), bwd),
            pl.BlockSpec(w_gate.shape, c4), pl.BlockSpec(b_gate.shape, c3), pl.BlockSpec(lam.shape, c3),
        ],
        out_specs=[pl.BlockSpec((tm, d), fwd), pl.BlockSpec((tm, d), bwd)],
        out_shape=[jax.ShapeDtypeStruct((n, d), BF16), jax.ShapeDtypeStruct((n, d), BF16)],
        scratch_shapes=[time_major] * 6 + [pltpu.VMEM((2, LRU_BLOCKS, d // LRU_BLOCKS), F32)],
        compiler_params=_params("arbitrary"),
        name="rglru_scan",
    )(xc, xc, w_gate, b_gate, lam)


def _first_argmax(vals):
    best = vals[0]
    idx = jnp.zeros_like(best)
    for k in range(1, len(vals)):
        better = vals[k] > best
        idx = jnp.where(better, float(k), idx)
        best = jnp.where(better, vals[k], best)
    return idx, best


def _pick(idx, vals):
    out = vals[0]
    for k in range(1, len(vals)):
        out = jnp.where(idx == float(k), vals[k], out)
    return out


def _route_tile(x, g_ref, mod_ref, wr_ref, br_ref, hx_ref, route_ref, cnt_ref, carry_ref):
    tm = x.shape[0]

    @pl.when(pl.program_id(0) == 0)
    def _():
        carry_ref[...] = jnp.zeros_like(carry_ref)

    h = _modulated(x, g_ref[...], mod_ref[...], 3)
    _store_token_tiles(hx_ref, h)
    logits = lax.dot_general(wr_ref[...], h, (((1,), (1,)), ((), ())),
                             preferred_element_type=F32, precision=HIGHEST)
    e = jnp.exp(logits - jnp.max(logits, axis=0, keepdims=True))
    scores = e / jnp.sum(e, axis=0, keepdims=True)
    sel = scores + br_ref[...]
    sel_rows = [sel[k:k + 1, :] for k in range(N_EXPERTS)]
    group_scores = []
    for gi in range(N_GROUPS):
        v = sel_rows[gi * GROUP_SIZE:(gi + 1) * GROUP_SIZE]
        best = v[0] + v[1]
        for a, b in PAIRS[1:]:
            best = jnp.maximum(best, v[a] + v[b])
        group_scores.append(best)
    grp, _ = _first_argmax(group_scores)
    in_sel = [_pick(grp, [sel_rows[gi * GROUP_SIZE + k] for gi in range(N_GROUPS)])
              for k in range(GROUP_SIZE)]
    i1, _ = _first_argmax(in_sel)
    rest = [jnp.where(i1 == float(k), -jnp.inf, in_sel[k]) for k in range(GROUP_SIZE)]
    i2, _ = _first_argmax(rest)
    lo = jnp.minimum(i1, i2)
    hi = jnp.maximum(i1, i2)
    pair = jnp.zeros_like(lo)
    for k, (slot_a, slot_b) in enumerate(PAIR_SLOTS):
        here = jnp.logical_and(lo == float(min(slot_a, slot_b)), hi == float(max(slot_a, slot_b)))
        pair = jnp.where(here, float(k), pair)
    cls = grp * float(len(PAIR_SLOTS)) + pair
    crow = lax.broadcasted_iota(jnp.int32, (CLASS_ROWS, tm), 0).astype(F32)
    onehot = jnp.where(crow == cls, 1.0, 0.0)
    rows = lax.broadcasted_iota(jnp.int32, (tm, tm), 0)
    cols = lax.broadcasted_iota(jnp.int32, (tm, tm), 1)
    upper = jnp.where(rows <= cols, 1.0, 0.0).astype(BF16)
    cum = jnp.dot(onehot.astype(BF16), upper, preferred_element_type=F32)
    carry = carry_ref[:, 0:1]
    rank = jnp.sum(onehot * (cum - 1.0 + carry), axis=0, keepdims=True)
    new_carry = carry + jnp.sum(onehot, axis=1, keepdims=True)
    carry_ref[...] = jnp.broadcast_to(new_carry, carry_ref.shape)
    cnt_ref[...] = jnp.broadcast_to(new_carry, cnt_ref.shape)
    zero = jnp.zeros_like(cls)
    route_ref[...] = jnp.concatenate([cls, rank, zero, zero, zero, zero, zero, zero], axis=0)


def _router_kernel(x_ref, g_ref, mod_ref, wr_ref, br_ref, hx_ref, route_ref, cnt_ref, carry_ref):
    _route_tile(x_ref[...], g_ref, mod_ref, wr_ref, br_ref, hx_ref, route_ref, cnt_ref, carry_ref)


def _router_call(xs, g, mods, layer, w_router_t, b_router_col, dims):
    tm, d, n = TOKEN_TILE, dims["d"], dims["n"]
    row = lambda i: (i, 0)
    const = lambda i: (0, 0)
    per = d // LANES
    return pl.pallas_call(
        _router_kernel,
        grid=(dims["tiles"],),
        in_specs=[pl.BlockSpec((tm, d), row), pl.BlockSpec((1, d), const), _mod_spec(layer, dims),
                  pl.BlockSpec(w_router_t.shape, const), pl.BlockSpec(b_router_col.shape, const)],
        out_specs=[pl.BlockSpec((tm * per, LANES), row),
                   pl.BlockSpec((SUBLANES, tm), lambda i: (0, i)),
                   pl.BlockSpec((CLASS_ROWS, LANES), const)],
        out_shape=[jax.ShapeDtypeStruct((n * per, LANES), F32),
                   jax.ShapeDtypeStruct((SUBLANES, n), F32),
                   jax.ShapeDtypeStruct((CLASS_ROWS, LANES), F32)],
        scratch_shapes=[pltpu.VMEM((CLASS_ROWS, LANES), F32)],
        compiler_params=_params("arbitrary"),
        name="router",
    )(xs, g, mods, w_router_t, b_router_col)


ROW_DMA_UNROLL = 8


def _dispatch_kernel(pos_ref, hx_ref, init_ref, o_ref, stage_ref, sems, *, tokens):
    del init_ref
    per = hx_ref.shape[0] // tokens
    i = pl.program_id(0)
    slot = i % 2
    base = i * tokens
    stage_ref[slot] = hx_ref[...]

    def issue(r, carry):
        src = pl.multiple_of(r * per, per)
        dst = pl.multiple_of(pos_ref[base + r], per)
        pltpu.make_async_copy(stage_ref.at[slot, pl.ds(src, per)], o_ref.at[pl.ds(dst, per)],
                              sems.at[slot]).start()
        return carry

    lax.fori_loop(0, tokens, issue, 0, unroll=ROW_DMA_UNROLL)

    def wait_tile(s):
        pltpu.make_async_copy(stage_ref.at[s], o_ref.at[pl.ds(0, tokens * per)], sems.at[s]).wait()

    @pl.when(i > 0)
    def _():
        wait_tile(1 - slot)

    @pl.when(i == pl.num_programs(0) - 1)
    def _():
        wait_tile(slot)


def _dispatch_call(pos_rows, hx, init, dims):
    tm = TOKEN_TILE
    per = dims["d"] // LANES
    return pl.pallas_call(
        functools.partial(_dispatch_kernel, tokens=tm),
        grid_spec=pltpu.PrefetchScalarGridSpec(
            num_scalar_prefetch=1,
            grid=(dims["tiles"],),
            in_specs=[pl.BlockSpec((tm * per, LANES), lambda i, p: (i, 0)),
                      pl.BlockSpec(memory_space=pl.ANY)],
            out_specs=pl.BlockSpec(memory_space=pl.ANY),
            scratch_shapes=[pltpu.VMEM((2, tm * per, LANES), hx.dtype), pltpu.SemaphoreType.DMA((2,))],
        ),
        out_shape=jax.ShapeDtypeStruct(init.shape, hx.dtype),
        input_output_aliases={2: 0},
        compiler_params=_params("arbitrary"),
        name="dispatch",
    )(pos_rows, hx, init)


def _expert_kernel(ea_ref, eb_ref, ok_ref, x_ref, wr_ref, w1a, w3a, w2a, w1b, w3b, w2b, o_ref):
    d = w1a.shape[0]
    te = x_ref.shape[0] // (d // LANES)
    t = pl.program_id(0)

    @pl.when(ok_ref[t] != 0)
    def _():
        h = _load_token_tiles(x_ref, te, d)
        x = h.astype(BF16)
        dw = wr_ref[pl.ds(ea_ref[t], 1), :] - wr_ref[pl.ds(eb_ref[t], 1), :]
        gate_a = jax.nn.sigmoid(jnp.sum(h * dw, axis=1, keepdims=True))
        gate_b = 1.0 - gate_a

        def expert(w1, w3, w2):
            a = jnp.dot(x, w1[...], preferred_element_type=F32)
            b = jnp.dot(x, w3[...], preferred_element_type=F32)
            u = (a * jax.nn.sigmoid(a)) * b
            return jnp.dot(u.astype(BF16), w2[...], preferred_element_type=F32)

        _store_token_tiles(o_ref, gate_a * expert(w1a, w3a, w2a) + gate_b * expert(w1b, w3b, w2b))

    @pl.when(ok_ref[t] == 0)
    def _():
        o_ref[...] = jnp.zeros_like(o_ref)


def _expert_call(tile_a, tile_b, tile_ok, xsorted, w_router_t, w1, w3, w2, layer, d):
    te = TOKEN_TILE
    per = d // LANES
    n_tiles = xsorted.shape[0] // (te * per)
    de = w1.shape[-1]
    sel_a = lambda t, ea, eb, ok: (layer, ea[t], 0, 0)
    sel_b = lambda t, ea, eb, ok: (layer, eb[t], 0, 0)
    used = lambda t, ea, eb, ok: (t, 0)
    return pl.pallas_call(
        _expert_kernel,
        grid_spec=pltpu.PrefetchScalarGridSpec(
            num_scalar_prefetch=3,
            grid=(n_tiles,),
            in_specs=[
                pl.BlockSpec((te * per, LANES), used),
                pl.BlockSpec(w_router_t.shape, lambda t, ea, eb, ok: (0, 0)),
                pl.BlockSpec((None, None, d, de), sel_a), pl.BlockSpec((None, None, d, de), sel_a),
                pl.BlockSpec((None, None, de, d), sel_a),
                pl.BlockSpec((None, None, d, de), sel_b), pl.BlockSpec((None, None, d, de), sel_b),
                pl.BlockSpec((None, None, de, d), sel_b),
            ],
            out_specs=pl.BlockSpec((te * per, LANES), used),
        ),
        out_shape=jax.ShapeDtypeStruct((n_tiles * te * per, LANES), F32),
        compiler_params=_params("arbitrary"),
        name="experts",
    )(tile_a, tile_b, tile_ok, xsorted, w_router_t, w1, w3, w2, w1, w3, w2)


def _combine_kernel(pos_ref, y_ref, x_ref, mod_ref, gf_ref, o_ref, buf_ref, sems,
                    *, tile_of, final_norm):
    tm, d = x_ref.shape
    per = d // LANES
    i = pl.program_id(0)
    slot = i % 2

    def gather(step, s):
        base = tile_of(step) * tm

        def issue(r, carry):
            src = pl.multiple_of(pos_ref[base + r], per)
            dst = pl.multiple_of(r * per, per)
            pltpu.make_async_copy(y_ref.at[pl.ds(src, per)], buf_ref.at[s, pl.ds(dst, per)],
                                  sems.at[s]).start()
            return carry

        lax.fori_loop(0, tm, issue, 0, unroll=ROW_DMA_UNROLL)

    @pl.when(i == 0)
    def _():
        gather(0, 0)

    @pl.when(i + 1 < pl.num_programs(0))
    def _():
        gather(i + 1, 1 - slot)

    pltpu.make_async_copy(y_ref.at[pl.ds(0, tm * per)], buf_ref.at[slot], sems.at[slot]).wait()
    out = x_ref[...] + mod_ref[:, 5 * d:6 * d] * _load_token_tiles(buf_ref.at[slot], tm, d)
    if final_norm:
        ms = jnp.mean(out * out, axis=-1, keepdims=True)
        out = out * lax.rsqrt(ms + NORM_EPS) * gf_ref[...]
    o_ref[...] = out


def _combine_call(pos, ysorted, xs, mods, layer, g_final, dims, final_norm):
    tm, d = TOKEN_TILE, dims["d"]
    tpb, ct, b = dims["tpb"], dims["ct"], dims["b"]
    if final_norm:
        lt = tpb - ct
        tile_of = lambda i: (i // lt) * tpb + ct + i % lt
        n_tiles = b * lt
    else:
        tile_of = lambda i: i
        n_tiles = dims["tiles"]
    mod_spec = pl.BlockSpec((None, None, 1, 6 * d),
                            lambda i, p: (layer, _mod_row(tile_of(i), tpb, ct, b), 0, 0))
    return pl.pallas_call(
        functools.partial(_combine_kernel, tile_of=tile_of, final_norm=final_norm),
        grid_spec=pltpu.PrefetchScalarGridSpec(
            num_scalar_prefetch=1,
            grid=(n_tiles,),
            in_specs=[
                pl.BlockSpec(memory_space=pl.ANY),
                pl.BlockSpec((tm, d), lambda i, p: (tile_of(i), 0)),
                mod_spec,
                pl.BlockSpec((1, d), lambda i, p: (0, 0)),
            ],
            out_specs=pl.BlockSpec((tm, d), lambda i, p: (i, 0)),
            scratch_shapes=[pltpu.VMEM((2, tm * (d // LANES), LANES), F32), pltpu.SemaphoreType.DMA((2,))],
        ),
        out_shape=jax.ShapeDtypeStruct((n_tiles * tm, d), F32),
        compiler_params=_params("arbitrary"),
        name="combine",
    )(pos, ysorted, xs, mods, g_final)


_CLASS_A = np.array([g * GROUP_SIZE + a for g in range(N_GROUPS) for a, _ in PAIR_SLOTS], np.int32)
_CLASS_B = np.array([g * GROUP_SIZE + b for g in range(N_GROUPS) for _, b in PAIR_SLOTS], np.int32)


def _routing_plan(route, counts, n_tiles_padded):
    te = TOKEN_TILE
    cls = route[0].astype(jnp.int32)
    rank = route[1].astype(jnp.int32)
    cnt = counts[:N_CLASSES, 0].astype(jnp.int32)
    tiles = (cnt + te - 1) // te
    tile_end = jnp.cumsum(tiles)
    tile_start = tile_end - tiles
    onehot = (cls[:, None] == jnp.arange(N_CLASSES, dtype=jnp.int32)[None, :]).astype(jnp.int32)
    pos = jnp.sum(onehot * (tile_start * te)[None, :], axis=1) + rank
    t = jnp.arange(n_tiles_padded, dtype=jnp.int32)
    ok = (t < tile_end[-1]).astype(jnp.int32)
    tcls = jnp.sum((tile_end[None, :] <= jnp.minimum(t, tile_end[-1] - 1)[:, None]).astype(jnp.int32), axis=1)
    tile_a = jnp.take(jnp.asarray(_CLASS_A), tcls)
    tile_b = jnp.take(jnp.asarray(_CLASS_B), tcls)
    return pos, tile_a, tile_b, ok


def _moe(xs, hx, route, counts, sorted_buf, mods, layer, w_router_t, w1, w3, w2, g_final, dims, final_norm):
    per = dims["d"] // LANES
    n_tiles_padded = sorted_buf.shape[0] // (TOKEN_TILE * per)
    pos, tile_a, tile_b, tile_ok = _routing_plan(route, counts, n_tiles_padded)
    xsorted = _dispatch_call(pos * per, hx, sorted_buf, dims)
    ysorted = _expert_call(tile_a, tile_b, tile_ok, xsorted, w_router_t, w1, w3, w2, layer, dims["d"])
    return _combine_call(pos * per, ysorted, xs, mods, layer, g_final, dims, final_norm), xsorted


def _pos_table(n_tokens, dim):
    rows = n_tokens // GRID_WIDTH
    r, col = jnp.meshgrid(jnp.arange(rows, dtype=F32), jnp.arange(GRID_WIDTH, dtype=F32), indexing="ij")
    quarter = dim // 4
    freqs = jnp.exp(-math.log(POS_BASE) * jnp.arange(quarter, dtype=F32) / quarter)

    def enc(p):
        ang = p.reshape(-1, 1) * freqs
        return jnp.concatenate([jnp.sin(ang), jnp.cos(ang)], axis=-1)

    return jnp.concatenate([enc(r), enc(col)], axis=-1)


def _mlstm_weights(w_in, b_gate):
    nq = HEADS * HEAD_QK
    nv = HEADS * HEAD_V
    wq = w_in[:, 0:nq] * (HEAD_QK ** -0.5)
    wk = w_in[:, nq:2 * nq]
    wv = w_in[:, 2 * nq:2 * nq + nv]
    wo = w_in[:, 2 * nq + nv:2 * nq + 2 * nv]
    wg = w_in[:, 2 * nq + 2 * nv:]
    wg_pad = jnp.pad(wg, ((0, 0), (0, LANES - wg.shape[1])))
    w_tok = jnp.concatenate([wq, wk, wg_pad], axis=1).astype(BF16)
    w_feat = jnp.concatenate([wv, wo, wg], axis=1).T.astype(BF16)
    bias = b_gate.reshape(-1).astype(F32)
    bias_c = jnp.pad(bias, (0, LANES - bias.shape[0])).reshape(1, LANES)
    bias_r = bias.reshape(-1, 1)
    return w_tok, w_feat, bias_c, bias_r


def kernel(x, c, ctx, c_ctx, w_ada, b_ada, g_mix, g_ffn, g_final, m_w_in, m_b_gate, m_g_head, m_w_out, r_w_in, r_conv_w, r_conv_b, r_w_gate, r_b_gate, r_lam, r_w_out, w_router, b_router, e_w1, e_w3, e_w2):
    batch, t_len, d = x.shape
    ctx_len = ctx.shape[1]
    depth = w_ada.shape[0]
    tm = TOKEN_TILE
    assert t_len % tm == 0 and ctx_len % tm == 0 and batch + 1 <= ADA_ROWS
    assert d == HEADS * HEAD_V and t_len % GRID_WIDTH == 0
    s_len = ctx_len + t_len
    dims = dict(b=batch, d=d, n=batch * s_len, tpb=s_len // tm, ct=ctx_len // tm,
                tiles=batch * s_len // tm)

    cvec = jnp.concatenate([c, c_ctx[None, :], jnp.zeros((ADA_ROWS - batch - 1, d), F32)], axis=0)
    mods = _ada_call(cvec, w_ada, b_ada).reshape(depth, ADA_ROWS, 1, 6 * d)
    xs = _stream_call(ctx.reshape(batch * ctx_len, d), x.reshape(batch * t_len, d),
                      _pos_table(t_len, d), dims)

    w_router_t = w_router.T.astype(F32)
    b_router_col = b_router.reshape(-1, 1).astype(F32)
    g_final2 = g_final.reshape(1, d)
    e_w1b, e_w3b, e_w2b = e_w1.astype(BF16), e_w3.astype(BF16), e_w2.astype(BF16)
    sorted_buf = jnp.zeros(((dims["tiles"] + N_CLASSES) * tm * (d // LANES), LANES), F32)
    out = None
    for i in range(depth):
        j = i // 2
        g_mix_i = g_mix[i].reshape(1, d)
        g_ffn_i = g_ffn[i].reshape(1, d)
        if i % 2 == 0:
            w_tok, w_feat, bias_c, bias_r = _mlstm_weights(m_w_in[j], m_b_gate[j])
            q, k, vt, ot, gc, gr = _mproj_call(xs, g_mix_i, mods, i, w_tok, w_feat, dims)
            hf, hb = _mlstm_call(q, k, vt, gc, gr, bias_c, bias_r, dims)
            w_out = (m_g_head[j][:, None] * m_w_out[j]).astype(BF16)
            xs = _readout_call(_mreadout_kernel, "mlstm_readout", True, hf, hb, ot, w_out, xs, mods, i, dims)
        else:
            gy, xc = _rproj_call(xs, g_mix_i, mods, i, r_w_in[j].astype(BF16), r_conv_w[j],
                                 r_conv_b[j].reshape(1, d), dims)
            hf, hb = _lru_call(xc, r_w_gate[j].astype(BF16), r_b_gate[j].reshape(2, 1, -1),
                               r_lam[j].reshape(2, 1, d), dims)
            xs = _readout_call(_readout_kernel, "rglru_readout", False, hf, hb, gy,
                               r_w_out[j].astype(BF16), xs, mods, i, dims)
        hx, route, counts = _router_call(xs, g_ffn_i, mods, i, w_router_t, b_router_col, dims)
        last = i == depth - 1
        res, sorted_buf = _moe(xs, hx, route, counts, sorted_buf, mods, i, w_router_t,
                               e_w1b, e_w3b, e_w2b, g_final2, dims, last)
        if last:
            out = res
        else:
            xs = res
    return out.reshape(batch, t_len, d)
```

```python
import functools
import math

import jax
import jax.numpy as jnp
import numpy as np
from jax import lax
from jax.experimental import pallas as pl
from jax.experimental.pallas import tpu as pltpu

F32 = jnp.float32
BF16 = jnp.bfloat16
HIGHEST = lax.Precision.HIGHEST

TOKEN_TILE = 256
LANES = 128
SUBLANES = 8
VMEM_LIMIT_BYTES = 56 * 1024 * 1024
NORM_EPS = 1e-6
GRID_WIDTH = 64
POS_BASE = 10000.0
HEADS = 8
HEAD_QK = 64
HEAD_V = 128
LRU_BLOCKS = 8
LRU_C = 8.0
CONV_TAPS = 4
CONV_LEFT = CONV_TAPS // 2
N_EXPERTS = 16
N_GROUPS = 4
GROUP_SIZE = N_EXPERTS // N_GROUPS
PAIRS = [(a, b) for a in range(GROUP_SIZE) for b in range(a + 1, GROUP_SIZE)]
PAIR_SLOTS = [(0, 1), (0, 2), (0, 3), (1, 3), (1, 2), (3, 2)]
N_CLASSES = N_GROUPS * len(PAIRS)
CLASS_ROWS = 32
ROUTE_COLS = LANES
ADA_ROWS = 16


def _params(*sem):
    return pltpu.CompilerParams(dimension_semantics=sem, vmem_limit_bytes=VMEM_LIMIT_BYTES)


def _mod_row(i, tiles_per_batch, ctx_tiles, batch):
    return jnp.where(i % tiles_per_batch < ctx_tiles, batch, i // tiles_per_batch)


def _bwd_tile(i, tiles_per_batch, ctx_tiles):
    b = i // tiles_per_batch
    j = i % tiles_per_batch
    jb = jnp.where(j < ctx_tiles, ctx_tiles - 1 - j, tiles_per_batch - 1 - (j - ctx_tiles))
    return b * tiles_per_batch + jb


def _ada_kernel(c_ref, w_ref, b_ref, o_ref):
    c = c_ref[...]
    s = c * jax.nn.sigmoid(c)
    o_ref[...] = jnp.dot(s, w_ref[...], preferred_element_type=F32, precision=HIGHEST) + b_ref[...]


def _ada_call(cvec, w_ada, b_ada):
    depth, d, n = w_ada.shape
    tn = n // 4
    return pl.pallas_call(
        _ada_kernel,
        grid=(depth, n // tn),
        in_specs=[
            pl.BlockSpec((ADA_ROWS, d), lambda l, j: (0, 0)),
            pl.BlockSpec((None, d, tn), lambda l, j: (l, 0, j)),
            pl.BlockSpec((None, 1, tn), lambda l, j: (l, 0, j)),
        ],
        out_specs=pl.BlockSpec((None, ADA_ROWS, tn), lambda l, j: (l, 0, j)),
        out_shape=jax.ShapeDtypeStruct((depth, ADA_ROWS, n), F32),
        compiler_params=_params("arbitrary", "arbitrary"),
        name="ada",
    )(cvec, w_ada, b_ada.reshape(depth, 1, n))


def _modulated(x, g, mod, shift_idx):
    d = x.shape[-1]
    ms = jnp.mean(x * x, axis=-1, keepdims=True)
    xn = x * lax.rsqrt(ms + NORM_EPS) * g
    shift = mod[:, shift_idx * d:(shift_idx + 1) * d]
    scale = mod[:, (shift_idx + 1) * d:(shift_idx + 2) * d]
    return xn * (1.0 + scale) + shift


def _mod_spec(layer, dims):
    tpb, ct, b = dims["tpb"], dims["ct"], dims["b"]
    return pl.BlockSpec((None, None, 1, 6 * dims["d"]),
                        lambda i, *_: (layer, _mod_row(i, tpb, ct, b), 0, 0))


def _store_token_tiles(ref, x):
    tokens, d = x.shape
    per = d // LANES
    for s in range(per):
        ref[pl.ds(s, tokens, stride=per), :] = x[:, s * LANES:(s + 1) * LANES]


def _load_token_tiles(ref, tokens, d):
    per = d // LANES
    return jnp.concatenate([ref[pl.ds(s, tokens, stride=per), :] for s in range(per)], axis=1)


def _gelu_tanh(y):
    return 0.5 * y * (1.0 + jnp.tanh(math.sqrt(2.0 / math.pi) * (y + 0.044715 * (y * y * y))))


SRC_BLOCK = 128
_NT = (((1,), (1,)), ((), ()))
_TN = (((0,), (0,)), ((), ()))


def _mproj_kernel(*refs, assemble, tiles_per_batch, ctx_tiles):
    if assemble:
        ctx_ref, lat_ref, pos_ref, g_ref, mod_ref, w_ref, wt_ref, xs_ref = refs[:8]
        is_ctx = pl.program_id(0) % tiles_per_batch < ctx_tiles
        x = jnp.where(is_ctx, ctx_ref[...], lat_ref[...] + pos_ref[...])
        xs_ref[...] = x
    else:
        x_ref, g_ref, mod_ref, w_ref, wt_ref = refs[:5]
        x = x_ref[...]
    q_ref, k_ref, vt_ref, ot_ref, gc_ref, gr_ref = refs[-6:]
    d = x.shape[-1]
    nq = q_ref.shape[-1]
    h = _modulated(x, g_ref[...], mod_ref[...], 0).astype(BF16)
    q_ref[...] = jnp.dot(h, w_ref[:, 0:nq], preferred_element_type=F32).astype(BF16)
    k_ref[...] = jnp.dot(h, w_ref[:, nq:2 * nq], preferred_element_type=F32).astype(BF16)
    gc_ref[...] = jnp.dot(h, w_ref[:, 2 * nq:2 * nq + LANES], preferred_element_type=F32)
    vt_ref[...] = lax.dot_general(wt_ref[0:d, :], h, _NT, preferred_element_type=F32).astype(BF16)
    ot = lax.dot_general(wt_ref[d:2 * d, :], h, _NT, preferred_element_type=F32)
    ot_ref[...] = jax.nn.sigmoid(ot).astype(BF16)
    gr_ref[...] = lax.dot_general(wt_ref[2 * d:2 * d + 4 * HEADS, :], h, _NT, preferred_element_type=F32)


def _mproj_call(stream, g, mods, layer, w_tok, w_feat, dims):
    tm, d, n = TOKEN_TILE, dims["d"], dims["n"]
    tpb, ct = dims["tpb"], dims["ct"]
    lt = tpb - ct
    nq = HEADS * HEAD_QK
    row = lambda i: (i, 0)
    col = lambda i: (0, i)
    const = lambda i: (0, 0)
    assemble = isinstance(stream, tuple)
    if assemble:
        stream_specs = [
            pl.BlockSpec((tm, d), lambda i: ((i // tpb) * ct + jnp.minimum(i % tpb, ct - 1), 0)),
            pl.BlockSpec((tm, d), lambda i: ((i // tpb) * lt + jnp.maximum(i % tpb - ct, 0), 0)),
            pl.BlockSpec((tm, d), lambda i: (jnp.maximum(i % tpb - ct, 0), 0)),
        ]
        stream_out = [pl.BlockSpec((tm, d), row)], [jax.ShapeDtypeStruct((n, d), F32)]
    else:
        stream, stream_specs, stream_out = (stream,), [pl.BlockSpec((tm, d), row)], ([], [])
    return pl.pallas_call(
        functools.partial(_mproj_kernel, assemble=assemble, tiles_per_batch=tpb, ctx_tiles=ct),
        grid=(dims["tiles"],),
        in_specs=stream_specs + [
            pl.BlockSpec((1, d), const),
            _mod_spec(layer, dims),
            pl.BlockSpec(w_tok.shape, const),
            pl.BlockSpec(w_feat.shape, const),
        ],
        out_specs=stream_out[0] + [
            pl.BlockSpec((tm, nq), row),
            pl.BlockSpec((tm, nq), row),
            pl.BlockSpec((d, tm), col),
            pl.BlockSpec((d, tm), col),
            pl.BlockSpec((tm, LANES), row),
            pl.BlockSpec((4 * HEADS, tm), col),
        ],
        out_shape=stream_out[1] + [
            jax.ShapeDtypeStruct((n, nq), BF16),
            jax.ShapeDtypeStruct((n, nq), BF16),
            jax.ShapeDtypeStruct((d, n), BF16),
            jax.ShapeDtypeStruct((d, n), BF16),
            jax.ShapeDtypeStruct((n, LANES), F32),
            jax.ShapeDtypeStruct((4 * HEADS, n), F32),
        ],
        compiler_params=_params("arbitrary"),
        name="mlstm_proj",
    )(*stream, g, mods, w_tok, w_feat)


def _mlstm_gate_stats(gc_ref, gr_ref, bc_ref, br_ref, m_ref, *, backward):
    tm = gc_ref.shape[0]
    gate_i = 2 * HEADS if backward else 0
    gate_f = gate_i + HEADS
    last = 0 if backward else tm - 1
    src = lax.broadcasted_iota(jnp.int32, (tm, tm), 0)
    tgt = lax.broadcasted_iota(jnp.int32, (tm, tm), 1)
    visible = (src >= tgt) if backward else (src <= tgt)
    neg_mask = jnp.where(visible, 0.0, -jnp.inf)
    xc = gc_ref[...] + bc_ref[...]
    xr = gr_ref[...] + br_ref[...]
    before = (src <= tgt) if backward else (src >= tgt)
    cum_c = _split_dot(jnp.where(before, 1.0, 0.0).astype(BF16), jax.nn.log_sigmoid(xc), left=True)
    cum_r = _split_dot(jnp.where(visible, 1.0, 0.0).astype(BF16), jax.nn.log_sigmoid(xr), left=False)
    b_row = cum_r[gate_f:gate_f + HEADS, :]
    c_row = xr[gate_i:gate_i + HEADS, :] - b_row
    c_col = xc - pltpu.roll(cum_c, LANES - HEADS, axis=1)
    run_col = c_col
    shift = 1
    while shift < tm:
        fill = jnp.full((shift, LANES), -jnp.inf, F32)
        if backward:
            moved = jnp.concatenate([run_col[shift:, :], fill], axis=0)
        else:
            moved = jnp.concatenate([fill, run_col[:tm - shift, :]], axis=0)
        run_col = jnp.maximum(run_col, moved)
        shift *= 2
    run = run_col.T[gate_i:gate_i + HEADS, :]
    m = m_ref[:, 0:1]
    log_inter = b_row + m
    m_pos = jnp.maximum(log_inter, b_row + run)
    shift_row = b_row - m_pos
    w_inter = jnp.exp(log_inter - m_pos)
    floor = jnp.exp(-m_pos)
    b_last = b_row[:, last:last + 1]
    m_new = m_pos[:, last:last + 1]
    w_src = jnp.exp(b_last + c_row - m_new)
    decay = jnp.exp(b_last + m - m_new)
    m_ref[...] = jnp.broadcast_to(m_new, m_ref.shape)
    return neg_mask, c_col, shift_row, w_inter, floor, w_src, decay


def _mlstm_head(h, q_ref, k_ref, vt_ref, o_ref, ct_ref, stats, *, backward):
    neg_mask, c_col, shift_row, w_inter, floor, w_src, decay = stats
    tm = q_ref.shape[0]
    gate_i = 2 * HEADS if backward else 0
    half = lax.broadcasted_iota(jnp.int32, (tm, LANES), 1) < HEAD_QK
    ones = jnp.ones((2 * SUBLANES, tm), BF16)
    if True:
        pair = slice((h // 2) * LANES, (h // 2 + 1) * LANES)
        qp = q_ref[:, pair]
        km = jnp.where(half if h % 2 == 0 else jnp.logical_not(half), k_ref[:, pair], 0.0).astype(BF16)
        vaug = jnp.concatenate([vt_ref[h * HEAD_V:(h + 1) * HEAD_V, :], ones], axis=0)
        ct = ct_ref[h]
        acc = w_inter[h:h + 1, :] * lax.dot_general(ct.astype(BF16), qp, _NT,
                                                    preferred_element_type=F32)
        for blk in range(tm // SRC_BLOCK):
            rows = slice(blk * SRC_BLOCK, (blk + 1) * SRC_BLOCK)
            st = lax.dot_general(km[rows, :], qp, _NT, preferred_element_type=F32)
            e = (neg_mask[rows, :] + shift_row[h:h + 1, :]) + c_col[rows, gate_i + h:gate_i + h + 1]
            p = (st * jnp.exp(e)).astype(BF16)
            acc = acc + jnp.dot(vaug[:, rows], p, preferred_element_type=F32)
        den = acc[HEAD_V:HEAD_V + 1, :]
        scale = 1.0 / jnp.maximum(jnp.abs(den), floor[h:h + 1, :])
        o_ref[h * HEAD_V:(h + 1) * HEAD_V, :] = (acc[0:HEAD_V, :] * scale).astype(o_ref.dtype)
        vs = (vaug.astype(F32) * w_src[h:h + 1, :]).astype(BF16)
        ct_ref[h] = decay[h:h + 1, :] * ct + jnp.dot(vs, km, preferred_element_type=F32)


def _split_dot(ones_mat, x, left):
    hi = x.astype(BF16)
    r1 = x - hi.astype(F32)
    mid = r1.astype(BF16)
    lo = (r1 - mid.astype(F32)).astype(BF16)
    out = None
    for piece in (lo, mid, hi):
        t = (jnp.dot(ones_mat, piece, preferred_element_type=F32) if left
             else jnp.dot(piece, ones_mat, preferred_element_type=F32))
        out = t if out is None else out + t
    return out


def _mlstm_kernel(qf, kf, vtf, gcf, grf, qb, kb, vtb, gcb, grb, bc_ref, br_ref, of_ref, ob_ref,
                  cf_ref, mf_ref, cb_ref, mb_ref, *, tiles_per_batch):
    @pl.when(pl.program_id(0) % tiles_per_batch == 0)
    def _():
        cf_ref[...] = jnp.zeros_like(cf_ref)
        cb_ref[...] = jnp.zeros_like(cb_ref)
        mf_ref[...] = jnp.zeros_like(mf_ref)
        mb_ref[...] = jnp.zeros_like(mb_ref)

    stats_f = _mlstm_gate_stats(gcf, grf, bc_ref, br_ref, mf_ref, backward=False)
    stats_b = _mlstm_gate_stats(gcb, grb, bc_ref, br_ref, mb_ref, backward=True)
    for h in range(HEADS):
        _mlstm_head(h, qf, kf, vtf, of_ref, cf_ref, stats_f, backward=False)
    for h in range(HEADS):
        _mlstm_head(h, qb, kb, vtb, ob_ref, cb_ref, stats_b, backward=True)


def _mlstm_call(q, k, vt, gc, gr, bias_c, bias_r, dims):
    tm, d, n = TOKEN_TILE, dims["d"], dims["n"]
    tpb, ct = dims["tpb"], dims["ct"]
    nq = q.shape[1]
    fwd = lambda i: (i, 0)
    bwd = lambda i: (_bwd_tile(i, tpb, ct), 0)
    fwd_t = lambda i: (0, i)
    bwd_t = lambda i: (0, _bwd_tile(i, tpb, ct))
    const = lambda i: (0, 0)
    state = [pltpu.VMEM((HEADS, HEAD_V + 2 * SUBLANES, LANES), F32), pltpu.VMEM((HEADS, LANES), F32)]
    return pl.pallas_call(
        functools.partial(_mlstm_kernel, tiles_per_batch=tpb),
        grid=(dims["tiles"],),
        in_specs=[
            pl.BlockSpec((tm, nq), fwd), pl.BlockSpec((tm, nq), fwd), pl.BlockSpec((d, tm), fwd_t),
            pl.BlockSpec((tm, LANES), fwd), pl.BlockSpec((4 * HEADS, tm), fwd_t),
            pl.BlockSpec((tm, nq), bwd), pl.BlockSpec((tm, nq), bwd), pl.BlockSpec((d, tm), bwd_t),
            pl.BlockSpec((tm, LANES), bwd), pl.BlockSpec((4 * HEADS, tm), bwd_t),
            pl.BlockSpec((1, LANES), const), pl.BlockSpec((4 * HEADS, 1), const),
        ],
        out_specs=[pl.BlockSpec((d, tm), fwd_t), pl.BlockSpec((d, tm), bwd_t)],
        out_shape=[jax.ShapeDtypeStruct((d, n), BF16), jax.ShapeDtypeStruct((d, n), BF16)],
        scratch_shapes=state + state,
        compiler_params=_params("arbitrary"),
        name="mlstm_scan",
    )(q, k, vt, gc, gr, q, k, vt, gc, gr, bias_c, bias_r)


def _mreadout_kernel(hf_ref, hb_ref, ot_ref, w_ref, x_ref, mod_ref, o_ref):
    d = x_ref.shape[-1]
    parts = []
    for h in range(HEADS):
        rows = slice(h * HEAD_V, (h + 1) * HEAD_V)
        hs = hf_ref[rows, :].astype(F32) + hb_ref[rows, :].astype(F32)
        r = lax.rsqrt(jnp.mean(hs * hs, axis=0, keepdims=True) + NORM_EPS)
        parts.append((hs * r * ot_ref[rows, :]).astype(BF16))
    t = jnp.concatenate(parts, axis=0)
    out = lax.dot_general(t, w_ref[...], _TN, preferred_element_type=F32)
    o_ref[...] = x_ref[...] + mod_ref[:, 2 * d:3 * d] * out


def _readout_kernel(hf_ref, hb_ref, gy_ref, w_ref, x_ref, mod_ref, o_ref):
    d = x_ref.shape[-1]
    t = _gelu_tanh(gy_ref[...].astype(F32)) * (hf_ref[...].astype(F32) + hb_ref[...].astype(F32))
    out = jnp.dot(t.astype(BF16), w_ref[...], preferred_element_type=F32)
    o_ref[...] = x_ref[...] + mod_ref[:, 2 * d:3 * d] * out


def _readout_call(kernel_fn, name, feature_major, hf, hb, aux, w_out, xs, mods, layer, dims):
    tm, d, n = TOKEN_TILE, dims["d"], dims["n"]
    row = lambda i: (i, 0)
    col = lambda i: (0, i)
    const = lambda i: (0, 0)
    mixer = pl.BlockSpec((d, tm), col) if feature_major else pl.BlockSpec((tm, d), row)
    return pl.pallas_call(
        kernel_fn,
        grid=(dims["tiles"],),
        in_specs=[mixer, mixer, mixer,
                  pl.BlockSpec((d, d), const), pl.BlockSpec((tm, d), row), _mod_spec(layer, dims)],
        out_specs=pl.BlockSpec((tm, d), row),
        out_shape=jax.ShapeDtypeStruct((n, d), F32),
        compiler_params=_params("arbitrary"),
        name=name,
    )(hf, hb, aux, w_out, xs, mods)


def _rproj_kernel(x_ref, g_ref, mod_ref, w_ref, cw_ref, cb_ref, gy_ref, xc_ref, ext_ref, cur_ref,
                  *, tiles, tiles_per_batch, ctx_tiles):
    tm, d = x_ref.shape
    i = pl.program_id(0)

    def in_segment_neighbours(t):
        j = t % tiles_per_batch
        first = jnp.logical_or(j == 0, j == ctx_tiles)
        last = jnp.logical_or(j == ctx_tiles - 1, j == tiles_per_batch - 1)
        return jnp.logical_not(first), jnp.logical_not(last)

    @pl.when(i == 0)
    def _():
        ext_ref[...] = jnp.zeros_like(ext_ref)
        cur_ref[...] = jnp.zeros_like(cur_ref)

    @pl.when(i < tiles)
    def _():
        h = _modulated(x_ref[...], g_ref[...], mod_ref[...], 0).astype(BF16)
        gy_ref[...] = jnp.dot(h, w_ref[:, 0:d], preferred_element_type=F32).astype(BF16)
        cur_ref[...] = jnp.dot(h, w_ref[:, d:2 * d], preferred_element_type=F32)

    @pl.when(i > 0)
    def _():
        _, next_ok = in_segment_neighbours(i - 1)
        head_ok = jnp.logical_and(next_ok, i < tiles)
        ext_ref[SUBLANES + tm:2 * SUBLANES + tm, :] = jnp.where(head_ok, cur_ref[0:SUBLANES, :], 0.0)
        xc = cb_ref[...] + ext_ref[pl.ds(SUBLANES - CONV_LEFT, tm), :] * cw_ref[0:1, :]
        for k in range(1, CONV_TAPS):
            xc = xc + ext_ref[pl.ds(SUBLANES - CONV_LEFT + k, tm), :] * cw_ref[k:k + 1, :]
        xc_ref[...] = xc

    prev_ok, _ = in_segment_neighbours(i)
    ext_ref[0:SUBLANES, :] = jnp.where(prev_ok, ext_ref[tm:tm + SUBLANES, :], 0.0)
    ext_ref[SUBLANES:SUBLANES + tm, :] = cur_ref[...]


def _rproj_call(xs, g, mods, layer, w_in, conv_w, conv_b, dims):
    tm, d, n, tiles = TOKEN_TILE, dims["d"], dims["n"], dims["tiles"]
    tpb, ct, b = dims["tpb"], dims["ct"], dims["b"]
    cur = lambda i: (jnp.minimum(i, tiles - 1), 0)
    lag = lambda i: (jnp.maximum(i - 1, 0), 0)
    const = lambda i: (0, 0)
    mod_spec = pl.BlockSpec((None, None, 1, 6 * d),
                            lambda i: (layer, _mod_row(jnp.minimum(i, tiles - 1), tpb, ct, b), 0, 0))
    return pl.pallas_call(
        functools.partial(_rproj_kernel, tiles=tiles, tiles_per_batch=tpb, ctx_tiles=ct),
        grid=(tiles + 1,),
        in_specs=[pl.BlockSpec((tm, d), cur), pl.BlockSpec((1, d), const), mod_spec,
                  pl.BlockSpec(w_in.shape, const), pl.BlockSpec(conv_w.shape, const),
                  pl.BlockSpec(conv_b.shape, const)],
        out_specs=[pl.BlockSpec((tm, d), cur), pl.BlockSpec((tm, d), lag)],
        out_shape=[jax.ShapeDtypeStruct((n, d), BF16), jax.ShapeDtypeStruct((n, d), F32)],
        scratch_shapes=[pltpu.VMEM((tm + 2 * SUBLANES, d), F32), pltpu.VMEM((tm, d), F32)],
        compiler_params=_params("arbitrary"),
        name="rglru_proj",
    )(xs, g, mods, w_in, conv_w, conv_b)


def _sigmoid_tanh(x):
    return 0.5 * jnp.tanh(0.5 * x) + 0.5


def _lru_direction(xc_ref, wg_ref, bg_ref, lam_ref, a_ref, b_ref, *, direction):
    tm, d = xc_ref.shape
    bw = d // LRU_BLOCKS
    z = -lam_ref[direction]
    softplus = jnp.maximum(z, 0.0) + jnp.log(1.0 + jnp.exp(-jnp.abs(z)))
    for nb in range(LRU_BLOCKS):
        sl = slice(nb * bw, (nb + 1) * bw)
        xb = xc_ref[:, sl]
        gg = jnp.dot(xb.astype(BF16), wg_ref[direction, nb], preferred_element_type=F32)
        gg = gg + bg_ref[direction, :, nb * 2 * bw:(nb + 1) * 2 * bw]
        r = _sigmoid_tanh(gg[:, :bw])
        gi = _sigmoid_tanh(gg[:, bw:])
        log_a = -LRU_C * r * softplus[:, sl]
        th = jnp.tanh(log_a)
        a_ref[pl.ds(nb, tm, stride=LRU_BLOCKS), :] = jnp.exp(log_a)
        b_ref[pl.ds(nb, tm, stride=LRU_BLOCKS), :] = jnp.sqrt(-2.0 * th / (1.0 - th)) * (gi * xb)


def _lru_kernel(xc_f, xc_b, wg_ref, bg_ref, lam_ref, of_ref, ob_ref, af_ref, bf_ref, ab_ref, bb_ref,
                hf_ref, hb_ref, h_ref, *, tiles_per_batch):
    tm, d = xc_f.shape
    bw = d // LRU_BLOCKS

    @pl.when(pl.program_id(0) % tiles_per_batch == 0)
    def _():
        h_ref[...] = jnp.zeros_like(h_ref)

    _lru_direction(xc_f, wg_ref, bg_ref, lam_ref, af_ref, bf_ref, direction=0)
    _lru_direction(xc_b, wg_ref, bg_ref, lam_ref, ab_ref, bb_ref, direction=1)

    def step(t, carry):
        hf, hb = carry
        rf = pl.multiple_of(t * LRU_BLOCKS, LRU_BLOCKS)
        rb = pl.multiple_of((tm - 1 - t) * LRU_BLOCKS, LRU_BLOCKS)
        hf = af_ref[pl.ds(rf, LRU_BLOCKS), :] * hf + bf_ref[pl.ds(rf, LRU_BLOCKS), :]
        hb = ab_ref[pl.ds(rb, LRU_BLOCKS), :] * hb + bb_ref[pl.ds(rb, LRU_BLOCKS), :]
        hf_ref[pl.ds(rf, LRU_BLOCKS), :] = hf
        hb_ref[pl.ds(rb, LRU_BLOCKS), :] = hb
        return hf, hb

    hf, hb = lax.fori_loop(0, tm, step, (h_ref[0], h_ref[1]), unroll=8)
    h_ref[0] = hf
    h_ref[1] = hb
    for nb in range(LRU_BLOCKS):
        sl = slice(nb * bw, (nb + 1) * bw)
        of_ref[:, sl] = hf_ref[pl.ds(nb, tm, stride=LRU_BLOCKS), :].astype(BF16)
        ob_ref[:, sl] = hb_ref[pl.ds(nb, tm, stride=LRU_BLOCKS), :].astype(BF16)


def _lru_call(xc, w_gate, b_gate, lam, dims):
    tm, d, n = TOKEN_TILE, dims["d"], dims["n"]
    tpb, ct = dims["tpb"], dims["ct"]
    fwd = lambda i: (i, 0)
    bwd = lambda i: (_bwd_tile(i, tpb, ct), 0)
    c3 = lambda i: (0, 0, 0)
    c4 = lambda i: (0, 0, 0, 0)
    time_major = pltpu.VMEM((tm * LRU_BLOCKS, d // LRU_BLOCKS), F32)
    return pl.pallas_call(
        functools.partial(_lru_kernel, tiles_per_batch=tpb),
        grid=(dims["tiles"],),
        in_specs=[
            pl.BlockSpec((tm, d), fwd), pl.BlockSpec((tm, d), bwd),
            pl.BlockSpec(w_gate.shape, c4), pl.BlockSpec(b_gate.shape, c3), pl.BlockSpec(lam.shape, c3),
        ],
        out_specs=[pl.BlockSpec((tm, d), fwd), pl.BlockSpec((tm, d), bwd)],
        out_shape=[jax.ShapeDtypeStruct((n, d), BF16), jax.ShapeDtypeStruct((n, d), BF16)],
        scratch_shapes=[time_major] * 6 + [pltpu.VMEM((2, LRU_BLOCKS, d // LRU_BLOCKS), F32)],
        compiler_params=_params("arbitrary"),
        name="rglru_scan",
    )(xc, xc, w_gate, b_gate, lam)


def _first_argmax(vals):
    best = vals[0]
    idx = jnp.zeros_like(best)
    for k in range(1, len(vals)):
        better = vals[k] > best
        idx = jnp.where(better, float(k), idx)
        best = jnp.where(better, vals[k], best)
    return idx, best


def _pick(idx, vals):
    out = vals[0]
    for k in range(1, len(vals)):
        out = jnp.where(idx == float(k), vals[k], out)
    return out


def _route_tile(x, g_ref, mod_ref, wr_ref, br_ref, hx_ref, route_ref, cnt_ref, carry_ref):
    tm = x.shape[0]
    h = _modulated(x, g_ref[...], mod_ref[...], 3)
    _store_token_tiles(hx_ref, h)
    hi = h.astype(BF16)
    lo = (h - hi.astype(F32)).astype(BF16)
    logits_tok = (jnp.dot(lo, wr_ref[0], preferred_element_type=F32)
                  + jnp.dot(hi, wr_ref[1], preferred_element_type=F32)
                  + jnp.dot(hi, wr_ref[0], preferred_element_type=F32))
    logits = logits_tok.T[0:N_EXPERTS, :]
    e = jnp.exp(logits - jnp.max(logits, axis=0, keepdims=True))
    scores = e / jnp.sum(e, axis=0, keepdims=True)
    sel = scores + br_ref[...]
    sel_rows = [sel[k:k + 1, :] for k in range(N_EXPERTS)]
    group_scores = []
    for gi in range(N_GROUPS):
        v = sel_rows[gi * GROUP_SIZE:(gi + 1) * GROUP_SIZE]
        best = v[0] + v[1]
        for a, b in PAIRS[1:]:
            best = jnp.maximum(best, v[a] + v[b])
        group_scores.append(best)
    grp, _ = _first_argmax(group_scores)
    in_sel = [_pick(grp, [sel_rows[gi * GROUP_SIZE + k] for gi in range(N_GROUPS)])
              for k in range(GROUP_SIZE)]
    i1, _ = _first_argmax(in_sel)
    rest = [jnp.where(i1 == float(k), -jnp.inf, in_sel[k]) for k in range(GROUP_SIZE)]
    i2, _ = _first_argmax(rest)
    lo = jnp.minimum(i1, i2)
    hi = jnp.maximum(i1, i2)
    pair = jnp.zeros_like(lo)
    for k, (slot_a, slot_b) in enumerate(PAIR_SLOTS):
        here = jnp.logical_and(lo == float(min(slot_a, slot_b)), hi == float(max(slot_a, slot_b)))
        pair = jnp.where(here, float(k), pair)
    cls = grp * float(len(PAIR_SLOTS)) + pair
    crow = lax.broadcasted_iota(jnp.int32, (CLASS_ROWS, tm), 0).astype(F32)
    onehot = jnp.where(crow == cls, 1.0, 0.0)
    rows = lax.broadcasted_iota(jnp.int32, (tm, tm), 0)
    cols = lax.broadcasted_iota(jnp.int32, (tm, tm), 1)
    upper = jnp.where(rows <= cols, 1.0, 0.0).astype(BF16)
    cum = jnp.dot(onehot.astype(BF16), upper, preferred_element_type=F32)
    carry = carry_ref[:, 0:1]
    rank = jnp.sum(onehot * (cum - 1.0 + carry), axis=0, keepdims=True)
    new_carry = carry + jnp.sum(onehot, axis=1, keepdims=True)
    carry_ref[...] = jnp.broadcast_to(new_carry, carry_ref.shape)
    cnt_ref[...] = jnp.broadcast_to(new_carry, cnt_ref.shape)
    zero = jnp.zeros_like(cls)
    route_ref[...] = jnp.concatenate([cls, rank, zero, zero, zero, zero, zero, zero], axis=0)


ROUTER_TILES = 2


def _router_kernel(x_ref, g_ref, *refs):
    mod_refs = refs[:ROUTER_TILES]
    wr_ref, br_ref, hx_ref, route_ref, cnt_ref, carry_ref = refs[ROUTER_TILES:]
    tm = x_ref.shape[0] // ROUTER_TILES
    rows = hx_ref.shape[0] // ROUTER_TILES

    @pl.when(pl.program_id(0) == 0)
    def _():
        carry_ref[...] = jnp.zeros_like(carry_ref)

    for k, mod_ref in enumerate(mod_refs):
        _route_tile(x_ref[k * tm:(k + 1) * tm, :], g_ref, mod_ref, wr_ref, br_ref,
                    hx_ref.at[pl.ds(k * rows, rows), :], route_ref.at[:, pl.ds(k * tm, tm)],
                    cnt_ref, carry_ref)


def _router_weights(w_router):
    w = jnp.pad(w_router.astype(F32), ((0, 0), (0, LANES - w_router.shape[1])))
    hi = w.astype(BF16)
    lo = (w - hi.astype(F32)).astype(BF16)
    return jnp.stack([hi, lo])


def _router_call(xs, g, mods, layer, w_router_t, b_router_col, dims):
    d, n = dims["d"], dims["n"]
    tpb, ct, b = dims["tpb"], dims["ct"], dims["b"]
    assert dims["tiles"] % ROUTER_TILES == 0
    tm = TOKEN_TILE * ROUTER_TILES
    row = lambda i: (i, 0)
    const = lambda i: (0, 0)
    per = d // LANES
    mod_specs = [pl.BlockSpec((None, None, 1, 6 * d),
                              lambda i, k=k: (layer, _mod_row(i * ROUTER_TILES + k, tpb, ct, b), 0, 0))
                 for k in range(ROUTER_TILES)]
    return pl.pallas_call(
        _router_kernel,
        grid=(dims["tiles"] // ROUTER_TILES,),
        in_specs=[pl.BlockSpec((tm, d), row), pl.BlockSpec((1, d), const), *mod_specs,
                  pl.BlockSpec(w_router_t.shape, lambda i: (0, 0, 0)),
                  pl.BlockSpec(b_router_col.shape, const)],
        out_specs=[pl.BlockSpec((tm * per, LANES), row),
                   pl.BlockSpec((SUBLANES, tm), lambda i: (0, i)),
                   pl.BlockSpec((CLASS_ROWS, LANES), const)],
        out_shape=[jax.ShapeDtypeStruct((n * per, LANES), F32),
                   jax.ShapeDtypeStruct((SUBLANES, n), F32),
                   jax.ShapeDtypeStruct((CLASS_ROWS, LANES), F32)],
        scratch_shapes=[pltpu.VMEM((CLASS_ROWS, LANES), F32)],
        compiler_params=_params("arbitrary"),
        name="router",
    )(xs, g, *([mods] * ROUTER_TILES), w_router_t, b_router_col)


ROW_DMA_UNROLL = 8


def _dispatch_kernel(pos_ref, hx_ref, init_ref, o_ref, stage_ref, sems, *, tokens):
    del init_ref
    per = hx_ref.shape[0] // tokens
    i = pl.program_id(0)
    slot = i % 2
    base = i * tokens
    stage_ref[slot] = hx_ref[...]

    def issue(r, carry):
        src = pl.multiple_of(r * per, per)
        dst = pl.multiple_of(pos_ref[base + r], per)
        pltpu.make_async_copy(stage_ref.at[slot, pl.ds(src, per)], o_ref.at[pl.ds(dst, per)],
                              sems.at[slot]).start()
        return carry

    lax.fori_loop(0, tokens, issue, 0, unroll=ROW_DMA_UNROLL)

    def wait_tile(s):
        pltpu.make_async_copy(stage_ref.at[s], o_ref.at[pl.ds(0, tokens * per)], sems.at[s]).wait()

    @pl.when(i > 0)
    def _():
        wait_tile(1 - slot)

    @pl.when(i == pl.num_programs(0) - 1)
    def _():
        wait_tile(slot)


def _dispatch_call(pos_rows, hx, init, dims):
    tm = TOKEN_TILE
    per = dims["d"] // LANES
    return pl.pallas_call(
        functools.partial(_dispatch_kernel, tokens=tm),
        grid_spec=pltpu.PrefetchScalarGridSpec(
            num_scalar_prefetch=1,
            grid=(dims["tiles"],),
            in_specs=[pl.BlockSpec((tm * per, LANES), lambda i, p: (i, 0)),
                      pl.BlockSpec(memory_space=pl.ANY)],
            out_specs=pl.BlockSpec(memory_space=pl.ANY),
            scratch_shapes=[pltpu.VMEM((2, tm * per, LANES), hx.dtype), pltpu.SemaphoreType.DMA((2,))],
        ),
        out_shape=jax.ShapeDtypeStruct(init.shape, hx.dtype),
        input_output_aliases={2: 0},
        compiler_params=_params("arbitrary"),
        name="dispatch",
    )(pos_rows, hx, init)


def _expert_kernel(ea_ref, eb_ref, ok_ref, x_ref, wr_ref, w1a, w3a, w2a, w1b, w3b, w2b, o_ref):
    d = w1a.shape[0]
    te = x_ref.shape[0] // (d // LANES)
    t = pl.program_id(0)

    @pl.when(ok_ref[t] != 0)
    def _():
        h = _load_token_tiles(x_ref, te, d)
        x = h.astype(BF16)
        dw = wr_ref[pl.ds(ea_ref[t], 1), :] - wr_ref[pl.ds(eb_ref[t], 1), :]
        gate_a = jax.nn.sigmoid(jnp.sum(h * dw, axis=1, keepdims=True))
        gate_b = 1.0 - gate_a

        def expert(w1, w3, w2):
            a = jnp.dot(x, w1[...], preferred_element_type=F32)
            b = jnp.dot(x, w3[...], preferred_element_type=F32)
            u = (a * jax.nn.sigmoid(a)) * b
            return jnp.dot(u.astype(BF16), w2[...], preferred_element_type=F32)

        _store_token_tiles(o_ref, gate_a * expert(w1a, w3a, w2a) + gate_b * expert(w1b, w3b, w2b))

    @pl.when(ok_ref[t] == 0)
    def _():
        o_ref[...] = jnp.zeros_like(o_ref)


def _expert_call(tile_a, tile_b, tile_ok, xsorted, w_router_t, w1, w3, w2, layer, d):
    te = TOKEN_TILE
    per = d // LANES
    n_tiles = xsorted.shape[0] // (te * per)
    de = w1.shape[-1]
    sel_a = lambda t, ea, eb, ok: (layer, ea[t], 0, 0)
    sel_b = lambda t, ea, eb, ok: (layer, eb[t], 0, 0)
    used = lambda t, ea, eb, ok: (t, 0)
    return pl.pallas_call(
        _expert_kernel,
        grid_spec=pltpu.PrefetchScalarGridSpec(
            num_scalar_prefetch=3,
            grid=(n_tiles,),
            in_specs=[
                pl.BlockSpec((te * per, LANES), used),
                pl.BlockSpec(w_router_t.shape, lambda t, ea, eb, ok: (0, 0)),
                pl.BlockSpec((None, None, d, de), sel_a), pl.BlockSpec((None, None, d, de), sel_a),
                pl.BlockSpec((None, None, de, d), sel_a),
                pl.BlockSpec((None, None, d, de), sel_b), pl.BlockSpec((None, None, d, de), sel_b),
                pl.BlockSpec((None, None, de, d), sel_b),
            ],
            out_specs=pl.BlockSpec((te * per, LANES), used),
        ),
        out_shape=jax.ShapeDtypeStruct((n_tiles * te * per, LANES), F32),
        compiler_params=_params("arbitrary"),
        name="experts",
    )(tile_a, tile_b, tile_ok, xsorted, w_router_t, w1, w3, w2, w1, w3, w2)


def _combine_kernel(pos_ref, y_ref, x_ref, mod_ref, gf_ref, o_ref, buf_ref, sems,
                    *, tile_of, final_norm):
    tm, d = x_ref.shape
    per = d // LANES
    i = pl.program_id(0)
    slot = i % 2

    def gather(step, s):
        base = tile_of(step) * tm

        def issue(r, carry):
            src = pl.multiple_of(pos_ref[base + r], per)
            dst = pl.multiple_of(r * per, per)
            pltpu.make_async_copy(y_ref.at[pl.ds(src, per)], buf_ref.at[s, pl.ds(dst, per)],
                                  sems.at[s]).start()
            return carry

        lax.fori_loop(0, tm, issue, 0, unroll=ROW_DMA_UNROLL)

    @pl.when(i == 0)
    def _():
        gather(0, 0)

    @pl.when(i + 1 < pl.num_programs(0))
    def _():
        gather(i + 1, 1 - slot)

    pltpu.make_async_copy(y_ref.at[pl.ds(0, tm * per)], buf_ref.at[slot], sems.at[slot]).wait()
    out = x_ref[...] + mod_ref[:, 5 * d:6 * d] * _load_token_tiles(buf_ref.at[slot], tm, d)
    if final_norm:
        ms = jnp.mean(out * out, axis=-1, keepdims=True)
        out = out * lax.rsqrt(ms + NORM_EPS) * gf_ref[...]
    o_ref[...] = out


def _combine_call(pos, ysorted, xs, mods, layer, g_final, dims, final_norm):
    tm, d = TOKEN_TILE, dims["d"]
    tpb, ct, b = dims["tpb"], dims["ct"], dims["b"]
    if final_norm:
        lt = tpb - ct
        tile_of = lambda i: (i // lt) * tpb + ct + i % lt
        n_tiles = b * lt
    else:
        tile_of = lambda i: i
        n_tiles = dims["tiles"]
    mod_spec = pl.BlockSpec((None, None, 1, 6 * d),
                            lambda i, p: (layer, _mod_row(tile_of(i), tpb, ct, b), 0, 0))
    return pl.pallas_call(
        functools.partial(_combine_kernel, tile_of=tile_of, final_norm=final_norm),
        grid_spec=pltpu.PrefetchScalarGridSpec(
            num_scalar_prefetch=1,
            grid=(n_tiles,),
            in_specs=[
                pl.BlockSpec(memory_space=pl.ANY),
                pl.BlockSpec((tm, d), lambda i, p: (tile_of(i), 0)),
                mod_spec,
                pl.BlockSpec((1, d), lambda i, p: (0, 0)),
            ],
            out_specs=pl.BlockSpec((tm, d), lambda i, p: (i, 0)),
            scratch_shapes=[pltpu.VMEM((2, tm * (d // LANES), LANES), F32), pltpu.SemaphoreType.DMA((2,))],
        ),
        out_shape=jax.ShapeDtypeStruct((n_tiles * tm, d), F32),
        compiler_params=_params("arbitrary"),
        name="combine",
    )(pos, ysorted, xs, mods, g_final)


_CLASS_A = np.array([g * GROUP_SIZE + a for g in range(N_GROUPS) for a, _ in PAIR_SLOTS], np.int32)
_CLASS_B = np.array([g * GROUP_SIZE + b for g in range(N_GROUPS) for _, b in PAIR_SLOTS], np.int32)


def _routing_plan(route, counts, n_tiles_padded):
    te = TOKEN_TILE
    cls = route[0].astype(jnp.int32)
    rank = route[1].astype(jnp.int32)
    cnt = counts[:N_CLASSES, 0].astype(jnp.int32)
    tiles = (cnt + te - 1) // te
    tile_end = jnp.cumsum(tiles)
    tile_start = tile_end - tiles
    onehot = (cls[:, None] == jnp.arange(N_CLASSES, dtype=jnp.int32)[None, :]).astype(jnp.int32)
    pos = jnp.sum(onehot * (tile_start * te)[None, :], axis=1) + rank
    t = jnp.arange(n_tiles_padded, dtype=jnp.int32)
    ok = (t < tile_end[-1]).astype(jnp.int32)
    tcls = jnp.sum((tile_end[None, :] <= jnp.minimum(t, tile_end[-1] - 1)[:, None]).astype(jnp.int32), axis=1)
    tile_a = jnp.take(jnp.asarray(_CLASS_A), tcls)
    tile_b = jnp.take(jnp.asarray(_CLASS_B), tcls)
    return pos, tile_a, tile_b, ok


def _moe(xs, hx, route, counts, sorted_buf, mods, layer, w_router_t, w1, w3, w2, g_final, dims, final_norm):
    per = dims["d"] // LANES
    n_tiles_padded = sorted_buf.shape[0] // (TOKEN_TILE * per)
    pos, tile_a, tile_b, tile_ok = _routing_plan(route, counts, n_tiles_padded)
    xsorted = _dispatch_call(pos * per, hx, sorted_buf, dims)
    ysorted = _expert_call(tile_a, tile_b, tile_ok, xsorted, w_router_t, w1, w3, w2, layer, dims["d"])
    return _combine_call(pos * per, ysorted, xs, mods, layer, g_final, dims, final_norm), xsorted


def _pos_table(n_tokens, dim):
    rows = n_tokens // GRID_WIDTH
    r, col = jnp.meshgrid(jnp.arange(rows, dtype=F32), jnp.arange(GRID_WIDTH, dtype=F32), indexing="ij")
    quarter = dim // 4
    freqs = jnp.exp(-math.log(POS_BASE) * jnp.arange(quarter, dtype=F32) / quarter)

    def enc(p):
        ang = p.reshape(-1, 1) * freqs
        return jnp.concatenate([jnp.sin(ang), jnp.cos(ang)], axis=-1)

    return jnp.concatenate([enc(r), enc(col)], axis=-1)


def _mlstm_weights(w_in, b_gate):
    nq = HEADS * HEAD_QK
    nv = HEADS * HEAD_V
    wq = w_in[:, 0:nq] * (HEAD_QK ** -0.5)
    wk = w_in[:, nq:2 * nq]
    wv = w_in[:, 2 * nq:2 * nq + nv]
    wo = w_in[:, 2 * nq + nv:2 * nq + 2 * nv]
    wg = w_in[:, 2 * nq + 2 * nv:]
    wg_pad = jnp.pad(wg, ((0, 0), (0, LANES - wg.shape[1])))
    w_tok = jnp.concatenate([wq, wk, wg_pad], axis=1).astype(BF16)
    w_feat = jnp.concatenate([wv, wo, wg], axis=1).T.astype(BF16)
    bias = b_gate.reshape(-1).astype(F32)
    bias_c = jnp.pad(bias, (0, LANES - bias.shape[0])).reshape(1, LANES)
    bias_r = bias.reshape(-1, 1)
    return w_tok, w_feat, bias_c, bias_r


def kernel(x, c, ctx, c_ctx, w_ada, b_ada, g_mix, g_ffn, g_final, m_w_in, m_b_gate, m_g_head, m_w_out, r_w_in, r_conv_w, r_conv_b, r_w_gate, r_b_gate, r_lam, r_w_out, w_router, b_router, e_w1, e_w3, e_w2):
    batch, t_len, d = x.shape
    ctx_len = ctx.shape[1]
    depth = w_ada.shape[0]
    tm = TOKEN_TILE
    assert t_len % tm == 0 and ctx_len % tm == 0 and batch + 1 <= ADA_ROWS
    assert d == HEADS * HEAD_V and t_len % GRID_WIDTH == 0
    s_len = ctx_len + t_len
    dims = dict(b=batch, d=d, n=batch * s_len, tpb=s_len // tm, ct=ctx_len // tm,
                tiles=batch * s_len // tm)

    cvec = jnp.concatenate([c, c_ctx[None, :], jnp.zeros((ADA_ROWS - batch - 1, d), F32)], axis=0)
    mods = _ada_call(cvec, w_ada, b_ada).reshape(depth, ADA_ROWS, 1, 6 * d)
    xs = (ctx.reshape(batch * ctx_len, d), x.reshape(batch * t_len, d), _pos_table(t_len, d))

    w_router_t = w_router.T.astype(F32)
    w_router_pieces = _router_weights(w_router)
    b_router_col = b_router.reshape(-1, 1).astype(F32)
    g_final2 = g_final.reshape(1, d)
    e_w1b, e_w3b, e_w2b = e_w1.astype(BF16), e_w3.astype(BF16), e_w2.astype(BF16)
    sorted_buf = jnp.zeros(((dims["tiles"] + N_CLASSES) * tm * (d // LANES), LANES), F32)
    out = None
    for i in range(depth):
        j = i // 2
        g_mix_i = g_mix[i].reshape(1, d)
        g_ffn_i = g_ffn[i].reshape(1, d)
        if i % 2 == 0:
            w_tok, w_feat, bias_c, bias_r = _mlstm_weights(m_w_in[j], m_b_gate[j])
            proj = _mproj_call(xs, g_mix_i, mods, i, w_tok, w_feat, dims)
            if isinstance(xs, tuple):
                xs, proj = proj[0], proj[1:]
            q, k, vt, ot, gc, gr = proj
            hf, hb = _mlstm_call(q, k, vt, gc, gr, bias_c, bias_r, dims)
            w_out = (m_g_head[j][:, None] * m_w_out[j]).astype(BF16)
            xs = _readout_call(_mreadout_kernel, "mlstm_readout", True, hf, hb, ot, w_out, xs, mods, i, dims)
        else:
            gy, xc = _rproj_call(xs, g_mix_i, mods, i, r_w_in[j].astype(BF16), r_conv_w[j],
                                 r_conv_b[j].reshape(1, d), dims)
            hf, hb = _lru_call(xc, r_w_gate[j].astype(BF16), r_b_gate[j].reshape(2, 1, -1),
                               r_lam[j].reshape(2, 1, d), dims)
            xs = _readout_call(_readout_kernel, "rglru_readout", False, hf, hb, gy,
                               r_w_out[j].astype(BF16), xs, mods, i, dims)
        hx, route, counts = _router_call(xs, g_ffn_i, mods, i, w_router_pieces, b_router_col, dims)
        last = i == depth - 1
        res, sorted_buf = _moe(xs, hx, route, counts, sorted_buf, mods, i, w_router_t,
                               e_w1b, e_w3b, e_w2b, g_final2, dims, last)
        if last:
            out = res
        else:
            xs = res
    return out.reshape(batch, t_len, d)
```

```python
import functools
import math

import jax
import jax.numpy as jnp
import numpy as np
from jax import lax
from jax.experimental import pallas as pl
from jax.experimental.pallas import tpu as pltpu

F32 = jnp.float32
BF16 = jnp.bfloat16
HIGHEST = lax.Precision.HIGHEST

TOKEN_TILE = 256
LANES = 128
SUBLANES = 8
VMEM_LIMIT_BYTES = 56 * 1024 * 1024
NORM_EPS = 1e-6
GRID_WIDTH = 64
POS_BASE = 10000.0
HEADS = 8
HEAD_QK = 64
HEAD_V = 128
LRU_BLOCKS = 8
LRU_C = 8.0
CONV_TAPS = 4
CONV_LEFT = CONV_TAPS // 2
N_EXPERTS = 16
N_GROUPS = 4
GROUP_SIZE = N_EXPERTS // N_GROUPS
PAIRS = [(a, b) for a in range(GROUP_SIZE) for b in range(a + 1, GROUP_SIZE)]
PAIR_SLOTS = [(0, 1), (0, 2), (0, 3), (1, 3), (1, 2), (3, 2)]
N_CLASSES = N_GROUPS * len(PAIRS)
CLASS_ROWS = 32
ROUTE_COLS = LANES
ADA_ROWS = 16


def _params(*sem):
    return pltpu.CompilerParams(dimension_semantics=sem, vmem_limit_bytes=VMEM_LIMIT_BYTES)


def _mod_row(i, tiles_per_batch, ctx_tiles, batch):
    return jnp.where(i % tiles_per_batch < ctx_tiles, batch, i // tiles_per_batch)


def _bwd_tile(i, tiles_per_batch, ctx_tiles):
    b = i // tiles_per_batch
    j = i % tiles_per_batch
    jb = jnp.where(j < ctx_tiles, ctx_tiles - 1 - j, tiles_per_batch - 1 - (j - ctx_tiles))
    return b * tiles_per_batch + jb


def _ada_kernel(c_ref, w_ref, b_ref, o_ref):
    c = c_ref[...]
    s = c * jax.nn.sigmoid(c)
    o_ref[...] = jnp.dot(s, w_ref[...], preferred_element_type=F32, precision=HIGHEST) + b_ref[...]


def _ada_call(cvec, w_ada, b_ada):
    depth, d, n = w_ada.shape
    tn = n // 4
    return pl.pallas_call(
        _ada_kernel,
        grid=(depth, n // tn),
        in_specs=[
            pl.BlockSpec((ADA_ROWS, d), lambda l, j: (0, 0)),
            pl.BlockSpec((None, d, tn), lambda l, j: (l, 0, j)),
            pl.BlockSpec((None, 1, tn), lambda l, j: (l, 0, j)),
        ],
        out_specs=pl.BlockSpec((None, ADA_ROWS, tn), lambda l, j: (l, 0, j)),
        out_shape=jax.ShapeDtypeStruct((depth, ADA_ROWS, n), F32),
        compiler_params=_params("arbitrary", "arbitrary"),
        name="ada",
    )(cvec, w_ada, b_ada.reshape(depth, 1, n))


def _modulated(x, g, mod, shift_idx):
    d = x.shape[-1]
    ms = jnp.mean(x * x, axis=-1, keepdims=True)
    xn = x * lax.rsqrt(ms + NORM_EPS) * g
    shift = mod[:, shift_idx * d:(shift_idx + 1) * d]
    scale = mod[:, (shift_idx + 1) * d:(shift_idx + 2) * d]
    return xn * (1.0 + scale) + shift


def _mod_spec(layer, dims):
    tpb, ct, b = dims["tpb"], dims["ct"], dims["b"]
    return pl.BlockSpec((None, None, 1, 6 * dims["d"]),
                        lambda i, *_: (layer, _mod_row(i, tpb, ct, b), 0, 0))


def _store_token_tiles(ref, x):
    tokens, d = x.shape
    per = d // LANES
    for s in range(per):
        ref[pl.ds(s, tokens, stride=per), :] = x[:, s * LANES:(s + 1) * LANES]


def _load_token_tiles(ref, tokens, d):
    per = d // LANES
    return jnp.concatenate([ref[pl.ds(s, tokens, stride=per), :] for s in range(per)], axis=1)


def _gelu_tanh(y):
    return 0.5 * y * (1.0 + jnp.tanh(math.sqrt(2.0 / math.pi) * (y + 0.044715 * (y * y * y))))


SRC_BLOCK = 128
_NT = (((1,), (1,)), ((), ()))
_TN = (((0,), (0,)), ((), ()))


def _mproj_kernel(*refs, assemble, tiles_per_batch, ctx_tiles):
    if assemble:
        ctx_ref, lat_ref, pos_ref, g_ref, mod_ref, w_ref, wt_ref, xs_ref = refs[:8]
        is_ctx = pl.program_id(0) % tiles_per_batch < ctx_tiles
        x = jnp.where(is_ctx, ctx_ref[...], lat_ref[...] + pos_ref[...])
        xs_ref[...] = x
    else:
        x_ref, g_ref, mod_ref, w_ref, wt_ref = refs[:5]
        x = x_ref[...]
    q_ref, k_ref, vt_ref, ot_ref, gc_ref, gr_ref = refs[-6:]
    d = x.shape[-1]
    nq = q_ref.shape[-1]
    h = _modulated(x, g_ref[...], mod_ref[...], 0).astype(BF16)
    q_ref[...] = jnp.dot(h, w_ref[:, 0:nq], preferred_element_type=F32).astype(BF16)
    k_ref[...] = jnp.dot(h, w_ref[:, nq:2 * nq], preferred_element_type=F32).astype(BF16)
    gc_ref[...] = jnp.dot(h, w_ref[:, 2 * nq:2 * nq + LANES], preferred_element_type=F32)
    vt_ref[...] = lax.dot_general(wt_ref[0:d, :], h, _NT, preferred_element_type=F32).astype(BF16)
    ot = lax.dot_general(wt_ref[d:2 * d, :], h, _NT, preferred_element_type=F32)
    ot_ref[...] = jax.nn.sigmoid(ot).astype(BF16)
    gr_ref[...] = lax.dot_general(wt_ref[2 * d:2 * d + 4 * HEADS, :], h, _NT, preferred_element_type=F32)


def _mproj_call(stream, g, mods, layer, w_tok, w_feat, dims):
    tm, d, n = TOKEN_TILE, dims["d"], dims["n"]
    tpb, ct = dims["tpb"], dims["ct"]
    lt = tpb - ct
    nq = HEADS * HEAD_QK
    row = lambda i: (i, 0)
    col = lambda i: (0, i)
    const = lambda i: (0, 0)
    assemble = isinstance(stream, tuple)
    if assemble:
        stream_specs = [
            pl.BlockSpec((tm, d), lambda i: ((i // tpb) * ct + jnp.minimum(i % tpb, ct - 1), 0)),
            pl.BlockSpec((tm, d), lambda i: ((i // tpb) * lt + jnp.maximum(i % tpb - ct, 0), 0)),
            pl.BlockSpec((tm, d), lambda i: (jnp.maximum(i % tpb - ct, 0), 0)),
        ]
        stream_out = [pl.BlockSpec((tm, d), row)], [jax.ShapeDtypeStruct((n, d), F32)]
    else:
        stream, stream_specs, stream_out = (stream,), [pl.BlockSpec((tm, d), row)], ([], [])
    return pl.pallas_call(
        functools.partial(_mproj_kernel, assemble=assemble, tiles_per_batch=tpb, ctx_tiles=ct),
        grid=(dims["tiles"],),
        in_specs=stream_specs + [
            pl.BlockSpec((1, d), const),
            _mod_spec(layer, dims),
            pl.BlockSpec(w_tok.shape, const),
            pl.BlockSpec(w_feat.shape, const),
        ],
        out_specs=stream_out[0] + [
            pl.BlockSpec((tm, nq), row),
            pl.BlockSpec((tm, nq), row),
            pl.BlockSpec((d, tm), col),
            pl.BlockSpec((d, tm), col),
            pl.BlockSpec((tm, LANES), row),
            pl.BlockSpec((4 * HEADS, tm), col),
        ],
        out_shape=stream_out[1] + [
            jax.ShapeDtypeStruct((n, nq), BF16),
            jax.ShapeDtypeStruct((n, nq), BF16),
            jax.ShapeDtypeStruct((d, n), BF16),
            jax.ShapeDtypeStruct((d, n), BF16),
            jax.ShapeDtypeStruct((n, LANES), F32),
            jax.ShapeDtypeStruct((4 * HEADS, n), F32),
        ],
        compiler_params=_params("arbitrary"),
        name="mlstm_proj",
    )(*stream, g, mods, w_tok, w_feat)


def _mlstm_gate_stats(gc_ref, gr_ref, bc_ref, br_ref, m_ref, *, backward):
    tm = gc_ref.shape[0]
    gate_i = 2 * HEADS if backward else 0
    gate_f = gate_i + HEADS
    last = 0 if backward else tm - 1
    src = lax.broadcasted_iota(jnp.int32, (tm, tm), 0)
    tgt = lax.broadcasted_iota(jnp.int32, (tm, tm), 1)
    visible = (src >= tgt) if backward else (src <= tgt)
    neg_mask = jnp.where(visible, 0.0, -jnp.inf)
    xc = gc_ref[...] + bc_ref[...]
    xr = gr_ref[...] + br_ref[...]
    before = (src <= tgt) if backward else (src >= tgt)
    cum_c = _split_dot(jnp.where(before, 1.0, 0.0).astype(BF16), jax.nn.log_sigmoid(xc), left=True)
    cum_r = _split_dot(jnp.where(visible, 1.0, 0.0).astype(BF16), jax.nn.log_sigmoid(xr), left=False)
    b_row = cum_r[gate_f:gate_f + HEADS, :]
    c_row = xr[gate_i:gate_i + HEADS, :] - b_row
    c_col = xc - pltpu.roll(cum_c, LANES - HEADS, axis=1)
    run_col = c_col
    shift = 1
    while shift < tm:
        fill = jnp.full((shift, LANES), -jnp.inf, F32)
        if backward:
            moved = jnp.concatenate([run_col[shift:, :], fill], axis=0)
        else:
            moved = jnp.concatenate([fill, run_col[:tm - shift, :]], axis=0)
        run_col = jnp.maximum(run_col, moved)
        shift *= 2
    run = run_col.T[gate_i:gate_i + HEADS, :]
    m = m_ref[:, 0:1]
    log_inter = b_row + m
    m_pos = jnp.maximum(log_inter, b_row + run)
    shift_row = b_row - m_pos
    w_inter = jnp.exp(log_inter - m_pos)
    floor = jnp.exp(-m_pos)
    b_last = b_row[:, last:last + 1]
    m_new = m_pos[:, last:last + 1]
    w_src = jnp.exp(b_last + c_row - m_new)
    decay = jnp.exp(b_last + m - m_new)
    m_ref[...] = jnp.broadcast_to(m_new, m_ref.shape)
    return neg_mask, c_col, shift_row, w_inter, floor, w_src, decay


def _mlstm_head(h, q_ref, k_ref, vt_ref, o_ref, ct_ref, stats, *, backward):
    neg_mask, c_col, shift_row, w_inter, floor, w_src, decay = stats
    tm = q_ref.shape[0]
    gate_i = 2 * HEADS if backward else 0
    half = lax.broadcasted_iota(jnp.int32, (tm, LANES), 1) < HEAD_QK
    ones = jnp.ones((2 * SUBLANES, tm), BF16)
    if True:
        pair = slice((h // 2) * LANES, (h // 2 + 1) * LANES)
        qp = q_ref[:, pair]
        km = jnp.where(half if h % 2 == 0 else jnp.logical_not(half), k_ref[:, pair], 0.0).astype(BF16)
        vaug = jnp.concatenate([vt_ref[h * HEAD_V:(h + 1) * HEAD_V, :], ones], axis=0)
        ct = ct_ref[h]
        acc = w_inter[h:h + 1, :] * lax.dot_general(ct.astype(BF16), qp, _NT,
                                                    preferred_element_type=F32)
        for blk in range(tm // SRC_BLOCK):
            rows = slice(blk * SRC_BLOCK, (blk + 1) * SRC_BLOCK)
            st = lax.dot_general(km[rows, :], qp, _NT, preferred_element_type=F32)
            e = (neg_mask[rows, :] + shift_row[h:h + 1, :]) + c_col[rows, gate_i + h:gate_i + h + 1]
            p = (st * jnp.exp(e)).astype(BF16)
            acc = acc + jnp.dot(vaug[:, rows], p, preferred_element_type=F32)
        den = acc[HEAD_V:HEAD_V + 1, :]
        scale = 1.0 / jnp.maximum(jnp.abs(den), floor[h:h + 1, :])
        o_ref[h * HEAD_V:(h + 1) * HEAD_V, :] = (acc[0:HEAD_V, :] * scale).astype(o_ref.dtype)
        vs = (vaug.astype(F32) * w_src[h:h + 1, :]).astype(BF16)
        ct_ref[h] = decay[h:h + 1, :] * ct + jnp.dot(vs, km, preferred_element_type=F32)


def _split_dot(ones_mat, x, left):
    hi = x.astype(BF16)
    r1 = x - hi.astype(F32)
    mid = r1.astype(BF16)
    lo = (r1 - mid.astype(F32)).astype(BF16)
    out = None
    for piece in (lo, mid, hi):
        t = (jnp.dot(ones_mat, piece, preferred_element_type=F32) if left
             else jnp.dot(piece, ones_mat, preferred_element_type=F32))
        out = t if out is None else out + t
    return out


def _mlstm_kernel(qf, kf, vtf, gcf, grf, qb, kb, vtb, gcb, grb, bc_ref, br_ref, of_ref, ob_ref,
                  cf_ref, mf_ref, cb_ref, mb_ref, *, tiles_per_batch):
    @pl.when(pl.program_id(0) % tiles_per_batch == 0)
    def _():
        cf_ref[...] = jnp.zeros_like(cf_ref)
        cb_ref[...] = jnp.zeros_like(cb_ref)
        mf_ref[...] = jnp.zeros_like(mf_ref)
        mb_ref[...] = jnp.zeros_like(mb_ref)

    stats_f = _mlstm_gate_stats(gcf, grf, bc_ref, br_ref, mf_ref, backward=False)
    stats_b = _mlstm_gate_stats(gcb, grb, bc_ref, br_ref, mb_ref, backward=True)
    for h in range(HEADS):
        _mlstm_head(h, qf, kf, vtf, of_ref, cf_ref, stats_f, backward=False)
    for h in range(HEADS):
        _mlstm_head(h, qb, kb, vtb, ob_ref, cb_ref, stats_b, backward=True)


def _mlstm_call(q, k, vt, gc, gr, bias_c, bias_r, dims):
    tm, d, n = TOKEN_TILE, dims["d"], dims["n"]
    tpb, ct = dims["tpb"], dims["ct"]
    nq = q.shape[1]
    fwd = lambda i: (i, 0)
    bwd = lambda i: (_bwd_tile(i, tpb, ct), 0)
    fwd_t = lambda i: (0, i)
    bwd_t = lambda i: (0, _bwd_tile(i, tpb, ct))
    const = lambda i: (0, 0)
    state = [pltpu.VMEM((HEADS, HEAD_V + 2 * SUBLANES, LANES), F32), pltpu.VMEM((HEADS, LANES), F32)]
    return pl.pallas_call(
        functools.partial(_mlstm_kernel, tiles_per_batch=tpb),
        grid=(dims["tiles"],),
        in_specs=[
            pl.BlockSpec((tm, nq), fwd), pl.BlockSpec((tm, nq), fwd), pl.BlockSpec((d, tm), fwd_t),
            pl.BlockSpec((tm, LANES), fwd), pl.BlockSpec((4 * HEADS, tm), fwd_t),
            pl.BlockSpec((tm, nq), bwd), pl.BlockSpec((tm, nq), bwd), pl.BlockSpec((d, tm), bwd_t),
            pl.BlockSpec((tm, LANES), bwd), pl.BlockSpec((4 * HEADS, tm), bwd_t),
            pl.BlockSpec((1, LANES), const), pl.BlockSpec((4 * HEADS, 1), const),
        ],
        out_specs=[pl.BlockSpec((d, tm), fwd_t), pl.BlockSpec((d, tm), bwd_t)],
        out_shape=[jax.ShapeDtypeStruct((d, n), BF16), jax.ShapeDtypeStruct((d, n), BF16)],
        scratch_shapes=state + state,
        compiler_params=_params("arbitrary"),
        name="mlstm_scan",
    )(q, k, vt, gc, gr, q, k, vt, gc, gr, bias_c, bias_r)


def _mreadout_kernel(hf_ref, hb_ref, ot_ref, w_ref, x_ref, mod_ref, o_ref):
    d = x_ref.shape[-1]
    parts = []
    for h in range(HEADS):
        rows = slice(h * HEAD_V, (h + 1) * HEAD_V)
        hs = hf_ref[rows, :].astype(F32) + hb_ref[rows, :].astype(F32)
        r = lax.rsqrt(jnp.mean(hs * hs, axis=0, keepdims=True) + NORM_EPS)
        parts.append((hs * r * ot_ref[rows, :]).astype(BF16))
    t = jnp.concatenate(parts, axis=0)
    out = lax.dot_general(t, w_ref[...], _TN, preferred_element_type=F32)
    o_ref[...] = x_ref[...] + mod_ref[:, 2 * d:3 * d] * out


def _readout_kernel(hf_ref, hb_ref, gy_ref, w_ref, x_ref, mod_ref, o_ref):
    d = x_ref.shape[-1]
    t = _gelu_tanh(gy_ref[...].astype(F32)) * (hf_ref[...].astype(F32) + hb_ref[...].astype(F32))
    out = jnp.dot(t.astype(BF16), w_ref[...], preferred_element_type=F32)
    o_ref[...] = x_ref[...] + mod_ref[:, 2 * d:3 * d] * out


def _readout_call(kernel_fn, name, feature_major, hf, hb, aux, w_out, xs, mods, layer, dims):
    tm, d, n = TOKEN_TILE, dims["d"], dims["n"]
    row = lambda i: (i, 0)
    col = lambda i: (0, i)
    const = lambda i: (0, 0)
    mixer = pl.BlockSpec((d, tm), col) if feature_major else pl.BlockSpec((tm, d), row)
    return pl.pallas_call(
        kernel_fn,
        grid=(dims["tiles"],),
        in_specs=[mixer, mixer, mixer,
                  pl.BlockSpec((d, d), const), pl.BlockSpec((tm, d), row), _mod_spec(layer, dims)],
        out_specs=pl.BlockSpec((tm, d), row),
        out_shape=jax.ShapeDtypeStruct((n, d), F32),
        compiler_params=_params("arbitrary"),
        name=name,
    )(hf, hb, aux, w_out, xs, mods)


def _rproj_kernel(x_ref, g_ref, mod_ref, w_ref, cw_ref, cb_ref, gy_ref, xc_ref, ext_ref, cur_ref,
                  *, tiles, tiles_per_batch, ctx_tiles):
    tm, d = x_ref.shape
    i = pl.program_id(0)

    def in_segment_neighbours(t):
        j = t % tiles_per_batch
        first = jnp.logical_or(j == 0, j == ctx_tiles)
        last = jnp.logical_or(j == ctx_tiles - 1, j == tiles_per_batch - 1)
        return jnp.logical_not(first), jnp.logical_not(last)

    @pl.when(i == 0)
    def _():
        ext_ref[...] = jnp.zeros_like(ext_ref)
        cur_ref[...] = jnp.zeros_like(cur_ref)

    @pl.when(i < tiles)
    def _():
        h = _modulated(x_ref[...], g_ref[...], mod_ref[...], 0).astype(BF16)
        gy_ref[...] = jnp.dot(h, w_ref[:, 0:d], preferred_element_type=F32).astype(BF16)
        cur_ref[...] = jnp.dot(h, w_ref[:, d:2 * d], preferred_element_type=F32)

    @pl.when(i > 0)
    def _():
        _, next_ok = in_segment_neighbours(i - 1)
        head_ok = jnp.logical_and(next_ok, i < tiles)
        ext_ref[SUBLANES + tm:2 * SUBLANES + tm, :] = jnp.where(head_ok, cur_ref[0:SUBLANES, :], 0.0)
        xc = cb_ref[...] + ext_ref[pl.ds(SUBLANES - CONV_LEFT, tm), :] * cw_ref[0:1, :]
        for k in range(1, CONV_TAPS):
            xc = xc + ext_ref[pl.ds(SUBLANES - CONV_LEFT + k, tm), :] * cw_ref[k:k + 1, :]
        xc_ref[...] = xc

    prev_ok, _ = in_segment_neighbours(i)
    ext_ref[0:SUBLANES, :] = jnp.where(prev_ok, ext_ref[tm:tm + SUBLANES, :], 0.0)
    ext_ref[SUBLANES:SUBLANES + tm, :] = cur_ref[...]


def _rproj_call(xs, g, mods, layer, w_in, conv_w, conv_b, dims):
    tm, d, n, tiles = TOKEN_TILE, dims["d"], dims["n"], dims["tiles"]
    tpb, ct, b = dims["tpb"], dims["ct"], dims["b"]
    cur = lambda i: (jnp.minimum(i, tiles - 1), 0)
    lag = lambda i: (jnp.maximum(i - 1, 0), 0)
    const = lambda i: (0, 0)
    mod_spec = pl.BlockSpec((None, None, 1, 6 * d),
                            lambda i: (layer, _mod_row(jnp.minimum(i, tiles - 1), tpb, ct, b), 0, 0))
    return pl.pallas_call(
        functools.partial(_rproj_kernel, tiles=tiles, tiles_per_batch=tpb, ctx_tiles=ct),
        grid=(tiles + 1,),
        in_specs=[pl.BlockSpec((tm, d), cur), pl.BlockSpec((1, d), const), mod_spec,
                  pl.BlockSpec(w_in.shape, const), pl.BlockSpec(conv_w.shape, const),
                  pl.BlockSpec(conv_b.shape, const)],
        out_specs=[pl.BlockSpec((tm, d), cur), pl.BlockSpec((tm, d), lag)],
        out_shape=[jax.ShapeDtypeStruct((n, d), BF16), jax.ShapeDtypeStruct((n, d), F32)],
        scratch_shapes=[pltpu.VMEM((tm + 2 * SUBLANES, d), F32), pltpu.VMEM((tm, d), F32)],
        compiler_params=_params("arbitrary"),
        name="rglru_proj",
    )(xs, g, mods, w_in, conv_w, conv_b)


def _lru_direction(xc_ref, wg_ref, bg_ref, lam_ref, a_ref, b_ref, *, direction):
    tm, d = xc_ref.shape
    bw = d // LRU_BLOCKS
    z = -lam_ref[direction]
    softplus = jnp.maximum(z, 0.0) + jnp.log(1.0 + jnp.exp(-jnp.abs(z)))
    k = (-0.5 * LRU_C) * softplus
    for nb in range(LRU_BLOCKS):
        sl = slice(nb * bw, (nb + 1) * bw)
        xb = xc_ref[:, sl]
        gg = jnp.dot(xb.astype(BF16), wg_ref[direction, nb], preferred_element_type=F32)
        t = jnp.tanh(gg + bg_ref[direction, :, nb * 2 * bw:(nb + 1) * 2 * bw])
        log_a = k[:, sl] * t[:, :bw] + k[:, sl]
        th = jnp.tanh(log_a)
        a_ref[pl.ds(nb, tm, stride=LRU_BLOCKS), :] = jnp.exp(log_a)
        b_ref[pl.ds(nb, tm, stride=LRU_BLOCKS), :] = (jnp.sqrt(-0.5 * th / (1.0 - th))
                                                      * ((t[:, bw:] + 1.0) * xb))


def _lru_kernel(xc_f, xc_b, wg_ref, bg_ref, lam_ref, of_ref, ob_ref, af_ref, bf_ref, ab_ref, bb_ref,
                hf_ref, hb_ref, h_ref, *, tiles_per_batch):
    tm, d = xc_f.shape
    bw = d // LRU_BLOCKS

    @pl.when(pl.program_id(0) % tiles_per_batch == 0)
    def _():
        h_ref[...] = jnp.zeros_like(h_ref)

    def scan(a_ref, b_ref, o_ref, state, order):
        h = h_ref[state]
        for t in order:
            rows = slice(t * LRU_BLOCKS, (t + 1) * LRU_BLOCKS)
            h = a_ref[rows, :] * h + b_ref[rows, :]
            o_ref[rows, :] = h
        h_ref[state] = h

    _lru_direction(xc_f, wg_ref, bg_ref, lam_ref, af_ref, bf_ref, direction=0)
    scan(af_ref, bf_ref, hf_ref, 0, range(tm))
    _lru_direction(xc_b, wg_ref, bg_ref, lam_ref, ab_ref, bb_ref, direction=1)
    scan(ab_ref, bb_ref, hb_ref, 1, range(tm - 1, -1, -1))
    for nb in range(LRU_BLOCKS):
        sl = slice(nb * bw, (nb + 1) * bw)
        of_ref[:, sl] = hf_ref[pl.ds(nb, tm, stride=LRU_BLOCKS), :].astype(BF16)
        ob_ref[:, sl] = hb_ref[pl.ds(nb, tm, stride=LRU_BLOCKS), :].astype(BF16)


def _lru_call(xc, w_gate, b_gate, lam, dims):
    tm, d, n = TOKEN_TILE, dims["d"], dims["n"]
    tpb, ct = dims["tpb"], dims["ct"]
    fwd = lambda i: (i, 0)
    bwd = lambda i: (_bwd_tile(i, tpb, ct), 0)
    c3 = lambda i: (0, 0, 0)
    c4 = lambda i: (0, 0, 0, 0)
    time_major = pltpu.VMEM((tm * LRU_BLOCKS, d // LRU_BLOCKS), F32)
    return pl.pallas_call(
        functools.partial(_lru_kernel, tiles_per_batch=tpb),
        grid=(dims["tiles"],),
        in_specs=[
            pl.BlockSpec((tm, d), fwd), pl.BlockSpec((tm, d), bwd),
            pl.BlockSpec(w_gate.shape, c4), pl.BlockSpec(b_gate.shape, c3), pl.BlockSpec(lam.shape, c3),
        ],
        out_specs=[pl.BlockSpec((tm, d), fwd), pl.BlockSpec((tm, d), bwd)],
        out_shape=[jax.ShapeDtypeStruct((n, d), BF16), jax.ShapeDtypeStruct((n, d), BF16)],
        scratch_shapes=[time_major] * 6 + [pltpu.VMEM((2, LRU_BLOCKS, d // LRU_BLOCKS), F32)],
        compiler_params=_params("arbitrary"),
        name="rglru_scan",
    )(xc, xc, w_gate, b_gate, lam)


def _first_argmax(vals):
    best = vals[0]
    idx = jnp.zeros_like(best)
    for k in range(1, len(vals)):
        better = vals[k] > best
        idx = jnp.where(better, float(k), idx)
        best = jnp.where(better, vals[k], best)
    return idx, best


def _pick(idx, vals):
    out = vals[0]
    for k in range(1, len(vals)):
        out = jnp.where(idx == float(k), vals[k], out)
    return out


def _route_tile(x, g_ref, mod_ref, wr_ref, br_ref, hx_ref, route_ref, cnt_ref, carry_ref):
    tm = x.shape[0]
    h = _modulated(x, g_ref[...], mod_ref[...], 3)
    _store_token_tiles(hx_ref, h)
    hi = h.astype(BF16)
    lo = (h - hi.astype(F32)).astype(BF16)
    logits_tok = (jnp.dot(lo, wr_ref[0], preferred_element_type=F32)
                  + jnp.dot(hi, wr_ref[1], preferred_element_type=F32)
                  + jnp.dot(hi, wr_ref[0], preferred_element_type=F32))
    logits = logits_tok.T[0:N_EXPERTS, :]
    e = jnp.exp(logits - jnp.max(logits, axis=0, keepdims=True))
    scores = e / jnp.sum(e, axis=0, keepdims=True)
    sel = scores + br_ref[...]
    sel_rows = [sel[k:k + 1, :] for k in range(N_EXPERTS)]
    group_scores = []
    for gi in range(N_GROUPS):
        v = sel_rows[gi * GROUP_SIZE:(gi + 1) * GROUP_SIZE]
        best = v[0] + v[1]
        for a, b in PAIRS[1:]:
            best = jnp.maximum(best, v[a] + v[b])
        group_scores.append(best)
    grp, _ = _first_argmax(group_scores)
    in_sel = [_pick(grp, [sel_rows[gi * GROUP_SIZE + k] for gi in range(N_GROUPS)])
              for k in range(GROUP_SIZE)]
    i1, _ = _first_argmax(in_sel)
    rest = [jnp.where(i1 == float(k), -jnp.inf, in_sel[k]) for k in range(GROUP_SIZE)]
    i2, _ = _first_argmax(rest)
    lo = jnp.minimum(i1, i2)
    hi = jnp.maximum(i1, i2)
    pair = jnp.zeros_like(lo)
    for k, (slot_a, slot_b) in enumerate(PAIR_SLOTS):
        here = jnp.logical_and(lo == float(min(slot_a, slot_b)), hi == float(max(slot_a, slot_b)))
        pair = jnp.where(here, float(k), pair)
    cls = grp * float(len(PAIR_SLOTS)) + pair
    crow = lax.broadcasted_iota(jnp.int32, (CLASS_ROWS, tm), 0).astype(F32)
    onehot = jnp.where(crow == cls, 1.0, 0.0)
    rows = lax.broadcasted_iota(jnp.int32, (tm, tm), 0)
    cols = lax.broadcasted_iota(jnp.int32, (tm, tm), 1)
    upper = jnp.where(rows <= cols, 1.0, 0.0).astype(BF16)
    cum = jnp.dot(onehot.astype(BF16), upper, preferred_element_type=F32)
    carry = carry_ref[:, 0:1]
    rank = jnp.sum(onehot * (cum - 1.0 + carry), axis=0, keepdims=True)
    new_carry = carry + jnp.sum(onehot, axis=1, keepdims=True)
    carry_ref[...] = jnp.broadcast_to(new_carry, carry_ref.shape)
    cnt_ref[...] = jnp.broadcast_to(new_carry, cnt_ref.shape)
    zero = jnp.zeros_like(cls)
    route_ref[...] = jnp.concatenate([cls, rank, zero, zero, zero, zero, zero, zero], axis=0)


ROUTER_TILES = 2


def _router_kernel(x_ref, g_ref, *refs):
    mod_refs = refs[:ROUTER_TILES]
    wr_ref, br_ref, hx_ref, route_ref, cnt_ref, carry_ref = refs[ROUTER_TILES:]
    tm = x_ref.shape[0] // ROUTER_TILES
    rows = hx_ref.shape[0] // ROUTER_TILES

    @pl.when(pl.program_id(0) == 0)
    def _():
        carry_ref[...] = jnp.zeros_like(carry_ref)

    for k, mod_ref in enumerate(mod_refs):
        _route_tile(x_ref[k * tm:(k + 1) * tm, :], g_ref, mod_ref, wr_ref, br_ref,
                    hx_ref.at[pl.ds(k * rows, rows), :], route_ref.at[:, pl.ds(k * tm, tm)],
                    cnt_ref, carry_ref)


def _router_weights(w_router):
    w = jnp.pad(w_router.astype(F32), ((0, 0), (0, LANES - w_router.shape[1])))
    hi = w.astype(BF16)
    lo = (w - hi.astype(F32)).astype(BF16)
    return jnp.stack([hi, lo])


def _router_call(xs, g, mods, layer, w_router_t, b_router_col, dims):
    d, n = dims["d"], dims["n"]
    tpb, ct, b = dims["tpb"], dims["ct"], dims["b"]
    assert dims["tiles"] % ROUTER_TILES == 0
    tm = TOKEN_TILE * ROUTER_TILES
    row = lambda i: (i, 0)
    const = lambda i: (0, 0)
    per = d // LANES
    mod_specs = [pl.BlockSpec((None, None, 1, 6 * d),
                              lambda i, k=k: (layer, _mod_row(i * ROUTER_TILES + k, tpb, ct, b), 0, 0))
                 for k in range(ROUTER_TILES)]
    return pl.pallas_call(
        _router_kernel,
        grid=(dims["tiles"] // ROUTER_TILES,),
        in_specs=[pl.BlockSpec((tm, d), row), pl.BlockSpec((1, d), const), *mod_specs,
                  pl.BlockSpec(w_router_t.shape, lambda i: (0, 0, 0)),
                  pl.BlockSpec(b_router_col.shape, const)],
        out_specs=[pl.BlockSpec((tm * per, LANES), row),
                   pl.BlockSpec((SUBLANES, tm), lambda i: (0, i)),
                   pl.BlockSpec((CLASS_ROWS, LANES), const)],
        out_shape=[jax.ShapeDtypeStruct((n * per, LANES), F32),
                   jax.ShapeDtypeStruct((SUBLANES, n), F32),
                   jax.ShapeDtypeStruct((CLASS_ROWS, LANES), F32)],
        scratch_shapes=[pltpu.VMEM((CLASS_ROWS, LANES), F32)],
        compiler_params=_params("arbitrary"),
        name="router",
    )(xs, g, *([mods] * ROUTER_TILES), w_router_t, b_router_col)


ROW_DMA_UNROLL = 8


def _dispatch_kernel(pos_ref, hx_ref, init_ref, o_ref, stage_ref, sems, *, tokens):
    del init_ref
    per = hx_ref.shape[0] // tokens
    i = pl.program_id(0)
    slot = i % 2
    base = i * tokens
    stage_ref[slot] = hx_ref[...]

    def issue(r, carry):
        src = pl.multiple_of(r * per, per)
        dst = pl.multiple_of(pos_ref[base + r], per)
        pltpu.make_async_copy(stage_ref.at[slot, pl.ds(src, per)], o_ref.at[pl.ds(dst, per)],
                              sems.at[slot]).start()
        return carry

    lax.fori_loop(0, tokens, issue, 0, unroll=ROW_DMA_UNROLL)

    def wait_tile(s):
        pltpu.make_async_copy(stage_ref.at[s], o_ref.at[pl.ds(0, tokens * per)], sems.at[s]).wait()

    @pl.when(i > 0)
    def _():
        wait_tile(1 - slot)

    @pl.when(i == pl.num_programs(0) - 1)
    def _():
        wait_tile(slot)


def _dispatch_call(pos_rows, hx, init, dims):
    tm = TOKEN_TILE
    per = dims["d"] // LANES
    return pl.pallas_call(
        functools.partial(_dispatch_kernel, tokens=tm),
        grid_spec=pltpu.PrefetchScalarGridSpec(
            num_scalar_prefetch=1,
            grid=(dims["tiles"],),
            in_specs=[pl.BlockSpec((tm * per, LANES), lambda i, p: (i, 0)),
                      pl.BlockSpec(memory_space=pl.ANY)],
            out_specs=pl.BlockSpec(memory_space=pl.ANY),
            scratch_shapes=[pltpu.VMEM((2, tm * per, LANES), hx.dtype), pltpu.SemaphoreType.DMA((2,))],
        ),
        out_shape=jax.ShapeDtypeStruct(init.shape, hx.dtype),
        input_output_aliases={2: 0},
        compiler_params=_params("arbitrary"),
        name="dispatch",
    )(pos_rows, hx, init)


def _expert_kernel(ea_ref, eb_ref, ok_ref, x_ref, wr_ref, w1a, w3a, w2a, w1b, w3b, w2b, o_ref):
    d = w1a.shape[0]
    te = x_ref.shape[0] // (d // LANES)
    t = pl.program_id(0)

    @pl.when(ok_ref[t] != 0)
    def _():
        h = _load_token_tiles(x_ref, te, d)
        x = h.astype(BF16)
        dw = wr_ref[pl.ds(ea_ref[t], 1), :] - wr_ref[pl.ds(eb_ref[t], 1), :]
        gate_a = jax.nn.sigmoid(jnp.sum(h * dw, axis=1, keepdims=True))
        gate_b = 1.0 - gate_a

        def expert(w1, w3, w2):
            a = jnp.dot(x, w1[...].astype(BF16), preferred_element_type=F32)
            b = jnp.dot(x, w3[...].astype(BF16), preferred_element_type=F32)
            u = (a * jax.nn.sigmoid(a)) * b
            return jnp.dot(u.astype(BF16), w2[...].astype(BF16), preferred_element_type=F32)

        _store_token_tiles(o_ref, gate_a * expert(w1a, w3a, w2a) + gate_b * expert(w1b, w3b, w2b))

    @pl.when(ok_ref[t] == 0)
    def _():
        o_ref[...] = jnp.zeros_like(o_ref)


def _expert_call(tile_a, tile_b, tile_ok, xsorted, w_router_t, w1, w3, w2, layer, d):
    te = TOKEN_TILE
    per = d // LANES
    n_tiles = xsorted.shape[0] // (te * per)
    de = w1.shape[-1]
    sel_a = lambda t, ea, eb, ok: (layer, ea[t], 0, 0)
    sel_b = lambda t, ea, eb, ok: (layer, eb[t], 0, 0)
    used = lambda t, ea, eb, ok: (t, 0)
    return pl.pallas_call(
        _expert_kernel,
        grid_spec=pltpu.PrefetchScalarGridSpec(
            num_scalar_prefetch=3,
            grid=(n_tiles,),
            in_specs=[
                pl.BlockSpec((te * per, LANES), used),
                pl.BlockSpec(w_router_t.shape, lambda t, ea, eb, ok: (0, 0)),
                pl.BlockSpec((None, None, d, de), sel_a), pl.BlockSpec((None, None, d, de), sel_a),
                pl.BlockSpec((None, None, de, d), sel_a),
                pl.BlockSpec((None, None, d, de), sel_b), pl.BlockSpec((None, None, d, de), sel_b),
                pl.BlockSpec((None, None, de, d), sel_b),
            ],
            out_specs=pl.BlockSpec((te * per, LANES), used),
        ),
        out_shape=jax.ShapeDtypeStruct((n_tiles * te * per, LANES), F32),
        compiler_params=_params("arbitrary"),
        name="experts",
    )(tile_a, tile_b, tile_ok, xsorted, w_router_t, w1, w3, w2, w1, w3, w2)


def _combine_kernel(pos_ref, y_ref, x_ref, mod_ref, gf_ref, o_ref, buf_ref, sems,
                    *, tile_of, final_norm):
    tm, d = x_ref.shape
    per = d // LANES
    i = pl.program_id(0)
    slot = i % 2

    def gather(step, s):
        base = tile_of(step) * tm

        def issue(r, carry):
            src = pl.multiple_of(pos_ref[base + r], per)
            dst = pl.multiple_of(r * per, per)
            pltpu.make_async_copy(y_ref.at[pl.ds(src, per)], buf_ref.at[s, pl.ds(dst, per)],
                                  sems.at[s]).start()
            return carry

        lax.fori_loop(0, tm, issue, 0, unroll=ROW_DMA_UNROLL)

    @pl.when(i == 0)
    def _():
        gather(0, 0)

    @pl.when(i + 1 < pl.num_programs(0))
    def _():
        gather(i + 1, 1 - slot)

    pltpu.make_async_copy(y_ref.at[pl.ds(0, tm * per)], buf_ref.at[slot], sems.at[slot]).wait()
    out = x_ref[...] + mod_ref[:, 5 * d:6 * d] * _load_token_tiles(buf_ref.at[slot], tm, d)
    if final_norm:
        ms = jnp.mean(out * out, axis=-1, keepdims=True)
        out = out * lax.rsqrt(ms + NORM_EPS) * gf_ref[...]
    o_ref[...] = out


def _combine_call(pos, ysorted, xs, mods, layer, g_final, dims, final_norm):
    tm, d = TOKEN_TILE, dims["d"]
    tpb, ct, b = dims["tpb"], dims["ct"], dims["b"]
    if final_norm:
        lt = tpb - ct
        tile_of = lambda i: (i // lt) * tpb + ct + i % lt
        n_tiles = b * lt
    else:
        tile_of = lambda i: i
        n_tiles = dims["tiles"]
    mod_spec = pl.BlockSpec((None, None, 1, 6 * d),
                            lambda i, p: (layer, _mod_row(tile_of(i), tpb, ct, b), 0, 0))
    return pl.pallas_call(
        functools.partial(_combine_kernel, tile_of=tile_of, final_norm=final_norm),
        grid_spec=pltpu.PrefetchScalarGridSpec(
            num_scalar_prefetch=1,
            grid=(n_tiles,),
            in_specs=[
                pl.BlockSpec(memory_space=pl.ANY),
                pl.BlockSpec((tm, d), lambda i, p: (tile_of(i), 0)),
                mod_spec,
                pl.BlockSpec((1, d), lambda i, p: (0, 0)),
            ],
            out_specs=pl.BlockSpec((tm, d), lambda i, p: (i, 0)),
            scratch_shapes=[pltpu.VMEM((2, tm * (d // LANES), LANES), F32), pltpu.SemaphoreType.DMA((2,))],
        ),
        out_shape=jax.ShapeDtypeStruct((n_tiles * tm, d), F32),
        compiler_params=_params("arbitrary"),
        name="combine",
    )(pos, ysorted, xs, mods, g_final)


_CLASS_A = np.array([g * GROUP_SIZE + a for g in range(N_GROUPS) for a, _ in PAIR_SLOTS], np.int32)
_CLASS_B = np.array([g * GROUP_SIZE + b for g in range(N_GROUPS) for _, b in PAIR_SLOTS], np.int32)


def _routing_plan(route, counts, n_tiles_padded):
    te = TOKEN_TILE
    cls = route[0].astype(jnp.int32)
    rank = route[1].astype(jnp.int32)
    cnt = counts[:N_CLASSES, 0].astype(jnp.int32)
    tiles = (cnt + te - 1) // te
    tile_end = jnp.cumsum(tiles)
    tile_start = tile_end - tiles
    onehot = (cls[:, None] == jnp.arange(N_CLASSES, dtype=jnp.int32)[None, :]).astype(jnp.int32)
    pos = jnp.sum(onehot * (tile_start * te)[None, :], axis=1) + rank
    t = jnp.arange(n_tiles_padded, dtype=jnp.int32)
    ok = (t < tile_end[-1]).astype(jnp.int32)
    tcls = jnp.sum((tile_end[None, :] <= jnp.minimum(t, tile_end[-1] - 1)[:, None]).astype(jnp.int32), axis=1)
    tile_a = jnp.take(jnp.asarray(_CLASS_A), tcls)
    tile_b = jnp.take(jnp.asarray(_CLASS_B), tcls)
    return pos, tile_a, tile_b, ok


def _moe(xs, hx, route, counts, sorted_buf, mods, layer, w_router_t, w1, w3, w2, g_final, dims, final_norm):
    per = dims["d"] // LANES
    n_tiles_padded = sorted_buf.shape[0] // (TOKEN_TILE * per)
    pos, tile_a, tile_b, tile_ok = _routing_plan(route, counts, n_tiles_padded)
    xsorted = _dispatch_call(pos * per, hx, sorted_buf, dims)
    ysorted = _expert_call(tile_a, tile_b, tile_ok, xsorted, w_router_t, w1, w3, w2, layer, dims["d"])
    return _combine_call(pos * per, ysorted, xs, mods, layer, g_final, dims, final_norm), xsorted


def _pos_table(n_tokens, dim):
    rows = n_tokens // GRID_WIDTH
    r, col = jnp.meshgrid(jnp.arange(rows, dtype=F32), jnp.arange(GRID_WIDTH, dtype=F32), indexing="ij")
    quarter = dim // 4
    freqs = jnp.exp(-math.log(POS_BASE) * jnp.arange(quarter, dtype=F32) / quarter)

    def enc(p):
        ang = p.reshape(-1, 1) * freqs
        return jnp.concatenate([jnp.sin(ang), jnp.cos(ang)], axis=-1)

    return jnp.concatenate([enc(r), enc(col)], axis=-1)


def _mlstm_weights(w_in, b_gate):
    nq = HEADS * HEAD_QK
    nv = HEADS * HEAD_V
    wq = w_in[:, 0:nq] * (HEAD_QK ** -0.5)
    wk = w_in[:, nq:2 * nq]
    wv = w_in[:, 2 * nq:2 * nq + nv]
    wo = w_in[:, 2 * nq + nv:2 * nq + 2 * nv]
    wg = w_in[:, 2 * nq + 2 * nv:]
    wg_pad = jnp.pad(wg, ((0, 0), (0, LANES - wg.shape[1])))
    w_tok = jnp.concatenate([wq, wk, wg_pad], axis=1).astype(BF16)
    w_feat = jnp.concatenate([wv, wo, wg], axis=1).T.astype(BF16)
    bias = b_gate.reshape(-1).astype(F32)
    bias_c = jnp.pad(bias, (0, LANES - bias.shape[0])).reshape(1, LANES)
    bias_r = bias.reshape(-1, 1)
    return w_tok, w_feat, bias_c, bias_r


def kernel(x, c, ctx, c_ctx, w_ada, b_ada, g_mix, g_ffn, g_final, m_w_in, m_b_gate, m_g_head, m_w_out, r_w_in, r_conv_w, r_conv_b, r_w_gate, r_b_gate, r_lam, r_w_out, w_router, b_router, e_w1, e_w3, e_w2):
    batch, t_len, d = x.shape
    ctx_len = ctx.shape[1]
    depth = w_ada.shape[0]
    tm = TOKEN_TILE
    assert t_len % tm == 0 and ctx_len % tm == 0 and batch + 1 <= ADA_ROWS
    assert d == HEADS * HEAD_V and t_len % GRID_WIDTH == 0
    s_len = ctx_len + t_len
    dims = dict(b=batch, d=d, n=batch * s_len, tpb=s_len // tm, ct=ctx_len // tm,
                tiles=batch * s_len // tm)

    cvec = jnp.concatenate([c, c_ctx[None, :], jnp.zeros((ADA_ROWS - batch - 1, d), F32)], axis=0)
    mods = _ada_call(cvec, w_ada, b_ada).reshape(depth, ADA_ROWS, 1, 6 * d)
    xs = (ctx.reshape(batch * ctx_len, d), x.reshape(batch * t_len, d), _pos_table(t_len, d))

    w_router_t = w_router.T.astype(F32)
    w_router_pieces = _router_weights(w_router)
    b_router_col = b_router.reshape(-1, 1).astype(F32)
    g_final2 = g_final.reshape(1, d)
    sorted_buf = jnp.zeros(((dims["tiles"] + N_CLASSES) * tm * (d // LANES), LANES), F32)
    out = None
    for i in range(depth):
        j = i // 2
        g_mix_i = g_mix[i].reshape(1, d)
        g_ffn_i = g_ffn[i].reshape(1, d)
        if i % 2 == 0:
            w_tok, w_feat, bias_c, bias_r = _mlstm_weights(m_w_in[j], m_b_gate[j])
            proj = _mproj_call(xs, g_mix_i, mods, i, w_tok, w_feat, dims)
            if isinstance(xs, tuple):
                xs, proj = proj[0], proj[1:]
            q, k, vt, ot, gc, gr = proj
            hf, hb = _mlstm_call(q, k, vt, gc, gr, bias_c, bias_r, dims)
            w_out = (m_g_head[j][:, None] * m_w_out[j]).astype(BF16)
            xs = _readout_call(_mreadout_kernel, "mlstm_readout", True, hf, hb, ot, w_out, xs, mods, i, dims)
        else:
            gy, xc = _rproj_call(xs, g_mix_i, mods, i, r_w_in[j].astype(BF16), r_conv_w[j],
                                 r_conv_b[j].reshape(1, d), dims)
            hf, hb = _lru_call(xc, (0.5 * r_w_gate[j]).astype(BF16), 0.5 * r_b_gate[j].reshape(2, 1, -1),
                               r_lam[j].reshape(2, 1, d), dims)
            xs = _readout_call(_readout_kernel, "rglru_readout", False, hf, hb, gy,
                               r_w_out[j].astype(BF16), xs, mods, i, dims)
        hx, route, counts = _router_call(xs, g_ffn_i, mods, i, w_router_pieces, b_router_col, dims)
        last = i == depth - 1
        res, sorted_buf = _moe(xs, hx, route, counts, sorted_buf, mods, i, w_router_t,
                               e_w1, e_w3, e_w2, g_final2, dims, last)
        if last:
            out = res
        else:
            xs = res
    return out.reshape(batch, t_len, d)
```

```python
import functools
import math

import jax
import jax.numpy as jnp
import numpy as np
from jax import lax
from jax.experimental import pallas as pl
from jax.experimental.pallas import tpu as pltpu

F32 = jnp.float32
BF16 = jnp.bfloat16
HIGHEST = lax.Precision.HIGHEST

TOKEN_TILE = 256
LANES = 128
SUBLANES = 8
VMEM_LIMIT_BYTES = 56 * 1024 * 1024
NORM_EPS = 1e-6
GRID_WIDTH = 64
POS_BASE = 10000.0
HEADS = 8
HEAD_QK = 64
HEAD_V = 128
LRU_BLOCKS = 8
LRU_C = 8.0
CONV_TAPS = 4
CONV_LEFT = CONV_TAPS // 2
N_EXPERTS = 16
N_GROUPS = 4
GROUP_SIZE = N_EXPERTS // N_GROUPS
PAIRS = [(a, b) for a in range(GROUP_SIZE) for b in range(a + 1, GROUP_SIZE)]
PAIR_SLOTS = [(0, 1), (0, 2), (0, 3), (1, 3), (1, 2), (3, 2)]
N_CLASSES = N_GROUPS * len(PAIRS)
CLASS_ROWS = 32
ROUTE_COLS = LANES
ADA_ROWS = 16


def _params(*sem):
    return pltpu.CompilerParams(dimension_semantics=sem, vmem_limit_bytes=VMEM_LIMIT_BYTES)


def _mod_row(i, tiles_per_batch, ctx_tiles, batch):
    return jnp.where(i % tiles_per_batch < ctx_tiles, batch, i // tiles_per_batch)


def _bwd_tile(i, tiles_per_batch, ctx_tiles):
    b = i // tiles_per_batch
    j = i % tiles_per_batch
    jb = jnp.where(j < ctx_tiles, ctx_tiles - 1 - j, tiles_per_batch - 1 - (j - ctx_tiles))
    return b * tiles_per_batch + jb


def _ada_kernel(c_ref, w_ref, b_ref, o_ref):
    c = c_ref[...]
    s = c * jax.nn.sigmoid(c)
    o_ref[...] = jnp.dot(s, w_ref[...], preferred_element_type=F32, precision=HIGHEST) + b_ref[...]


def _ada_call(cvec, w_ada, b_ada):
    depth, d, n = w_ada.shape
    tn = n // 4
    return pl.pallas_call(
        _ada_kernel,
        grid=(depth, n // tn),
        in_specs=[
            pl.BlockSpec((ADA_ROWS, d), lambda l, j: (0, 0)),
            pl.BlockSpec((None, d, tn), lambda l, j: (l, 0, j)),
            pl.BlockSpec((None, 1, tn), lambda l, j: (l, 0, j)),
        ],
        out_specs=pl.BlockSpec((None, ADA_ROWS, tn), lambda l, j: (l, 0, j)),
        out_shape=jax.ShapeDtypeStruct((depth, ADA_ROWS, n), F32),
        compiler_params=_params("arbitrary", "arbitrary"),
        name="ada",
    )(cvec, w_ada, b_ada.reshape(depth, 1, n))


def _modulated(x, g, mod, shift_idx):
    d = x.shape[-1]
    ms = jnp.mean(x * x, axis=-1, keepdims=True)
    xn = x * lax.rsqrt(ms + NORM_EPS) * g
    shift = mod[:, shift_idx * d:(shift_idx + 1) * d]
    scale = mod[:, (shift_idx + 1) * d:(shift_idx + 2) * d]
    return xn * (1.0 + scale) + shift


def _mod_spec(layer, dims):
    tpb, ct, b = dims["tpb"], dims["ct"], dims["b"]
    return pl.BlockSpec((None, None, 1, 6 * dims["d"]),
                        lambda i, *_: (layer, _mod_row(i, tpb, ct, b), 0, 0))


def _store_token_tiles(ref, x):
    tokens, d = x.shape
    per = d // LANES
    for s in range(per):
        ref[pl.ds(s, tokens, stride=per), :] = x[:, s * LANES:(s + 1) * LANES]


def _load_token_tiles(ref, tokens, d):
    per = d // LANES
    return jnp.concatenate([ref[pl.ds(s, tokens, stride=per), :] for s in range(per)], axis=1)


def _gelu_tanh(y):
    return 0.5 * y * (1.0 + jnp.tanh(math.sqrt(2.0 / math.pi) * (y + 0.044715 * (y * y * y))))


SRC_BLOCK = 128
_NT = (((1,), (1,)), ((), ()))
_TN = (((0,), (0,)), ((), ()))


def _mproj_kernel(*refs, assemble, tiles_per_batch, ctx_tiles):
    if assemble:
        ctx_ref, lat_ref, pos_ref, g_ref, mod_ref, w_ref, wt_ref, xs_ref = refs[:8]
        is_ctx = pl.program_id(0) % tiles_per_batch < ctx_tiles
        x = jnp.where(is_ctx, ctx_ref[...], lat_ref[...] + pos_ref[...])
        xs_ref[...] = x
    else:
        x_ref, g_ref, mod_ref, w_ref, wt_ref = refs[:5]
        x = x_ref[...]
    q_ref, k_ref, vt_ref, ot_ref, gc_ref, gr_ref = refs[-6:]
    d = x.shape[-1]
    nq = q_ref.shape[-1]
    h = _modulated(x, g_ref[...], mod_ref[...], 0).astype(BF16)
    q_ref[...] = jnp.dot(h, w_ref[:, 0:nq], preferred_element_type=F32).astype(BF16)
    k_ref[...] = jnp.dot(h, w_ref[:, nq:2 * nq], preferred_element_type=F32).astype(BF16)
    gc_ref[...] = jnp.dot(h, w_ref[:, 2 * nq:2 * nq + LANES], preferred_element_type=F32)
    vt_ref[...] = lax.dot_general(wt_ref[0:d, :], h, _NT, preferred_element_type=F32).astype(BF16)
    ot = lax.dot_general(wt_ref[d:2 * d, :], h, _NT, preferred_element_type=F32)
    ot_ref[...] = jax.nn.sigmoid(ot).astype(BF16)
    gr_ref[...] = lax.dot_general(wt_ref[2 * d:2 * d + 4 * HEADS, :], h, _NT, preferred_element_type=F32)


def _mproj_call(stream, g, mods, layer, w_tok, w_feat, dims):
    tm, d, n = TOKEN_TILE, dims["d"], dims["n"]
    tpb, ct = dims["tpb"], dims["ct"]
    lt = tpb - ct
    nq = HEADS * HEAD_QK
    row = lambda i: (i, 0)
    col = lambda i: (0, i)
    const = lambda i: (0, 0)
    assemble = isinstance(stream, tuple)
    if assemble:
        stream_specs = [
            pl.BlockSpec((tm, d), lambda i: ((i // tpb) * ct + jnp.minimum(i % tpb, ct - 1), 0)),
            pl.BlockSpec((tm, d), lambda i: ((i // tpb) * lt + jnp.maximum(i % tpb - ct, 0), 0)),
            pl.BlockSpec((tm, d), lambda i: (jnp.maximum(i % tpb - ct, 0), 0)),
        ]
        stream_out = [pl.BlockSpec((tm, d), row)], [jax.ShapeDtypeStruct((n, d), F32)]
    else:
        stream, stream_specs, stream_out = (stream,), [pl.BlockSpec((tm, d), row)], ([], [])
    return pl.pallas_call(
        functools.partial(_mproj_kernel, assemble=assemble, tiles_per_batch=tpb, ctx_tiles=ct),
        grid=(dims["tiles"],),
        in_specs=stream_specs + [
            pl.BlockSpec((1, d), const),
            _mod_spec(layer, dims),
            pl.BlockSpec(w_tok.shape, const),
            pl.BlockSpec(w_feat.shape, const),
        ],
        out_specs=stream_out[0] + [
            pl.BlockSpec((tm, nq), row),
            pl.BlockSpec((tm, nq), row),
            pl.BlockSpec((d, tm), col),
            pl.BlockSpec((d, tm), col),
            pl.BlockSpec((tm, LANES), row),
            pl.BlockSpec((4 * HEADS, tm), col),
        ],
        out_shape=stream_out[1] + [
            jax.ShapeDtypeStruct((n, nq), BF16),
            jax.ShapeDtypeStruct((n, nq), BF16),
            jax.ShapeDtypeStruct((d, n), BF16),
            jax.ShapeDtypeStruct((d, n), BF16),
            jax.ShapeDtypeStruct((n, LANES), F32),
            jax.ShapeDtypeStruct((4 * HEADS, n), F32),
        ],
        compiler_params=_params("arbitrary"),
        name="mlstm_proj",
    )(*stream, g, mods, w_tok, w_feat)


def _mlstm_gate_stats(gc_ref, gr_ref, bc_ref, br_ref, m_ref, *, backward):
    tm = gc_ref.shape[0]
    gate_i = 2 * HEADS if backward else 0
    gate_f = gate_i + HEADS
    last = 0 if backward else tm - 1
    src = lax.broadcasted_iota(jnp.int32, (tm, tm), 0)
    tgt = lax.broadcasted_iota(jnp.int32, (tm, tm), 1)
    visible = (src >= tgt) if backward else (src <= tgt)
    neg_mask = jnp.where(visible, 0.0, -jnp.inf)
    xc = gc_ref[...] + bc_ref[...]
    xr = gr_ref[...] + br_ref[...]
    before = (src <= tgt) if backward else (src >= tgt)
    cum_c = _split_dot(jnp.where(before, 1.0, 0.0).astype(BF16), jax.nn.log_sigmoid(xc), left=True)
    cum_r = _split_dot(jnp.where(visible, 1.0, 0.0).astype(BF16), jax.nn.log_sigmoid(xr), left=False)
    b_row = cum_r[gate_f:gate_f + HEADS, :]
    c_row = xr[gate_i:gate_i + HEADS, :] - b_row
    c_col = xc - pltpu.roll(cum_c, LANES - HEADS, axis=1)
    run_col = c_col
    shift = 1
    while shift < tm:
        fill = jnp.full((shift, LANES), -jnp.inf, F32)
        if backward:
            moved = jnp.concatenate([run_col[shift:, :], fill], axis=0)
        else:
            moved = jnp.concatenate([fill, run_col[:tm - shift, :]], axis=0)
        run_col = jnp.maximum(run_col, moved)
        shift *= 2
    run = run_col.T[gate_i:gate_i + HEADS, :]
    m = m_ref[:, 0:1]
    log_inter = b_row + m
    m_pos = jnp.maximum(log_inter, b_row + run)
    shift_row = b_row - m_pos
    w_inter = jnp.exp(log_inter - m_pos)
    floor = jnp.exp(-m_pos)
    b_last = b_row[:, last:last + 1]
    m_new = m_pos[:, last:last + 1]
    w_src = jnp.exp(b_last + c_row - m_new)
    decay = jnp.exp(b_last + m - m_new)
    m_ref[...] = jnp.broadcast_to(m_new, m_ref.shape)
    return neg_mask, c_col, shift_row, w_inter, floor, w_src, decay


def _mlstm_head(h, q_ref, k_ref, vt_ref, o_ref, ct_ref, stats, *, backward):
    neg_mask, c_col, shift_row, w_inter, floor, w_src, decay = stats
    tm = q_ref.shape[0]
    gate_i = 2 * HEADS if backward else 0
    half = lax.broadcasted_iota(jnp.int32, (tm, LANES), 1) < HEAD_QK
    ones = jnp.ones((2 * SUBLANES, tm), BF16)
    if True:
        pair = slice((h // 2) * LANES, (h // 2 + 1) * LANES)
        qp = q_ref[:, pair]
        km = jnp.where(half if h % 2 == 0 else jnp.logical_not(half), k_ref[:, pair], 0.0).astype(BF16)
        vaug = jnp.concatenate([vt_ref[h * HEAD_V:(h + 1) * HEAD_V, :], ones], axis=0)
        ct = ct_ref[h]
        acc = w_inter[h:h + 1, :] * lax.dot_general(ct.astype(BF16), qp, _NT,
                                                    preferred_element_type=F32)
        for blk in range(tm // SRC_BLOCK):
            rows = slice(blk * SRC_BLOCK, (blk + 1) * SRC_BLOCK)
            st = lax.dot_general(km[rows, :], qp, _NT, preferred_element_type=F32)
            e = (neg_mask[rows, :] + shift_row[h:h + 1, :]) + c_col[rows, gate_i + h:gate_i + h + 1]
            p = (st * jnp.exp(e)).astype(BF16)
            acc = acc + jnp.dot(vaug[:, rows], p, preferred_element_type=F32)
        den = acc[HEAD_V:HEAD_V + 1, :]
        scale = 1.0 / jnp.maximum(jnp.abs(den), floor[h:h + 1, :])
        o_ref[h * HEAD_V:(h + 1) * HEAD_V, :] = (acc[0:HEAD_V, :] * scale).astype(o_ref.dtype)
        vs = (vaug.astype(F32) * w_src[h:h + 1, :]).astype(BF16)
        ct_ref[h] = decay[h:h + 1, :] * ct + jnp.dot(vs, km, preferred_element_type=F32)


def _split_dot(ones_mat, x, left):
    hi = x.astype(BF16)
    r1 = x - hi.astype(F32)
    mid = r1.astype(BF16)
    lo = (r1 - mid.astype(F32)).astype(BF16)
    out = None
    for piece in (lo, mid, hi):
        t = (jnp.dot(ones_mat, piece, preferred_element_type=F32) if left
             else jnp.dot(piece, ones_mat, preferred_element_type=F32))
        out = t if out is None else out + t
    return out


def _mlstm_kernel(qf, kf, vtf, gcf, grf, qb, kb, vtb, gcb, grb, bc_ref, br_ref, of_ref, ob_ref,
                  cf_ref, mf_ref, cb_ref, mb_ref, *, tiles_per_batch):
    @pl.when(pl.program_id(0) % tiles_per_batch == 0)
    def _():
        cf_ref[...] = jnp.zeros_like(cf_ref)
        cb_ref[...] = jnp.zeros_like(cb_ref)
        mf_ref[...] = jnp.zeros_like(mf_ref)
        mb_ref[...] = jnp.zeros_like(mb_ref)

    stats_f = _mlstm_gate_stats(gcf, grf, bc_ref, br_ref, mf_ref, backward=False)
    stats_b = _mlstm_gate_stats(gcb, grb, bc_ref, br_ref, mb_ref, backward=True)
    for h in range(HEADS):
        _mlstm_head(h, qf, kf, vtf, of_ref, cf_ref, stats_f, backward=False)
    for h in range(HEADS):
        _mlstm_head(h, qb, kb, vtb, ob_ref, cb_ref, stats_b, backward=True)


def _mlstm_call(q, k, vt, gc, gr, bias_c, bias_r, dims):
    tm, d, n = TOKEN_TILE, dims["d"], dims["n"]
    tpb, ct = dims["tpb"], dims["ct"]
    nq = q.shape[1]
    fwd = lambda i: (i, 0)
    bwd = lambda i: (_bwd_tile(i, tpb, ct), 0)
    fwd_t = lambda i: (0, i)
    bwd_t = lambda i: (0, _bwd_tile(i, tpb, ct))
    const = lambda i: (0, 0)
    state = [pltpu.VMEM((HEADS, HEAD_V + 2 * SUBLANES, LANES), F32), pltpu.VMEM((HEADS, LANES), F32)]
    return pl.pallas_call(
        functools.partial(_mlstm_kernel, tiles_per_batch=tpb),
        grid=(dims["tiles"],),
        in_specs=[
            pl.BlockSpec((tm, nq), fwd), pl.BlockSpec((tm, nq), fwd), pl.BlockSpec((d, tm), fwd_t),
            pl.BlockSpec((tm, LANES), fwd), pl.BlockSpec((4 * HEADS, tm), fwd_t),
            pl.BlockSpec((tm, nq), bwd), pl.BlockSpec((tm, nq), bwd), pl.BlockSpec((d, tm), bwd_t),
            pl.BlockSpec((tm, LANES), bwd), pl.BlockSpec((4 * HEADS, tm), bwd_t),
            pl.BlockSpec((1, LANES), const), pl.BlockSpec((4 * HEADS, 1), const),
        ],
        out_specs=[pl.BlockSpec((d, tm), fwd_t), pl.BlockSpec((d, tm), bwd_t)],
        out_shape=[jax.ShapeDtypeStruct((d, n), BF16), jax.ShapeDtypeStruct((d, n), BF16)],
        scratch_shapes=state + state,
        compiler_params=_params("arbitrary"),
        name="mlstm_scan",
    )(q, k, vt, gc, gr, q, k, vt, gc, gr, bias_c, bias_r)


def _finish_readout(x_new, mod_ref, g_ref, wr_ref, br_ref, o_ref, route_ref, cnt_ref, carry_ref):
    @pl.when(pl.program_id(0) == 0)
    def _():
        carry_ref[...] = jnp.zeros_like(carry_ref)

    o_ref[...] = x_new
    _route_tile(x_new, g_ref, mod_ref, wr_ref, br_ref, route_ref, cnt_ref, carry_ref)


def _mreadout_kernel(hf_ref, hb_ref, ot_ref, w_ref, x_ref, mod_ref, *route_refs):
    d = x_ref.shape[-1]
    parts = []
    for h in range(HEADS):
        rows = slice(h * HEAD_V, (h + 1) * HEAD_V)
        hs = hf_ref[rows, :].astype(F32) + hb_ref[rows, :].astype(F32)
        r = lax.rsqrt(jnp.mean(hs * hs, axis=0, keepdims=True) + NORM_EPS)
        parts.append((hs * r * ot_ref[rows, :]).astype(BF16))
    t = jnp.concatenate(parts, axis=0)
    out = lax.dot_general(t, w_ref[...], _TN, preferred_element_type=F32)
    _finish_readout(x_ref[...] + mod_ref[:, 2 * d:3 * d] * out, mod_ref, *route_refs)


def _readout_kernel(hf_ref, hb_ref, gy_ref, w_ref, x_ref, mod_ref, *route_refs):
    d = x_ref.shape[-1]
    t = _gelu_tanh(gy_ref[...].astype(F32)) * (hf_ref[...].astype(F32) + hb_ref[...].astype(F32))
    out = jnp.dot(t.astype(BF16), w_ref[...], preferred_element_type=F32)
    _finish_readout(x_ref[...] + mod_ref[:, 2 * d:3 * d] * out, mod_ref, *route_refs)


def _readout_call(kernel_fn, name, feature_major, hf, hb, aux, w_out, xs, mods, layer,
                  g_ffn, w_router_pieces, b_router_col, dims):
    tm, d, n = TOKEN_TILE, dims["d"], dims["n"]
    row = lambda i: (i, 0)
    col = lambda i: (0, i)
    const = lambda i: (0, 0)
    mixer = pl.BlockSpec((d, tm), col) if feature_major else pl.BlockSpec((tm, d), row)
    return pl.pallas_call(
        kernel_fn,
        grid=(dims["tiles"],),
        in_specs=[mixer, mixer, mixer,
                  pl.BlockSpec((d, d), const), pl.BlockSpec((tm, d), row), _mod_spec(layer, dims),
                  pl.BlockSpec((1, d), const),
                  pl.BlockSpec(w_router_pieces.shape, lambda i: (0, 0, 0)),
                  pl.BlockSpec(b_router_col.shape, const)],
        out_specs=[pl.BlockSpec((tm, d), row),
                   pl.BlockSpec((SUBLANES, tm), col),
                   pl.BlockSpec((CLASS_ROWS, LANES), const)],
        out_shape=[jax.ShapeDtypeStruct((n, d), F32),
                   jax.ShapeDtypeStruct((SUBLANES, n), F32),
                   jax.ShapeDtypeStruct((CLASS_ROWS, LANES), F32)],
        scratch_shapes=[pltpu.VMEM((CLASS_ROWS, LANES), F32)],
        compiler_params=_params("arbitrary"),
        name=name,
    )(hf, hb, aux, w_out, xs, mods, g_ffn, w_router_pieces, b_router_col)


def _rproj_kernel(x_ref, g_ref, mod_ref, w_ref, cw_ref, cb_ref, gy_ref, xc_ref, ext_ref, cur_ref,
                  *, tiles, tiles_per_batch, ctx_tiles):
    tm, d = x_ref.shape
    i = pl.program_id(0)

    def in_segment_neighbours(t):
        j = t % tiles_per_batch
        first = jnp.logical_or(j == 0, j == ctx_tiles)
        last = jnp.logical_or(j == ctx_tiles - 1, j == tiles_per_batch - 1)
        return jnp.logical_not(first), jnp.logical_not(last)

    @pl.when(i == 0)
    def _():
        ext_ref[...] = jnp.zeros_like(ext_ref)
        cur_ref[...] = jnp.zeros_like(cur_ref)

    @pl.when(i < tiles)
    def _():
        h = _modulated(x_ref[...], g_ref[...], mod_ref[...], 0).astype(BF16)
        gy_ref[...] = jnp.dot(h, w_ref[:, 0:d], preferred_element_type=F32).astype(BF16)
        cur_ref[...] = jnp.dot(h, w_ref[:, d:2 * d], preferred_element_type=F32)

    @pl.when(i > 0)
    def _():
        _, next_ok = in_segment_neighbours(i - 1)
        head_ok = jnp.logical_and(next_ok, i < tiles)
        ext_ref[SUBLANES + tm:2 * SUBLANES + tm, :] = jnp.where(head_ok, cur_ref[0:SUBLANES, :], 0.0)
        xc = cb_ref[...] + ext_ref[pl.ds(SUBLANES - CONV_LEFT, tm), :] * cw_ref[0:1, :]
        for k in range(1, CONV_TAPS):
            xc = xc + ext_ref[pl.ds(SUBLANES - CONV_LEFT + k, tm), :] * cw_ref[k:k + 1, :]
        xc_ref[...] = xc

    prev_ok, _ = in_segment_neighbours(i)
    ext_ref[0:SUBLANES, :] = jnp.where(prev_ok, ext_ref[tm:tm + SUBLANES, :], 0.0)
    ext_ref[SUBLANES:SUBLANES + tm, :] = cur_ref[...]


def _rproj_call(xs, g, mods, layer, w_in, conv_w, conv_b, dims):
    tm, d, n, tiles = TOKEN_TILE, dims["d"], dims["n"], dims["tiles"]
    tpb, ct, b = dims["tpb"], dims["ct"], dims["b"]
    cur = lambda i: (jnp.minimum(i, tiles - 1), 0)
    lag = lambda i: (jnp.maximum(i - 1, 0), 0)
    const = lambda i: (0, 0)
    mod_spec = pl.BlockSpec((None, None, 1, 6 * d),
                            lambda i: (layer, _mod_row(jnp.minimum(i, tiles - 1), tpb, ct, b), 0, 0))
    return pl.pallas_call(
        functools.partial(_rproj_kernel, tiles=tiles, tiles_per_batch=tpb, ctx_tiles=ct),
        grid=(tiles + 1,),
        in_specs=[pl.BlockSpec((tm, d), cur), pl.BlockSpec((1, d), const), mod_spec,
                  pl.BlockSpec(w_in.shape, const), pl.BlockSpec(conv_w.shape, const),
                  pl.BlockSpec(conv_b.shape, const)],
        out_specs=[pl.BlockSpec((tm, d), cur), pl.BlockSpec((tm, d), lag)],
        out_shape=[jax.ShapeDtypeStruct((n, d), BF16), jax.ShapeDtypeStruct((n, d), F32)],
        scratch_shapes=[pltpu.VMEM((tm + 2 * SUBLANES, d), F32), pltpu.VMEM((tm, d), F32)],
        compiler_params=_params("arbitrary"),
        name="rglru_proj",
    )(xs, g, mods, w_in, conv_w, conv_b)


def _lru_direction(xc_ref, wg_ref, bg_ref, lam_ref, a_ref, b_ref, *, direction):
    tm, d = xc_ref.shape
    bw = d // LRU_BLOCKS
    z = -lam_ref[direction]
    softplus = jnp.maximum(z, 0.0) + jnp.log(1.0 + jnp.exp(-jnp.abs(z)))
    k = (-0.5 * LRU_C) * softplus
    for nb in range(LRU_BLOCKS):
        sl = slice(nb * bw, (nb + 1) * bw)
        xb = xc_ref[:, sl]
        gg = jnp.dot(xb.astype(BF16), wg_ref[direction, nb], preferred_element_type=F32)
        t = jnp.tanh(gg + bg_ref[direction, :, nb * 2 * bw:(nb + 1) * 2 * bw])
        log_a = k[:, sl] * t[:, :bw] + k[:, sl]
        th = jnp.tanh(log_a)
        a_ref[pl.ds(nb, tm, stride=LRU_BLOCKS), :] = jnp.exp(log_a)
        b_ref[pl.ds(nb, tm, stride=LRU_BLOCKS), :] = (jnp.sqrt(-0.5 * th / (1.0 - th))
                                                      * ((t[:, bw:] + 1.0) * xb))


def _lru_kernel(xc_f, xc_b, wg_ref, bg_ref, lam_ref, of_ref, ob_ref, af_ref, bf_ref, ab_ref, bb_ref,
                hf_ref, hb_ref, h_ref, *, tiles_per_batch):
    tm, d = xc_f.shape
    bw = d // LRU_BLOCKS

    @pl.when(pl.program_id(0) % tiles_per_batch == 0)
    def _():
        h_ref[...] = jnp.zeros_like(h_ref)

    def scan(a_ref, b_ref, o_ref, state, order):
        h = h_ref[state]
        for t in order:
            rows = slice(t * LRU_BLOCKS, (t + 1) * LRU_BLOCKS)
            h = a_ref[rows, :] * h + b_ref[rows, :]
            o_ref[rows, :] = h
        h_ref[state] = h

    _lru_direction(xc_f, wg_ref, bg_ref, lam_ref, af_ref, bf_ref, direction=0)
    scan(af_ref, bf_ref, hf_ref, 0, range(tm))
    _lru_direction(xc_b, wg_ref, bg_ref, lam_ref, ab_ref, bb_ref, direction=1)
    scan(ab_ref, bb_ref, hb_ref, 1, range(tm - 1, -1, -1))
    for nb in range(LRU_BLOCKS):
        sl = slice(nb * bw, (nb + 1) * bw)
        of_ref[:, sl] = hf_ref[pl.ds(nb, tm, stride=LRU_BLOCKS), :].astype(BF16)
        ob_ref[:, sl] = hb_ref[pl.ds(nb, tm, stride=LRU_BLOCKS), :].astype(BF16)


def _lru_call(xc, w_gate, b_gate, lam, dims):
    tm, d, n = TOKEN_TILE, dims["d"], dims["n"]
    tpb, ct = dims["tpb"], dims["ct"]
    fwd = lambda i: (i, 0)
    bwd = lambda i: (_bwd_tile(i, tpb, ct), 0)
    c3 = lambda i: (0, 0, 0)
    c4 = lambda i: (0, 0, 0, 0)
    time_major = pltpu.VMEM((tm * LRU_BLOCKS, d // LRU_BLOCKS), F32)
    return pl.pallas_call(
        functools.partial(_lru_kernel, tiles_per_batch=tpb),
        grid=(dims["tiles"],),
        in_specs=[
            pl.BlockSpec((tm, d), fwd), pl.BlockSpec((tm, d), bwd),
            pl.BlockSpec(w_gate.shape, c4), pl.BlockSpec(b_gate.shape, c3), pl.BlockSpec(lam.shape, c3),
        ],
        out_specs=[pl.BlockSpec((tm, d), fwd), pl.BlockSpec((tm, d), bwd)],
        out_shape=[jax.ShapeDtypeStruct((n, d), BF16), jax.ShapeDtypeStruct((n, d), BF16)],
        scratch_shapes=[time_major] * 6 + [pltpu.VMEM((2, LRU_BLOCKS, d // LRU_BLOCKS), F32)],
        compiler_params=_params("arbitrary"),
        name="rglru_scan",
    )(xc, xc, w_gate, b_gate, lam)


def _first_argmax(vals):
    best = vals[0]
    idx = jnp.zeros_like(best)
    for k in range(1, len(vals)):
        better = vals[k] > best
        idx = jnp.where(better, float(k), idx)
        best = jnp.where(better, vals[k], best)
    return idx, best


def _pick(idx, vals):
    out = vals[0]
    for k in range(1, len(vals)):
        out = jnp.where(idx == float(k), vals[k], out)
    return out


def _route_tile(x, g_ref, mod_ref, wr_ref, br_ref, route_ref, cnt_ref, carry_ref):
    tm = x.shape[0]
    h = _modulated(x, g_ref[...], mod_ref[...], 3)
    hi = h.astype(BF16)
    lo = (h - hi.astype(F32)).astype(BF16)
    logits_tok = (jnp.dot(lo, wr_ref[0], preferred_element_type=F32)
                  + jnp.dot(hi, wr_ref[1], preferred_element_type=F32)
                  + jnp.dot(hi, wr_ref[0], preferred_element_type=F32))
    logits = logits_tok.T[0:N_EXPERTS, :]
    e = jnp.exp(logits - jnp.max(logits, axis=0, keepdims=True))
    scores = e / jnp.sum(e, axis=0, keepdims=True)
    sel = scores + br_ref[...]
    sel_rows = [sel[k:k + 1, :] for k in range(N_EXPERTS)]
    group_scores = []
    for gi in range(N_GROUPS):
        v = sel_rows[gi * GROUP_SIZE:(gi + 1) * GROUP_SIZE]
        best = v[0] + v[1]
        for a, b in PAIRS[1:]:
            best = jnp.maximum(best, v[a] + v[b])
        group_scores.append(best)
    grp, _ = _first_argmax(group_scores)
    in_sel = [_pick(grp, [sel_rows[gi * GROUP_SIZE + k] for gi in range(N_GROUPS)])
              for k in range(GROUP_SIZE)]
    i1, _ = _first_argmax(in_sel)
    rest = [jnp.where(i1 == float(k), -jnp.inf, in_sel[k]) for k in range(GROUP_SIZE)]
    i2, _ = _first_argmax(rest)
    lo = jnp.minimum(i1, i2)
    hi = jnp.maximum(i1, i2)
    pair = jnp.zeros_like(lo)
    for k, (slot_a, slot_b) in enumerate(PAIR_SLOTS):
        here = jnp.logical_and(lo == float(min(slot_a, slot_b)), hi == float(max(slot_a, slot_b)))
        pair = jnp.where(here, float(k), pair)
    cls = grp * float(len(PAIR_SLOTS)) + pair
    crow = lax.broadcasted_iota(jnp.int32, (CLASS_ROWS, tm), 0).astype(F32)
    onehot = jnp.where(crow == cls, 1.0, 0.0)
    rows = lax.broadcasted_iota(jnp.int32, (tm, tm), 0)
    cols = lax.broadcasted_iota(jnp.int32, (tm, tm), 1)
    upper = jnp.where(rows <= cols, 1.0, 0.0).astype(BF16)
    cum = jnp.dot(onehot.astype(BF16), upper, preferred_element_type=F32)
    carry = carry_ref[:, 0:1]
    rank = jnp.sum(onehot * (cum - 1.0 + carry), axis=0, keepdims=True)
    new_carry = carry + jnp.sum(onehot, axis=1, keepdims=True)
    carry_ref[...] = jnp.broadcast_to(new_carry, carry_ref.shape)
    cnt_ref[...] = jnp.broadcast_to(new_carry, cnt_ref.shape)
    zero = jnp.zeros_like(cls)
    route_ref[...] = jnp.concatenate([cls, rank, zero, zero, zero, zero, zero, zero], axis=0)


def _router_weights(w_router):
    w = jnp.pad(w_router.astype(F32), ((0, 0), (0, LANES - w_router.shape[1])))
    hi = w.astype(BF16)
    lo = (w - hi.astype(F32)).astype(BF16)
    return jnp.stack([hi, lo])


ROW_DMA_UNROLL = 8


def _dispatch_kernel(pos_ref, x_ref, g_ref, mod_ref, init_ref, o_ref, stage_ref, sems):
    del init_ref
    tokens, d = x_ref.shape
    per = d // LANES
    i = pl.program_id(0)
    slot = i % 2
    base = i * tokens
    _store_token_tiles(stage_ref.at[slot], _modulated(x_ref[...], g_ref[...], mod_ref[...], 3))

    def issue(r, carry):
        src = pl.multiple_of(r * per, per)
        dst = pl.multiple_of(pos_ref[base + r], per)
        pltpu.make_async_copy(stage_ref.at[slot, pl.ds(src, per)], o_ref.at[pl.ds(dst, per)],
                              sems.at[slot]).start()
        return carry

    lax.fori_loop(0, tokens, issue, 0, unroll=ROW_DMA_UNROLL)

    def wait_tile(s):
        pltpu.make_async_copy(stage_ref.at[s], o_ref.at[pl.ds(0, tokens * per)], sems.at[s]).wait()

    @pl.when(i > 0)
    def _():
        wait_tile(1 - slot)

    @pl.when(i == pl.num_programs(0) - 1)
    def _():
        wait_tile(slot)


def _dispatch_call(pos_rows, xs, g, mods, layer, init, dims):
    tm, d = TOKEN_TILE, dims["d"]
    per = d // LANES
    tpb, ct, b = dims["tpb"], dims["ct"], dims["b"]
    return pl.pallas_call(
        _dispatch_kernel,
        grid_spec=pltpu.PrefetchScalarGridSpec(
            num_scalar_prefetch=1,
            grid=(dims["tiles"],),
            in_specs=[pl.BlockSpec((tm, d), lambda i, p: (i, 0)),
                      pl.BlockSpec((1, d), lambda i, p: (0, 0)),
                      pl.BlockSpec((None, None, 1, 6 * d),
                                   lambda i, p: (layer, _mod_row(i, tpb, ct, b), 0, 0)),
                      pl.BlockSpec(memory_space=pl.ANY)],
            out_specs=pl.BlockSpec(memory_space=pl.ANY),
            scratch_shapes=[pltpu.VMEM((2, tm * per, LANES), F32), pltpu.SemaphoreType.DMA((2,))],
        ),
        out_shape=jax.ShapeDtypeStruct(init.shape, F32),
        input_output_aliases={4: 0},
        compiler_params=_params("arbitrary"),
        name="dispatch",
    )(pos_rows, xs, g, mods, init)


def _expert_kernel(ea_ref, eb_ref, ok_ref, x_ref, wr_ref, w1a, w3a, w2a, w1b, w3b, w2b, o_ref):
    d = w1a.shape[0]
    te = x_ref.shape[0] // (d // LANES)
    t = pl.program_id(0)

    @pl.when(ok_ref[t] != 0)
    def _():
        h = _load_token_tiles(x_ref, te, d)
        x = h.astype(BF16)
        dw = wr_ref[pl.ds(ea_ref[t], 1), :] - wr_ref[pl.ds(eb_ref[t], 1), :]
        gate_a = jax.nn.sigmoid(jnp.sum(h * dw, axis=1, keepdims=True))
        gate_b = 1.0 - gate_a

        def expert(w1, w3, w2):
            a = jnp.dot(x, w1[...].astype(BF16), preferred_element_type=F32)
            b = jnp.dot(x, w3[...].astype(BF16), preferred_element_type=F32)
            u = (a * jax.nn.sigmoid(a)) * b
            return jnp.dot(u.astype(BF16), w2[...].astype(BF16), preferred_element_type=F32)

        _store_token_tiles(o_ref, gate_a * expert(w1a, w3a, w2a) + gate_b * expert(w1b, w3b, w2b))

    @pl.when(ok_ref[t] == 0)
    def _():
        o_ref[...] = jnp.zeros_like(o_ref)


def _expert_call(tile_a, tile_b, tile_ok, xsorted, w_router_t, w1, w3, w2, layer, d):
    te = TOKEN_TILE
    per = d // LANES
    n_tiles = xsorted.shape[0] // (te * per)
    de = w1.shape[-1]
    sel_a = lambda t, ea, eb, ok: (layer, ea[t], 0, 0)
    sel_b = lambda t, ea, eb, ok: (layer, eb[t], 0, 0)
    used = lambda t, ea, eb, ok: (t, 0)
    return pl.pallas_call(
        _expert_kernel,
        grid_spec=pltpu.PrefetchScalarGridSpec(
            num_scalar_prefetch=3,
            grid=(n_tiles,),
            in_specs=[
                pl.BlockSpec((te * per, LANES), used),
                pl.BlockSpec(w_router_t.shape, lambda t, ea, eb, ok: (0, 0)),
                pl.BlockSpec((None, None, d, de), sel_a), pl.BlockSpec((None, None, d, de), sel_a),
                pl.BlockSpec((None, None, de, d), sel_a),
                pl.BlockSpec((None, None, d, de), sel_b), pl.BlockSpec((None, None, d, de), sel_b),
                pl.BlockSpec((None, None, de, d), sel_b),
            ],
            out_specs=pl.BlockSpec((te * per, LANES), used),
        ),
        out_shape=jax.ShapeDtypeStruct((n_tiles * te * per, LANES), F32),
        compiler_params=_params("arbitrary"),
        name="experts",
    )(tile_a, tile_b, tile_ok, xsorted, w_router_t, w1, w3, w2, w1, w3, w2)


def _combine_kernel(pos_ref, y_ref, x_ref, mod_ref, gf_ref, o_ref, buf_ref, sems,
                    *, tile_of, final_norm):
    tm, d = x_ref.shape
    per = d // LANES
    i = pl.program_id(0)
    slot = i % 2

    def gather(step, s):
        base = tile_of(step) * tm

        def issue(r, carry):
            src = pl.multiple_of(pos_ref[base + r], per)
            dst = pl.multiple_of(r * per, per)
            pltpu.make_async_copy(y_ref.at[pl.ds(src, per)], buf_ref.at[s, pl.ds(dst, per)],
                                  sems.at[s]).start()
            return carry

        lax.fori_loop(0, tm, issue, 0, unroll=ROW_DMA_UNROLL)

    @pl.when(i == 0)
    def _():
        gather(0, 0)

    @pl.when(i + 1 < pl.num_programs(0))
    def _():
        gather(i + 1, 1 - slot)

    pltpu.make_async_copy(y_ref.at[pl.ds(0, tm * per)], buf_ref.at[slot], sems.at[slot]).wait()
    out = x_ref[...] + mod_ref[:, 5 * d:6 * d] * _load_token_tiles(buf_ref.at[slot], tm, d)
    if final_norm:
        ms = jnp.mean(out * out, axis=-1, keepdims=True)
        out = out * lax.rsqrt(ms + NORM_EPS) * gf_ref[...]
    o_ref[...] = out


def _combine_call(pos, ysorted, xs, mods, layer, g_final, dims, final_norm):
    tm, d = TOKEN_TILE, dims["d"]
    tpb, ct, b = dims["tpb"], dims["ct"], dims["b"]
    if final_norm:
        lt = tpb - ct
        tile_of = lambda i: (i // lt) * tpb + ct + i % lt
        n_tiles = b * lt
    else:
        tile_of = lambda i: i
        n_tiles = dims["tiles"]
    mod_spec = pl.BlockSpec((None, None, 1, 6 * d),
                            lambda i, p: (layer, _mod_row(tile_of(i), tpb, ct, b), 0, 0))
    return pl.pallas_call(
        functools.partial(_combine_kernel, tile_of=tile_of, final_norm=final_norm),
        grid_spec=pltpu.PrefetchScalarGridSpec(
            num_scalar_prefetch=1,
            grid=(n_tiles,),
            in_specs=[
                pl.BlockSpec(memory_space=pl.ANY),
                pl.BlockSpec((tm, d), lambda i, p: (tile_of(i), 0)),
                mod_spec,
                pl.BlockSpec((1, d), lambda i, p: (0, 0)),
            ],
            out_specs=pl.BlockSpec((tm, d), lambda i, p: (i, 0)),
            scratch_shapes=[pltpu.VMEM((2, tm * (d // LANES), LANES), F32), pltpu.SemaphoreType.DMA((2,))],
        ),
        out_shape=jax.ShapeDtypeStruct((n_tiles * tm, d), F32),
        compiler_params=_params("arbitrary"),
        name="combine",
    )(pos, ysorted, xs, mods, g_final)


_CLASS_A = np.array([g * GROUP_SIZE + a for g in range(N_GROUPS) for a, _ in PAIR_SLOTS], np.int32)
_CLASS_B = np.array([g * GROUP_SIZE + b for g in range(N_GROUPS) for _, b in PAIR_SLOTS], np.int32)


def _routing_plan(route, counts, n_tiles_padded):
    te = TOKEN_TILE
    cls = route[0].astype(jnp.int32)
    rank = route[1].astype(jnp.int32)
    cnt = counts[:N_CLASSES, 0].astype(jnp.int32)
    tiles = (cnt + te - 1) // te
    tile_end = jnp.cumsum(tiles)
    tile_start = tile_end - tiles
    onehot = (cls[:, None] == jnp.arange(N_CLASSES, dtype=jnp.int32)[None, :]).astype(jnp.int32)
    pos = jnp.sum(onehot * (tile_start * te)[None, :], axis=1) + rank
    t = jnp.arange(n_tiles_padded, dtype=jnp.int32)
    ok = (t < tile_end[-1]).astype(jnp.int32)
    tcls = jnp.sum((tile_end[None, :] <= jnp.minimum(t, tile_end[-1] - 1)[:, None]).astype(jnp.int32), axis=1)
    tile_a = jnp.take(jnp.asarray(_CLASS_A), tcls)
    tile_b = jnp.take(jnp.asarray(_CLASS_B), tcls)
    return pos, tile_a, tile_b, ok


def _moe(xs, g_ffn, route, counts, sorted_buf, mods, layer, w_router_t, w1, w3, w2, g_final, dims, final_norm):
    per = dims["d"] // LANES
    n_tiles_padded = sorted_buf.shape[0] // (TOKEN_TILE * per)
    pos, tile_a, tile_b, tile_ok = _routing_plan(route, counts, n_tiles_padded)
    xsorted = _dispatch_call(pos * per, xs, g_ffn, mods, layer, sorted_buf, dims)
    ysorted = _expert_call(tile_a, tile_b, tile_ok, xsorted, w_router_t, w1, w3, w2, layer, dims["d"])
    return _combine_call(pos * per, ysorted, xs, mods, layer, g_final, dims, final_norm), xsorted


def _pos_table(n_tokens, dim):
    rows = n_tokens // GRID_WIDTH
    r, col = jnp.meshgrid(jnp.arange(rows, dtype=F32), jnp.arange(GRID_WIDTH, dtype=F32), indexing="ij")
    quarter = dim // 4
    freqs = jnp.exp(-math.log(POS_BASE) * jnp.arange(quarter, dtype=F32) / quarter)

    def enc(p):
        ang = p.reshape(-1, 1) * freqs
        return jnp.concatenate([jnp.sin(ang), jnp.cos(ang)], axis=-1)

    return jnp.concatenate([enc(r), enc(col)], axis=-1)


def _mlstm_weights(w_in, b_gate):
    nq = HEADS * HEAD_QK
    nv = HEADS * HEAD_V
    wq = w_in[:, 0:nq] * (HEAD_QK ** -0.5)
    wk = w_in[:, nq:2 * nq]
    wv = w_in[:, 2 * nq:2 * nq + nv]
    wo = w_in[:, 2 * nq + nv:2 * nq + 2 * nv]
    wg = w_in[:, 2 * nq + 2 * nv:]
    wg_pad = jnp.pad(wg, ((0, 0), (0, LANES - wg.shape[1])))
    w_tok = jnp.concatenate([wq, wk, wg_pad], axis=1).astype(BF16)
    w_feat = jnp.concatenate([wv, wo, wg], axis=1).T.astype(BF16)
    bias = b_gate.reshape(-1).astype(F32)
    bias_c = jnp.pad(bias, (0, LANES - bias.shape[0])).reshape(1, LANES)
    bias_r = bias.reshape(-1, 1)
    return w_tok, w_feat, bias_c, bias_r


def kernel(x, c, ctx, c_ctx, w_ada, b_ada, g_mix, g_ffn, g_final, m_w_in, m_b_gate, m_g_head, m_w_out, r_w_in, r_conv_w, r_conv_b, r_w_gate, r_b_gate, r_lam, r_w_out, w_router, b_router, e_w1, e_w3, e_w2):
    batch, t_len, d = x.shape
    ctx_len = ctx.shape[1]
    depth = w_ada.shape[0]
    tm = TOKEN_TILE
    assert t_len % tm == 0 and ctx_len % tm == 0 and batch + 1 <= ADA_ROWS
    assert d == HEADS * HEAD_V and t_len % GRID_WIDTH == 0
    s_len = ctx_len + t_len
    dims = dict(b=batch, d=d, n=batch * s_len, tpb=s_len // tm, ct=ctx_len // tm,
                tiles=batch * s_len // tm)

    cvec = jnp.concatenate([c, c_ctx[None, :], jnp.zeros((ADA_ROWS - batch - 1, d), F32)], axis=0)
    mods = _ada_call(cvec, w_ada, b_ada).reshape(depth, ADA_ROWS, 1, 6 * d)
    xs = (ctx.reshape(batch * ctx_len, d), x.reshape(batch * t_len, d), _pos_table(t_len, d))

    w_router_t = w_router.T.astype(F32)
    w_router_pieces = _router_weights(w_router)
    b_router_col = b_router.reshape(-1, 1).astype(F32)
    g_final2 = g_final.reshape(1, d)
    sorted_buf = jnp.zeros(((dims["tiles"] + N_CLASSES) * tm * (d // LANES), LANES), F32)
    out = None
    for i in range(depth):
        j = i // 2
        g_mix_i = g_mix[i].reshape(1, d)
        g_ffn_i = g_ffn[i].reshape(1, d)
        if i % 2 == 0:
            w_tok, w_feat, bias_c, bias_r = _mlstm_weights(m_w_in[j], m_b_gate[j])
            proj = _mproj_call(xs, g_mix_i, mods, i, w_tok, w_feat, dims)
            if isinstance(xs, tuple):
                xs, proj = proj[0], proj[1:]
            q, k, vt, ot, gc, gr = proj
            hf, hb = _mlstm_call(q, k, vt, gc, gr, bias_c, bias_r, dims)
            w_out = (m_g_head[j][:, None] * m_w_out[j]).astype(BF16)
            xs, route, counts = _readout_call(_mreadout_kernel, "mlstm_readout", True, hf, hb, ot, w_out,
                                              xs, mods, i, g_ffn_i, w_router_pieces, b_router_col, dims)
        else:
            gy, xc = _rproj_call(xs, g_mix_i, mods, i, r_w_in[j].astype(BF16), r_conv_w[j],
                                 r_conv_b[j].reshape(1, d), dims)
            hf, hb = _lru_call(xc, (0.5 * r_w_gate[j]).astype(BF16), 0.5 * r_b_gate[j].reshape(2, 1, -1),
                               r_lam[j].reshape(2, 1, d), dims)
            xs, route, counts = _readout_call(_readout_kernel, "rglru_readout", False, hf, hb, gy,
                                              r_w_out[j].astype(BF16), xs, mods, i, g_ffn_i,
                                              w_router_pieces, b_router_col, dims)
        last = i == depth - 1
        res, sorted_buf = _moe(xs, g_ffn_i, route, counts, sorted_buf, mods, i, w_router_t,
                               e_w1, e_w3, e_w2, g_final2, dims, last)
        if last:
            out = res
        else:
            xs = res
    return out.reshape(batch, t_len, d)
```

```python
import functools
import math

import jax
import jax.numpy as jnp
import numpy as np
from jax import lax
from jax.experimental import pallas as pl
from jax.experimental.pallas import tpu as pltpu

F32 = jnp.float32
BF16 = jnp.bfloat16
HIGHEST = lax.Precision.HIGHEST

TOKEN_TILE = 256
LANES = 128
SUBLANES = 8
VMEM_LIMIT_BYTES = 56 * 1024 * 1024
NORM_EPS = 1e-6
GRID_WIDTH = 64
POS_BASE = 10000.0
HEADS = 8
HEAD_QK = 64
HEAD_V = 128
LRU_BLOCKS = 8
LRU_C = 8.0
CONV_TAPS = 4
CONV_LEFT = CONV_TAPS // 2
N_EXPERTS = 16
N_GROUPS = 4
GROUP_SIZE = N_EXPERTS // N_GROUPS
PAIRS = [(a, b) for a in range(GROUP_SIZE) for b in range(a + 1, GROUP_SIZE)]
PAIR_SLOTS = [(0, 1), (0, 2), (0, 3), (1, 3), (1, 2), (3, 2)]
N_CLASSES = N_GROUPS * len(PAIRS)
CLASS_ROWS = 32
ROUTE_COLS = LANES
ADA_ROWS = 16


def _params(*sem):
    return pltpu.CompilerParams(dimension_semantics=sem, vmem_limit_bytes=VMEM_LIMIT_BYTES)


def _mod_row(i, tiles_per_batch, ctx_tiles, batch):
    return jnp.where(i % tiles_per_batch < ctx_tiles, batch, i // tiles_per_batch)


def _bwd_tile(i, tiles_per_batch, ctx_tiles):
    b = i // tiles_per_batch
    j = i % tiles_per_batch
    jb = jnp.where(j < ctx_tiles, ctx_tiles - 1 - j, tiles_per_batch - 1 - (j - ctx_tiles))
    return b * tiles_per_batch + jb


def _ada_kernel(c_ref, w_ref, b_ref, o_ref):
    c = c_ref[...]
    s = c * jax.nn.sigmoid(c)
    o_ref[...] = jnp.dot(s, w_ref[...], preferred_element_type=F32, precision=HIGHEST) + b_ref[...]


def _ada_call(cvec, w_ada, b_ada):
    depth, d, n = w_ada.shape
    tn = n // 4
    return pl.pallas_call(
        _ada_kernel,
        grid=(depth, n // tn),
        in_specs=[
            pl.BlockSpec((ADA_ROWS, d), lambda l, j: (0, 0)),
            pl.BlockSpec((None, d, tn), lambda l, j: (l, 0, j)),
            pl.BlockSpec((None, 1, tn), lambda l, j: (l, 0, j)),
        ],
        out_specs=pl.BlockSpec((None, ADA_ROWS, tn), lambda l, j: (l, 0, j)),
        out_shape=jax.ShapeDtypeStruct((depth, ADA_ROWS, n), F32),
        compiler_params=_params("arbitrary", "arbitrary"),
        name="ada",
    )(cvec, w_ada, b_ada.reshape(depth, 1, n))


def _modulated(x, g, mod, shift_idx):
    d = x.shape[-1]
    ms = jnp.mean(x * x, axis=-1, keepdims=True)
    xn = x * lax.rsqrt(ms + NORM_EPS) * g
    shift = mod[:, shift_idx * d:(shift_idx + 1) * d]
    scale = mod[:, (shift_idx + 1) * d:(shift_idx + 2) * d]
    return xn * (1.0 + scale) + shift


def _mod_spec(layer, dims):
    tpb, ct, b = dims["tpb"], dims["ct"], dims["b"]
    return pl.BlockSpec((None, None, 1, 6 * dims["d"]),
                        lambda i, *_: (layer, _mod_row(i, tpb, ct, b), 0, 0))


def _store_token_tiles(ref, x):
    tokens, d = x.shape
    per = d // LANES
    for s in range(per):
        ref[pl.ds(s, tokens, stride=per), :] = x[:, s * LANES:(s + 1) * LANES]


def _load_token_tiles(ref, tokens, d):
    per = d // LANES
    return jnp.concatenate([ref[pl.ds(s, tokens, stride=per), :] for s in range(per)], axis=1)


def _gelu_tanh(y):
    return 0.5 * y * (1.0 + jnp.tanh(math.sqrt(2.0 / math.pi) * (y + 0.044715 * (y * y * y))))


SRC_BLOCK = 128
_NT = (((1,), (1,)), ((), ()))
_TN = (((0,), (0,)), ((), ()))


def _mproj_kernel(*refs, assemble, tiles_per_batch, ctx_tiles):
    if assemble:
        ctx_ref, lat_ref, pos_ref, g_ref, mod_ref, w_ref, wt_ref, xs_ref = refs[:8]
        is_ctx = pl.program_id(0) % tiles_per_batch < ctx_tiles
        x = jnp.where(is_ctx, ctx_ref[...], lat_ref[...] + pos_ref[...])
        xs_ref[...] = x
    else:
        x_ref, g_ref, mod_ref, w_ref, wt_ref = refs[:5]
        x = x_ref[...]
    q_ref, k_ref, vt_ref, ot_ref, gc_ref, gr_ref = refs[-6:]
    d = x.shape[-1]
    nq = q_ref.shape[-1]
    h = _modulated(x, g_ref[...], mod_ref[...], 0).astype(BF16)
    q_ref[...] = jnp.dot(h, w_ref[:, 0:nq], preferred_element_type=F32).astype(BF16)
    k_ref[...] = jnp.dot(h, w_ref[:, nq:2 * nq], preferred_element_type=F32).astype(BF16)
    gc_ref[...] = jnp.dot(h, w_ref[:, 2 * nq:2 * nq + LANES], preferred_element_type=F32)
    vt_ref[...] = lax.dot_general(wt_ref[0:d, :], h, _NT, preferred_element_type=F32).astype(BF16)
    ot = lax.dot_general(wt_ref[d:2 * d, :], h, _NT, preferred_element_type=F32)
    ot_ref[...] = jax.nn.sigmoid(ot).astype(BF16)
    gr_ref[...] = lax.dot_general(wt_ref[2 * d:2 * d + 4 * HEADS, :], h, _NT, preferred_element_type=F32)


def _mproj_call(stream, g, mods, layer, w_tok, w_feat, dims):
    tm, d, n = TOKEN_TILE, dims["d"], dims["n"]
    tpb, ct = dims["tpb"], dims["ct"]
    lt = tpb - ct
    nq = HEADS * HEAD_QK
    row = lambda i: (i, 0)
    col = lambda i: (0, i)
    const = lambda i: (0, 0)
    assemble = isinstance(stream, tuple)
    if assemble:
        stream_specs = [
            pl.BlockSpec((tm, d), lambda i: ((i // tpb) * ct + jnp.minimum(i % tpb, ct - 1), 0)),
            pl.BlockSpec((tm, d), lambda i: ((i // tpb) * lt + jnp.maximum(i % tpb - ct, 0), 0)),
            pl.BlockSpec((tm, d), lambda i: (jnp.maximum(i % tpb - ct, 0), 0)),
        ]
        stream_out = [pl.BlockSpec((tm, d), row)], [jax.ShapeDtypeStruct((n, d), F32)]
    else:
        stream, stream_specs, stream_out = (stream,), [pl.BlockSpec((tm, d), row)], ([], [])
    return pl.pallas_call(
        functools.partial(_mproj_kernel, assemble=assemble, tiles_per_batch=tpb, ctx_tiles=ct),
        grid=(dims["tiles"],),
        in_specs=stream_specs + [
            pl.BlockSpec((1, d), const),
            _mod_spec(layer, dims),
            pl.BlockSpec(w_tok.shape, const),
            pl.BlockSpec(w_feat.shape, const),
        ],
        out_specs=stream_out[0] + [
            pl.BlockSpec((tm, nq), row),
            pl.BlockSpec((tm, nq), row),
            pl.BlockSpec((d, tm), col),
            pl.BlockSpec((d, tm), col),
            pl.BlockSpec((tm, LANES), row),
            pl.BlockSpec((4 * HEADS, tm), col),
        ],
        out_shape=stream_out[1] + [
            jax.ShapeDtypeStruct((n, nq), BF16),
            jax.ShapeDtypeStruct((n, nq), BF16),
            jax.ShapeDtypeStruct((d, n), BF16),
            jax.ShapeDtypeStruct((d, n), BF16),
            jax.ShapeDtypeStruct((n, LANES), F32),
            jax.ShapeDtypeStruct((4 * HEADS, n), F32),
        ],
        compiler_params=_params("arbitrary"),
        name="mlstm_proj",
    )(*stream, g, mods, w_tok, w_feat)


def _mlstm_gate_stats(gc_ref, gr_ref, bc_ref, br_ref, m_ref, *, backward):
    tm = gc_ref.shape[0]
    gate_i = 2 * HEADS if backward else 0
    gate_f = gate_i + HEADS
    last = 0 if backward else tm - 1
    src = lax.broadcasted_iota(jnp.int32, (tm, tm), 0)
    tgt = lax.broadcasted_iota(jnp.int32, (tm, tm), 1)
    visible = (src >= tgt) if backward else (src <= tgt)
    neg_mask = jnp.where(visible, 0.0, -jnp.inf)
    xc = gc_ref[...] + bc_ref[...]
    xr = gr_ref[...] + br_ref[...]
    before = (src <= tgt) if backward else (src >= tgt)
    cum_c = _split_dot(jnp.where(before, 1.0, 0.0).astype(BF16), jax.nn.log_sigmoid(xc), left=True)
    cum_r = _split_dot(jnp.where(visible, 1.0, 0.0).astype(BF16), jax.nn.log_sigmoid(xr), left=False)
    b_row = cum_r[gate_f:gate_f + HEADS, :]
    c_row = xr[gate_i:gate_i + HEADS, :] - b_row
    c_col = xc - pltpu.roll(cum_c, LANES - HEADS, axis=1)
    run_col = c_col
    shift = 1
    while shift < tm:
        fill = jnp.full((shift, LANES), -jnp.inf, F32)
        if backward:
            moved = jnp.concatenate([run_col[shift:, :], fill], axis=0)
        else:
            moved = jnp.concatenate([fill, run_col[:tm - shift, :]], axis=0)
        run_col = jnp.maximum(run_col, moved)
        shift *= 2
    run = run_col.T[gate_i:gate_i + HEADS, :]
    m = m_ref[:, 0:1]
    log_inter = b_row + m
    m_pos = jnp.maximum(log_inter, b_row + run)
    shift_row = b_row - m_pos
    w_inter = jnp.exp(log_inter - m_pos)
    floor = jnp.exp(-m_pos)
    b_last = b_row[:, last:last + 1]
    m_new = m_pos[:, last:last + 1]
    w_src = jnp.exp(b_last + c_row - m_new)
    decay = jnp.exp(b_last + m - m_new)
    m_ref[...] = jnp.broadcast_to(m_new, m_ref.shape)
    return neg_mask, c_col, shift_row, w_inter, floor, w_src, decay


def _mlstm_head(h, q_ref, k_ref, vt_ref, o_ref, ct_ref, stats, *, backward):
    neg_mask, c_col, shift_row, w_inter, floor, w_src, decay = stats
    tm = q_ref.shape[0]
    gate_i = 2 * HEADS if backward else 0
    half = lax.broadcasted_iota(jnp.int32, (tm, LANES), 1) < HEAD_QK
    ones = jnp.ones((2 * SUBLANES, tm), BF16)
    if True:
        pair = slice((h // 2) * LANES, (h // 2 + 1) * LANES)
        qp = q_ref[:, pair]
        km = jnp.where(half if h % 2 == 0 else jnp.logical_not(half), k_ref[:, pair], 0.0).astype(BF16)
        vaug = jnp.concatenate([vt_ref[h * HEAD_V:(h + 1) * HEAD_V, :], ones], axis=0)
        ct = ct_ref[h]
        acc = w_inter[h:h + 1, :] * lax.dot_general(ct.astype(BF16), qp, _NT,
                                                    preferred_element_type=F32)
        for blk in range(tm // SRC_BLOCK):
            rows = slice(blk * SRC_BLOCK, (blk + 1) * SRC_BLOCK)
            st = lax.dot_general(km[rows, :], qp, _NT, preferred_element_type=F32)
            e = (neg_mask[rows, :] + shift_row[h:h + 1, :]) + c_col[rows, gate_i + h:gate_i + h + 1]
            p = (st * jnp.exp(e)).astype(BF16)
            acc = acc + jnp.dot(vaug[:, rows], p, preferred_element_type=F32)
        den = acc[HEAD_V:HEAD_V + 1, :]
        scale = 1.0 / jnp.maximum(jnp.abs(den), floor[h:h + 1, :])
        o_ref[h * HEAD_V:(h + 1) * HEAD_V, :] = (acc[0:HEAD_V, :] * scale).astype(o_ref.dtype)
        vs = (vaug.astype(F32) * w_src[h:h + 1, :]).astype(BF16)
        ct_ref[h] = decay[h:h + 1, :] * ct + jnp.dot(vs, km, preferred_element_type=F32)


def _split_dot(ones_mat, x, left):
    hi = x.astype(BF16)
    r1 = x - hi.astype(F32)
    mid = r1.astype(BF16)
    lo = (r1 - mid.astype(F32)).astype(BF16)
    out = None
    for piece in (lo, mid, hi):
        t = (jnp.dot(ones_mat, piece, preferred_element_type=F32) if left
             else jnp.dot(piece, ones_mat, preferred_element_type=F32))
        out = t if out is None else out + t
    return out


def _mlstm_kernel(qf, kf, vtf, gcf, grf, qb, kb, vtb, gcb, grb, bc_ref, br_ref, of_ref, ob_ref,
                  cf_ref, mf_ref, cb_ref, mb_ref, *, tiles_per_batch):
    @pl.when(pl.program_id(0) % tiles_per_batch == 0)
    def _():
        cf_ref[...] = jnp.zeros_like(cf_ref)
        cb_ref[...] = jnp.zeros_like(cb_ref)
        mf_ref[...] = jnp.zeros_like(mf_ref)
        mb_ref[...] = jnp.zeros_like(mb_ref)

    stats_f = _mlstm_gate_stats(gcf, grf, bc_ref, br_ref, mf_ref, backward=False)
    stats_b = _mlstm_gate_stats(gcb, grb, bc_ref, br_ref, mb_ref, backward=True)
    for h in range(HEADS):
        _mlstm_head(h, qf, kf, vtf, of_ref, cf_ref, stats_f, backward=False)
    for h in range(HEADS):
        _mlstm_head(h, qb, kb, vtb, ob_ref, cb_ref, stats_b, backward=True)


def _mlstm_call(q, k, vt, gc, gr, bias_c, bias_r, dims):
    tm, d, n = TOKEN_TILE, dims["d"], dims["n"]
    tpb, ct = dims["tpb"], dims["ct"]
    nq = q.shape[1]
    fwd = lambda i: (i, 0)
    bwd = lambda i: (_bwd_tile(i, tpb, ct), 0)
    fwd_t = lambda i: (0, i)
    bwd_t = lambda i: (0, _bwd_tile(i, tpb, ct))
    const = lambda i: (0, 0)
    state = [pltpu.VMEM((HEADS, HEAD_V + 2 * SUBLANES, LANES), F32), pltpu.VMEM((HEADS, LANES), F32)]
    return pl.pallas_call(
        functools.partial(_mlstm_kernel, tiles_per_batch=tpb),
        grid=(dims["tiles"],),
        in_specs=[
            pl.BlockSpec((tm, nq), fwd), pl.BlockSpec((tm, nq), fwd), pl.BlockSpec((d, tm), fwd_t),
            pl.BlockSpec((tm, LANES), fwd), pl.BlockSpec((4 * HEADS, tm), fwd_t),
            pl.BlockSpec((tm, nq), bwd), pl.BlockSpec((tm, nq), bwd), pl.BlockSpec((d, tm), bwd_t),
            pl.BlockSpec((tm, LANES), bwd), pl.BlockSpec((4 * HEADS, tm), bwd_t),
            pl.BlockSpec((1, LANES), const), pl.BlockSpec((4 * HEADS, 1), const),
        ],
        out_specs=[pl.BlockSpec((d, tm), fwd_t), pl.BlockSpec((d, tm), bwd_t)],
        out_shape=[jax.ShapeDtypeStruct((d, n), BF16), jax.ShapeDtypeStruct((d, n), BF16)],
        scratch_shapes=state + state,
        compiler_params=_params("arbitrary"),
        name="mlstm_scan",
    )(q, k, vt, gc, gr, q, k, vt, gc, gr, bias_c, bias_r)


def _finish_readout(x_new, mod_ref, g_ref, wr_ref, br_ref, o_ref, route_ref, cnt_ref, carry_ref):
    @pl.when(pl.program_id(0) == 0)
    def _():
        carry_ref[...] = jnp.zeros_like(carry_ref)

    o_ref[...] = x_new
    _route_tile(x_new, g_ref, mod_ref, wr_ref, br_ref, route_ref, cnt_ref, carry_ref)


def _mreadout_kernel(hf_ref, hb_ref, ot_ref, w_ref, x_ref, mod_ref, *route_refs):
    d = x_ref.shape[-1]
    parts = []
    for h in range(HEADS):
        rows = slice(h * HEAD_V, (h + 1) * HEAD_V)
        hs = hf_ref[rows, :].astype(F32) + hb_ref[rows, :].astype(F32)
        r = lax.rsqrt(jnp.mean(hs * hs, axis=0, keepdims=True) + NORM_EPS)
        parts.append((hs * r * ot_ref[rows, :]).astype(BF16))
    t = jnp.concatenate(parts, axis=0)
    out = lax.dot_general(t, w_ref[...], _TN, preferred_element_type=F32)
    _finish_readout(x_ref[...] + mod_ref[:, 2 * d:3 * d] * out, mod_ref, *route_refs)


def _readout_kernel(hf_ref, hb_ref, gy_ref, w_ref, x_ref, mod_ref, *route_refs):
    d = x_ref.shape[-1]
    t = _gelu_tanh(gy_ref[...].astype(F32)) * (hf_ref[...].astype(F32) + hb_ref[...].astype(F32))
    out = jnp.dot(t.astype(BF16), w_ref[...], preferred_element_type=F32)
    _finish_readout(x_ref[...] + mod_ref[:, 2 * d:3 * d] * out, mod_ref, *route_refs)


def _readout_call(kernel_fn, name, feature_major, hf, hb, aux, w_out, xs, mods, layer,
                  g_ffn, w_router_pieces, b_router_col, dims):
    tm, d, n = TOKEN_TILE, dims["d"], dims["n"]
    row = lambda i: (i, 0)
    col = lambda i: (0, i)
    const = lambda i: (0, 0)
    mixer = pl.BlockSpec((d, tm), col) if feature_major else pl.BlockSpec((tm, d), row)
    return pl.pallas_call(
        kernel_fn,
        grid=(dims["tiles"],),
        in_specs=[mixer, mixer, mixer,
                  pl.BlockSpec((d, d), const), pl.BlockSpec((tm, d), row), _mod_spec(layer, dims),
                  pl.BlockSpec((1, d), const),
                  pl.BlockSpec(w_router_pieces.shape, lambda i: (0, 0, 0)),
                  pl.BlockSpec(b_router_col.shape, const)],
        out_specs=[pl.BlockSpec((tm, d), row),
                   pl.BlockSpec((SUBLANES, tm), col),
                   pl.BlockSpec((CLASS_ROWS, LANES), const)],
        out_shape=[jax.ShapeDtypeStruct((n, d), F32),
                   jax.ShapeDtypeStruct((SUBLANES, n), F32),
                   jax.ShapeDtypeStruct((CLASS_ROWS, LANES), F32)],
        scratch_shapes=[pltpu.VMEM((CLASS_ROWS, LANES), F32)],
        compiler_params=_params("arbitrary"),
        name=name,
    )(hf, hb, aux, w_out, xs, mods, g_ffn, w_router_pieces, b_router_col)


def _rproj_kernel(x_ref, g_ref, mod_ref, w_ref, cw_ref, cb_ref, gy_ref, xc_ref, ext_ref,
                  *, tiles_per_batch, ctx_tiles):
    tm, d = x_ref.shape
    i = pl.program_id(0)

    def in_segment_neighbours(t):
        j = t % tiles_per_batch
        first = jnp.logical_or(j == 0, j == ctx_tiles)
        last = jnp.logical_or(j == ctx_tiles - 1, j == tiles_per_batch - 1)
        return jnp.logical_not(first), jnp.logical_not(last)

    @pl.when(i == 0)
    def _():
        ext_ref[...] = jnp.zeros_like(ext_ref)

    h = _modulated(x_ref[...], g_ref[...], mod_ref[...], 0).astype(BF16)
    gy_ref[...] = jnp.dot(h, w_ref[:, 0:d], preferred_element_type=F32).astype(BF16)
    cur = jnp.dot(h, w_ref[:, d:2 * d], preferred_element_type=F32)

    xc = cb_ref[...] + ext_ref[pl.ds(SUBLANES - CONV_LEFT, tm), :] * cw_ref[0:1, :]
    for k in range(1, CONV_TAPS):
        xc = xc + ext_ref[pl.ds(SUBLANES - CONV_LEFT + k, tm), :] * cw_ref[k:k + 1, :]
    xc_ref[...] = xc
    _, next_ok = in_segment_neighbours(jnp.maximum(i - 1, 0))
    ahead = jnp.where(next_ok, cur[0:1, :] * cw_ref[CONV_TAPS - 1:CONV_TAPS, :], 0.0)
    xc_ref[tm - 1:tm, :] = xc[tm - 1:tm, :] + ahead

    prev_ok, _ = in_segment_neighbours(i)
    ext_ref[0:SUBLANES, :] = jnp.where(prev_ok, ext_ref[tm:tm + SUBLANES, :], 0.0)
    ext_ref[SUBLANES:SUBLANES + tm, :] = cur


def _rproj_call(xs, g, mods, layer, w_in, conv_w, conv_b, dims):
    tm, d, n, tiles = TOKEN_TILE, dims["d"], dims["n"], dims["tiles"]
    tpb, ct, b = dims["tpb"], dims["ct"], dims["b"]
    cur = lambda i: (jnp.minimum(i, tiles - 1), 0)
    lag = lambda i: (jnp.maximum(i - 1, 0), 0)
    const = lambda i: (0, 0)
    mod_spec = pl.BlockSpec((None, None, 1, 6 * d),
                            lambda i: (layer, _mod_row(jnp.minimum(i, tiles - 1), tpb, ct, b), 0, 0))
    return pl.pallas_call(
        functools.partial(_rproj_kernel, tiles_per_batch=tpb, ctx_tiles=ct),
        grid=(tiles + 1,),
        in_specs=[pl.BlockSpec((tm, d), cur), pl.BlockSpec((1, d), const), mod_spec,
                  pl.BlockSpec(w_in.shape, const), pl.BlockSpec(conv_w.shape, const),
                  pl.BlockSpec(conv_b.shape, const)],
        out_specs=[pl.BlockSpec((tm, d), cur), pl.BlockSpec((tm, d), lag)],
        out_shape=[jax.ShapeDtypeStruct((n, d), BF16), jax.ShapeDtypeStruct((n, d), F32)],
        scratch_shapes=[pltpu.VMEM((tm + 2 * SUBLANES, d), F32)],
        compiler_params=_params("arbitrary"),
        name="rglru_proj",
    )(xs, g, mods, w_in, conv_w, conv_b)


def _lru_direction(xc_ref, wg_ref, bg_ref, lam_ref, a_ref, b_ref, *, direction):
    tm, d = xc_ref.shape
    bw = d // LRU_BLOCKS
    z = -lam_ref[direction]
    softplus = jnp.maximum(z, 0.0) + jnp.log(1.0 + jnp.exp(-jnp.abs(z)))
    k = (-0.5 * LRU_C) * softplus
    for nb in range(LRU_BLOCKS):
        sl = slice(nb * bw, (nb + 1) * bw)
        xb = xc_ref[:, sl]
        gg = jnp.dot(xb.astype(BF16), wg_ref[direction, nb], preferred_element_type=F32)
        t = jnp.tanh(gg + bg_ref[direction, :, nb * 2 * bw:(nb + 1) * 2 * bw])
        log_a = k[:, sl] * t[:, :bw] + k[:, sl]
        th = jnp.tanh(log_a)
        a_ref[pl.ds(nb, tm, stride=LRU_BLOCKS), :] = jnp.exp(log_a)
        b_ref[pl.ds(nb, tm, stride=LRU_BLOCKS), :] = (jnp.sqrt(-0.5 * th / (1.0 - th))
                                                      * ((t[:, bw:] + 1.0) * xb))


def _lru_kernel(xc_f, xc_b, wg_ref, bg_ref, lam_ref, of_ref, ob_ref, af_ref, bf_ref, ab_ref, bb_ref,
                hf_ref, hb_ref, h_ref, *, tiles_per_batch):
    tm, d = xc_f.shape
    bw = d // LRU_BLOCKS

    @pl.when(pl.program_id(0) % tiles_per_batch == 0)
    def _():
        h_ref[...] = jnp.zeros_like(h_ref)

    def scan(a_ref, b_ref, o_ref, state, order):
        h = h_ref[state]
        for t in order:
            rows = slice(t * LRU_BLOCKS, (t + 1) * LRU_BLOCKS)
            h = a_ref[rows, :] * h + b_ref[rows, :]
            o_ref[rows, :] = h
        h_ref[state] = h

    _lru_direction(xc_f, wg_ref, bg_ref, lam_ref, af_ref, bf_ref, direction=0)
    scan(af_ref, bf_ref, hf_ref, 0, range(tm))
    _lru_direction(xc_b, wg_ref, bg_ref, lam_ref, ab_ref, bb_ref, direction=1)
    scan(ab_ref, bb_ref, hb_ref, 1, range(tm - 1, -1, -1))
    for nb in range(LRU_BLOCKS):
        sl = slice(nb * bw, (nb + 1) * bw)
        of_ref[:, sl] = hf_ref[pl.ds(nb, tm, stride=LRU_BLOCKS), :].astype(BF16)
        ob_ref[:, sl] = hb_ref[pl.ds(nb, tm, stride=LRU_BLOCKS), :].astype(BF16)


def _lru_call(xc, w_gate, b_gate, lam, dims):
    tm, d, n = TOKEN_TILE, dims["d"], dims["n"]
    tpb, ct = dims["tpb"], dims["ct"]
    fwd = lambda i: (i, 0)
    bwd = lambda i: (_bwd_tile(i, tpb, ct), 0)
    c3 = lambda i: (0, 0, 0)
    c4 = lambda i: (0, 0, 0, 0)
    time_major = pltpu.VMEM((tm * LRU_BLOCKS, d // LRU_BLOCKS), F32)
    return pl.pallas_call(
        functools.partial(_lru_kernel, tiles_per_batch=tpb),
        grid=(dims["tiles"],),
        in_specs=[
            pl.BlockSpec((tm, d), fwd), pl.BlockSpec((tm, d), bwd),
            pl.BlockSpec(w_gate.shape, c4), pl.BlockSpec(b_gate.shape, c3), pl.BlockSpec(lam.shape, c3),
        ],
        out_specs=[pl.BlockSpec((tm, d), fwd), pl.BlockSpec((tm, d), bwd)],
        out_shape=[jax.ShapeDtypeStruct((n, d), BF16), jax.ShapeDtypeStruct((n, d), BF16)],
        scratch_shapes=[time_major] * 6 + [pltpu.VMEM((2, LRU_BLOCKS, d // LRU_BLOCKS), F32)],
        compiler_params=_params("arbitrary"),
        name="rglru_scan",
    )(xc, xc, w_gate, b_gate, lam)


def _first_argmax(vals):
    best = vals[0]
    idx = jnp.zeros_like(best)
    for k in range(1, len(vals)):
        better = vals[k] > best
        idx = jnp.where(better, float(k), idx)
        best = jnp.where(better, vals[k], best)
    return idx, best


def _pick(idx, vals):
    out = vals[0]
    for k in range(1, len(vals)):
        out = jnp.where(idx == float(k), vals[k], out)
    return out


def _route_tile(x, g_ref, mod_ref, wr_ref, br_ref, route_ref, cnt_ref, carry_ref):
    tm = x.shape[0]
    h = _modulated(x, g_ref[...], mod_ref[...], 3)
    hi = h.astype(BF16)
    lo = (h - hi.astype(F32)).astype(BF16)
    logits_tok = (jnp.dot(lo, wr_ref[0], preferred_element_type=F32)
                  + jnp.dot(hi, wr_ref[1], preferred_element_type=F32)
                  + jnp.dot(hi, wr_ref[0], preferred_element_type=F32))
    logits = logits_tok.T[0:N_EXPERTS, :]
    e = jnp.exp(logits - jnp.max(logits, axis=0, keepdims=True))
    scores = e / jnp.sum(e, axis=0, keepdims=True)
    sel = scores + br_ref[...]
    sel_rows = [sel[k:k + 1, :] for k in range(N_EXPERTS)]
    group_scores = []
    for gi in range(N_GROUPS):
        v = sel_rows[gi * GROUP_SIZE:(gi + 1) * GROUP_SIZE]
        best = v[0] + v[1]
        for a, b in PAIRS[1:]:
            best = jnp.maximum(best, v[a] + v[b])
        group_scores.append(best)
    grp, _ = _first_argmax(group_scores)
    in_sel = [_pick(grp, [sel_rows[gi * GROUP_SIZE + k] for gi in range(N_GROUPS)])
              for k in range(GROUP_SIZE)]
    i1, _ = _first_argmax(in_sel)
    rest = [jnp.where(i1 == float(k), -jnp.inf, in_sel[k]) for k in range(GROUP_SIZE)]
    i2, _ = _first_argmax(rest)
    lo = jnp.minimum(i1, i2)
    hi = jnp.maximum(i1, i2)
    pair = jnp.zeros_like(lo)
    for k, (slot_a, slot_b) in enumerate(PAIR_SLOTS):
        here = jnp.logical_and(lo == float(min(slot_a, slot_b)), hi == float(max(slot_a, slot_b)))
        pair = jnp.where(here, float(k), pair)
    cls = grp * float(len(PAIR_SLOTS)) + pair
    crow = lax.broadcasted_iota(jnp.int32, (CLASS_ROWS, tm), 0).astype(F32)
    onehot = jnp.where(crow == cls, 1.0, 0.0)
    rows = lax.broadcasted_iota(jnp.int32, (tm, tm), 0)
    cols = lax.broadcasted_iota(jnp.int32, (tm, tm), 1)
    upper = jnp.where(rows <= cols, 1.0, 0.0).astype(BF16)
    cum = jnp.dot(onehot.astype(BF16), upper, preferred_element_type=F32)
    carry = carry_ref[:, 0:1]
    rank = jnp.sum(onehot * (cum - 1.0 + carry), axis=0, keepdims=True)
    new_carry = carry + jnp.sum(onehot, axis=1, keepdims=True)
    carry_ref[...] = jnp.broadcast_to(new_carry, carry_ref.shape)
    cnt_ref[...] = jnp.broadcast_to(new_carry, cnt_ref.shape)
    zero = jnp.zeros_like(cls)
    route_ref[...] = jnp.concatenate([cls, rank, zero, zero, zero, zero, zero, zero], axis=0)


def _router_weights(w_router):
    w = jnp.pad(w_router.astype(F32), ((0, 0), (0, LANES - w_router.shape[1])))
    hi = w.astype(BF16)
    lo = (w - hi.astype(F32)).astype(BF16)
    return jnp.stack([hi, lo])


ROW_DMA_UNROLL = 8


def _dispatch_kernel(pos_ref, x_ref, g_ref, mod_ref, init_ref, o_ref, stage_ref, sems):
    del init_ref
    tokens, d = x_ref.shape
    per = d // LANES
    i = pl.program_id(0)
    slot = i % 2
    base = i * tokens
    _store_token_tiles(stage_ref.at[slot], _modulated(x_ref[...], g_ref[...], mod_ref[...], 3))

    def issue(r, carry):
        src = pl.multiple_of(r * per, per)
        dst = pl.multiple_of(pos_ref[base + r], per)
        pltpu.make_async_copy(stage_ref.at[slot, pl.ds(src, per)], o_ref.at[pl.ds(dst, per)],
                              sems.at[slot]).start()
        return carry

    lax.fori_loop(0, tokens, issue, 0, unroll=ROW_DMA_UNROLL)

    def wait_tile(s):
        pltpu.make_async_copy(stage_ref.at[s], o_ref.at[pl.ds(0, tokens * per)], sems.at[s]).wait()

    @pl.when(i > 0)
    def _():
        wait_tile(1 - slot)

    @pl.when(i == pl.num_programs(0) - 1)
    def _():
        wait_tile(slot)


def _dispatch_call(pos_rows, xs, g, mods, layer, init, dims):
    tm, d = TOKEN_TILE, dims["d"]
    per = d // LANES
    tpb, ct, b = dims["tpb"], dims["ct"], dims["b"]
    return pl.pallas_call(
        _dispatch_kernel,
        grid_spec=pltpu.PrefetchScalarGridSpec(
            num_scalar_prefetch=1,
            grid=(dims["tiles"],),
            in_specs=[pl.BlockSpec((tm, d), lambda i, p: (i, 0)),
                      pl.BlockSpec((1, d), lambda i, p: (0, 0)),
                      pl.BlockSpec((None, None, 1, 6 * d),
                                   lambda i, p: (layer, _mod_row(i, tpb, ct, b), 0, 0)),
                      pl.BlockSpec(memory_space=pl.ANY)],
            out_specs=pl.BlockSpec(memory_space=pl.ANY),
            scratch_shapes=[pltpu.VMEM((2, tm * per, LANES), F32), pltpu.SemaphoreType.DMA((2,))],
        ),
        out_shape=jax.ShapeDtypeStruct(init.shape, F32),
        input_output_aliases={4: 0},
        compiler_params=_params("arbitrary"),
        name="dispatch",
    )(pos_rows, xs, g, mods, init)


def _expert_kernel(ea_ref, eb_ref, ok_ref, x_ref, wr_ref, w1a, w3a, w2a, w1b, w3b, w2b, o_ref):
    d = w1a.shape[0]
    te = x_ref.shape[0] // (d // LANES)
    t = pl.program_id(0)

    @pl.when(ok_ref[t] != 0)
    def _():
        h = _load_token_tiles(x_ref, te, d)
        x = h.astype(BF16)
        dw = wr_ref[pl.ds(ea_ref[t], 1), :] - wr_ref[pl.ds(eb_ref[t], 1), :]
        gate_a = jax.nn.sigmoid(jnp.sum(h * dw, axis=1, keepdims=True))
        gate_b = 1.0 - gate_a

        def expert(w1, w3, w2):
            a = jnp.dot(x, w1[...].astype(BF16), preferred_element_type=F32)
            b = jnp.dot(x, w3[...].astype(BF16), preferred_element_type=F32)
            u = (a * jax.nn.sigmoid(a)) * b
            return jnp.dot(u.astype(BF16), w2[...].astype(BF16), preferred_element_type=F32)

        _store_token_tiles(o_ref, gate_a * expert(w1a, w3a, w2a) + gate_b * expert(w1b, w3b, w2b))

    @pl.when(ok_ref[t] == 0)
    def _():
        o_ref[...] = jnp.zeros_like(o_ref)


def _expert_call(tile_a, tile_b, tile_ok, xsorted, w_router_t, w1, w3, w2, layer, d):
    te = TOKEN_TILE
    per = d // LANES
    n_tiles = xsorted.shape[0] // (te * per)
    de = w1.shape[-1]
    sel_a = lambda t, ea, eb, ok: (layer, ea[t], 0, 0)
    sel_b = lambda t, ea, eb, ok: (layer, eb[t], 0, 0)
    used = lambda t, ea, eb, ok: (t, 0)
    return pl.pallas_call(
        _expert_kernel,
        grid_spec=pltpu.PrefetchScalarGridSpec(
            num_scalar_prefetch=3,
            grid=(n_tiles,),
            in_specs=[
                pl.BlockSpec((te * per, LANES), used),
                pl.BlockSpec(w_router_t.shape, lambda t, ea, eb, ok: (0, 0)),
                pl.BlockSpec((None, None, d, de), sel_a), pl.BlockSpec((None, None, d, de), sel_a),
                pl.BlockSpec((None, None, de, d), sel_a),
                pl.BlockSpec((None, None, d, de), sel_b), pl.BlockSpec((None, None, d, de), sel_b),
                pl.BlockSpec((None, None, de, d), sel_b),
            ],
            out_specs=pl.BlockSpec((te * per, LANES), used),
        ),
        out_shape=jax.ShapeDtypeStruct((n_tiles * te * per, LANES), F32),
        compiler_params=_params("arbitrary"),
        name="experts",
    )(tile_a, tile_b, tile_ok, xsorted, w_router_t, w1, w3, w2, w1, w3, w2)


def _combine_kernel(pos_ref, y_ref, x_ref, mod_ref, gf_ref, o_ref, buf_ref, sems,
                    *, tile_of, final_norm):
    tm, d = x_ref.shape
    per = d // LANES
    i = pl.program_id(0)
    slot = i % 2

    def gather(step, s):
        base = tile_of(step) * tm

        def issue(r, carry):
            src = pl.multiple_of(pos_ref[base + r], per)
            dst = pl.multiple_of(r * per, per)
            pltpu.make_async_copy(y_ref.at[pl.ds(src, per)], buf_ref.at[s, pl.ds(dst, per)],
                                  sems.at[s]).start()
            return carry

        lax.fori_loop(0, tm, issue, 0, unroll=ROW_DMA_UNROLL)

    @pl.when(i == 0)
    def _():
        gather(0, 0)

    @pl.when(i + 1 < pl.num_programs(0))
    def _():
        gather(i + 1, 1 - slot)

    pltpu.make_async_copy(y_ref.at[pl.ds(0, tm * per)], buf_ref.at[slot], sems.at[slot]).wait()
    out = x_ref[...] + mod_ref[:, 5 * d:6 * d] * _load_token_tiles(buf_ref.at[slot], tm, d)
    if final_norm:
        ms = jnp.mean(out * out, axis=-1, keepdims=True)
        out = out * lax.rsqrt(ms + NORM_EPS) * gf_ref[...]
    o_ref[...] = out


def _combine_call(pos, ysorted, xs, mods, layer, g_final, dims, final_norm):
    tm, d = TOKEN_TILE, dims["d"]
    tpb, ct, b = dims["tpb"], dims["ct"], dims["b"]
    if final_norm:
        lt = tpb - ct
        tile_of = lambda i: (i // lt) * tpb + ct + i % lt
        n_tiles = b * lt
    else:
        tile_of = lambda i: i
        n_tiles = dims["tiles"]
    mod_spec = pl.BlockSpec((None, None, 1, 6 * d),
                            lambda i, p: (layer, _mod_row(tile_of(i), tpb, ct, b), 0, 0))
    return pl.pallas_call(
        functools.partial(_combine_kernel, tile_of=tile_of, final_norm=final_norm),
        grid_spec=pltpu.PrefetchScalarGridSpec(
            num_scalar_prefetch=1,
            grid=(n_tiles,),
            in_specs=[
                pl.BlockSpec(memory_space=pl.ANY),
                pl.BlockSpec((tm, d), lambda i, p: (tile_of(i), 0)),
                mod_spec,
                pl.BlockSpec((1, d), lambda i, p: (0, 0)),
            ],
            out_specs=pl.BlockSpec((tm, d), lambda i, p: (i, 0)),
            scratch_shapes=[pltpu.VMEM((2, tm * (d // LANES), LANES), F32), pltpu.SemaphoreType.DMA((2,))],
        ),
        out_shape=jax.ShapeDtypeStruct((n_tiles * tm, d), F32),
        compiler_params=_params("arbitrary"),
        name="combine",
    )(pos, ysorted, xs, mods, g_final)


_CLASS_A = np.array([g * GROUP_SIZE + a for g in range(N_GROUPS) for a, _ in PAIR_SLOTS], np.int32)
_CLASS_B = np.array([g * GROUP_SIZE + b for g in range(N_GROUPS) for _, b in PAIR_SLOTS], np.int32)


def _routing_plan(route, counts, n_tiles_padded):
    te = TOKEN_TILE
    cls = route[0].astype(jnp.int32)
    rank = route[1].astype(jnp.int32)
    cnt = counts[:N_CLASSES, 0].astype(jnp.int32)
    tiles = (cnt + te - 1) // te
    tile_end = jnp.cumsum(tiles)
    tile_start = tile_end - tiles
    onehot = (cls[:, None] == jnp.arange(N_CLASSES, dtype=jnp.int32)[None, :]).astype(jnp.int32)
    pos = jnp.sum(onehot * (tile_start * te)[None, :], axis=1) + rank
    t = jnp.arange(n_tiles_padded, dtype=jnp.int32)
    ok = (t < tile_end[-1]).astype(jnp.int32)
    tcls = jnp.sum((tile_end[None, :] <= jnp.minimum(t, tile_end[-1] - 1)[:, None]).astype(jnp.int32), axis=1)
    tile_a = jnp.take(jnp.asarray(_CLASS_A), tcls)
    tile_b = jnp.take(jnp.asarray(_CLASS_B), tcls)
    return pos, tile_a, tile_b, ok


def _moe(xs, g_ffn, route, counts, sorted_buf, mods, layer, w_router_t, w1, w3, w2, g_final, dims, final_norm):
    per = dims["d"] // LANES
    n_tiles_padded = sorted_buf.shape[0] // (TOKEN_TILE * per)
    pos, tile_a, tile_b, tile_ok = _routing_plan(route, counts, n_tiles_padded)
    xsorted = _dispatch_call(pos * per, xs, g_ffn, mods, layer, sorted_buf, dims)
    ysorted = _expert_call(tile_a, tile_b, tile_ok, xsorted, w_router_t, w1, w3, w2, layer, dims["d"])
    return _combine_call(pos * per, ysorted, xs, mods, layer, g_final, dims, final_norm), xsorted


def _pos_table(n_tokens, dim):
    rows = n_tokens // GRID_WIDTH
    r, col = jnp.meshgrid(jnp.arange(rows, dtype=F32), jnp.arange(GRID_WIDTH, dtype=F32), indexing="ij")
    quarter = dim // 4
    freqs = jnp.exp(-math.log(POS_BASE) * jnp.arange(quarter, dtype=F32) / quarter)

    def enc(p):
        ang = p.reshape(-1, 1) * freqs
        return jnp.concatenate([jnp.sin(ang), jnp.cos(ang)], axis=-1)

    return jnp.concatenate([enc(r), enc(col)], axis=-1)


def _mlstm_weights(w_in, b_gate):
    nq = HEADS * HEAD_QK
    nv = HEADS * HEAD_V
    wq = w_in[:, 0:nq] * (HEAD_QK ** -0.5)
    wk = w_in[:, nq:2 * nq]
    wv = w_in[:, 2 * nq:2 * nq + nv]
    wo = w_in[:, 2 * nq + nv:2 * nq + 2 * nv]
    wg = w_in[:, 2 * nq + 2 * nv:]
    wg_pad = jnp.pad(wg, ((0, 0), (0, LANES - wg.shape[1])))
    w_tok = jnp.concatenate([wq, wk, wg_pad], axis=1).astype(BF16)
    w_feat = jnp.concatenate([wv, wo, wg], axis=1).T.astype(BF16)
    bias = b_gate.reshape(-1).astype(F32)
    bias_c = jnp.pad(bias, (0, LANES - bias.shape[0])).reshape(1, LANES)
    bias_r = bias.reshape(-1, 1)
    return w_tok, w_feat, bias_c, bias_r


def kernel(x, c, ctx, c_ctx, w_ada, b_ada, g_mix, g_ffn, g_final, m_w_in, m_b_gate, m_g_head, m_w_out, r_w_in, r_conv_w, r_conv_b, r_w_gate, r_b_gate, r_lam, r_w_out, w_router, b_router, e_w1, e_w3, e_w2):
    batch, t_len, d = x.shape
    ctx_len = ctx.shape[1]
    depth = w_ada.shape[0]
    tm = TOKEN_TILE
    assert t_len % tm == 0 and ctx_len % tm == 0 and batch + 1 <= ADA_ROWS
    assert d == HEADS * HEAD_V and t_len % GRID_WIDTH == 0
    s_len = ctx_len + t_len
    dims = dict(b=batch, d=d, n=batch * s_len, tpb=s_len // tm, ct=ctx_len // tm,
                tiles=batch * s_len // tm)

    cvec = jnp.concatenate([c, c_ctx[None, :], jnp.zeros((ADA_ROWS - batch - 1, d), F32)], axis=0)
    mods = _ada_call(cvec, w_ada, b_ada).reshape(depth, ADA_ROWS, 1, 6 * d)
    xs = (ctx.reshape(batch * ctx_len, d), x.reshape(batch * t_len, d), _pos_table(t_len, d))

    w_router_t = w_router.T.astype(F32)
    w_router_pieces = _router_weights(w_router)
    b_router_col = b_router.reshape(-1, 1).astype(F32)
    g_final2 = g_final.reshape(1, d)
    sorted_buf = jnp.zeros(((dims["tiles"] + N_CLASSES) * tm * (d // LANES), LANES), F32)
    out = None
    for i in range(depth):
        j = i // 2
        g_mix_i = g_mix[i].reshape(1, d)
        g_ffn_i = g_ffn[i].reshape(1, d)
        if i % 2 == 0:
            w_tok, w_feat, bias_c, bias_r = _mlstm_weights(m_w_in[j], m_b_gate[j])
            proj = _mproj_call(xs, g_mix_i, mods, i, w_tok, w_feat, dims)
            if isinstance(xs, tuple):
                xs, proj = proj[0], proj[1:]
            q, k, vt, ot, gc, gr = proj
            hf, hb = _mlstm_call(q, k, vt, gc, gr, bias_c, bias_r, dims)
            w_out = (m_g_head[j][:, None] * m_w_out[j]).astype(BF16)
            xs, route, counts = _readout_call(_mreadout_kernel, "mlstm_readout", True, hf, hb, ot, w_out,
                                              xs, mods, i, g_ffn_i, w_router_pieces, b_router_col, dims)
        else:
            gy, xc = _rproj_call(xs, g_mix_i, mods, i, r_w_in[j].astype(BF16), r_conv_w[j],
                                 r_conv_b[j].reshape(1, d), dims)
            hf, hb = _lru_call(xc, (0.5 * r_w_gate[j]).astype(BF16), 0.5 * r_b_gate[j].reshape(2, 1, -1),
                               r_lam[j].reshape(2, 1, d), dims)
            xs, route, counts = _readout_call(_readout_kernel, "rglru_readout", False, hf, hb, gy,
                                              r_w_out[j].astype(BF16), xs, mods, i, g_ffn_i,
                                              w_router_pieces, b_router_col, dims)
        last = i == depth - 1
        res, sorted_buf = _moe(xs, g_ffn_i, route, counts, sorted_buf, mods, i, w_router_t,
                               e_w1, e_w3, e_w2, g_final2, dims, last)
        if last:
            out = res
        else:
            xs = res
    return out.reshape(batch, t_len, d)
```

```python
import functools
import math

import jax
import jax.numpy as jnp
import numpy as np
from jax import lax
from jax.experimental import pallas as pl
from jax.experimental.pallas import tpu as pltpu

F32 = jnp.float32
BF16 = jnp.bfloat16
HIGHEST = lax.Precision.HIGHEST

TOKEN_TILE = 256
LANES = 128
SUBLANES = 8
VMEM_LIMIT_BYTES = 56 * 1024 * 1024
NORM_EPS = 1e-6
GRID_WIDTH = 64
POS_BASE = 10000.0
HEADS = 8
HEAD_QK = 64
HEAD_V = 128
LRU_BLOCKS = 8
LRU_C = 8.0
CONV_TAPS = 4
CONV_LEFT = CONV_TAPS // 2
N_EXPERTS = 16
N_GROUPS = 4
GROUP_SIZE = N_EXPERTS // N_GROUPS
PAIRS = [(a, b) for a in range(GROUP_SIZE) for b in range(a + 1, GROUP_SIZE)]
PAIR_SLOTS = [(0, 1), (0, 2), (0, 3), (1, 3), (1, 2), (3, 2)]
N_CLASSES = N_GROUPS * len(PAIRS)
CLASS_ROWS = 32
ROUTE_COLS = LANES
ADA_ROWS = 16


def _params(*sem):
    return pltpu.CompilerParams(dimension_semantics=sem, vmem_limit_bytes=VMEM_LIMIT_BYTES)


def _mod_row(i, tiles_per_batch, ctx_tiles, batch):
    return jnp.where(i % tiles_per_batch < ctx_tiles, batch, i // tiles_per_batch)


def _bwd_tile(i, tiles_per_batch, ctx_tiles):
    b = i // tiles_per_batch
    j = i % tiles_per_batch
    jb = jnp.where(j < ctx_tiles, ctx_tiles - 1 - j, tiles_per_batch - 1 - (j - ctx_tiles))
    return b * tiles_per_batch + jb


def _ada_kernel(c_ref, w_ref, b_ref, o_ref):
    c = c_ref[...]
    s = c * jax.nn.sigmoid(c)
    o_ref[...] = jnp.dot(s, w_ref[...], preferred_element_type=F32, precision=HIGHEST) + b_ref[...]


def _ada_call(cvec, w_ada, b_ada):
    depth, d, n = w_ada.shape
    tn = n // 4
    return pl.pallas_call(
        _ada_kernel,
        grid=(depth, n // tn),
        in_specs=[
            pl.BlockSpec((ADA_ROWS, d), lambda l, j: (0, 0)),
            pl.BlockSpec((None, d, tn), lambda l, j: (l, 0, j)),
            pl.BlockSpec((None, 1, tn), lambda l, j: (l, 0, j)),
        ],
        out_specs=pl.BlockSpec((None, ADA_ROWS, tn), lambda l, j: (l, 0, j)),
        out_shape=jax.ShapeDtypeStruct((depth, ADA_ROWS, n), F32),
        compiler_params=_params("arbitrary", "arbitrary"),
        name="ada",
    )(cvec, w_ada, b_ada.reshape(depth, 1, n))


def _modulated(x, g, mod, shift_idx):
    d = x.shape[-1]
    ms = jnp.mean(x * x, axis=-1, keepdims=True)
    xn = x * lax.rsqrt(ms + NORM_EPS) * g
    shift = mod[:, shift_idx * d:(shift_idx + 1) * d]
    scale = mod[:, (shift_idx + 1) * d:(shift_idx + 2) * d]
    return xn * (1.0 + scale) + shift


def _mod_spec(layer, dims):
    tpb, ct, b = dims["tpb"], dims["ct"], dims["b"]
    return pl.BlockSpec((None, None, 1, 6 * dims["d"]),
                        lambda i, *_: (layer, _mod_row(i, tpb, ct, b), 0, 0))


def _store_token_tiles(ref, x):
    tokens, d = x.shape
    per = d // LANES
    for s in range(per):
        ref[pl.ds(s, tokens, stride=per), :] = x[:, s * LANES:(s + 1) * LANES]


def _load_token_tiles(ref, tokens, d):
    per = d // LANES
    return jnp.concatenate([ref[pl.ds(s, tokens, stride=per), :] for s in range(per)], axis=1)


def _gelu_tanh(y):
    return 0.5 * y * (1.0 + jnp.tanh(math.sqrt(2.0 / math.pi) * (y + 0.044715 * (y * y * y))))


SRC_BLOCK = 128
_NT = (((1,), (1,)), ((), ()))
_TN = (((0,), (0,)), ((), ()))


def _mproj_kernel(*refs, assemble, tiles_per_batch, ctx_tiles):
    if assemble:
        ctx_ref, lat_ref, pos_ref, g_ref, mod_ref, w_ref, wt_ref, xs_ref = refs[:8]
        is_ctx = pl.program_id(0) % tiles_per_batch < ctx_tiles
        x = jnp.where(is_ctx, ctx_ref[...], lat_ref[...] + pos_ref[...])
        xs_ref[...] = x
    else:
        x_ref, g_ref, mod_ref, w_ref, wt_ref = refs[:5]
        x = x_ref[...]
    q_ref, k_ref, vt_ref, ot_ref, gc_ref, gr_ref = refs[-6:]
    d = x.shape[-1]
    nq = q_ref.shape[-1]
    h = _modulated(x, g_ref[...], mod_ref[...], 0).astype(BF16)
    q_ref[...] = jnp.dot(h, w_ref[:, 0:nq], preferred_element_type=F32).astype(BF16)
    k_ref[...] = jnp.dot(h, w_ref[:, nq:2 * nq], preferred_element_type=F32).astype(BF16)
    gc_ref[...] = jnp.dot(h, w_ref[:, 2 * nq:2 * nq + LANES], preferred_element_type=F32)
    vt_ref[...] = lax.dot_general(wt_ref[0:d, :], h, _NT, preferred_element_type=F32).astype(BF16)
    ot = lax.dot_general(wt_ref[d:2 * d, :], h, _NT, preferred_element_type=F32)
    ot_ref[...] = jax.nn.sigmoid(ot).astype(BF16)
    gr_ref[...] = lax.dot_general(wt_ref[2 * d:2 * d + 4 * HEADS, :], h, _NT, preferred_element_type=F32)


def _mproj_call(stream, g, mods, layer, w_tok, w_feat, dims):
    tm, d, n = TOKEN_TILE, dims["d"], dims["n"]
    tpb, ct = dims["tpb"], dims["ct"]
    lt = tpb - ct
    nq = HEADS * HEAD_QK
    row = lambda i: (i, 0)
    col = lambda i: (0, i)
    const = lambda i: (0, 0)
    assemble = isinstance(stream, tuple)
    if assemble:
        stream_specs = [
            pl.BlockSpec((tm, d), lambda i: ((i // tpb) * ct + jnp.minimum(i % tpb, ct - 1), 0)),
            pl.BlockSpec((tm, d), lambda i: ((i // tpb) * lt + jnp.maximum(i % tpb - ct, 0), 0)),
            pl.BlockSpec((tm, d), lambda i: (jnp.maximum(i % tpb - ct, 0), 0)),
        ]
        stream_out = [pl.BlockSpec((tm, d), row)], [jax.ShapeDtypeStruct((n, d), F32)]
    else:
        stream, stream_specs, stream_out = (stream,), [pl.BlockSpec((tm, d), row)], ([], [])
    return pl.pallas_call(
        functools.partial(_mproj_kernel, assemble=assemble, tiles_per_batch=tpb, ctx_tiles=ct),
        grid=(dims["tiles"],),
        in_specs=stream_specs + [
            pl.BlockSpec((1, d), const),
            _mod_spec(layer, dims),
            pl.BlockSpec(w_tok.shape, const),
            pl.BlockSpec(w_feat.shape, const),
        ],
        out_specs=stream_out[0] + [
            pl.BlockSpec((tm, nq), row),
            pl.BlockSpec((tm, nq), row),
            pl.BlockSpec((d, tm), col),
            pl.BlockSpec((d, tm), col),
            pl.BlockSpec((tm, LANES), row),
            pl.BlockSpec((4 * HEADS, tm), col),
        ],
        out_shape=stream_out[1] + [
            jax.ShapeDtypeStruct((n, nq), BF16),
            jax.ShapeDtypeStruct((n, nq), BF16),
            jax.ShapeDtypeStruct((d, n), BF16),
            jax.ShapeDtypeStruct((d, n), BF16),
            jax.ShapeDtypeStruct((n, LANES), F32),
            jax.ShapeDtypeStruct((4 * HEADS, n), F32),
        ],
        compiler_params=_params("arbitrary"),
        name="mlstm_proj",
    )(*stream, g, mods, w_tok, w_feat)


def _mlstm_gate_stats(gc_ref, gr_ref, bc_ref, br_ref, m_ref, *, backward):
    tm = gc_ref.shape[0]
    gate_i = 2 * HEADS if backward else 0
    gate_f = gate_i + HEADS
    last = 0 if backward else tm - 1
    src = lax.broadcasted_iota(jnp.int32, (tm, tm), 0)
    tgt = lax.broadcasted_iota(jnp.int32, (tm, tm), 1)
    visible = (src >= tgt) if backward else (src <= tgt)
    neg_mask = jnp.where(visible, 0.0, -jnp.inf)
    xc = gc_ref[...] + bc_ref[...]
    xr = gr_ref[...] + br_ref[...]
    before = (src <= tgt) if backward else (src >= tgt)
    cum_c = _split_dot(jnp.where(before, 1.0, 0.0).astype(BF16), jax.nn.log_sigmoid(xc), left=True)
    cum_r = _split_dot(jnp.where(visible, 1.0, 0.0).astype(BF16), jax.nn.log_sigmoid(xr), left=False)
    b_row = cum_r[gate_f:gate_f + HEADS, :]
    c_row = xr[gate_i:gate_i + HEADS, :] - b_row
    c_col = xc - pltpu.roll(cum_c, LANES - HEADS, axis=1)
    run_col = c_col
    shift = 1
    while shift < tm:
        fill = jnp.full((shift, LANES), -jnp.inf, F32)
        if backward:
            moved = jnp.concatenate([run_col[shift:, :], fill], axis=0)
        else:
            moved = jnp.concatenate([fill, run_col[:tm - shift, :]], axis=0)
        run_col = jnp.maximum(run_col, moved)
        shift *= 2
    run = run_col.T[gate_i:gate_i + HEADS, :]
    m = m_ref[:, 0:1]
    log_inter = b_row + m
    m_pos = jnp.maximum(log_inter, b_row + run)
    shift_row = b_row - m_pos
    w_inter = jnp.exp(log_inter - m_pos)
    floor = jnp.exp(-m_pos)
    b_last = b_row[:, last:last + 1]
    m_new = m_pos[:, last:last + 1]
    w_src = jnp.exp(b_last + c_row - m_new)
    decay = jnp.exp(b_last + m - m_new)
    m_ref[...] = jnp.broadcast_to(m_new, m_ref.shape)
    return neg_mask, c_col, shift_row, w_inter, floor, w_src, decay


def _mlstm_head(h, q_ref, k_ref, vt_ref, o_ref, ct_ref, stats, *, backward):
    neg_mask, c_col, shift_row, w_inter, floor, w_src, decay = stats
    tm = q_ref.shape[0]
    gate_i = 2 * HEADS if backward else 0
    half = lax.broadcasted_iota(jnp.int32, (tm, LANES), 1) < HEAD_QK
    ones = jnp.ones((2 * SUBLANES, tm), BF16)
    if True:
        pair = slice((h // 2) * LANES, (h // 2 + 1) * LANES)
        qp = q_ref[:, pair]
        km = jnp.where(half if h % 2 == 0 else jnp.logical_not(half), k_ref[:, pair], 0.0).astype(BF16)
        vaug = jnp.concatenate([vt_ref[h * HEAD_V:(h + 1) * HEAD_V, :], ones], axis=0)
        ct = ct_ref[h]
        acc = w_inter[h:h + 1, :] * lax.dot_general(ct.astype(BF16), qp, _NT,
                                                    preferred_element_type=F32)
        src_block = SRC_BLOCK if (h + int(backward)) % 2 == 0 else tm
        for blk in range(tm // src_block):
            rows = slice(blk * src_block, (blk + 1) * src_block)
            st = lax.dot_general(km[rows, :], qp, _NT, preferred_element_type=F32)
            e = (neg_mask[rows, :] + shift_row[h:h + 1, :]) + c_col[rows, gate_i + h:gate_i + h + 1]
            p = (st * jnp.exp(e)).astype(BF16)
            acc = acc + jnp.dot(vaug[:, rows], p, preferred_element_type=F32)
        den = acc[HEAD_V:HEAD_V + 1, :]
        scale = 1.0 / jnp.maximum(jnp.abs(den), floor[h:h + 1, :])
        o_ref[h * HEAD_V:(h + 1) * HEAD_V, :] = (acc[0:HEAD_V, :] * scale).astype(o_ref.dtype)
        vs = (vaug.astype(F32) * w_src[h:h + 1, :]).astype(BF16)
        ct_ref[h] = decay[h:h + 1, :] * ct + jnp.dot(vs, km, preferred_element_type=F32)


def _split_dot(ones_mat, x, left):
    hi = x.astype(BF16)
    r1 = x - hi.astype(F32)
    mid = r1.astype(BF16)
    lo = (r1 - mid.astype(F32)).astype(BF16)
    out = None
    for piece in (lo, mid, hi):
        t = (jnp.dot(ones_mat, piece, preferred_element_type=F32) if left
             else jnp.dot(piece, ones_mat, preferred_element_type=F32))
        out = t if out is None else out + t
    return out


def _mlstm_kernel(qf, kf, vtf, gcf, grf, qb, kb, vtb, gcb, grb, bc_ref, br_ref, of_ref, ob_ref,
                  cf_ref, mf_ref, cb_ref, mb_ref, *, tiles_per_batch):
    @pl.when(pl.program_id(0) % tiles_per_batch == 0)
    def _():
        cf_ref[...] = jnp.zeros_like(cf_ref)
        cb_ref[...] = jnp.zeros_like(cb_ref)
        mf_ref[...] = jnp.zeros_like(mf_ref)
        mb_ref[...] = jnp.zeros_like(mb_ref)

    stats_f = _mlstm_gate_stats(gcf, grf, bc_ref, br_ref, mf_ref, backward=False)
    stats_b = _mlstm_gate_stats(gcb, grb, bc_ref, br_ref, mb_ref, backward=True)
    for h in range(HEADS):
        _mlstm_head(h, qf, kf, vtf, of_ref, cf_ref, stats_f, backward=False)
    for h in range(HEADS):
        _mlstm_head(h, qb, kb, vtb, ob_ref, cb_ref, stats_b, backward=True)


def _mlstm_call(q, k, vt, gc, gr, bias_c, bias_r, dims):
    tm, d, n = TOKEN_TILE, dims["d"], dims["n"]
    tpb, ct = dims["tpb"], dims["ct"]
    nq = q.shape[1]
    fwd = lambda i: (i, 0)
    bwd = lambda i: (_bwd_tile(i, tpb, ct), 0)
    fwd_t = lambda i: (0, i)
    bwd_t = lambda i: (0, _bwd_tile(i, tpb, ct))
    const = lambda i: (0, 0)
    state = [pltpu.VMEM((HEADS, HEAD_V + 2 * SUBLANES, LANES), F32), pltpu.VMEM((HEADS, LANES), F32)]
    return pl.pallas_call(
        functools.partial(_mlstm_kernel, tiles_per_batch=tpb),
        grid=(dims["tiles"],),
        in_specs=[
            pl.BlockSpec((tm, nq), fwd), pl.BlockSpec((tm, nq), fwd), pl.BlockSpec((d, tm), fwd_t),
            pl.BlockSpec((tm, LANES), fwd), pl.BlockSpec((4 * HEADS, tm), fwd_t),
            pl.BlockSpec((tm, nq), bwd), pl.BlockSpec((tm, nq), bwd), pl.BlockSpec((d, tm), bwd_t),
            pl.BlockSpec((tm, LANES), bwd), pl.BlockSpec((4 * HEADS, tm), bwd_t),
            pl.BlockSpec((1, LANES), const), pl.BlockSpec((4 * HEADS, 1), const),
        ],
        out_specs=[pl.BlockSpec((d, tm), fwd_t), pl.BlockSpec((d, tm), bwd_t)],
        out_shape=[jax.ShapeDtypeStruct((d, n), BF16), jax.ShapeDtypeStruct((d, n), BF16)],
        scratch_shapes=state + state,
        compiler_params=_params("arbitrary"),
        name="mlstm_scan",
    )(q, k, vt, gc, gr, q, k, vt, gc, gr, bias_c, bias_r)


def _finish_readout(x_new, mod_ref, g_ref, wr_ref, br_ref, o_ref, route_ref, cnt_ref, carry_ref):
    @pl.when(pl.program_id(0) == 0)
    def _():
        carry_ref[...] = jnp.zeros_like(carry_ref)

    o_ref[...] = x_new
    _route_tile(x_new, g_ref, mod_ref, wr_ref, br_ref, route_ref, cnt_ref, carry_ref)


def _mreadout_kernel(hf_ref, hb_ref, ot_ref, w_ref, x_ref, mod_ref, *route_refs):
    d = x_ref.shape[-1]
    parts = []
    for h in range(HEADS):
        rows = slice(h * HEAD_V, (h + 1) * HEAD_V)
        hs = hf_ref[rows, :].astype(F32) + hb_ref[rows, :].astype(F32)
        r = lax.rsqrt(jnp.mean(hs * hs, axis=0, keepdims=True) + NORM_EPS)
        parts.append((hs * r * ot_ref[rows, :]).astype(BF16))
    t = jnp.concatenate(parts, axis=0)
    out = lax.dot_general(t, w_ref[...], _TN, preferred_element_type=F32)
    _finish_readout(x_ref[...] + mod_ref[:, 2 * d:3 * d] * out, mod_ref, *route_refs)


def _readout_kernel(hf_ref, hb_ref, gy_ref, w_ref, x_ref, mod_ref, *route_refs):
    d = x_ref.shape[-1]
    t = _gelu_tanh(gy_ref[...].astype(F32)) * (hf_ref[...].astype(F32) + hb_ref[...].astype(F32))
    out = jnp.dot(t.astype(BF16), w_ref[...], preferred_element_type=F32)
    _finish_readout(x_ref[...] + mod_ref[:, 2 * d:3 * d] * out, mod_ref, *route_refs)


def _readout_call(kernel_fn, name, feature_major, hf, hb, aux, w_out, xs, mods, layer,
                  g_ffn, w_router_pieces, b_router_col, dims):
    tm, d, n = TOKEN_TILE, dims["d"], dims["n"]
    row = lambda i: (i, 0)
    col = lambda i: (0, i)
    const = lambda i: (0, 0)
    mixer = pl.BlockSpec((d, tm), col) if feature_major else pl.BlockSpec((tm, d), row)
    return pl.pallas_call(
        kernel_fn,
        grid=(dims["tiles"],),
        in_specs=[mixer, mixer, mixer,
                  pl.BlockSpec((d, d), const), pl.BlockSpec((tm, d), row), _mod_spec(layer, dims),
                  pl.BlockSpec((1, d), const),
                  pl.BlockSpec(w_router_pieces.shape, lambda i: (0, 0, 0)),
                  pl.BlockSpec(b_router_col.shape, const)],
        out_specs=[pl.BlockSpec((tm, d), row),
                   pl.BlockSpec((SUBLANES, tm), col),
                   pl.BlockSpec((CLASS_ROWS, LANES), const)],
        out_shape=[jax.ShapeDtypeStruct((n, d), F32),
                   jax.ShapeDtypeStruct((SUBLANES, n), F32),
                   jax.ShapeDtypeStruct((CLASS_ROWS, LANES), F32)],
        scratch_shapes=[pltpu.VMEM((CLASS_ROWS, LANES), F32)],
        compiler_params=_params("arbitrary"),
        name=name,
    )(hf, hb, aux, w_out, xs, mods, g_ffn, w_router_pieces, b_router_col)


def _rproj_kernel(x_ref, g_ref, mod_ref, w_ref, cw_ref, cb_ref, gy_ref, xc_ref, ext_ref,
                  *, tiles_per_batch, ctx_tiles):
    tm, d = x_ref.shape
    i = pl.program_id(0)

    def in_segment_neighbours(t):
        j = t % tiles_per_batch
        first = jnp.logical_or(j == 0, j == ctx_tiles)
        last = jnp.logical_or(j == ctx_tiles - 1, j == tiles_per_batch - 1)
        return jnp.logical_not(first), jnp.logical_not(last)

    @pl.when(i == 0)
    def _():
        ext_ref[...] = jnp.zeros_like(ext_ref)

    h = _modulated(x_ref[...], g_ref[...], mod_ref[...], 0).astype(BF16)
    gy_ref[...] = jnp.dot(h, w_ref[:, 0:d], preferred_element_type=F32).astype(BF16)
    cur = jnp.dot(h, w_ref[:, d:2 * d], preferred_element_type=F32)

    xc = cb_ref[...] + ext_ref[pl.ds(SUBLANES - CONV_LEFT, tm), :] * cw_ref[0:1, :]
    for k in range(1, CONV_TAPS):
        xc = xc + ext_ref[pl.ds(SUBLANES - CONV_LEFT + k, tm), :] * cw_ref[k:k + 1, :]
    xc_ref[...] = xc
    _, next_ok = in_segment_neighbours(jnp.maximum(i - 1, 0))
    ahead = jnp.where(next_ok, cur[0:1, :] * cw_ref[CONV_TAPS - 1:CONV_TAPS, :], 0.0)
    xc_ref[tm - 1:tm, :] = xc[tm - 1:tm, :] + ahead

    prev_ok, _ = in_segment_neighbours(i)
    ext_ref[0:SUBLANES, :] = jnp.where(prev_ok, ext_ref[tm:tm + SUBLANES, :], 0.0)
    ext_ref[SUBLANES:SUBLANES + tm, :] = cur


def _rproj_call(xs, g, mods, layer, w_in, conv_w, conv_b, dims):
    tm, d, n, tiles = TOKEN_TILE, dims["d"], dims["n"], dims["tiles"]
    tpb, ct, b = dims["tpb"], dims["ct"], dims["b"]
    cur = lambda i: (jnp.minimum(i, tiles - 1), 0)
    lag = lambda i: (jnp.maximum(i - 1, 0), 0)
    const = lambda i: (0, 0)
    mod_spec = pl.BlockSpec((None, None, 1, 6 * d),
                            lambda i: (layer, _mod_row(jnp.minimum(i, tiles - 1), tpb, ct, b), 0, 0))
    return pl.pallas_call(
        functools.partial(_rproj_kernel, tiles_per_batch=tpb, ctx_tiles=ct),
        grid=(tiles + 1,),
        in_specs=[pl.BlockSpec((tm, d), cur), pl.BlockSpec((1, d), const), mod_spec,
                  pl.BlockSpec(w_in.shape, const), pl.BlockSpec(conv_w.shape, const),
                  pl.BlockSpec(conv_b.shape, const)],
        out_specs=[pl.BlockSpec((tm, d), cur), pl.BlockSpec((tm, d), lag)],
        out_shape=[jax.ShapeDtypeStruct((n, d), BF16), jax.ShapeDtypeStruct((n, d), F32)],
        scratch_shapes=[pltpu.VMEM((tm + 2 * SUBLANES, d), F32)],
        compiler_params=_params("arbitrary"),
        name="rglru_proj",
    )(xs, g, mods, w_in, conv_w, conv_b)


def _lru_direction(xc_ref, wg_ref, bg_ref, lam_ref, a_ref, b_ref, *, direction):
    tm, d = xc_ref.shape
    bw = d // LRU_BLOCKS
    z = -lam_ref[direction]
    softplus = jnp.maximum(z, 0.0) + jnp.log(1.0 + jnp.exp(-jnp.abs(z)))
    k = (-0.5 * LRU_C) * softplus
    for nb in range(LRU_BLOCKS):
        sl = slice(nb * bw, (nb + 1) * bw)
        xb = xc_ref[:, sl]
        gg = jnp.dot(xb.astype(BF16), wg_ref[direction, nb], preferred_element_type=F32)
        t = jnp.tanh(gg + bg_ref[direction, :, nb * 2 * bw:(nb + 1) * 2 * bw])
        log_a = k[:, sl] * t[:, :bw] + k[:, sl]
        th = jnp.tanh(log_a)
        a_ref[pl.ds(nb, tm, stride=LRU_BLOCKS), :] = jnp.exp(log_a)
        b_ref[pl.ds(nb, tm, stride=LRU_BLOCKS), :] = (jnp.sqrt(-0.5 * th / (1.0 - th))
                                                      * ((t[:, bw:] + 1.0) * xb))


def _lru_kernel(xc_f, xc_b, wg_ref, bg_ref, lam_ref, of_ref, ob_ref, af_ref, bf_ref, ab_ref, bb_ref,
                hf_ref, hb_ref, h_ref, *, tiles_per_batch):
    tm, d = xc_f.shape
    bw = d // LRU_BLOCKS

    @pl.when(pl.program_id(0) % tiles_per_batch == 0)
    def _():
        h_ref[...] = jnp.zeros_like(h_ref)

    def scan(a_ref, b_ref, o_ref, state, order):
        h = h_ref[state]
        for t in order:
            rows = slice(t * LRU_BLOCKS, (t + 1) * LRU_BLOCKS)
            h = a_ref[rows, :] * h + b_ref[rows, :]
            o_ref[rows, :] = h
        h_ref[state] = h

    _lru_direction(xc_f, wg_ref, bg_ref, lam_ref, af_ref, bf_ref, direction=0)
    scan(af_ref, bf_ref, hf_ref, 0, range(tm))
    _lru_direction(xc_b, wg_ref, bg_ref, lam_ref, ab_ref, bb_ref, direction=1)
    scan(ab_ref, bb_ref, hb_ref, 1, range(tm - 1, -1, -1))
    for nb in range(LRU_BLOCKS):
        sl = slice(nb * bw, (nb + 1) * bw)
        of_ref[:, sl] = hf_ref[pl.ds(nb, tm, stride=LRU_BLOCKS), :].astype(BF16)
        ob_ref[:, sl] = hb_ref[pl.ds(nb, tm, stride=LRU_BLOCKS), :].astype(BF16)


def _lru_call(xc, w_gate, b_gate, lam, dims):
    tm, d, n = TOKEN_TILE, dims["d"], dims["n"]
    tpb, ct = dims["tpb"], dims["ct"]
    fwd = lambda i: (i, 0)
    bwd = lambda i: (_bwd_tile(i, tpb, ct), 0)
    c3 = lambda i: (0, 0, 0)
    c4 = lambda i: (0, 0, 0, 0)
    time_major = pltpu.VMEM((tm * LRU_BLOCKS, d // LRU_BLOCKS), F32)
    return pl.pallas_call(
        functools.partial(_lru_kernel, tiles_per_batch=tpb),
        grid=(dims["tiles"],),
        in_specs=[
            pl.BlockSpec((tm, d), fwd), pl.BlockSpec((tm, d), bwd),
            pl.BlockSpec(w_gate.shape, c4), pl.BlockSpec(b_gate.shape, c3), pl.BlockSpec(lam.shape, c3),
        ],
        out_specs=[pl.BlockSpec((tm, d), fwd), pl.BlockSpec((tm, d), bwd)],
        out_shape=[jax.ShapeDtypeStruct((n, d), BF16), jax.ShapeDtypeStruct((n, d), BF16)],
        scratch_shapes=[time_major] * 6 + [pltpu.VMEM((2, LRU_BLOCKS, d // LRU_BLOCKS), F32)],
        compiler_params=_params("arbitrary"),
        name="rglru_scan",
    )(xc, xc, w_gate, b_gate, lam)


def _first_argmax(vals):
    best = vals[0]
    idx = jnp.zeros_like(best)
    for k in range(1, len(vals)):
        better = vals[k] > best
        idx = jnp.where(better, float(k), idx)
        best = jnp.where(better, vals[k], best)
    return idx, best


def _pick(idx, vals):
    out = vals[0]
    for k in range(1, len(vals)):
        out = jnp.where(idx == float(k), vals[k], out)
    return out


def _route_tile(x, g_ref, mod_ref, wr_ref, br_ref, route_ref, cnt_ref, carry_ref):
    tm = x.shape[0]
    h = _modulated(x, g_ref[...], mod_ref[...], 3)
    hi = h.astype(BF16)
    lo = (h - hi.astype(F32)).astype(BF16)
    logits_tok = (jnp.dot(lo, wr_ref[0], preferred_element_type=F32)
                  + jnp.dot(hi, wr_ref[1], preferred_element_type=F32)
                  + jnp.dot(hi, wr_ref[0], preferred_element_type=F32))
    logits = logits_tok.T[0:N_EXPERTS, :]
    e = jnp.exp(logits - jnp.max(logits, axis=0, keepdims=True))
    scores = e / jnp.sum(e, axis=0, keepdims=True)
    sel = scores + br_ref[...]
    sel_rows = [sel[k:k + 1, :] for k in range(N_EXPERTS)]
    group_scores = []
    for gi in range(N_GROUPS):
        v = sel_rows[gi * GROUP_SIZE:(gi + 1) * GROUP_SIZE]
        best = v[0] + v[1]
        for a, b in PAIRS[1:]:
            best = jnp.maximum(best, v[a] + v[b])
        group_scores.append(best)
    grp, _ = _first_argmax(group_scores)
    in_sel = [_pick(grp, [sel_rows[gi * GROUP_SIZE + k] for gi in range(N_GROUPS)])
              for k in range(GROUP_SIZE)]
    i1, _ = _first_argmax(in_sel)
    rest = [jnp.where(i1 == float(k), -jnp.inf, in_sel[k]) for k in range(GROUP_SIZE)]
    i2, _ = _first_argmax(rest)
    lo = jnp.minimum(i1, i2)
    hi = jnp.maximum(i1, i2)
    pair = jnp.zeros_like(lo)
    for k, (slot_a, slot_b) in enumerate(PAIR_SLOTS):
        here = jnp.logical_and(lo == float(min(slot_a, slot_b)), hi == float(max(slot_a, slot_b)))
        pair = jnp.where(here, float(k), pair)
    cls = grp * float(len(PAIR_SLOTS)) + pair
    crow = lax.broadcasted_iota(jnp.int32, (CLASS_ROWS, tm), 0).astype(F32)
    onehot = jnp.where(crow == cls, 1.0, 0.0)
    rows = lax.broadcasted_iota(jnp.int32, (tm, tm), 0)
    cols = lax.broadcasted_iota(jnp.int32, (tm, tm), 1)
    upper = jnp.where(rows <= cols, 1.0, 0.0).astype(BF16)
    cum = jnp.dot(onehot.astype(BF16), upper, preferred_element_type=F32)
    carry = carry_ref[:, 0:1]
    rank = jnp.sum(onehot * (cum - 1.0 + carry), axis=0, keepdims=True)
    new_carry = carry + jnp.sum(onehot, axis=1, keepdims=True)
    carry_ref[...] = jnp.broadcast_to(new_carry, carry_ref.shape)
    cnt_ref[...] = jnp.broadcast_to(new_carry, cnt_ref.shape)
    zero = jnp.zeros_like(cls)
    route_ref[...] = jnp.concatenate([cls, rank, zero, zero, zero, zero, zero, zero], axis=0)


def _router_weights(w_router):
    w = jnp.pad(w_router.astype(F32), ((0, 0), (0, LANES - w_router.shape[1])))
    hi = w.astype(BF16)
    lo = (w - hi.astype(F32)).astype(BF16)
    return jnp.stack([hi, lo])


ROW_DMA_UNROLL = 8


def _dispatch_kernel(pos_ref, x_ref, g_ref, mod_ref, init_ref, o_ref, stage_ref, sems):
    del init_ref
    tokens, d = x_ref.shape
    per = d // LANES
    i = pl.program_id(0)
    slot = i % 2
    base = i * tokens
    _store_token_tiles(stage_ref.at[slot], _modulated(x_ref[...], g_ref[...], mod_ref[...], 3))

    def issue(r, carry):
        src = pl.multiple_of(r * per, per)
        dst = pl.multiple_of(pos_ref[base + r], per)
        pltpu.make_async_copy(stage_ref.at[slot, pl.ds(src, per)], o_ref.at[pl.ds(dst, per)],
                              sems.at[slot]).start()
        return carry

    lax.fori_loop(0, tokens, issue, 0, unroll=ROW_DMA_UNROLL)

    def wait_tile(s):
        pltpu.make_async_copy(stage_ref.at[s], o_ref.at[pl.ds(0, tokens * per)], sems.at[s]).wait()

    @pl.when(i > 0)
    def _():
        wait_tile(1 - slot)

    @pl.when(i == pl.num_programs(0) - 1)
    def _():
        wait_tile(slot)


def _dispatch_call(pos_rows, xs, g, mods, layer, init, dims):
    tm, d = TOKEN_TILE, dims["d"]
    per = d // LANES
    tpb, ct, b = dims["tpb"], dims["ct"], dims["b"]
    return pl.pallas_call(
        _dispatch_kernel,
        grid_spec=pltpu.PrefetchScalarGridSpec(
            num_scalar_prefetch=1,
            grid=(dims["tiles"],),
            in_specs=[pl.BlockSpec((tm, d), lambda i, p: (i, 0)),
                      pl.BlockSpec((1, d), lambda i, p: (0, 0)),
                      pl.BlockSpec((None, None, 1, 6 * d),
                                   lambda i, p: (layer, _mod_row(i, tpb, ct, b), 0, 0)),
                      pl.BlockSpec(memory_space=pl.ANY)],
            out_specs=pl.BlockSpec(memory_space=pl.ANY),
            scratch_shapes=[pltpu.VMEM((2, tm * per, LANES), F32), pltpu.SemaphoreType.DMA((2,))],
        ),
        out_shape=jax.ShapeDtypeStruct(init.shape, F32),
        input_output_aliases={4: 0},
        compiler_params=_params("arbitrary"),
        name="dispatch",
    )(pos_rows, xs, g, mods, init)


def _expert_kernel(ea_ref, eb_ref, ok_ref, x_ref, wr_ref, w1a, w3a, w2a, w1b, w3b, w2b, o_ref):
    d = w1a.shape[0]
    te = x_ref.shape[0] // (d // LANES)
    t = pl.program_id(0)

    @pl.when(ok_ref[t] != 0)
    def _():
        h = _load_token_tiles(x_ref, te, d)
        x = h.astype(BF16)
        dw = wr_ref[pl.ds(ea_ref[t], 1), :] - wr_ref[pl.ds(eb_ref[t], 1), :]
        gate_a = jax.nn.sigmoid(jnp.sum(h * dw, axis=1, keepdims=True))
        gate_b = 1.0 - gate_a

        def expert(w1, w3, w2):
            a = jnp.dot(x, w1[...].astype(BF16), preferred_element_type=F32)
            b = jnp.dot(x, w3[...].astype(BF16), preferred_element_type=F32)
            u = (a * jax.nn.sigmoid(a)) * b
            return jnp.dot(u.astype(BF16), w2[...].astype(BF16), preferred_element_type=F32)

        _store_token_tiles(o_ref, gate_a * expert(w1a, w3a, w2a) + gate_b * expert(w1b, w3b, w2b))

    @pl.when(ok_ref[t] == 0)
    def _():
        o_ref[...] = jnp.zeros_like(o_ref)


def _expert_call(tile_a, tile_b, tile_ok, xsorted, w_router_t, w1, w3, w2, layer, d):
    te = TOKEN_TILE
    per = d // LANES
    n_tiles = xsorted.shape[0] // (te * per)
    de = w1.shape[-1]
    sel_a = lambda t, ea, eb, ok: (layer, ea[t], 0, 0)
    sel_b = lambda t, ea, eb, ok: (layer, eb[t], 0, 0)
    used = lambda t, ea, eb, ok: (t, 0)
    return pl.pallas_call(
        _expert_kernel,
        grid_spec=pltpu.PrefetchScalarGridSpec(
            num_scalar_prefetch=3,
            grid=(n_tiles,),
            in_specs=[
                pl.BlockSpec((te * per, LANES), used),
                pl.BlockSpec(w_router_t.shape, lambda t, ea, eb, ok: (0, 0)),
                pl.BlockSpec((None, None, d, de), sel_a), pl.BlockSpec((None, None, d, de), sel_a),
                pl.BlockSpec((None, None, de, d), sel_a),
                pl.BlockSpec((None, None, d, de), sel_b), pl.BlockSpec((None, None, d, de), sel_b),
                pl.BlockSpec((None, None, de, d), sel_b),
            ],
            out_specs=pl.BlockSpec((te * per, LANES), used),
        ),
        out_shape=jax.ShapeDtypeStruct((n_tiles * te * per, LANES), F32),
        compiler_params=_params("arbitrary"),
        name="experts",
    )(tile_a, tile_b, tile_ok, xsorted, w_router_t, w1, w3, w2, w1, w3, w2)


def _combine_kernel(pos_ref, y_ref, x_ref, mod_ref, gf_ref, o_ref, buf_ref, sems,
                    *, tile_of, final_norm):
    tm, d = x_ref.shape
    per = d // LANES
    i = pl.program_id(0)
    slot = i % 2

    def gather(step, s):
        base = tile_of(step) * tm

        def issue(r, carry):
            src = pl.multiple_of(pos_ref[base + r], per)
            dst = pl.multiple_of(r * per, per)
            pltpu.make_async_copy(y_ref.at[pl.ds(src, per)], buf_ref.at[s, pl.ds(dst, per)],
                                  sems.at[s]).start()
            return carry

        lax.fori_loop(0, tm, issue, 0, unroll=ROW_DMA_UNROLL)

    @pl.when(i == 0)
    def _():
        gather(0, 0)

    @pl.when(i + 1 < pl.num_programs(0))
    def _():
        gather(i + 1, 1 - slot)

    pltpu.make_async_copy(y_ref.at[pl.ds(0, tm * per)], buf_ref.at[slot], sems.at[slot]).wait()
    out = x_ref[...] + mod_ref[:, 5 * d:6 * d] * _load_token_tiles(buf_ref.at[slot], tm, d)
    if final_norm:
        ms = jnp.mean(out * out, axis=-1, keepdims=True)
        out = out * lax.rsqrt(ms + NORM_EPS) * gf_ref[...]
    o_ref[...] = out


def _combine_call(pos, ysorted, xs, mods, layer, g_final, dims, final_norm):
    tm, d = TOKEN_TILE, dims["d"]
    tpb, ct, b = dims["tpb"], dims["ct"], dims["b"]
    if final_norm:
        lt = tpb - ct
        tile_of = lambda i: (i // lt) * tpb + ct + i % lt
        n_tiles = b * lt
    else:
        tile_of = lambda i: i
        n_tiles = dims["tiles"]
    mod_spec = pl.BlockSpec((None, None, 1, 6 * d),
                            lambda i, p: (layer, _mod_row(tile_of(i), tpb, ct, b), 0, 0))
    return pl.pallas_call(
        functools.partial(_combine_kernel, tile_of=tile_of, final_norm=final_norm),
        grid_spec=pltpu.PrefetchScalarGridSpec(
            num_scalar_prefetch=1,
            grid=(n_tiles,),
            in_specs=[
                pl.BlockSpec(memory_space=pl.ANY),
                pl.BlockSpec((tm, d), lambda i, p: (tile_of(i), 0)),
                mod_spec,
                pl.BlockSpec((1, d), lambda i, p: (0, 0)),
            ],
            out_specs=pl.BlockSpec((tm, d), lambda i, p: (i, 0)),
            scratch_shapes=[pltpu.VMEM((2, tm * (d // LANES), LANES), F32), pltpu.SemaphoreType.DMA((2,))],
        ),
        out_shape=jax.ShapeDtypeStruct((n_tiles * tm, d), F32),
        compiler_params=_params("arbitrary"),
        name="combine",
    )(pos, ysorted, xs, mods, g_final)


_CLASS_A = np.array([g * GROUP_SIZE + a for g in range(N_GROUPS) for a, _ in PAIR_SLOTS], np.int32)
_CLASS_B = np.array([g * GROUP_SIZE + b for g in range(N_GROUPS) for _, b in PAIR_SLOTS], np.int32)


def _routing_plan(route, counts, n_tiles_padded):
    te = TOKEN_TILE
    cls = route[0].astype(jnp.int32)
    rank = route[1].astype(jnp.int32)
    cnt = counts[:N_CLASSES, 0].astype(jnp.int32)
    tiles = (cnt + te - 1) // te
    tile_end = jnp.cumsum(tiles)
    tile_start = tile_end - tiles
    onehot = (cls[:, None] == jnp.arange(N_CLASSES, dtype=jnp.int32)[None, :]).astype(jnp.int32)
    pos = jnp.sum(onehot * (tile_start * te)[None, :], axis=1) + rank
    t = jnp.arange(n_tiles_padded, dtype=jnp.int32)
    ok = (t < tile_end[-1]).astype(jnp.int32)
    tcls = jnp.sum((tile_end[None, :] <= jnp.minimum(t, tile_end[-1] - 1)[:, None]).astype(jnp.int32), axis=1)
    tile_a = jnp.take(jnp.asarray(_CLASS_A), tcls)
    tile_b = jnp.take(jnp.asarray(_CLASS_B), tcls)
    return pos, tile_a, tile_b, ok


def _moe(xs, g_ffn, route, counts, sorted_buf, mods, layer, w_router_t, w1, w3, w2, g_final, dims, final_norm):
    per = dims["d"] // LANES
    n_tiles_padded = sorted_buf.shape[0] // (TOKEN_TILE * per)
    pos, tile_a, tile_b, tile_ok = _routing_plan(route, counts, n_tiles_padded)
    xsorted = _dispatch_call(pos * per, xs, g_ffn, mods, layer, sorted_buf, dims)
    ysorted = _expert_call(tile_a, tile_b, tile_ok, xsorted, w_router_t, w1, w3, w2, layer, dims["d"])
    return _combine_call(pos * per, ysorted, xs, mods, layer, g_final, dims, final_norm), xsorted


def _pos_table(n_tokens, dim):
    rows = n_tokens // GRID_WIDTH
    r, col = jnp.meshgrid(jnp.arange(rows, dtype=F32), jnp.arange(GRID_WIDTH, dtype=F32), indexing="ij")
    quarter = dim // 4
    freqs = jnp.exp(-math.log(POS_BASE) * jnp.arange(quarter, dtype=F32) / quarter)

    def enc(p):
        ang = p.reshape(-1, 1) * freqs
        return jnp.concatenate([jnp.sin(ang), jnp.cos(ang)], axis=-1)

    return jnp.concatenate([enc(r), enc(col)], axis=-1)


def _mlstm_weights(w_in, b_gate):
    nq = HEADS * HEAD_QK
    nv = HEADS * HEAD_V
    wq = w_in[:, 0:nq] * (HEAD_QK ** -0.5)
    wk = w_in[:, nq:2 * nq]
    wv = w_in[:, 2 * nq:2 * nq + nv]
    wo = w_in[:, 2 * nq + nv:2 * nq + 2 * nv]
    wg = w_in[:, 2 * nq + 2 * nv:]
    wg_pad = jnp.pad(wg, ((0, 0), (0, LANES - wg.shape[1])))
    w_tok = jnp.concatenate([wq, wk, wg_pad], axis=1).astype(BF16)
    w_feat = jnp.concatenate([wv, wo, wg], axis=1).T.astype(BF16)
    bias = b_gate.reshape(-1).astype(F32)
    bias_c = jnp.pad(bias, (0, LANES - bias.shape[0])).reshape(1, LANES)
    bias_r = bias.reshape(-1, 1)
    return w_tok, w_feat, bias_c, bias_r


def kernel(x, c, ctx, c_ctx, w_ada, b_ada, g_mix, g_ffn, g_final, m_w_in, m_b_gate, m_g_head, m_w_out, r_w_in, r_conv_w, r_conv_b, r_w_gate, r_b_gate, r_lam, r_w_out, w_router, b_router, e_w1, e_w3, e_w2):
    batch, t_len, d = x.shape
    ctx_len = ctx.shape[1]
    depth = w_ada.shape[0]
    tm = TOKEN_TILE
    assert t_len % tm == 0 and ctx_len % tm == 0 and batch + 1 <= ADA_ROWS
    assert d == HEADS * HEAD_V and t_len % GRID_WIDTH == 0
    s_len = ctx_len + t_len
    dims = dict(b=batch, d=d, n=batch * s_len, tpb=s_len // tm, ct=ctx_len // tm,
                tiles=batch * s_len // tm)

    cvec = jnp.concatenate([c, c_ctx[None, :], jnp.zeros((ADA_ROWS - batch - 1, d), F32)], axis=0)
    mods = _ada_call(cvec, w_ada, b_ada).reshape(depth, ADA_ROWS, 1, 6 * d)
    xs = (ctx.reshape(batch * ctx_len, d), x.reshape(batch * t_len, d), _pos_table(t_len, d))

    w_router_t = w_router.T.astype(F32)
    w_router_pieces = _router_weights(w_router)
    b_router_col = b_router.reshape(-1, 1).astype(F32)
    g_final2 = g_final.reshape(1, d)
    sorted_buf = jnp.zeros(((dims["tiles"] + N_CLASSES) * tm * (d // LANES), LANES), F32)
    out = None
    for i in range(depth):
        j = i // 2
        g_mix_i = g_mix[i].reshape(1, d)
        g_ffn_i = g_ffn[i].reshape(1, d)
        if i % 2 == 0:
            w_tok, w_feat, bias_c, bias_r = _mlstm_weights(m_w_in[j], m_b_gate[j])
            proj = _mproj_call(xs, g_mix_i, mods, i, w_tok, w_feat, dims)
            if isinstance(xs, tuple):
                xs, proj = proj[0], proj[1:]
            q, k, vt, ot, gc, gr = proj
            hf, hb = _mlstm_call(q, k, vt, gc, gr, bias_c, bias_r, dims)
            w_out = (m_g_head[j][:, None] * m_w_out[j]).astype(BF16)
            xs, route, counts = _readout_call(_mreadout_kernel, "mlstm_readout", True, hf, hb, ot, w_out,
                                              xs, mods, i, g_ffn_i, w_router_pieces, b_router_col, dims)
        else:
            gy, xc = _rproj_call(xs, g_mix_i, mods, i, r_w_in[j].astype(BF16), r_conv_w[j],
                                 r_conv_b[j].reshape(1, d), dims)
            hf, hb = _lru_call(xc, (0.5 * r_w_gate[j]).astype(BF16), 0.5 * r_b_gate[j].reshape(2, 1, -1),
                               r_lam[j].reshape(2, 1, d), dims)
            xs, route, counts = _readout_call(_readout_kernel, "rglru_readout", False, hf, hb, gy,
                                              r_w_out[j].astype(BF16), xs, mods, i, g_ffn_i,
                                              w_router_pieces, b_router_col, dims)
        last = i == depth - 1
        res, sorted_buf = _moe(xs, g_ffn_i, route, counts, sorted_buf, mods, i, w_router_t,
                               e_w1, e_w3, e_w2, g_final2, dims, last)
        if last:
            out = res
        else:
            xs = res
    return out.reshape(batch, t_len, d)
```

```python
import functools
import math

import jax
import jax.numpy as jnp
import numpy as np
from jax import lax
from jax.experimental import pallas as pl
from jax.experimental.pallas import tpu as pltpu

F32 = jnp.float32
BF16 = jnp.bfloat16
HIGHEST = lax.Precision.HIGHEST

TOKEN_TILE = 256
LANES = 128
SUBLANES = 8
VMEM_LIMIT_BYTES = 56 * 1024 * 1024
NORM_EPS = 1e-6
GRID_WIDTH = 64
POS_BASE = 10000.0
HEADS = 8
HEAD_QK = 64
HEAD_V = 128
LRU_BLOCKS = 8
LRU_C = 8.0
CONV_TAPS = 4
CONV_LEFT = CONV_TAPS // 2
N_EXPERTS = 16
N_GROUPS = 4
GROUP_SIZE = N_EXPERTS // N_GROUPS
PAIRS = [(a, b) for a in range(GROUP_SIZE) for b in range(a + 1, GROUP_SIZE)]
PAIR_SLOTS = [(0, 1), (0, 2), (0, 3), (1, 3), (1, 2), (3, 2)]
N_CLASSES = N_GROUPS * len(PAIRS)
CLASS_ROWS = 32
ADA_ROWS = 16


def _params(*sem):
    return pltpu.CompilerParams(dimension_semantics=sem, vmem_limit_bytes=VMEM_LIMIT_BYTES)


def _mod_row(i, tiles_per_batch, ctx_tiles, batch):
    return jnp.where(i % tiles_per_batch < ctx_tiles, batch, i // tiles_per_batch)


def _bwd_tile(i, tiles_per_batch, ctx_tiles):
    b = i // tiles_per_batch
    j = i % tiles_per_batch
    jb = jnp.where(j < ctx_tiles, ctx_tiles - 1 - j, tiles_per_batch - 1 - (j - ctx_tiles))
    return b * tiles_per_batch + jb


def _ada_kernel(c_ref, w_ref, b_ref, o_ref):
    c = c_ref[...]
    s = c * jax.nn.sigmoid(c)
    o_ref[...] = jnp.dot(s, w_ref[...], preferred_element_type=F32, precision=HIGHEST) + b_ref[...]


def _ada_call(cvec, w_ada, b_ada):
    depth, d, n = w_ada.shape
    tn = n // 4
    return pl.pallas_call(
        _ada_kernel,
        grid=(depth, n // tn),
        in_specs=[
            pl.BlockSpec((ADA_ROWS, d), lambda l, j: (0, 0)),
            pl.BlockSpec((None, d, tn), lambda l, j: (l, 0, j)),
            pl.BlockSpec((None, 1, tn), lambda l, j: (l, 0, j)),
        ],
        out_specs=pl.BlockSpec((None, ADA_ROWS, tn), lambda l, j: (l, 0, j)),
        out_shape=jax.ShapeDtypeStruct((depth, ADA_ROWS, n), F32),
        compiler_params=_params("arbitrary", "arbitrary"),
        name="ada",
    )(cvec, w_ada, b_ada.reshape(depth, 1, n))


def _modulated(x, g, mod, shift_idx):
    d = x.shape[-1]
    ms = jnp.mean(x * x, axis=-1, keepdims=True)
    xn = x * lax.rsqrt(ms + NORM_EPS) * g
    shift = mod[:, shift_idx * d:(shift_idx + 1) * d]
    scale = mod[:, (shift_idx + 1) * d:(shift_idx + 2) * d]
    return xn * (1.0 + scale) + shift


def _mod_spec(layer, dims):
    tpb, ct, b = dims["tpb"], dims["ct"], dims["b"]
    return pl.BlockSpec((None, None, 1, 6 * dims["d"]),
                        lambda i, *_: (layer, _mod_row(i, tpb, ct, b), 0, 0))


def _store_token_tiles(ref, x):
    tokens, d = x.shape
    per = d // LANES
    for s in range(per):
        ref[pl.ds(s, tokens, stride=per), :] = x[:, s * LANES:(s + 1) * LANES]


def _load_token_tiles(ref, tokens, d):
    per = d // LANES
    return jnp.concatenate([ref[pl.ds(s, tokens, stride=per), :] for s in range(per)], axis=1)


def _gelu_tanh(y):
    return 0.5 * y * (1.0 + jnp.tanh(math.sqrt(2.0 / math.pi) * (y + 0.044715 * (y * y * y))))


SRC_BLOCK = 128
_NT = (((1,), (1,)), ((), ()))
_TN = (((0,), (0,)), ((), ()))


def _mproj_kernel(*refs, assemble, tiles_per_batch, ctx_tiles):
    if assemble:
        ctx_ref, lat_ref, pos_ref, g_ref, mod_ref, w_ref, wt_ref, xs_ref = refs[:8]
        is_ctx = pl.program_id(0) % tiles_per_batch < ctx_tiles
        x = jnp.where(is_ctx, ctx_ref[...], lat_ref[...] + pos_ref[...])
        xs_ref[...] = x
    else:
        x_ref, g_ref, mod_ref, w_ref, wt_ref = refs[:5]
        x = x_ref[...]
    q_ref, k_ref, vt_ref, ot_ref, gc_ref, gr_ref = refs[-6:]
    d = x.shape[-1]
    nq = q_ref.shape[-1]
    h = _modulated(x, g_ref[...], mod_ref[...], 0).astype(BF16)
    q_ref[...] = jnp.dot(h, w_ref[:, 0:nq], preferred_element_type=F32).astype(BF16)
    k_ref[...] = jnp.dot(h, w_ref[:, nq:2 * nq], preferred_element_type=F32).astype(BF16)
    gc_ref[...] = jnp.dot(h, w_ref[:, 2 * nq:2 * nq + LANES], preferred_element_type=F32)
    vt_ref[...] = lax.dot_general(wt_ref[0:d, :], h, _NT, preferred_element_type=F32).astype(BF16)
    ot = lax.dot_general(wt_ref[d:2 * d, :], h, _NT, preferred_element_type=F32)
    ot_ref[...] = jax.nn.sigmoid(ot).astype(BF16)
    gr_ref[...] = lax.dot_general(wt_ref[2 * d:2 * d + 4 * HEADS, :], h, _NT, preferred_element_type=F32)


def _mproj_call(stream, g, mods, layer, w_tok, w_feat, dims):
    tm, d, n = TOKEN_TILE, dims["d"], dims["n"]
    tpb, ct = dims["tpb"], dims["ct"]
    lt = tpb - ct
    nq = HEADS * HEAD_QK
    row = lambda i: (i, 0)
    col = lambda i: (0, i)
    const = lambda i: (0, 0)
    assemble = isinstance(stream, tuple)
    if assemble:
        stream_specs = [
            pl.BlockSpec((tm, d), lambda i: ((i // tpb) * ct + jnp.minimum(i % tpb, ct - 1), 0)),
            pl.BlockSpec((tm, d), lambda i: ((i // tpb) * lt + jnp.maximum(i % tpb - ct, 0), 0)),
            pl.BlockSpec((tm, d), lambda i: (jnp.maximum(i % tpb - ct, 0), 0)),
        ]
        stream_out = [pl.BlockSpec((tm, d), row)], [jax.ShapeDtypeStruct((n, d), F32)]
    else:
        stream, stream_specs, stream_out = (stream,), [pl.BlockSpec((tm, d), row)], ([], [])
    return pl.pallas_call(
        functools.partial(_mproj_kernel, assemble=assemble, tiles_per_batch=tpb, ctx_tiles=ct),
        grid=(dims["tiles"],),
        in_specs=stream_specs + [
            pl.BlockSpec((1, d), const),
            _mod_spec(layer, dims),
            pl.BlockSpec(w_tok.shape, const),
            pl.BlockSpec(w_feat.shape, const),
        ],
        out_specs=stream_out[0] + [
            pl.BlockSpec((tm, nq), row),
            pl.BlockSpec((tm, nq), row),
            pl.BlockSpec((d, tm), col),
            pl.BlockSpec((d, tm), col),
            pl.BlockSpec((tm, LANES), row),
            pl.BlockSpec((4 * HEADS, tm), col),
        ],
        out_shape=stream_out[1] + [
            jax.ShapeDtypeStruct((n, nq), BF16),
            jax.ShapeDtypeStruct((n, nq), BF16),
            jax.ShapeDtypeStruct((d, n), BF16),
            jax.ShapeDtypeStruct((d, n), BF16),
            jax.ShapeDtypeStruct((n, LANES), F32),
            jax.ShapeDtypeStruct((4 * HEADS, n), F32),
        ],
        compiler_params=_params("arbitrary"),
        name="mlstm_proj",
    )(*stream, g, mods, w_tok, w_feat)


def _mlstm_gate_stats(gc_ref, gr_ref, bc_ref, br_ref, m_ref, *, backward):
    tm = gc_ref.shape[0]
    gate_i = 2 * HEADS if backward else 0
    gate_f = gate_i + HEADS
    last = 0 if backward else tm - 1
    src = lax.broadcasted_iota(jnp.int32, (tm, tm), 0)
    tgt = lax.broadcasted_iota(jnp.int32, (tm, tm), 1)
    visible = (src >= tgt) if backward else (src <= tgt)
    neg_mask = jnp.where(visible, 0.0, -jnp.inf)
    xc = gc_ref[...] + bc_ref[...]
    xr = gr_ref[...] + br_ref[...]
    before = (src <= tgt) if backward else (src >= tgt)
    cum_c = _split_dot(jnp.where(before, 1.0, 0.0).astype(BF16), jax.nn.log_sigmoid(xc), left=True)
    cum_r = _split_dot(jnp.where(visible, 1.0, 0.0).astype(BF16), jax.nn.log_sigmoid(xr), left=False)
    b_row = cum_r[gate_f:gate_f + HEADS, :]
    c_row = xr[gate_i:gate_i + HEADS, :] - b_row
    c_col = xc - pltpu.roll(cum_c, LANES - HEADS, axis=1)
    run_col = c_col
    shift = 1
    while shift < tm:
        fill = jnp.full((shift, LANES), -jnp.inf, F32)
        if backward:
            moved = jnp.concatenate([run_col[shift:, :], fill], axis=0)
        else:
            moved = jnp.concatenate([fill, run_col[:tm - shift, :]], axis=0)
        run_col = jnp.maximum(run_col, moved)
        shift *= 2
    run = run_col.T[gate_i:gate_i + HEADS, :]
    m = m_ref[:, 0:1]
    log_inter = b_row + m
    m_pos = jnp.maximum(log_inter, b_row + run)
    shift_row = b_row - m_pos
    w_inter = jnp.exp(log_inter - m_pos)
    floor = jnp.exp(-m_pos)
    b_last = b_row[:, last:last + 1]
    m_new = m_pos[:, last:last + 1]
    w_src = jnp.exp(b_last + c_row - m_new)
    decay = jnp.exp(b_last + m - m_new)
    m_ref[...] = jnp.broadcast_to(m_new, m_ref.shape)
    return neg_mask, c_col, shift_row, w_inter, floor, w_src, decay


def _mlstm_head(h, q_ref, k_ref, vt_ref, o_ref, ct_ref, stats, *, backward):
    neg_mask, c_col, shift_row, w_inter, floor, w_src, decay = stats
    tm = q_ref.shape[0]
    gate_i = 2 * HEADS if backward else 0
    half = lax.broadcasted_iota(jnp.int32, (tm, LANES), 1) < HEAD_QK
    ones = jnp.ones((2 * SUBLANES, tm), BF16)
    pair = slice((h // 2) * LANES, (h // 2 + 1) * LANES)
    qp = q_ref[:, pair]
    km = jnp.where(half if h % 2 == 0 else jnp.logical_not(half), k_ref[:, pair], 0.0).astype(BF16)
    vaug = jnp.concatenate([vt_ref[h * HEAD_V:(h + 1) * HEAD_V, :], ones], axis=0)
    ct = ct_ref[h]
    acc = w_inter[h:h + 1, :] * lax.dot_general(ct.astype(BF16), qp, _NT, preferred_element_type=F32)
    src_block = SRC_BLOCK if (h + int(backward)) % 2 == 0 else tm
    for blk in range(tm // src_block):
        rows = slice(blk * src_block, (blk + 1) * src_block)
        st = lax.dot_general(km[rows, :], qp, _NT, preferred_element_type=F32)
        e = (neg_mask[rows, :] + shift_row[h:h + 1, :]) + c_col[rows, gate_i + h:gate_i + h + 1]
        p = (st * jnp.exp(e)).astype(BF16)
        acc = acc + jnp.dot(vaug[:, rows], p, preferred_element_type=F32)
    den = acc[HEAD_V:HEAD_V + 1, :]
    scale = 1.0 / jnp.maximum(jnp.abs(den), floor[h:h + 1, :])
    o_ref[h * HEAD_V:(h + 1) * HEAD_V, :] = (acc[0:HEAD_V, :] * scale).astype(o_ref.dtype)
    vs = (vaug.astype(F32) * w_src[h:h + 1, :]).astype(BF16)
    ct_ref[h] = decay[h:h + 1, :] * ct + jnp.dot(vs, km, preferred_element_type=F32)


def _split_dot(ones_mat, x, left):
    hi = x.astype(BF16)
    r1 = x - hi.astype(F32)
    mid = r1.astype(BF16)
    lo = (r1 - mid.astype(F32)).astype(BF16)
    out = None
    for piece in (lo, mid, hi):
        t = (jnp.dot(ones_mat, piece, preferred_element_type=F32) if left
             else jnp.dot(piece, ones_mat, preferred_element_type=F32))
        out = t if out is None else out + t
    return out


def _mlstm_kernel(qf, kf, vtf, gcf, grf, qb, kb, vtb, gcb, grb, bc_ref, br_ref, of_ref, ob_ref,
                  cf_ref, mf_ref, cb_ref, mb_ref, *, tiles_per_batch):
    @pl.when(pl.program_id(0) % tiles_per_batch == 0)
    def _():
        cf_ref[...] = jnp.zeros_like(cf_ref)
        cb_ref[...] = jnp.zeros_like(cb_ref)
        mf_ref[...] = jnp.zeros_like(mf_ref)
        mb_ref[...] = jnp.zeros_like(mb_ref)

    stats_f = _mlstm_gate_stats(gcf, grf, bc_ref, br_ref, mf_ref, backward=False)
    stats_b = _mlstm_gate_stats(gcb, grb, bc_ref, br_ref, mb_ref, backward=True)
    for h in range(HEADS):
        _mlstm_head(h, qf, kf, vtf, of_ref, cf_ref, stats_f, backward=False)
    for h in range(HEADS):
        _mlstm_head(h, qb, kb, vtb, ob_ref, cb_ref, stats_b, backward=True)


def _mlstm_call(q, k, vt, gc, gr, bias_c, bias_r, dims):
    tm, d, n = TOKEN_TILE, dims["d"], dims["n"]
    tpb, ct = dims["tpb"], dims["ct"]
    nq = q.shape[1]
    fwd = lambda i: (i, 0)
    bwd = lambda i: (_bwd_tile(i, tpb, ct), 0)
    fwd_t = lambda i: (0, i)
    bwd_t = lambda i: (0, _bwd_tile(i, tpb, ct))
    const = lambda i: (0, 0)
    state = [pltpu.VMEM((HEADS, HEAD_V + 2 * SUBLANES, LANES), F32), pltpu.VMEM((HEADS, LANES), F32)]
    return pl.pallas_call(
        functools.partial(_mlstm_kernel, tiles_per_batch=tpb),
        grid=(dims["tiles"],),
        in_specs=[
            pl.BlockSpec((tm, nq), fwd), pl.BlockSpec((tm, nq), fwd), pl.BlockSpec((d, tm), fwd_t),
            pl.BlockSpec((tm, LANES), fwd), pl.BlockSpec((4 * HEADS, tm), fwd_t),
            pl.BlockSpec((tm, nq), bwd), pl.BlockSpec((tm, nq), bwd), pl.BlockSpec((d, tm), bwd_t),
            pl.BlockSpec((tm, LANES), bwd), pl.BlockSpec((4 * HEADS, tm), bwd_t),
            pl.BlockSpec((1, LANES), const), pl.BlockSpec((4 * HEADS, 1), const),
        ],
        out_specs=[pl.BlockSpec((d, tm), fwd_t), pl.BlockSpec((d, tm), bwd_t)],
        out_shape=[jax.ShapeDtypeStruct((d, n), BF16), jax.ShapeDtypeStruct((d, n), BF16)],
        scratch_shapes=state + state,
        compiler_params=_params("arbitrary"),
        name="mlstm_scan",
    )(q, k, vt, gc, gr, q, k, vt, gc, gr, bias_c, bias_r)


def _finish_readout(x_new, mod_ref, g_ref, wr_ref, br_ref, o_ref, route_ref, cnt_ref, carry_ref):
    @pl.when(pl.program_id(0) == 0)
    def _():
        carry_ref[...] = jnp.zeros_like(carry_ref)

    o_ref[...] = x_new
    _route_tile(x_new, g_ref, mod_ref, wr_ref, br_ref, route_ref, cnt_ref, carry_ref)


def _mreadout_kernel(hf_ref, hb_ref, ot_ref, w_ref, x_ref, mod_ref, *route_refs):
    d = x_ref.shape[-1]
    parts = []
    for h in range(HEADS):
        rows = slice(h * HEAD_V, (h + 1) * HEAD_V)
        hs = hf_ref[rows, :].astype(F32) + hb_ref[rows, :].astype(F32)
        r = lax.rsqrt(jnp.mean(hs * hs, axis=0, keepdims=True) + NORM_EPS)
        parts.append((hs * r * ot_ref[rows, :]).astype(BF16))
    t = jnp.concatenate(parts, axis=0)
    out = lax.dot_general(t, w_ref[...], _TN, preferred_element_type=F32)
    _finish_readout(x_ref[...] + mod_ref[:, 2 * d:3 * d] * out, mod_ref, *route_refs)


def _readout_kernel(hf_ref, hb_ref, gy_ref, w_ref, x_ref, mod_ref, *route_refs):
    d = x_ref.shape[-1]
    t = _gelu_tanh(gy_ref[...].astype(F32)) * (hf_ref[...].astype(F32) + hb_ref[...].astype(F32))
    out = jnp.dot(t.astype(BF16), w_ref[...], preferred_element_type=F32)
    _finish_readout(x_ref[...] + mod_ref[:, 2 * d:3 * d] * out, mod_ref, *route_refs)


def _readout_call(kernel_fn, name, feature_major, hf, hb, aux, w_out, xs, mods, layer,
                  g_ffn, w_router_pieces, b_router_col, dims):
    tm, d, n = TOKEN_TILE, dims["d"], dims["n"]
    row = lambda i: (i, 0)
    col = lambda i: (0, i)
    const = lambda i: (0, 0)
    mixer = pl.BlockSpec((d, tm), col) if feature_major else pl.BlockSpec((tm, d), row)
    return pl.pallas_call(
        kernel_fn,
        grid=(dims["tiles"],),
        in_specs=[mixer, mixer, mixer,
                  pl.BlockSpec((d, d), const), pl.BlockSpec((tm, d), row), _mod_spec(layer, dims),
                  pl.BlockSpec((1, d), const),
                  pl.BlockSpec(w_router_pieces.shape, lambda i: (0, 0, 0)),
                  pl.BlockSpec(b_router_col.shape, const)],
        out_specs=[pl.BlockSpec((tm, d), row),
                   pl.BlockSpec((SUBLANES, tm), col),
                   pl.BlockSpec((CLASS_ROWS, LANES), const)],
        out_shape=[jax.ShapeDtypeStruct((n, d), F32),
                   jax.ShapeDtypeStruct((SUBLANES, n), F32),
                   jax.ShapeDtypeStruct((CLASS_ROWS, LANES), F32)],
        scratch_shapes=[pltpu.VMEM((CLASS_ROWS, LANES), F32)],
        compiler_params=_params("arbitrary"),
        name=name,
    )(hf, hb, aux, w_out, xs, mods, g_ffn, w_router_pieces, b_router_col)


def _rproj_kernel(x_ref, g_ref, mod_ref, w_ref, cw_ref, cb_ref, gy_ref, xc_ref, ext_ref,
                  *, tiles_per_batch, ctx_tiles):
    tm, d = x_ref.shape
    i = pl.program_id(0)

    def in_segment_neighbours(t):
        j = t % tiles_per_batch
        first = jnp.logical_or(j == 0, j == ctx_tiles)
        last = jnp.logical_or(j == ctx_tiles - 1, j == tiles_per_batch - 1)
        return jnp.logical_not(first), jnp.logical_not(last)

    @pl.when(i == 0)
    def _():
        ext_ref[...] = jnp.zeros_like(ext_ref)

    h = _modulated(x_ref[...], g_ref[...], mod_ref[...], 0).astype(BF16)
    gy_ref[...] = jnp.dot(h, w_ref[:, 0:d], preferred_element_type=F32).astype(BF16)
    cur = jnp.dot(h, w_ref[:, d:2 * d], preferred_element_type=F32)

    xc = cb_ref[...] + ext_ref[pl.ds(SUBLANES - CONV_LEFT, tm), :] * cw_ref[0:1, :]
    for k in range(1, CONV_TAPS):
        xc = xc + ext_ref[pl.ds(SUBLANES - CONV_LEFT + k, tm), :] * cw_ref[k:k + 1, :]
    xc_ref[...] = xc
    _, next_ok = in_segment_neighbours(jnp.maximum(i - 1, 0))
    ahead = jnp.where(next_ok, cur[0:1, :] * cw_ref[CONV_TAPS - 1:CONV_TAPS, :], 0.0)
    xc_ref[tm - 1:tm, :] = xc[tm - 1:tm, :] + ahead

    prev_ok, _ = in_segment_neighbours(i)
    ext_ref[0:SUBLANES, :] = jnp.where(prev_ok, ext_ref[tm:tm + SUBLANES, :], 0.0)
    ext_ref[SUBLANES:SUBLANES + tm, :] = cur


def _rproj_call(xs, g, mods, layer, w_in, conv_w, conv_b, dims):
    tm, d, n, tiles = TOKEN_TILE, dims["d"], dims["n"], dims["tiles"]
    tpb, ct, b = dims["tpb"], dims["ct"], dims["b"]
    cur = lambda i: (jnp.minimum(i, tiles - 1), 0)
    lag = lambda i: (jnp.maximum(i - 1, 0), 0)
    const = lambda i: (0, 0)
    mod_spec = pl.BlockSpec((None, None, 1, 6 * d),
                            lambda i: (layer, _mod_row(jnp.minimum(i, tiles - 1), tpb, ct, b), 0, 0))
    return pl.pallas_call(
        functools.partial(_rproj_kernel, tiles_per_batch=tpb, ctx_tiles=ct),
        grid=(tiles + 1,),
        in_specs=[pl.BlockSpec((tm, d), cur), pl.BlockSpec((1, d), const), mod_spec,
                  pl.BlockSpec(w_in.shape, const), pl.BlockSpec(conv_w.shape, const),
                  pl.BlockSpec(conv_b.shape, const)],
        out_specs=[pl.BlockSpec((tm, d), cur), pl.BlockSpec((tm, d), lag)],
        out_shape=[jax.ShapeDtypeStruct((n, d), BF16), jax.ShapeDtypeStruct((n, d), F32)],
        scratch_shapes=[pltpu.VMEM((tm + 2 * SUBLANES, d), F32)],
        compiler_params=_params("arbitrary"),
        name="rglru_proj",
    )(xs, g, mods, w_in, conv_w, conv_b)


def _lru_direction(xc_ref, wg_ref, bg_ref, lam_ref, a_ref, b_ref, *, direction):
    tm, d = xc_ref.shape
    bw = d // LRU_BLOCKS
    z = -lam_ref[direction]
    softplus = jnp.maximum(z, 0.0) + jnp.log(1.0 + jnp.exp(-jnp.abs(z)))
    k = (-0.5 * LRU_C) * softplus
    for nb in range(LRU_BLOCKS):
        sl = slice(nb * bw, (nb + 1) * bw)
        xb = xc_ref[:, sl]
        gg = jnp.dot(xb.astype(BF16), wg_ref[direction, nb], preferred_element_type=F32)
        t = jnp.tanh(gg + bg_ref[direction, :, nb * 2 * bw:(nb + 1) * 2 * bw])
        log_a = k[:, sl] * t[:, :bw] + k[:, sl]
        th = jnp.tanh(log_a)
        a_ref[pl.ds(nb, tm, stride=LRU_BLOCKS), :] = jnp.exp(log_a)
        b_ref[pl.ds(nb, tm, stride=LRU_BLOCKS), :] = (jnp.sqrt(-0.5 * th / (1.0 - th))
                                                      * ((t[:, bw:] + 1.0) * xb))


def _lru_kernel(xc_f, xc_b, wg_ref, bg_ref, lam_ref, of_ref, ob_ref, af_ref, bf_ref, ab_ref, bb_ref,
                hf_ref, hb_ref, h_ref, *, tiles_per_batch):
    tm, d = xc_f.shape
    bw = d // LRU_BLOCKS

    @pl.when(pl.program_id(0) % tiles_per_batch == 0)
    def _():
        h_ref[...] = jnp.zeros_like(h_ref)

    def scan(a_ref, b_ref, o_ref, state, order):
        h = h_ref[state]
        for t in order:
            rows = slice(t * LRU_BLOCKS, (t + 1) * LRU_BLOCKS)
            h = a_ref[rows, :] * h + b_ref[rows, :]
            o_ref[rows, :] = h
        h_ref[state] = h

    _lru_direction(xc_f, wg_ref, bg_ref, lam_ref, af_ref, bf_ref, direction=0)
    scan(af_ref, bf_ref, hf_ref, 0, range(tm))
    _lru_direction(xc_b, wg_ref, bg_ref, lam_ref, ab_ref, bb_ref, direction=1)
    scan(ab_ref, bb_ref, hb_ref, 1, range(tm - 1, -1, -1))
    for nb in range(LRU_BLOCKS):
        sl = slice(nb * bw, (nb + 1) * bw)
        of_ref[:, sl] = hf_ref[pl.ds(nb, tm, stride=LRU_BLOCKS), :].astype(BF16)
        ob_ref[:, sl] = hb_ref[pl.ds(nb, tm, stride=LRU_BLOCKS), :].astype(BF16)


def _lru_call(xc, w_gate, b_gate, lam, dims):
    tm, d, n = TOKEN_TILE, dims["d"], dims["n"]
    tpb, ct = dims["tpb"], dims["ct"]
    fwd = lambda i: (i, 0)
    bwd = lambda i: (_bwd_tile(i, tpb, ct), 0)
    c3 = lambda i: (0, 0, 0)
    c4 = lambda i: (0, 0, 0, 0)
    time_major = pltpu.VMEM((tm * LRU_BLOCKS, d // LRU_BLOCKS), F32)
    return pl.pallas_call(
        functools.partial(_lru_kernel, tiles_per_batch=tpb),
        grid=(dims["tiles"],),
        in_specs=[
            pl.BlockSpec((tm, d), fwd), pl.BlockSpec((tm, d), bwd),
            pl.BlockSpec(w_gate.shape, c4), pl.BlockSpec(b_gate.shape, c3), pl.BlockSpec(lam.shape, c3),
        ],
        out_specs=[pl.BlockSpec((tm, d), fwd), pl.BlockSpec((tm, d), bwd)],
        out_shape=[jax.ShapeDtypeStruct((n, d), BF16), jax.ShapeDtypeStruct((n, d), BF16)],
        scratch_shapes=[time_major] * 6 + [pltpu.VMEM((2, LRU_BLOCKS, d // LRU_BLOCKS), F32)],
        compiler_params=_params("arbitrary"),
        name="rglru_scan",
    )(xc, xc, w_gate, b_gate, lam)


def _first_argmax(vals):
    best = vals[0]
    idx = jnp.zeros_like(best)
    for k in range(1, len(vals)):
        better = vals[k] > best
        idx = jnp.where(better, float(k), idx)
        best = jnp.where(better, vals[k], best)
    return idx, best


def _pick(idx, vals):
    out = vals[0]
    for k in range(1, len(vals)):
        out = jnp.where(idx == float(k), vals[k], out)
    return out


def _route_tile(x, g_ref, mod_ref, wr_ref, br_ref, route_ref, cnt_ref, carry_ref):
    tm = x.shape[0]
    h = _modulated(x, g_ref[...], mod_ref[...], 3)
    hi = h.astype(BF16)
    lo = (h - hi.astype(F32)).astype(BF16)
    logits_tok = (jnp.dot(lo, wr_ref[0], preferred_element_type=F32)
                  + jnp.dot(hi, wr_ref[1], preferred_element_type=F32)
                  + jnp.dot(hi, wr_ref[0], preferred_element_type=F32))
    logits = logits_tok.T[0:N_EXPERTS, :]
    e = jnp.exp(logits - jnp.max(logits, axis=0, keepdims=True))
    scores = e / jnp.sum(e, axis=0, keepdims=True)
    sel = scores + br_ref[...]
    sel_rows = [sel[k:k + 1, :] for k in range(N_EXPERTS)]
    group_scores = []
    for gi in range(N_GROUPS):
        v = sel_rows[gi * GROUP_SIZE:(gi + 1) * GROUP_SIZE]
        best = v[0] + v[1]
        for a, b in PAIRS[1:]:
            best = jnp.maximum(best, v[a] + v[b])
        group_scores.append(best)
    grp, _ = _first_argmax(group_scores)
    in_sel = [_pick(grp, [sel_rows[gi * GROUP_SIZE + k] for gi in range(N_GROUPS)])
              for k in range(GROUP_SIZE)]
    i1, _ = _first_argmax(in_sel)
    rest = [jnp.where(i1 == float(k), -jnp.inf, in_sel[k]) for k in range(GROUP_SIZE)]
    i2, _ = _first_argmax(rest)
    lo = jnp.minimum(i1, i2)
    hi = jnp.maximum(i1, i2)
    pair = jnp.zeros_like(lo)
    for k, (slot_a, slot_b) in enumerate(PAIR_SLOTS):
        here = jnp.logical_and(lo == float(min(slot_a, slot_b)), hi == float(max(slot_a, slot_b)))
        pair = jnp.where(here, float(k), pair)
    cls = grp * float(len(PAIR_SLOTS)) + pair
    crow = lax.broadcasted_iota(jnp.int32, (CLASS_ROWS, tm), 0).astype(F32)
    onehot = jnp.where(crow == cls, 1.0, 0.0)
    rows = lax.broadcasted_iota(jnp.int32, (tm, tm), 0)
    cols = lax.broadcasted_iota(jnp.int32, (tm, tm), 1)
    upper = jnp.where(rows <= cols, 1.0, 0.0).astype(BF16)
    cum = jnp.dot(onehot.astype(BF16), upper, preferred_element_type=F32)
    carry = carry_ref[:, 0:1]
    rank = jnp.sum(onehot * (cum - 1.0 + carry), axis=0, keepdims=True)
    new_carry = carry + jnp.sum(onehot, axis=1, keepdims=True)
    carry_ref[...] = jnp.broadcast_to(new_carry, carry_ref.shape)
    cnt_ref[...] = jnp.broadcast_to(new_carry, cnt_ref.shape)
    zero = jnp.zeros_like(cls)
    route_ref[...] = jnp.concatenate([cls, rank, zero, zero, zero, zero, zero, zero], axis=0)


def _router_weights(w_router):
    w = jnp.pad(w_router.astype(F32), ((0, 0), (0, LANES - w_router.shape[1])))
    hi = w.astype(BF16)
    lo = (w - hi.astype(F32)).astype(BF16)
    return jnp.stack([hi, lo])


ROW_DMA_UNROLL = 8


def _dispatch_kernel(pos_ref, x_ref, g_ref, mod_ref, init_ref, o_ref, stage_ref, sems):
    del init_ref
    tokens, d = x_ref.shape
    per = d // LANES
    i = pl.program_id(0)
    slot = i % 2
    base = i * tokens
    _store_token_tiles(stage_ref.at[slot], _modulated(x_ref[...], g_ref[...], mod_ref[...], 3))

    def issue(r, carry):
        src = pl.multiple_of(r * per, per)
        dst = pl.multiple_of(pos_ref[base + r], per)
        pltpu.make_async_copy(stage_ref.at[slot, pl.ds(src, per)], o_ref.at[pl.ds(dst, per)],
                              sems.at[slot]).start()
        return carry

    lax.fori_loop(0, tokens, issue, 0, unroll=ROW_DMA_UNROLL)

    def wait_tile(s):
        pltpu.make_async_copy(stage_ref.at[s], o_ref.at[pl.ds(0, tokens * per)], sems.at[s]).wait()

    @pl.when(i > 0)
    def _():
        wait_tile(1 - slot)

    @pl.when(i == pl.num_programs(0) - 1)
    def _():
        wait_tile(slot)


def _dispatch_call(pos_rows, xs, g, mods, layer, init, dims):
    tm, d = TOKEN_TILE, dims["d"]
    per = d // LANES
    tpb, ct, b = dims["tpb"], dims["ct"], dims["b"]
    return pl.pallas_call(
        _dispatch_kernel,
        grid_spec=pltpu.PrefetchScalarGridSpec(
            num_scalar_prefetch=1,
            grid=(dims["tiles"],),
            in_specs=[pl.BlockSpec((tm, d), lambda i, p: (i, 0)),
                      pl.BlockSpec((1, d), lambda i, p: (0, 0)),
                      pl.BlockSpec((None, None, 1, 6 * d),
                                   lambda i, p: (layer, _mod_row(i, tpb, ct, b), 0, 0)),
                      pl.BlockSpec(memory_space=pl.ANY)],
            out_specs=pl.BlockSpec(memory_space=pl.ANY),
            scratch_shapes=[pltpu.VMEM((2, tm * per, LANES), F32), pltpu.SemaphoreType.DMA((2,))],
        ),
        out_shape=jax.ShapeDtypeStruct(init.shape, F32),
        input_output_aliases={4: 0},
        compiler_params=_params("arbitrary"),
        name="dispatch",
    )(pos_rows, xs, g, mods, init)


def _expert_kernel(ea_ref, eb_ref, ok_ref, x_ref, wr_ref, w1a, w3a, w2a, w1b, w3b, w2b, o_ref):
    d = w1a.shape[0]
    te = x_ref.shape[0] // (d // LANES)
    t = pl.program_id(0)

    @pl.when(ok_ref[t] != 0)
    def _():
        h = _load_token_tiles(x_ref, te, d)
        x = h.astype(BF16)
        dw = wr_ref[pl.ds(ea_ref[t], 1), :] - wr_ref[pl.ds(eb_ref[t], 1), :]
        gate_a = jax.nn.sigmoid(jnp.sum(h * dw, axis=1, keepdims=True))
        gate_b = 1.0 - gate_a

        def expert(w1, w3, w2):
            a = jnp.dot(x, w1[...].astype(BF16), preferred_element_type=F32)
            b = jnp.dot(x, w3[...].astype(BF16), preferred_element_type=F32)
            u = (a * jax.nn.sigmoid(a)) * b
            return jnp.dot(u.astype(BF16), w2[...].astype(BF16), preferred_element_type=F32)

        _store_token_tiles(o_ref, gate_a * expert(w1a, w3a, w2a) + gate_b * expert(w1b, w3b, w2b))

    @pl.when(ok_ref[t] == 0)
    def _():
        o_ref[...] = jnp.zeros_like(o_ref)


def _expert_call(tile_a, tile_b, tile_ok, xsorted, w_router_t, w1, w3, w2, layer, d):
    te = TOKEN_TILE
    per = d // LANES
    n_tiles = xsorted.shape[0] // (te * per)
    de = w1.shape[-1]
    sel_a = lambda t, ea, eb, ok: (layer, ea[t], 0, 0)
    sel_b = lambda t, ea, eb, ok: (layer, eb[t], 0, 0)
    used = lambda t, ea, eb, ok: (t, 0)
    return pl.pallas_call(
        _expert_kernel,
        grid_spec=pltpu.PrefetchScalarGridSpec(
            num_scalar_prefetch=3,
            grid=(n_tiles,),
            in_specs=[
                pl.BlockSpec((te * per, LANES), used),
                pl.BlockSpec(w_router_t.shape, lambda t, ea, eb, ok: (0, 0)),
                pl.BlockSpec((None, None, d, de), sel_a), pl.BlockSpec((None, None, d, de), sel_a),
                pl.BlockSpec((None, None, de, d), sel_a),
                pl.BlockSpec((None, None, d, de), sel_b), pl.BlockSpec((None, None, d, de), sel_b),
                pl.BlockSpec((None, None, de, d), sel_b),
            ],
            out_specs=pl.BlockSpec((te * per, LANES), used),
        ),
        out_shape=jax.ShapeDtypeStruct((n_tiles * te * per, LANES), F32),
        compiler_params=_params("arbitrary"),
        name="experts",
    )(tile_a, tile_b, tile_ok, xsorted, w_router_t, w1, w3, w2, w1, w3, w2)


def _combine_kernel(pos_ref, y_ref, x_ref, mod_ref, gf_ref, o_ref, buf_ref, sems,
                    *, tile_of, final_norm):
    tm, d = x_ref.shape
    per = d // LANES
    i = pl.program_id(0)
    slot = i % 2

    def gather(step, s):
        base = tile_of(step) * tm

        def issue(r, carry):
            src = pl.multiple_of(pos_ref[base + r], per)
            dst = pl.multiple_of(r * per, per)
            pltpu.make_async_copy(y_ref.at[pl.ds(src, per)], buf_ref.at[s, pl.ds(dst, per)],
                                  sems.at[s]).start()
            return carry

        lax.fori_loop(0, tm, issue, 0, unroll=ROW_DMA_UNROLL)

    @pl.when(i == 0)
    def _():
        gather(0, 0)

    @pl.when(i + 1 < pl.num_programs(0))
    def _():
        gather(i + 1, 1 - slot)

    pltpu.make_async_copy(y_ref.at[pl.ds(0, tm * per)], buf_ref.at[slot], sems.at[slot]).wait()
    out = x_ref[...] + mod_ref[:, 5 * d:6 * d] * _load_token_tiles(buf_ref.at[slot], tm, d)
    if final_norm:
        ms = jnp.mean(out * out, axis=-1, keepdims=True)
        out = out * lax.rsqrt(ms + NORM_EPS) * gf_ref[...]
    o_ref[...] = out


def _combine_call(pos, ysorted, xs, mods, layer, g_final, dims, final_norm):
    tm, d = TOKEN_TILE, dims["d"]
    tpb, ct, b = dims["tpb"], dims["ct"], dims["b"]
    if final_norm:
        lt = tpb - ct
        tile_of = lambda i: (i // lt) * tpb + ct + i % lt
        n_tiles = b * lt
    else:
        tile_of = lambda i: i
        n_tiles = dims["tiles"]
    mod_spec = pl.BlockSpec((None, None, 1, 6 * d),
                            lambda i, p: (layer, _mod_row(tile_of(i), tpb, ct, b), 0, 0))
    return pl.pallas_call(
        functools.partial(_combine_kernel, tile_of=tile_of, final_norm=final_norm),
        grid_spec=pltpu.PrefetchScalarGridSpec(
            num_scalar_prefetch=1,
            grid=(n_tiles,),
            in_specs=[
                pl.BlockSpec(memory_space=pl.ANY),
                pl.BlockSpec((tm, d), lambda i, p: (tile_of(i), 0)),
                mod_spec,
                pl.BlockSpec((1, d), lambda i, p: (0, 0)),
            ],
            out_specs=pl.BlockSpec((tm, d), lambda i, p: (i, 0)),
            scratch_shapes=[pltpu.VMEM((2, tm * (d // LANES), LANES), F32), pltpu.SemaphoreType.DMA((2,))],
        ),
        out_shape=jax.ShapeDtypeStruct((n_tiles * tm, d), F32),
        compiler_params=_params("arbitrary"),
        name="combine",
    )(pos, ysorted, xs, mods, g_final)


_CLASS_A = np.array([g * GROUP_SIZE + a for g in range(N_GROUPS) for a, _ in PAIR_SLOTS], np.int32)
_CLASS_B = np.array([g * GROUP_SIZE + b for g in range(N_GROUPS) for _, b in PAIR_SLOTS], np.int32)


def _routing_plan(route, counts, n_tiles_padded):
    te = TOKEN_TILE
    cls = route[0].astype(jnp.int32)
    rank = route[1].astype(jnp.int32)
    cnt = counts[:N_CLASSES, 0].astype(jnp.int32)
    tiles = (cnt + te - 1) // te
    tile_end = jnp.cumsum(tiles)
    tile_start = tile_end - tiles
    onehot = (cls[:, None] == jnp.arange(N_CLASSES, dtype=jnp.int32)[None, :]).astype(jnp.int32)
    pos = jnp.sum(onehot * (tile_start * te)[None, :], axis=1) + rank
    t = jnp.arange(n_tiles_padded, dtype=jnp.int32)
    ok = (t < tile_end[-1]).astype(jnp.int32)
    tcls = jnp.sum((tile_end[None, :] <= jnp.minimum(t, tile_end[-1] - 1)[:, None]).astype(jnp.int32), axis=1)
    tile_a = jnp.take(jnp.asarray(_CLASS_A), tcls)
    tile_b = jnp.take(jnp.asarray(_CLASS_B), tcls)
    return pos, tile_a, tile_b, ok


def _moe(xs, g_ffn, route, counts, sorted_buf, mods, layer, w_router_t, w1, w3, w2, g_final, dims, final_norm):
    per = dims["d"] // LANES
    n_tiles_padded = sorted_buf.shape[0] // (TOKEN_TILE * per)
    pos, tile_a, tile_b, tile_ok = _routing_plan(route, counts, n_tiles_padded)
    xsorted = _dispatch_call(pos * per, xs, g_ffn, mods, layer, sorted_buf, dims)
    ysorted = _expert_call(tile_a, tile_b, tile_ok, xsorted, w_router_t, w1, w3, w2, layer, dims["d"])
    return _combine_call(pos * per, ysorted, xs, mods, layer, g_final, dims, final_norm), xsorted


def _pos_table(n_tokens, dim):
    rows = n_tokens // GRID_WIDTH
    r, col = jnp.meshgrid(jnp.arange(rows, dtype=F32), jnp.arange(GRID_WIDTH, dtype=F32), indexing="ij")
    quarter = dim // 4
    freqs = jnp.exp(-math.log(POS_BASE) * jnp.arange(quarter, dtype=F32) / quarter)

    def enc(p):
        ang = p.reshape(-1, 1) * freqs
        return jnp.concatenate([jnp.sin(ang), jnp.cos(ang)], axis=-1)

    return jnp.concatenate([enc(r), enc(col)], axis=-1)


def _mlstm_weights(w_in, b_gate):
    nq = HEADS * HEAD_QK
    nv = HEADS * HEAD_V
    wq = w_in[:, 0:nq] * (HEAD_QK ** -0.5)
    wk = w_in[:, nq:2 * nq]
    wv = w_in[:, 2 * nq:2 * nq + nv]
    wo = w_in[:, 2 * nq + nv:2 * nq + 2 * nv]
    wg = w_in[:, 2 * nq + 2 * nv:]
    wg_pad = jnp.pad(wg, ((0, 0), (0, LANES - wg.shape[1])))
    w_tok = jnp.concatenate([wq, wk, wg_pad], axis=1).astype(BF16)
    w_feat = jnp.concatenate([wv, wo, wg], axis=1).T.astype(BF16)
    bias = b_gate.reshape(-1).astype(F32)
    bias_c = jnp.pad(bias, (0, LANES - bias.shape[0])).reshape(1, LANES)
    bias_r = bias.reshape(-1, 1)
    return w_tok, w_feat, bias_c, bias_r


def kernel(x, c, ctx, c_ctx, w_ada, b_ada, g_mix, g_ffn, g_final, m_w_in, m_b_gate, m_g_head, m_w_out, r_w_in, r_conv_w, r_conv_b, r_w_gate, r_b_gate, r_lam, r_w_out, w_router, b_router, e_w1, e_w3, e_w2):
    batch, t_len, d = x.shape
    ctx_len = ctx.shape[1]
    depth = w_ada.shape[0]
    tm = TOKEN_TILE
    assert t_len % tm == 0 and ctx_len % tm == 0 and batch + 1 <= ADA_ROWS
    assert d == HEADS * HEAD_V and t_len % GRID_WIDTH == 0
    s_len = ctx_len + t_len
    dims = dict(b=batch, d=d, n=batch * s_len, tpb=s_len // tm, ct=ctx_len // tm,
                tiles=batch * s_len // tm)

    cvec = jnp.concatenate([c, c_ctx[None, :], jnp.zeros((ADA_ROWS - batch - 1, d), F32)], axis=0)
    mods = _ada_call(cvec, w_ada, b_ada).reshape(depth, ADA_ROWS, 1, 6 * d)
    xs = (ctx.reshape(batch * ctx_len, d), x.reshape(batch * t_len, d), _pos_table(t_len, d))

    w_router_t = w_router.T.astype(F32)
    w_router_pieces = _router_weights(w_router)
    b_router_col = b_router.reshape(-1, 1).astype(F32)
    g_final2 = g_final.reshape(1, d)
    sorted_buf = jnp.zeros(((dims["tiles"] + N_CLASSES) * tm * (d // LANES), LANES), F32)
    out = None
    for i in range(depth):
        j = i // 2
        g_mix_i = g_mix[i].reshape(1, d)
        g_ffn_i = g_ffn[i].reshape(1, d)
        if i % 2 == 0:
            w_tok, w_feat, bias_c, bias_r = _mlstm_weights(m_w_in[j], m_b_gate[j])
            proj = _mproj_call(xs, g_mix_i, mods, i, w_tok, w_feat, dims)
            if isinstance(xs, tuple):
                xs, proj = proj[0], proj[1:]
            q, k, vt, ot, gc, gr = proj
            hf, hb = _mlstm_call(q, k, vt, gc, gr, bias_c, bias_r, dims)
            w_out = (m_g_head[j][:, None] * m_w_out[j]).astype(BF16)
            xs, route, counts = _readout_call(_mreadout_kernel, "mlstm_readout", True, hf, hb, ot, w_out,
                                              xs, mods, i, g_ffn_i, w_router_pieces, b_router_col, dims)
        else:
            gy, xc = _rproj_call(xs, g_mix_i, mods, i, r_w_in[j].astype(BF16), r_conv_w[j],
                                 r_conv_b[j].reshape(1, d), dims)
            hf, hb = _lru_call(xc, (0.5 * r_w_gate[j]).astype(BF16), 0.5 * r_b_gate[j].reshape(2, 1, -1),
                               r_lam[j].reshape(2, 1, d), dims)
            xs, route, counts = _readout_call(_readout_kernel, "rglru_readout", False, hf, hb, gy,
                                              r_w_out[j].astype(BF16), xs, mods, i, g_ffn_i,
                                              w_router_pieces, b_router_col, dims)
        last = i == depth - 1
        res, sorted_buf = _moe(xs, g_ffn_i, route, counts, sorted_buf, mods, i, w_router_t,
                               e_w1, e_w3, e_w2, g_final2, dims, last)
        if last:
            out = res
        else:
            xs = res
    return out.reshape(batch, t_len, d)
```

```python
import functools
import math

import jax
import jax.numpy as jnp
import numpy as np
from jax import lax
from jax.experimental import pallas as pl
from jax.experimental.pallas import tpu as pltpu

F32 = jnp.float32
BF16 = jnp.bfloat16
HIGHEST = lax.Precision.HIGHEST

TOKEN_TILE = 256
LANES = 128
SUBLANES = 8
VMEM_LIMIT_BYTES = 56 * 1024 * 1024
NORM_EPS = 1e-6
GRID_WIDTH = 64
POS_BASE = 10000.0
HEADS = 8
HEAD_QK = 64
HEAD_V = 128
LRU_BLOCKS = 8
LRU_C = 8.0
SCAN_PARTS = 4
CONV_TAPS = 4
CONV_LEFT = CONV_TAPS // 2
N_EXPERTS = 16
N_GROUPS = 4
GROUP_SIZE = N_EXPERTS // N_GROUPS
PAIRS = [(a, b) for a in range(GROUP_SIZE) for b in range(a + 1, GROUP_SIZE)]
PAIR_SLOTS = [(0, 1), (0, 2), (0, 3), (1, 3), (1, 2), (3, 2)]
N_CLASSES = N_GROUPS * len(PAIRS)
CLASS_ROWS = 32
ADA_ROWS = 16


def _params(*sem):
    return pltpu.CompilerParams(dimension_semantics=sem, vmem_limit_bytes=VMEM_LIMIT_BYTES)


def _mod_row(i, tiles_per_batch, ctx_tiles, batch):
    return jnp.where(i % tiles_per_batch < ctx_tiles, batch, i // tiles_per_batch)


def _bwd_tile(i, tiles_per_batch, ctx_tiles):
    b = i // tiles_per_batch
    j = i % tiles_per_batch
    jb = jnp.where(j < ctx_tiles, ctx_tiles - 1 - j, tiles_per_batch - 1 - (j - ctx_tiles))
    return b * tiles_per_batch + jb


def _ada_kernel(c_ref, w_ref, b_ref, o_ref):
    c = c_ref[...]
    s = c * jax.nn.sigmoid(c)
    o_ref[...] = jnp.dot(s, w_ref[...], preferred_element_type=F32, precision=HIGHEST) + b_ref[...]


def _ada_call(cvec, w_ada, b_ada):
    depth, d, n = w_ada.shape
    tn = n // 4
    return pl.pallas_call(
        _ada_kernel,
        grid=(depth, n // tn),
        in_specs=[
            pl.BlockSpec((ADA_ROWS, d), lambda l, j: (0, 0)),
            pl.BlockSpec((None, d, tn), lambda l, j: (l, 0, j)),
            pl.BlockSpec((None, 1, tn), lambda l, j: (l, 0, j)),
        ],
        out_specs=pl.BlockSpec((None, ADA_ROWS, tn), lambda l, j: (l, 0, j)),
        out_shape=jax.ShapeDtypeStruct((depth, ADA_ROWS, n), F32),
        compiler_params=_params("arbitrary", "arbitrary"),
        name="ada",
    )(cvec, w_ada, b_ada.reshape(depth, 1, n))


def _modulated(x, g, mod, shift_idx):
    d = x.shape[-1]
    ms = jnp.mean(x * x, axis=-1, keepdims=True)
    xn = x * lax.rsqrt(ms + NORM_EPS) * g
    shift = mod[:, shift_idx * d:(shift_idx + 1) * d]
    scale = mod[:, (shift_idx + 1) * d:(shift_idx + 2) * d]
    return xn * (1.0 + scale) + shift


def _mod_spec(layer, dims):
    tpb, ct, b = dims["tpb"], dims["ct"], dims["b"]
    return pl.BlockSpec((None, None, 1, 6 * dims["d"]),
                        lambda i, *_: (layer, _mod_row(i, tpb, ct, b), 0, 0))


def _store_token_tiles(ref, x):
    tokens, d = x.shape
    per = d // LANES
    for s in range(per):
        ref[pl.ds(s, tokens, stride=per), :] = x[:, s * LANES:(s + 1) * LANES]


def _load_token_tiles(ref, tokens, d):
    per = d // LANES
    return jnp.concatenate([ref[pl.ds(s, tokens, stride=per), :] for s in range(per)], axis=1)


def _gelu_tanh(y):
    return 0.5 * y * (1.0 + jnp.tanh(math.sqrt(2.0 / math.pi) * (y + 0.044715 * (y * y * y))))


SRC_BLOCK = 128
_NT = (((1,), (1,)), ((), ()))
_TN = (((0,), (0,)), ((), ()))


def _mproj_kernel(*refs, assemble, tiles_per_batch, ctx_tiles):
    if assemble:
        ctx_ref, lat_ref, pos_ref, g_ref, mod_ref, w_ref, wt_ref, xs_ref = refs[:8]
        is_ctx = pl.program_id(0) % tiles_per_batch < ctx_tiles
        x = jnp.where(is_ctx, ctx_ref[...], lat_ref[...] + pos_ref[...])
        xs_ref[...] = x
    else:
        x_ref, g_ref, mod_ref, w_ref, wt_ref = refs[:5]
        x = x_ref[...]
    q_ref, k_ref, vt_ref, ot_ref, gc_ref, gr_ref = refs[-6:]
    d = x.shape[-1]
    nq = q_ref.shape[-1]
    h = _modulated(x, g_ref[...], mod_ref[...], 0).astype(BF16)
    q_ref[...] = jnp.dot(h, w_ref[:, 0:nq], preferred_element_type=F32).astype(BF16)
    k_ref[...] = jnp.dot(h, w_ref[:, nq:2 * nq], preferred_element_type=F32).astype(BF16)
    gc_ref[...] = jnp.dot(h, w_ref[:, 2 * nq:2 * nq + LANES], preferred_element_type=F32)
    vt_ref[...] = lax.dot_general(wt_ref[0:d, :], h, _NT, preferred_element_type=F32).astype(BF16)
    ot = lax.dot_general(wt_ref[d:2 * d, :], h, _NT, preferred_element_type=F32)
    ot_ref[...] = jax.nn.sigmoid(ot).astype(BF16)
    gr_ref[...] = lax.dot_general(wt_ref[2 * d:2 * d + 4 * HEADS, :], h, _NT, preferred_element_type=F32)


def _mproj_call(stream, g, mods, layer, w_tok, w_feat, dims):
    tm, d, n = TOKEN_TILE, dims["d"], dims["n"]
    tpb, ct = dims["tpb"], dims["ct"]
    lt = tpb - ct
    nq = HEADS * HEAD_QK
    row = lambda i: (i, 0)
    col = lambda i: (0, i)
    const = lambda i: (0, 0)
    assemble = isinstance(stream, tuple)
    if assemble:
        stream_specs = [
            pl.BlockSpec((tm, d), lambda i: ((i // tpb) * ct + jnp.minimum(i % tpb, ct - 1), 0)),
            pl.BlockSpec((tm, d), lambda i: ((i // tpb) * lt + jnp.maximum(i % tpb - ct, 0), 0)),
            pl.BlockSpec((tm, d), lambda i: (jnp.maximum(i % tpb - ct, 0), 0)),
        ]
        stream_out = [pl.BlockSpec((tm, d), row)], [jax.ShapeDtypeStruct((n, d), F32)]
    else:
        stream, stream_specs, stream_out = (stream,), [pl.BlockSpec((tm, d), row)], ([], [])
    return pl.pallas_call(
        functools.partial(_mproj_kernel, assemble=assemble, tiles_per_batch=tpb, ctx_tiles=ct),
        grid=(dims["tiles"],),
        in_specs=stream_specs + [
            pl.BlockSpec((1, d), const),
            _mod_spec(layer, dims),
            pl.BlockSpec(w_tok.shape, const),
            pl.BlockSpec(w_feat.shape, const),
        ],
        out_specs=stream_out[0] + [
            pl.BlockSpec((tm, nq), row),
            pl.BlockSpec((tm, nq), row),
            pl.BlockSpec((d, tm), col),
            pl.BlockSpec((d, tm), col),
            pl.BlockSpec((tm, LANES), row),
            pl.BlockSpec((4 * HEADS, tm), col),
        ],
        out_shape=stream_out[1] + [
            jax.ShapeDtypeStruct((n, nq), BF16),
            jax.ShapeDtypeStruct((n, nq), BF16),
            jax.ShapeDtypeStruct((d, n), BF16),
            jax.ShapeDtypeStruct((d, n), BF16),
            jax.ShapeDtypeStruct((n, LANES), F32),
            jax.ShapeDtypeStruct((4 * HEADS, n), F32),
        ],
        compiler_params=_params("arbitrary"),
        name="mlstm_proj",
    )(*stream, g, mods, w_tok, w_feat)


def _mlstm_gate_stats(gc_ref, gr_ref, bc_ref, br_ref, m_ref, *, backward):
    tm = gc_ref.shape[0]
    gate_i = 2 * HEADS if backward else 0
    gate_f = gate_i + HEADS
    last = 0 if backward else tm - 1
    src = lax.broadcasted_iota(jnp.int32, (tm, tm), 0)
    tgt = lax.broadcasted_iota(jnp.int32, (tm, tm), 1)
    visible = (src >= tgt) if backward else (src <= tgt)
    neg_mask = jnp.where(visible, 0.0, -jnp.inf)
    xc = gc_ref[...] + bc_ref[...]
    xr = gr_ref[...] + br_ref[...]
    before = (src <= tgt) if backward else (src >= tgt)
    cum_c = _split_dot(jnp.where(before, 1.0, 0.0).astype(BF16), jax.nn.log_sigmoid(xc), left=True)
    cum_r = _split_dot(jnp.where(visible, 1.0, 0.0).astype(BF16), jax.nn.log_sigmoid(xr), left=False)
    b_row = cum_r[gate_f:gate_f + HEADS, :]
    c_row = xr[gate_i:gate_i + HEADS, :] - b_row
    c_col = xc - pltpu.roll(cum_c, LANES - HEADS, axis=1)
    run_col = c_col
    shift = 1
    while shift < tm:
        fill = jnp.full((shift, LANES), -jnp.inf, F32)
        if backward:
            moved = jnp.concatenate([run_col[shift:, :], fill], axis=0)
        else:
            moved = jnp.concatenate([fill, run_col[:tm - shift, :]], axis=0)
        run_col = jnp.maximum(run_col, moved)
        shift *= 2
    run = run_col.T[gate_i:gate_i + HEADS, :]
    m = m_ref[:, 0:1]
    log_inter = b_row + m
    m_pos = jnp.maximum(log_inter, b_row + run)
    shift_row = b_row - m_pos
    w_inter = jnp.exp(log_inter - m_pos)
    floor = jnp.exp(-m_pos)
    b_last = b_row[:, last:last + 1]
    m_new = m_pos[:, last:last + 1]
    w_src = jnp.exp(b_last + c_row - m_new)
    decay = jnp.exp(b_last + m - m_new)
    m_ref[...] = jnp.broadcast_to(m_new, m_ref.shape)
    return neg_mask, c_col, shift_row, w_inter, floor, w_src, decay


def _mlstm_head(h, q_ref, k_ref, vt_ref, o_ref, ct_ref, stats, *, backward):
    neg_mask, c_col, shift_row, w_inter, floor, w_src, decay = stats
    tm = q_ref.shape[0]
    gate_i = 2 * HEADS if backward else 0
    half = lax.broadcasted_iota(jnp.int32, (tm, LANES), 1) < HEAD_QK
    ones = jnp.ones((2 * SUBLANES, tm), BF16)
    pair = slice((h // 2) * LANES, (h // 2 + 1) * LANES)
    qp = q_ref[:, pair]
    km = jnp.where(half if h % 2 == 0 else jnp.logical_not(half), k_ref[:, pair], 0.0).astype(BF16)
    vaug = jnp.concatenate([vt_ref[h * HEAD_V:(h + 1) * HEAD_V, :], ones], axis=0)
    ct = ct_ref[h]
    acc = w_inter[h:h + 1, :] * lax.dot_general(ct.astype(BF16), qp, _NT, preferred_element_type=F32)
    src_block = SRC_BLOCK if (h + int(backward)) % 2 == 0 else tm
    for blk in range(tm // src_block):
        rows = slice(blk * src_block, (blk + 1) * src_block)
        st = lax.dot_general(km[rows, :], qp, _NT, preferred_element_type=F32)
        e = (neg_mask[rows, :] + shift_row[h:h + 1, :]) + c_col[rows, gate_i + h:gate_i + h + 1]
        p = (st * jnp.exp(e)).astype(BF16)
        acc = acc + jnp.dot(vaug[:, rows], p, preferred_element_type=F32)
    den = acc[HEAD_V:HEAD_V + 1, :]
    scale = 1.0 / jnp.maximum(jnp.abs(den), floor[h:h + 1, :])
    o_ref[h * HEAD_V:(h + 1) * HEAD_V, :] = (acc[0:HEAD_V, :] * scale).astype(o_ref.dtype)
    vs = (vaug.astype(F32) * w_src[h:h + 1, :]).astype(BF16)
    ct_ref[h] = decay[h:h + 1, :] * ct + jnp.dot(vs, km, preferred_element_type=F32)


def _split_dot(ones_mat, x, left):
    hi = x.astype(BF16)
    r1 = x - hi.astype(F32)
    mid = r1.astype(BF16)
    lo = (r1 - mid.astype(F32)).astype(BF16)
    out = None
    for piece in (lo, mid, hi):
        t = (jnp.dot(ones_mat, piece, preferred_element_type=F32) if left
             else jnp.dot(piece, ones_mat, preferred_element_type=F32))
        out = t if out is None else out + t
    return out


def _mlstm_kernel(qf, kf, vtf, gcf, grf, qb, kb, vtb, gcb, grb, bc_ref, br_ref, of_ref, ob_ref,
                  cf_ref, mf_ref, cb_ref, mb_ref, *, tiles_per_batch):
    @pl.when(pl.program_id(0) % tiles_per_batch == 0)
    def _():
        cf_ref[...] = jnp.zeros_like(cf_ref)
        cb_ref[...] = jnp.zeros_like(cb_ref)
        mf_ref[...] = jnp.zeros_like(mf_ref)
        mb_ref[...] = jnp.zeros_like(mb_ref)

    stats_f = _mlstm_gate_stats(gcf, grf, bc_ref, br_ref, mf_ref, backward=False)
    stats_b = _mlstm_gate_stats(gcb, grb, bc_ref, br_ref, mb_ref, backward=True)
    for h in range(HEADS):
        _mlstm_head(h, qf, kf, vtf, of_ref, cf_ref, stats_f, backward=False)
    for h in range(HEADS):
        _mlstm_head(h, qb, kb, vtb, ob_ref, cb_ref, stats_b, backward=True)


def _mlstm_call(q, k, vt, gc, gr, bias_c, bias_r, dims):
    tm, d, n = TOKEN_TILE, dims["d"], dims["n"]
    tpb, ct = dims["tpb"], dims["ct"]
    nq = q.shape[1]
    fwd = lambda i: (i, 0)
    bwd = lambda i: (_bwd_tile(i, tpb, ct), 0)
    fwd_t = lambda i: (0, i)
    bwd_t = lambda i: (0, _bwd_tile(i, tpb, ct))
    const = lambda i: (0, 0)
    state = [pltpu.VMEM((HEADS, HEAD_V + 2 * SUBLANES, LANES), F32), pltpu.VMEM((HEADS, LANES), F32)]
    return pl.pallas_call(
        functools.partial(_mlstm_kernel, tiles_per_batch=tpb),
        grid=(dims["tiles"],),
        in_specs=[
            pl.BlockSpec((tm, nq), fwd), pl.BlockSpec((tm, nq), fwd), pl.BlockSpec((d, tm), fwd_t),
            pl.BlockSpec((tm, LANES), fwd), pl.BlockSpec((4 * HEADS, tm), fwd_t),
            pl.BlockSpec((tm, nq), bwd), pl.BlockSpec((tm, nq), bwd), pl.BlockSpec((d, tm), bwd_t),
            pl.BlockSpec((tm, LANES), bwd), pl.BlockSpec((4 * HEADS, tm), bwd_t),
            pl.BlockSpec((1, LANES), const), pl.BlockSpec((4 * HEADS, 1), const),
        ],
        out_specs=[pl.BlockSpec((d, tm), fwd_t), pl.BlockSpec((d, tm), bwd_t)],
        out_shape=[jax.ShapeDtypeStruct((d, n), BF16), jax.ShapeDtypeStruct((d, n), BF16)],
        scratch_shapes=state + state,
        compiler_params=_params("arbitrary"),
        name="mlstm_scan",
    )(q, k, vt, gc, gr, q, k, vt, gc, gr, bias_c, bias_r)


def _finish_readout(x_new, mod_ref, g_ref, wr_ref, br_ref, o_ref, route_ref, cnt_ref, carry_ref):
    @pl.when(pl.program_id(0) == 0)
    def _():
        carry_ref[...] = jnp.zeros_like(carry_ref)

    o_ref[...] = x_new
    _route_tile(x_new, g_ref, mod_ref, wr_ref, br_ref, route_ref, cnt_ref, carry_ref)


def _mreadout_kernel(hf_ref, hb_ref, ot_ref, w_ref, x_ref, mod_ref, *route_refs):
    d = x_ref.shape[-1]
    parts = []
    for h in range(HEADS):
        rows = slice(h * HEAD_V, (h + 1) * HEAD_V)
        hs = hf_ref[rows, :].astype(F32) + hb_ref[rows, :].astype(F32)
        r = lax.rsqrt(jnp.mean(hs * hs, axis=0, keepdims=True) + NORM_EPS)
        parts.append((hs * r * ot_ref[rows, :]).astype(BF16))
    t = jnp.concatenate(parts, axis=0)
    out = lax.dot_general(t, w_ref[...], _TN, preferred_element_type=F32)
    _finish_readout(x_ref[...] + mod_ref[:, 2 * d:3 * d] * out, mod_ref, *route_refs)


def _readout_kernel(hf_ref, hb_ref, gy_ref, w_ref, x_ref, mod_ref, *route_refs):
    d = x_ref.shape[-1]
    t = _gelu_tanh(gy_ref[...].astype(F32)) * (hf_ref[...].astype(F32) + hb_ref[...].astype(F32))
    out = jnp.dot(t.astype(BF16), w_ref[...], preferred_element_type=F32)
    _finish_readout(x_ref[...] + mod_ref[:, 2 * d:3 * d] * out, mod_ref, *route_refs)


def _readout_call(kernel_fn, name, feature_major, hf, hb, aux, w_out, xs, mods, layer,
                  g_ffn, w_router_pieces, b_router_col, dims):
    tm, d, n = TOKEN_TILE, dims["d"], dims["n"]
    row = lambda i: (i, 0)
    col = lambda i: (0, i)
    const = lambda i: (0, 0)
    mixer = pl.BlockSpec((d, tm), col) if feature_major else pl.BlockSpec((tm, d), row)
    return pl.pallas_call(
        kernel_fn,
        grid=(dims["tiles"],),
        in_specs=[mixer, mixer, mixer,
                  pl.BlockSpec((d, d), const), pl.BlockSpec((tm, d), row), _mod_spec(layer, dims),
                  pl.BlockSpec((1, d), const),
                  pl.BlockSpec(w_router_pieces.shape, lambda i: (0, 0, 0)),
                  pl.BlockSpec(b_router_col.shape, const)],
        out_specs=[pl.BlockSpec((tm, d), row),
                   pl.BlockSpec((SUBLANES, tm), col),
                   pl.BlockSpec((CLASS_ROWS, LANES), const)],
        out_shape=[jax.ShapeDtypeStruct((n, d), F32),
                   jax.ShapeDtypeStruct((SUBLANES, n), F32),
                   jax.ShapeDtypeStruct((CLASS_ROWS, LANES), F32)],
        scratch_shapes=[pltpu.VMEM((CLASS_ROWS, LANES), F32)],
        compiler_params=_params("arbitrary"),
        name=name,
    )(hf, hb, aux, w_out, xs, mods, g_ffn, w_router_pieces, b_router_col)


def _rproj_kernel(x_ref, g_ref, mod_ref, w_ref, cw_ref, cb_ref, gy_ref, xc_ref, ext_ref,
                  *, tiles_per_batch, ctx_tiles):
    tm, d = x_ref.shape
    i = pl.program_id(0)

    def in_segment_neighbours(t):
        j = t % tiles_per_batch
        first = jnp.logical_or(j == 0, j == ctx_tiles)
        last = jnp.logical_or(j == ctx_tiles - 1, j == tiles_per_batch - 1)
        return jnp.logical_not(first), jnp.logical_not(last)

    @pl.when(i == 0)
    def _():
        ext_ref[...] = jnp.zeros_like(ext_ref)

    h = _modulated(x_ref[...], g_ref[...], mod_ref[...], 0).astype(BF16)
    gy_ref[...] = jnp.dot(h, w_ref[:, 0:d], preferred_element_type=F32).astype(BF16)
    cur = jnp.dot(h, w_ref[:, d:2 * d], preferred_element_type=F32)

    xc = cb_ref[...] + ext_ref[pl.ds(SUBLANES - CONV_LEFT, tm), :] * cw_ref[0:1, :]
    for k in range(1, CONV_TAPS):
        xc = xc + ext_ref[pl.ds(SUBLANES - CONV_LEFT + k, tm), :] * cw_ref[k:k + 1, :]
    xc_ref[...] = xc
    _, next_ok = in_segment_neighbours(jnp.maximum(i - 1, 0))
    ahead = jnp.where(next_ok, cur[0:1, :] * cw_ref[CONV_TAPS - 1:CONV_TAPS, :], 0.0)
    xc_ref[tm - 1:tm, :] = xc[tm - 1:tm, :] + ahead

    prev_ok, _ = in_segment_neighbours(i)
    ext_ref[0:SUBLANES, :] = jnp.where(prev_ok, ext_ref[tm:tm + SUBLANES, :], 0.0)
    ext_ref[SUBLANES:SUBLANES + tm, :] = cur


def _rproj_call(xs, g, mods, layer, w_in, conv_w, conv_b, dims):
    tm, d, n, tiles = TOKEN_TILE, dims["d"], dims["n"], dims["tiles"]
    tpb, ct, b = dims["tpb"], dims["ct"], dims["b"]
    cur = lambda i: (jnp.minimum(i, tiles - 1), 0)
    lag = lambda i: (jnp.maximum(i - 1, 0), 0)
    const = lambda i: (0, 0)
    mod_spec = pl.BlockSpec((None, None, 1, 6 * d),
                            lambda i: (layer, _mod_row(jnp.minimum(i, tiles - 1), tpb, ct, b), 0, 0))
    return pl.pallas_call(
        functools.partial(_rproj_kernel, tiles_per_batch=tpb, ctx_tiles=ct),
        grid=(tiles + 1,),
        in_specs=[pl.BlockSpec((tm, d), cur), pl.BlockSpec((1, d), const), mod_spec,
                  pl.BlockSpec(w_in.shape, const), pl.BlockSpec(conv_w.shape, const),
                  pl.BlockSpec(conv_b.shape, const)],
        out_specs=[pl.BlockSpec((tm, d), cur), pl.BlockSpec((tm, d), lag)],
        out_shape=[jax.ShapeDtypeStruct((n, d), BF16), jax.ShapeDtypeStruct((n, d), F32)],
        scratch_shapes=[pltpu.VMEM((tm + 2 * SUBLANES, d), F32)],
        compiler_params=_params("arbitrary"),
        name="rglru_proj",
    )(xs, g, mods, w_in, conv_w, conv_b)


def _lru_direction(xc_ref, wg_ref, bg_ref, lam_ref, a_ref, b_ref, *, direction, r0, nr):
    tm, d = xc_ref.shape
    bw = d // LRU_BLOCKS
    z = -lam_ref[direction]
    softplus = jnp.maximum(z, 0.0) + jnp.log(1.0 + jnp.exp(-jnp.abs(z)))
    k = (-0.5 * LRU_C) * softplus
    for nb in range(LRU_BLOCKS):
        sl = slice(nb * bw, (nb + 1) * bw)
        xb = xc_ref[r0:r0 + nr, sl]
        gg = jnp.dot(xb.astype(BF16), wg_ref[direction, nb], preferred_element_type=F32)
        t = jnp.tanh(gg + bg_ref[direction, :, nb * 2 * bw:(nb + 1) * 2 * bw])
        log_a = k[:, sl] * t[:, :bw] + k[:, sl]
        th = jnp.tanh(log_a)
        first = nb + r0 * LRU_BLOCKS
        a_ref[pl.ds(first, nr, stride=LRU_BLOCKS), :] = jnp.exp(log_a)
        b_ref[pl.ds(first, nr, stride=LRU_BLOCKS), :] = (jnp.sqrt(-0.5 * th / (1.0 - th))
                                                         * ((t[:, bw:] + 1.0) * xb))


def _lru_kernel(xc_f, xc_b, wg_ref, bg_ref, lam_ref, of_ref, ob_ref, af_ref, bf_ref, ab_ref, bb_ref,
                hf_ref, hb_ref, h_ref, *, tiles_per_batch):
    tm, d = xc_f.shape
    bw = d // LRU_BLOCKS

    @pl.when(pl.program_id(0) % tiles_per_batch == 0)
    def _():
        h_ref[...] = jnp.zeros_like(h_ref)

    def scan(a_ref, b_ref, o_ref, state, order):
        h = h_ref[state]
        for t in order:
            rows = slice(t * LRU_BLOCKS, (t + 1) * LRU_BLOCKS)
            h = a_ref[rows, :] * h + b_ref[rows, :]
            o_ref[rows, :] = h
        h_ref[state] = h

    part = tm // SCAN_PARTS
    gates = (wg_ref, bg_ref, lam_ref)
    for p in range(SCAN_PARTS):
        _lru_direction(xc_f, *gates, af_ref, bf_ref, direction=0, r0=p * part, nr=part)
        scan(af_ref, bf_ref, hf_ref, 0, range(p * part, (p + 1) * part))
    for p in range(SCAN_PARTS - 1, -1, -1):
        _lru_direction(xc_b, *gates, ab_ref, bb_ref, direction=1, r0=p * part, nr=part)
        scan(ab_ref, bb_ref, hb_ref, 1, range((p + 1) * part - 1, p * part - 1, -1))
    for nb in range(LRU_BLOCKS):
        sl = slice(nb * bw, (nb + 1) * bw)
        of_ref[:, sl] = hf_ref[pl.ds(nb, tm, stride=LRU_BLOCKS), :].astype(BF16)
        ob_ref[:, sl] = hb_ref[pl.ds(nb, tm, stride=LRU_BLOCKS), :].astype(BF16)


def _lru_call(xc, w_gate, b_gate, lam, dims):
    tm, d, n = TOKEN_TILE, dims["d"], dims["n"]
    tpb, ct = dims["tpb"], dims["ct"]
    fwd = lambda i: (i, 0)
    bwd = lambda i: (_bwd_tile(i, tpb, ct), 0)
    c3 = lambda i: (0, 0, 0)
    c4 = lambda i: (0, 0, 0, 0)
    time_major = pltpu.VMEM((tm * LRU_BLOCKS, d // LRU_BLOCKS), F32)
    return pl.pallas_call(
        functools.partial(_lru_kernel, tiles_per_batch=tpb),
        grid=(dims["tiles"],),
        in_specs=[
            pl.BlockSpec((tm, d), fwd), pl.BlockSpec((tm, d), bwd),
            pl.BlockSpec(w_gate.shape, c4), pl.BlockSpec(b_gate.shape, c3), pl.BlockSpec(lam.shape, c3),
        ],
        out_specs=[pl.BlockSpec((tm, d), fwd), pl.BlockSpec((tm, d), bwd)],
        out_shape=[jax.ShapeDtypeStruct((n, d), BF16), jax.ShapeDtypeStruct((n, d), BF16)],
        scratch_shapes=[time_major] * 6 + [pltpu.VMEM((2, LRU_BLOCKS, d // LRU_BLOCKS), F32)],
        compiler_params=_params("arbitrary"),
        name="rglru_scan",
    )(xc, xc, w_gate, b_gate, lam)


def _first_argmax(vals):
    best = vals[0]
    idx = jnp.zeros_like(best)
    for k in range(1, len(vals)):
        better = vals[k] > best
        idx = jnp.where(better, float(k), idx)
        best = jnp.where(better, vals[k], best)
    return idx, best


def _pick(idx, vals):
    out = vals[0]
    for k in range(1, len(vals)):
        out = jnp.where(idx == float(k), vals[k], out)
    return out


def _route_tile(x, g_ref, mod_ref, wr_ref, br_ref, route_ref, cnt_ref, carry_ref):
    tm = x.shape[0]
    h = _modulated(x, g_ref[...], mod_ref[...], 3)
    hi = h.astype(BF16)
    lo = (h - hi.astype(F32)).astype(BF16)
    logits_tok = (jnp.dot(lo, wr_ref[0], preferred_element_type=F32)
                  + jnp.dot(hi, wr_ref[1], preferred_element_type=F32)
                  + jnp.dot(hi, wr_ref[0], preferred_element_type=F32))
    logits = logits_tok.T[0:N_EXPERTS, :]
    e = jnp.exp(logits - jnp.max(logits, axis=0, keepdims=True))
    scores = e / jnp.sum(e, axis=0, keepdims=True)
    sel = scores + br_ref[...]
    sel_rows = [sel[k:k + 1, :] for k in range(N_EXPERTS)]
    group_scores = []
    for gi in range(N_GROUPS):
        v = sel_rows[gi * GROUP_SIZE:(gi + 1) * GROUP_SIZE]
        best = v[0] + v[1]
        for a, b in PAIRS[1:]:
            best = jnp.maximum(best, v[a] + v[b])
        group_scores.append(best)
    grp, _ = _first_argmax(group_scores)
    in_sel = [_pick(grp, [sel_rows[gi * GROUP_SIZE + k] for gi in range(N_GROUPS)])
              for k in range(GROUP_SIZE)]
    i1, _ = _first_argmax(in_sel)
    rest = [jnp.where(i1 == float(k), -jnp.inf, in_sel[k]) for k in range(GROUP_SIZE)]
    i2, _ = _first_argmax(rest)
    lo = jnp.minimum(i1, i2)
    hi = jnp.maximum(i1, i2)
    pair = jnp.zeros_like(lo)
    for k, (slot_a, slot_b) in enumerate(PAIR_SLOTS):
        here = jnp.logical_and(lo == float(min(slot_a, slot_b)), hi == float(max(slot_a, slot_b)))
        pair = jnp.where(here, float(k), pair)
    cls = grp * float(len(PAIR_SLOTS)) + pair
    crow = lax.broadcasted_iota(jnp.int32, (CLASS_ROWS, tm), 0).astype(F32)
    onehot = jnp.where(crow == cls, 1.0, 0.0)
    rows = lax.broadcasted_iota(jnp.int32, (tm, tm), 0)
    cols = lax.broadcasted_iota(jnp.int32, (tm, tm), 1)
    upper = jnp.where(rows <= cols, 1.0, 0.0).astype(BF16)
    cum = jnp.dot(onehot.astype(BF16), upper, preferred_element_type=F32)
    carry = carry_ref[:, 0:1]
    rank = jnp.sum(onehot * (cum - 1.0 + carry), axis=0, keepdims=True)
    new_carry = carry + jnp.sum(onehot, axis=1, keepdims=True)
    carry_ref[...] = jnp.broadcast_to(new_carry, carry_ref.shape)
    cnt_ref[...] = jnp.broadcast_to(new_carry, cnt_ref.shape)
    zero = jnp.zeros_like(cls)
    route_ref[...] = jnp.concatenate([cls, rank, zero, zero, zero, zero, zero, zero], axis=0)


def _router_weights(w_router):
    w = jnp.pad(w_router.astype(F32), ((0, 0), (0, LANES - w_router.shape[1])))
    hi = w.astype(BF16)
    lo = (w - hi.astype(F32)).astype(BF16)
    return jnp.stack([hi, lo])


ROW_DMA_UNROLL = 8


def _dispatch_kernel(pos_ref, x_ref, g_ref, mod_ref, init_ref, o_ref, stage_ref, sems):
    del init_ref
    tokens, d = x_ref.shape
    per = d // LANES
    i = pl.program_id(0)
    slot = i % 2
    base = i * tokens
    _store_token_tiles(stage_ref.at[slot], _modulated(x_ref[...], g_ref[...], mod_ref[...], 3))

    def issue(r, carry):
        src = pl.multiple_of(r * per, per)
        dst = pl.multiple_of(pos_ref[base + r], per)
        pltpu.make_async_copy(stage_ref.at[slot, pl.ds(src, per)], o_ref.at[pl.ds(dst, per)],
                              sems.at[slot]).start()
        return carry

    lax.fori_loop(0, tokens, issue, 0, unroll=ROW_DMA_UNROLL)

    def wait_tile(s):
        pltpu.make_async_copy(stage_ref.at[s], o_ref.at[pl.ds(0, tokens * per)], sems.at[s]).wait()

    @pl.when(i > 0)
    def _():
        wait_tile(1 - slot)

    @pl.when(i == pl.num_programs(0) - 1)
    def _():
        wait_tile(slot)


def _dispatch_call(pos_rows, xs, g, mods, layer, init, dims):
    tm, d = TOKEN_TILE, dims["d"]
    per = d // LANES
    tpb, ct, b = dims["tpb"], dims["ct"], dims["b"]
    return pl.pallas_call(
        _dispatch_kernel,
        grid_spec=pltpu.PrefetchScalarGridSpec(
            num_scalar_prefetch=1,
            grid=(dims["tiles"],),
            in_specs=[pl.BlockSpec((tm, d), lambda i, p: (i, 0)),
                      pl.BlockSpec((1, d), lambda i, p: (0, 0)),
                      pl.BlockSpec((None, None, 1, 6 * d),
                                   lambda i, p: (layer, _mod_row(i, tpb, ct, b), 0, 0)),
                      pl.BlockSpec(memory_space=pl.ANY)],
            out_specs=pl.BlockSpec(memory_space=pl.ANY),
            scratch_shapes=[pltpu.VMEM((2, tm * per, LANES), F32), pltpu.SemaphoreType.DMA((2,))],
        ),
        out_shape=jax.ShapeDtypeStruct(init.shape, F32),
        input_output_aliases={4: 0},
        compiler_params=_params("arbitrary"),
        name="dispatch",
    )(pos_rows, xs, g, mods, init)


def _expert_kernel(ea_ref, eb_ref, ok_ref, x_ref, wr_ref, w1a, w3a, w2a, w1b, w3b, w2b, o_ref):
    d = w1a.shape[0]
    te = x_ref.shape[0] // (d // LANES)
    t = pl.program_id(0)

    @pl.when(ok_ref[t] != 0)
    def _():
        h = _load_token_tiles(x_ref, te, d)
        x = h.astype(BF16)
        dw = wr_ref[pl.ds(ea_ref[t], 1), :] - wr_ref[pl.ds(eb_ref[t], 1), :]
        gate_a = jax.nn.sigmoid(jnp.sum(h * dw, axis=1, keepdims=True))
        gate_b = 1.0 - gate_a

        def expert(w1, w3, w2):
            a = jnp.dot(x, w1[...].astype(BF16), preferred_element_type=F32)
            b = jnp.dot(x, w3[...].astype(BF16), preferred_element_type=F32)
            u = (a * jax.nn.sigmoid(a)) * b
            return jnp.dot(u.astype(BF16), w2[...].astype(BF16), preferred_element_type=F32)

        _store_token_tiles(o_ref, gate_a * expert(w1a, w3a, w2a) + gate_b * expert(w1b, w3b, w2b))

    @pl.when(ok_ref[t] == 0)
    def _():
        o_ref[...] = jnp.zeros_like(o_ref)


def _expert_call(tile_a, tile_b, tile_ok, xsorted, w_router_t, w1, w3, w2, layer, d):
    te = TOKEN_TILE
    per = d // LANES
    n_tiles = xsorted.shape[0] // (te * per)
    de = w1.shape[-1]
    sel_a = lambda t, ea, eb, ok: (layer, ea[t], 0, 0)
    sel_b = lambda t, ea, eb, ok: (layer, eb[t], 0, 0)
    used = lambda t, ea, eb, ok: (t, 0)
    return pl.pallas_call(
        _expert_kernel,
        grid_spec=pltpu.PrefetchScalarGridSpec(
            num_scalar_prefetch=3,
            grid=(n_tiles,),
            in_specs=[
                pl.BlockSpec((te * per, LANES), used),
                pl.BlockSpec(w_router_t.shape, lambda t, ea, eb, ok: (0, 0)),
                pl.BlockSpec((None, None, d, de), sel_a), pl.BlockSpec((None, None, d, de), sel_a),
                pl.BlockSpec((None, None, de, d), sel_a),
                pl.BlockSpec((None, None, d, de), sel_b), pl.BlockSpec((None, None, d, de), sel_b),
                pl.BlockSpec((None, None, de, d), sel_b),
            ],
            out_specs=pl.BlockSpec((te * per, LANES), used),
        ),
        out_shape=jax.ShapeDtypeStruct((n_tiles * te * per, LANES), F32),
        compiler_params=_params("arbitrary"),
        name="experts",
    )(tile_a, tile_b, tile_ok, xsorted, w_router_t, w1, w3, w2, w1, w3, w2)


def _combine_kernel(pos_ref, y_ref, x_ref, mod_ref, gf_ref, o_ref, buf_ref, sems,
                    *, tile_of, final_norm):
    tm, d = x_ref.shape
    per = d // LANES
    i = pl.program_id(0)
    slot = i % 2

    def gather(step, s):
        base = tile_of(step) * tm

        def issue(r, carry):
            src = pl.multiple_of(pos_ref[base + r], per)
            dst = pl.multiple_of(r * per, per)
            pltpu.make_async_copy(y_ref.at[pl.ds(src, per)], buf_ref.at[s, pl.ds(dst, per)],
                                  sems.at[s]).start()
            return carry

        lax.fori_loop(0, tm, issue, 0, unroll=ROW_DMA_UNROLL)

    @pl.when(i == 0)
    def _():
        gather(0, 0)

    @pl.when(i + 1 < pl.num_programs(0))
    def _():
        gather(i + 1, 1 - slot)

    pltpu.make_async_copy(y_ref.at[pl.ds(0, tm * per)], buf_ref.at[slot], sems.at[slot]).wait()
    out = x_ref[...] + mod_ref[:, 5 * d:6 * d] * _load_token_tiles(buf_ref.at[slot], tm, d)
    if final_norm:
        ms = jnp.mean(out * out, axis=-1, keepdims=True)
        out = out * lax.rsqrt(ms + NORM_EPS) * gf_ref[...]
    o_ref[...] = out


def _combine_call(pos, ysorted, xs, mods, layer, g_final, dims, final_norm):
    tm, d = TOKEN_TILE, dims["d"]
    tpb, ct, b = dims["tpb"], dims["ct"], dims["b"]
    if final_norm:
        lt = tpb - ct
        tile_of = lambda i: (i // lt) * tpb + ct + i % lt
        n_tiles = b * lt
    else:
        tile_of = lambda i: i
        n_tiles = dims["tiles"]
    mod_spec = pl.BlockSpec((None, None, 1, 6 * d),
                            lambda i, p: (layer, _mod_row(tile_of(i), tpb, ct, b), 0, 0))
    return pl.pallas_call(
        functools.partial(_combine_kernel, tile_of=tile_of, final_norm=final_norm),
        grid_spec=pltpu.PrefetchScalarGridSpec(
            num_scalar_prefetch=1,
            grid=(n_tiles,),
            in_specs=[
                pl.BlockSpec(memory_space=pl.ANY),
                pl.BlockSpec((tm, d), lambda i, p: (tile_of(i), 0)),
                mod_spec,
                pl.BlockSpec((1, d), lambda i, p: (0, 0)),
            ],
            out_specs=pl.BlockSpec((tm, d), lambda i, p: (i, 0)),
            scratch_shapes=[pltpu.VMEM((2, tm * (d // LANES), LANES), F32), pltpu.SemaphoreType.DMA((2,))],
        ),
        out_shape=jax.ShapeDtypeStruct((n_tiles * tm, d), F32),
        compiler_params=_params("arbitrary"),
        name="combine",
    )(pos, ysorted, xs, mods, g_final)


_CLASS_A = np.array([g * GROUP_SIZE + a for g in range(N_GROUPS) for a, _ in PAIR_SLOTS], np.int32)
_CLASS_B = np.array([g * GROUP_SIZE + b for g in range(N_GROUPS) for _, b in PAIR_SLOTS], np.int32)


def _routing_plan(route, counts, n_tiles_padded):
    te = TOKEN_TILE
    cls = route[0].astype(jnp.int32)
    rank = route[1].astype(jnp.int32)
    cnt = counts[:N_CLASSES, 0].astype(jnp.int32)
    tiles = (cnt + te - 1) // te
    tile_end = jnp.cumsum(tiles)
    tile_start = tile_end - tiles
    onehot = (cls[:, None] == jnp.arange(N_CLASSES, dtype=jnp.int32)[None, :]).astype(jnp.int32)
    pos = jnp.sum(onehot * (tile_start * te)[None, :], axis=1) + rank
    t = jnp.arange(n_tiles_padded, dtype=jnp.int32)
    ok = (t < tile_end[-1]).astype(jnp.int32)
    tcls = jnp.sum((tile_end[None, :] <= jnp.minimum(t, tile_end[-1] - 1)[:, None]).astype(jnp.int32), axis=1)
    tile_a = jnp.take(jnp.asarray(_CLASS_A), tcls)
    tile_b = jnp.take(jnp.asarray(_CLASS_B), tcls)
    return pos, tile_a, tile_b, ok


def _moe(xs, g_ffn, route, counts, sorted_buf, mods, layer, w_router_t, w1, w3, w2, g_final, dims, final_norm):
    per = dims["d"] // LANES
    n_tiles_padded = sorted_buf.shape[0] // (TOKEN_TILE * per)
    pos, tile_a, tile_b, tile_ok = _routing_plan(route, counts, n_tiles_padded)
    xsorted = _dispatch_call(pos * per, xs, g_ffn, mods, layer, sorted_buf, dims)
    ysorted = _expert_call(tile_a, tile_b, tile_ok, xsorted, w_router_t, w1, w3, w2, layer, dims["d"])
    return _combine_call(pos * per, ysorted, xs, mods, layer, g_final, dims, final_norm), xsorted


def _pos_table(n_tokens, dim):
    rows = n_tokens // GRID_WIDTH
    r, col = jnp.meshgrid(jnp.arange(rows, dtype=F32), jnp.arange(GRID_WIDTH, dtype=F32), indexing="ij")
    quarter = dim // 4
    freqs = jnp.exp(-math.log(POS_BASE) * jnp.arange(quarter, dtype=F32) / quarter)

    def enc(p):
        ang = p.reshape(-1, 1) * freqs
        return jnp.concatenate([jnp.sin(ang), jnp.cos(ang)], axis=-1)

    return jnp.concatenate([enc(r), enc(col)], axis=-1)


def _mlstm_weights(w_in, b_gate):
    nq = HEADS * HEAD_QK
    nv = HEADS * HEAD_V
    wq = w_in[:, 0:nq] * (HEAD_QK ** -0.5)
    wk = w_in[:, nq:2 * nq]
    wv = w_in[:, 2 * nq:2 * nq + nv]
    wo = w_in[:, 2 * nq + nv:2 * nq + 2 * nv]
    wg = w_in[:, 2 * nq + 2 * nv:]
    wg_pad = jnp.pad(wg, ((0, 0), (0, LANES - wg.shape[1])))
    w_tok = jnp.concatenate([wq, wk, wg_pad], axis=1).astype(BF16)
    w_feat = jnp.concatenate([wv, wo, wg], axis=1).T.astype(BF16)
    bias = b_gate.reshape(-1).astype(F32)
    bias_c = jnp.pad(bias, (0, LANES - bias.shape[0])).reshape(1, LANES)
    bias_r = bias.reshape(-1, 1)
    return w_tok, w_feat, bias_c, bias_r


def kernel(x, c, ctx, c_ctx, w_ada, b_ada, g_mix, g_ffn, g_final, m_w_in, m_b_gate, m_g_head, m_w_out, r_w_in, r_conv_w, r_conv_b, r_w_gate, r_b_gate, r_lam, r_w_out, w_router, b_router, e_w1, e_w3, e_w2):
    batch, t_len, d = x.shape
    ctx_len = ctx.shape[1]
    depth = w_ada.shape[0]
    tm = TOKEN_TILE
    assert t_len % tm == 0 and ctx_len % tm == 0 and batch + 1 <= ADA_ROWS
    assert d == HEADS * HEAD_V and t_len % GRID_WIDTH == 0
    s_len = ctx_len + t_len
    dims = dict(b=batch, d=d, n=batch * s_len, tpb=s_len // tm, ct=ctx_len // tm,
                tiles=batch * s_len // tm)

    cvec = jnp.concatenate([c, c_ctx[None, :], jnp.zeros((ADA_ROWS - batch - 1, d), F32)], axis=0)
    mods = _ada_call(cvec, w_ada, b_ada).reshape(depth, ADA_ROWS, 1, 6 * d)
    xs = (ctx.reshape(batch * ctx_len, d), x.reshape(batch * t_len, d), _pos_table(t_len, d))

    w_router_t = w_router.T.astype(F32)
    w_router_pieces = _router_weights(w_router)
    b_router_col = b_router.reshape(-1, 1).astype(F32)
    g_final2 = g_final.reshape(1, d)
    sorted_buf = jnp.zeros(((dims["tiles"] + N_CLASSES) * tm * (d // LANES), LANES), F32)
    out = None
    for i in range(depth):
        j = i // 2
        g_mix_i = g_mix[i].reshape(1, d)
        g_ffn_i = g_ffn[i].reshape(1, d)
        if i % 2 == 0:
            w_tok, w_feat, bias_c, bias_r = _mlstm_weights(m_w_in[j], m_b_gate[j])
            proj = _mproj_call(xs, g_mix_i, mods, i, w_tok, w_feat, dims)
            if isinstance(xs, tuple):
                xs, proj = proj[0], proj[1:]
            q, k, vt, ot, gc, gr = proj
            hf, hb = _mlstm_call(q, k, vt, gc, gr, bias_c, bias_r, dims)
            w_out = (m_g_head[j][:, None] * m_w_out[j]).astype(BF16)
            xs, route, counts = _readout_call(_mreadout_kernel, "mlstm_readout", True, hf, hb, ot, w_out,
                                              xs, mods, i, g_ffn_i, w_router_pieces, b_router_col, dims)
        else:
            gy, xc = _rproj_call(xs, g_mix_i, mods, i, r_w_in[j].astype(BF16), r_conv_w[j],
                                 r_conv_b[j].reshape(1, d), dims)
            hf, hb = _lru_call(xc, (0.5 * r_w_gate[j]).astype(BF16), 0.5 * r_b_gate[j].reshape(2, 1, -1),
                               r_lam[j].reshape(2, 1, d), dims)
            xs, route, counts = _readout_call(_readout_kernel, "rglru_readout", False, hf, hb, gy,
                                              r_w_out[j].astype(BF16), xs, mods, i, g_ffn_i,
                                              w_router_pieces, b_router_col, dims)
        last = i == depth - 1
        res, sorted_buf = _moe(xs, g_ffn_i, route, counts, sorted_buf, mods, i, w_router_t,
                               e_w1, e_w3, e_w2, g_final2, dims, last)
        if last:
            out = res
        else:
            xs = res
    return out.reshape(batch, t_len, d)
```
